```python
import jax, jax.numpy as jnp
from jax import lax
import numpy as np

D_MODEL = 2048
BATCH = 8
SEQ = 2048
DEPTH = 4

CHUNK = 64
A_HEAD_DIM = 128
A_WIDTH = D_MODEL // 2
A_HEADS = A_WIDTH // A_HEAD_DIM
A_LEFT_CHUNKS = 8
A_BAND = A_LEFT_CHUNKS + 1
A_MAX_REL = 256
B_WIDTH = D_MODEL - A_WIDTH
CONV_WIDTH = 3
C_BLOCK = 128
C_WIDTH = D_MODEL
C_GROUPS = 8
FFN_HIDDEN = -(-8 * D_MODEL // (3 * 256)) * 256
N_EVEN = (DEPTH + 1) // 2
N_ODD = DEPTH // 2
EPS = 1e-6
NEG_INF = -1e30

kernel_name = 'hybrid_chunk_attn_conv_gmlp_trunk'


def rms_norm(x, g):
    x32 = x.astype(jnp.float32)
    y = x32 * lax.rsqrt(jnp.mean(x32 * x32, axis=-1, keepdims=True) + EPS)
    return (y * g.astype(jnp.float32)).astype(x.dtype)


def layer_norm(x, g, b):
    x32 = x.astype(jnp.float32)
    mu = jnp.mean(x32, axis=-1, keepdims=True)
    xc = x32 - mu
    var = jnp.mean(xc * xc, axis=-1, keepdims=True)
    y = xc * lax.rsqrt(var + EPS) * g.astype(jnp.float32) + b.astype(jnp.float32)
    return y.astype(x.dtype)


def band_attention(q, k, v, rel_bias):
    bsz, seq, heads, dh = q.shape
    nc = seq // CHUNK
    qc = q.reshape(bsz, nc, CHUNK, heads, dh)
    pad = ((0, 0), (A_LEFT_CHUNKS * CHUNK, 0), (0, 0), (0, 0))
    kp = jnp.pad(k, pad).reshape(bsz, nc + A_LEFT_CHUNKS, CHUNK, heads, dh)
    vp = jnp.pad(v, pad).reshape(bsz, nc + A_LEFT_CHUNKS, CHUNK, heads, dh)
    band_idx = jnp.arange(nc)[:, None] + jnp.arange(A_BAND)[None, :]
    kb = kp[:, band_idx].reshape(bsz, nc, A_BAND * CHUNK, heads, dh)
    vb = vp[:, band_idx].reshape(bsz, nc, A_BAND * CHUNK, heads, dh)
    scores = jnp.einsum('bnqhd,bnkhd->bhnqk', qc, kb).astype(jnp.float32) * (dh ** -0.5)
    q_pos = jnp.arange(CHUNK)[:, None] + A_LEFT_CHUNKS * CHUNK
    k_pos = jnp.arange(A_BAND * CHUNK)[None, :]
    rel_idx = jnp.clip(q_pos - k_pos, -A_MAX_REL, A_MAX_REL) + A_MAX_REL
    bias = rel_bias.astype(jnp.float32)[:, rel_idx]
    scores = scores + bias[:, None]
    valid = jnp.repeat(band_idx >= A_LEFT_CHUNKS, CHUNK, axis=1)
    scores = jnp.where(valid[None, None, :, None, :], scores, NEG_INF)
    p = jax.nn.softmax(scores, axis=-1).astype(v.dtype)
    out = jnp.einsum('bhnqk,bnkhd->bnqhd', p, vb)
    return out.reshape(bsz, seq, heads * dh)


def gated_short_conv(b_gate, c_gate, h, conv_w):
    z = c_gate * h
    seq = z.shape[1]
    zp = jnp.pad(z, ((0, 0), (CONV_WIDTH - 1, 0), (0, 0)))
    y = conv_w[0] * zp[:, 0:seq]
    for j in range(1, CONV_WIDTH):
        y = y + conv_w[j] * zp[:, j:j + seq]
    return b_gate * y


def mixer_ab(h, w_in, rel_bias, conv_w, w_out):
    bsz, seq, _ = h.shape
    proj = h @ w_in
    cuts = [A_WIDTH, 2 * A_WIDTH, 3 * A_WIDTH, 3 * A_WIDTH + B_WIDTH, 3 * A_WIDTH + 2 * B_WIDTH]
    q, k, v, b_gate, c_gate, hv = jnp.split(proj, cuts, axis=-1)
    shp = (bsz, seq, A_HEADS, A_HEAD_DIM)
    attn = band_attention(q.reshape(shp), k.reshape(shp), v.reshape(shp), rel_bias)
    conv = gated_short_conv(b_gate, c_gate, hv, conv_w)
    return jnp.concatenate([attn, conv], axis=-1) @ w_out


def mixer_c(h, w_in, ln_g, ln_b, w_s, b_s, w_out):
    bsz, seq, _ = h.shape
    nb = seq // C_BLOCK
    z = jax.nn.gelu(h @ w_in, approximate=False)
    u, v = jnp.split(z, 2, axis=-1)
    v = layer_norm(v, ln_g, ln_b)
    pos = jnp.arange(C_BLOCK)
    mask = (pos[None, :] // CHUNK) <= (pos[:, None] // CHUNK)
    w_m = jnp.where(mask[None], w_s, jnp.zeros_like(w_s))
    vg = v.reshape(bsz, nb, C_BLOCK, C_GROUPS, C_WIDTH // C_GROUPS)
    s = jnp.einsum('gts,bnsgd->bntgd', w_m, vg) + jnp.transpose(b_s)[:, :, None]
    return (u * s.reshape(bsz, seq, C_WIDTH)) @ w_out


def swiglu(h, w_gate, w_up, w_down):
    return (jax.nn.silu(h @ w_gate) * (h @ w_up)) @ w_down


def _fwd_setup_inputs(seed: int = 0) -> dict:
    key = jax.random.key(seed)
    ks = jax.random.split(key, 17)
    nrm = jax.random.normal
    f32 = jnp.float32
    in_ab = 3 * A_WIDTH + 3 * B_WIDTH
    return {
        'x': nrm(ks[0], (BATCH, SEQ, D_MODEL), f32),
        'mix_norm': 1.0 + 0.02 * nrm(ks[1], (DEPTH, D_MODEL), f32),
        'ab_w_in': nrm(ks[2], (N_EVEN, D_MODEL, in_ab), f32) * D_MODEL ** -0.5,
        'ab_rel_bias': 0.5 * nrm(ks[3], (N_EVEN, A_HEADS, 2 * A_MAX_REL + 1), f32),
        'ab_conv_w': nrm(ks[4], (N_EVEN, CONV_WIDTH, B_WIDTH), f32) * CONV_WIDTH ** -0.5,
        'ab_w_out': nrm(ks[5], (N_EVEN, A_WIDTH + B_WIDTH, D_MODEL), f32) * (A_WIDTH + B_WIDTH) ** -0.5,
        'c_w_in': nrm(ks[6], (N_ODD, D_MODEL, 2 * C_WIDTH), f32) * D_MODEL ** -0.5,
        'c_ln_g': 1.0 + 0.02 * nrm(ks[7], (N_ODD, C_WIDTH), f32),
        'c_ln_b': 0.02 * nrm(ks[8], (N_ODD, C_WIDTH), f32),
        'c_w_s': nrm(ks[9], (N_ODD, C_GROUPS, C_BLOCK, C_BLOCK), f32) * C_BLOCK ** -0.5,
        'c_b_s': 1.0 + 0.02 * nrm(ks[10], (N_ODD, C_GROUPS, C_BLOCK), f32),
        'c_w_out': nrm(ks[11], (N_ODD, C_WIDTH, D_MODEL), f32) * C_WIDTH ** -0.5,
        'ffn_norm': 1.0 + 0.02 * nrm(ks[12], (DEPTH, D_MODEL), f32),
        'ffn_w_gate': nrm(ks[13], (DEPTH, D_MODEL, FFN_HIDDEN), f32) * D_MODEL ** -0.5,
        'ffn_w_up': nrm(ks[14], (DEPTH, D_MODEL, FFN_HIDDEN), f32) * D_MODEL ** -0.5,
        'ffn_w_down': nrm(ks[15], (DEPTH, FFN_HIDDEN, D_MODEL), f32) * FFN_HIDDEN ** -0.5,
        'final_norm': 1.0 + 0.02 * nrm(ks[16], (D_MODEL,), f32),
    }


def _fwd_reference(x, mix_norm, ab_w_in, ab_rel_bias, ab_conv_w, ab_w_out, c_w_in, c_ln_g, c_ln_b,
              c_w_s, c_b_s, c_w_out, ffn_norm, ffn_w_gate, ffn_w_up, ffn_w_down, final_norm):
    for layer in range(DEPTH):
        i = layer // 2
        h = rms_norm(x, mix_norm[layer])
        if layer % 2 == 0:
            x = x + mixer_ab(h, ab_w_in[i], ab_rel_bias[i], ab_conv_w[i], ab_w_out[i])
        else:
            x = x + mixer_c(h, c_w_in[i], c_ln_g[i], c_ln_b[i], c_w_s[i], c_b_s[i], c_w_out[i])
        h = rms_norm(x, ffn_norm[layer])
        x = x + swiglu(h, ffn_w_gate[layer], ffn_w_up[layer], ffn_w_down[layer])
    return rms_norm(x, final_norm)


import jax as _jax
import jax.numpy as _jnp

TWIN_FORMAT = 'train_step'
FWD_PARAMS = ['x', 'mix_norm', 'ab_w_in', 'ab_rel_bias', 'ab_conv_w', 'ab_w_out', 'c_w_in', 'c_ln_g', 'c_ln_b', 'c_w_s', 'c_b_s', 'c_w_out', 'ffn_norm', 'ffn_w_gate', 'ffn_w_up', 'ffn_w_down', 'final_norm']
TWIN_WEIGHTS = ['mix_norm', 'ab_w_in', 'ab_rel_bias', 'ab_conv_w', 'ab_w_out', 'c_w_in', 'c_ln_g', 'c_ln_b', 'c_w_s', 'c_b_s', 'c_w_out', 'ffn_norm', 'ffn_w_gate', 'ffn_w_up', 'ffn_w_down', 'final_norm']
TWIN_DIFF_INPUT = 'x'
TWIN_INPUTS = ['x', 'mix_norm', 'ab_w_in', 'ab_rel_bias', 'ab_conv_w', 'ab_w_out', 'c_w_in', 'c_ln_g', 'c_ln_b', 'c_w_s', 'c_b_s', 'c_w_out', 'ffn_norm', 'ffn_w_gate', 'ffn_w_up', 'ffn_w_down', 'final_norm', 'loss_target', 'm_mix_norm', 'm_ab_w_in', 'm_ab_rel_bias', 'm_ab_conv_w', 'm_ab_w_out', 'm_c_w_in', 'm_c_ln_g', 'm_c_ln_b', 'm_c_w_s', 'm_c_b_s', 'm_c_w_out', 'm_ffn_norm', 'm_ffn_w_gate', 'm_ffn_w_up', 'm_ffn_w_down', 'm_final_norm', 'v_mix_norm', 'v_ab_w_in', 'v_ab_rel_bias', 'v_ab_conv_w', 'v_ab_w_out', 'v_c_w_in', 'v_c_ln_g', 'v_c_ln_b', 'v_c_w_s', 'v_c_b_s', 'v_c_w_out', 'v_ffn_norm', 'v_ffn_w_gate', 'v_ffn_w_up', 'v_ffn_w_down', 'v_final_norm']
TWIN_OUTPUTS = ['loss', 'grad_x', 'grad_mix_norm', 'grad_ab_w_in', 'grad_ab_rel_bias', 'grad_ab_conv_w', 'grad_ab_w_out', 'grad_c_w_in', 'grad_c_ln_g', 'grad_c_ln_b', 'grad_c_w_s', 'grad_c_b_s', 'grad_c_w_out', 'grad_ffn_norm', 'grad_ffn_w_gate', 'grad_ffn_w_up', 'grad_ffn_w_down', 'grad_final_norm', 'delta_mix_norm', 'delta_ab_w_in', 'delta_ab_rel_bias', 'delta_ab_conv_w', 'delta_ab_w_out', 'delta_c_w_in', 'delta_c_ln_g', 'delta_c_ln_b', 'delta_c_w_s', 'delta_c_b_s', 'delta_c_w_out', 'delta_ffn_norm', 'delta_ffn_w_gate', 'delta_ffn_w_up', 'delta_ffn_w_down', 'delta_final_norm', 'new_m_mix_norm', 'new_m_ab_w_in', 'new_m_ab_rel_bias', 'new_m_ab_conv_w', 'new_m_ab_w_out', 'new_m_c_w_in', 'new_m_c_ln_g', 'new_m_c_ln_b', 'new_m_c_w_s', 'new_m_c_b_s', 'new_m_c_w_out', 'new_m_ffn_norm', 'new_m_ffn_w_gate', 'new_m_ffn_w_up', 'new_m_ffn_w_down', 'new_m_final_norm', 'new_v_mix_norm', 'new_v_ab_w_in', 'new_v_ab_rel_bias', 'new_v_ab_conv_w', 'new_v_ab_w_out', 'new_v_c_w_in', 'new_v_c_ln_g', 'new_v_c_ln_b', 'new_v_c_w_s', 'new_v_c_b_s', 'new_v_c_w_out', 'new_v_ffn_norm', 'new_v_ffn_w_gate', 'new_v_ffn_w_up', 'new_v_ffn_w_down', 'new_v_final_norm']
TWIN_LEAF_KINDS = {'loss': 'loss', 'grad_x': 'grad_x', 'grad_mix_norm': 'grad_w', 'grad_ab_w_in': 'grad_w', 'grad_ab_rel_bias': 'grad_w', 'grad_ab_conv_w': 'grad_w', 'grad_ab_w_out': 'grad_w', 'grad_c_w_in': 'grad_w', 'grad_c_ln_g': 'grad_w', 'grad_c_ln_b': 'grad_w', 'grad_c_w_s': 'grad_w', 'grad_c_b_s': 'grad_w', 'grad_c_w_out': 'grad_w', 'grad_ffn_norm': 'grad_w', 'grad_ffn_w_gate': 'grad_w', 'grad_ffn_w_up': 'grad_w', 'grad_ffn_w_down': 'grad_w', 'grad_final_norm': 'grad_w', 'delta_mix_norm': 'delta_w', 'delta_ab_w_in': 'delta_w', 'delta_ab_rel_bias': 'delta_w', 'delta_ab_conv_w': 'delta_w', 'delta_ab_w_out': 'delta_w', 'delta_c_w_in': 'delta_w', 'delta_c_ln_g': 'delta_w', 'delta_c_ln_b': 'delta_w', 'delta_c_w_s': 'delta_w', 'delta_c_b_s': 'delta_w', 'delta_c_w_out': 'delta_w', 'delta_ffn_norm': 'delta_w', 'delta_ffn_w_gate': 'delta_w', 'delta_ffn_w_up': 'delta_w', 'delta_ffn_w_down': 'delta_w', 'delta_final_norm': 'delta_w', 'new_m_mix_norm': 'new_m', 'new_m_ab_w_in': 'new_m', 'new_m_ab_rel_bias': 'new_m', 'new_m_ab_conv_w': 'new_m', 'new_m_ab_w_out': 'new_m', 'new_m_c_w_in': 'new_m', 'new_m_c_ln_g': 'new_m', 'new_m_c_ln_b': 'new_m', 'new_m_c_w_s': 'new_m', 'new_m_c_b_s': 'new_m', 'new_m_c_w_out': 'new_m', 'new_m_ffn_norm': 'new_m', 'new_m_ffn_w_gate': 'new_m', 'new_m_ffn_w_up': 'new_m', 'new_m_ffn_w_down': 'new_m', 'new_m_final_norm': 'new_m', 'new_v_mix_norm': 'new_v', 'new_v_ab_w_in': 'new_v', 'new_v_ab_rel_bias': 'new_v', 'new_v_ab_conv_w': 'new_v', 'new_v_ab_w_out': 'new_v', 'new_v_c_w_in': 'new_v', 'new_v_c_ln_g': 'new_v', 'new_v_c_ln_b': 'new_v', 'new_v_c_w_s': 'new_v', 'new_v_c_b_s': 'new_v', 'new_v_c_w_out': 'new_v', 'new_v_ffn_norm': 'new_v', 'new_v_ffn_w_gate': 'new_v', 'new_v_ffn_w_up': 'new_v', 'new_v_ffn_w_down': 'new_v', 'new_v_final_norm': 'new_v'}


def _forward(args):
    return _fwd_reference(*[args[k] for k in FWD_PARAMS])


def _output_shape():
    out = _jax.eval_shape(lambda: _forward(_fwd_setup_inputs(0)))
    return out.shape, out.dtype

N_MICROBATCH = 1
ADAM_LR = 0.001
ADAM_B1 = 0.9
ADAM_B2 = 0.999
ADAM_EPS = 1e-08
ADAM_WD = 0.01
ADAM_STEP = 10
PER_EXAMPLE_BATCH_AXIS = {'x': 0, 'loss_target': 0}
SHARED_INPUTS = []
_WEIGHT_DTYPES = {'mix_norm': _jnp.float32, 'ab_w_in': _jnp.float32, 'ab_rel_bias': _jnp.float32, 'ab_conv_w': _jnp.float32, 'ab_w_out': _jnp.float32, 'c_w_in': _jnp.float32, 'c_ln_g': _jnp.float32, 'c_ln_b': _jnp.float32, 'c_w_s': _jnp.float32, 'c_b_s': _jnp.float32, 'c_w_out': _jnp.float32, 'ffn_norm': _jnp.float32, 'ffn_w_gate': _jnp.float32, 'ffn_w_up': _jnp.float32, 'ffn_w_down': _jnp.float32, 'final_norm': _jnp.float32}
MOMENT_SCALE = {'mix_norm': 6.542244e-02, 'ab_w_in': 4.650325e-02, 'ab_rel_bias': 4.848876e-03, 'ab_conv_w': 6.524629e-02, 'ab_w_out': 4.652812e-02, 'c_w_in': 3.375997e-02, 'c_ln_g': 2.430423e-02, 'c_ln_b': 2.425402e-02, 'c_w_s': 3.443998e-02, 'c_b_s': 4.031027e-02, 'c_w_out': 3.799552e-02, 'ffn_norm': 4.062667e-02, 'ffn_w_gate': 1.766135e-02, 'ffn_w_up': 1.711194e-02, 'ffn_w_down': 2.835765e-02, 'final_norm': 8.006883e+00}


def _to_microbatches(a, axis):
    t = _jnp.moveaxis(a, axis, 0)
    t = t.reshape((N_MICROBATCH, t.shape[0] // N_MICROBATCH) + t.shape[1:])
    return _jnp.moveaxis(t, 1, axis + 1)


def setup_inputs(seed: int = 0) -> dict:
    inp = _fwd_setup_inputs(seed)
    key = _jax.random.fold_in(_jax.random.key(seed), 7919)
    shape, _ = _output_shape()
    out = dict(inp)
    out["loss_target"] = _jax.random.normal(_jax.random.fold_in(key, 0), shape, _jnp.float32)
    for i, name in enumerate(TWIN_WEIGHTS):
        w = inp[name].astype(_jnp.float32)
        if MOMENT_SCALE is None:
            s = _jnp.sqrt(_jnp.mean(_jnp.square(w)) + 1e-30)
        else:
            s = MOMENT_SCALE[name]
        km, kv = _jax.random.split(_jax.random.fold_in(key, i + 1))
        out[name] = w
        out["m_" + name] = s * _jax.random.normal(km, w.shape, _jnp.float32)
        out["v_" + name] = (s * s) * _jax.random.uniform(kv, w.shape, _jnp.float32, 0.5, 1.5)
    if N_MICROBATCH > 1:
        for name, axis in PER_EXAMPLE_BATCH_AXIS.items():
            out[name] = _to_microbatches(out[name], axis)
    return {'x': out['x'], 'mix_norm': out['mix_norm'], 'ab_w_in': out['ab_w_in'], 'ab_rel_bias': out['ab_rel_bias'], 'ab_conv_w': out['ab_conv_w'], 'ab_w_out': out['ab_w_out'], 'c_w_in': out['c_w_in'], 'c_ln_g': out['c_ln_g'], 'c_ln_b': out['c_ln_b'], 'c_w_s': out['c_w_s'], 'c_b_s': out['c_b_s'], 'c_w_out': out['c_w_out'], 'ffn_norm': out['ffn_norm'], 'ffn_w_gate': out['ffn_w_gate'], 'ffn_w_up': out['ffn_w_up'], 'ffn_w_down': out['ffn_w_down'], 'final_norm': out['final_norm'], 'loss_target': out['loss_target'], 'm_mix_norm': out['m_mix_norm'], 'm_ab_w_in': out['m_ab_w_in'], 'm_ab_rel_bias': out['m_ab_rel_bias'], 'm_ab_conv_w': out['m_ab_conv_w'], 'm_ab_w_out': out['m_ab_w_out'], 'm_c_w_in': out['m_c_w_in'], 'm_c_ln_g': out['m_c_ln_g'], 'm_c_ln_b': out['m_c_ln_b'], 'm_c_w_s': out['m_c_w_s'], 'm_c_b_s': out['m_c_b_s'], 'm_c_w_out': out['m_c_w_out'], 'm_ffn_norm': out['m_ffn_norm'], 'm_ffn_w_gate': out['m_ffn_w_gate'], 'm_ffn_w_up': out['m_ffn_w_up'], 'm_ffn_w_down': out['m_ffn_w_down'], 'm_final_norm': out['m_final_norm'], 'v_mix_norm': out['v_mix_norm'], 'v_ab_w_in': out['v_ab_w_in'], 'v_ab_rel_bias': out['v_ab_rel_bias'], 'v_ab_conv_w': out['v_ab_conv_w'], 'v_ab_w_out': out['v_ab_w_out'], 'v_c_w_in': out['v_c_w_in'], 'v_c_ln_g': out['v_c_ln_g'], 'v_c_ln_b': out['v_c_ln_b'], 'v_c_w_s': out['v_c_w_s'], 'v_c_b_s': out['v_c_b_s'], 'v_c_w_out': out['v_c_w_out'], 'v_ffn_norm': out['v_ffn_norm'], 'v_ffn_w_gate': out['v_ffn_w_gate'], 'v_ffn_w_up': out['v_ffn_w_up'], 'v_ffn_w_down': out['v_ffn_w_down'], 'v_final_norm': out['v_final_norm']}


def _loss(weights, diff, rest, loss_target):
    with _jax.named_scope("forward"):
        args = {**rest, TWIN_DIFF_INPUT: diff, **{k: w.astype(_WEIGHT_DTYPES[k]) for k, w in weights.items()}}
        y = _forward(args)
    with _jax.named_scope("loss_head"):
        err = _jnp.square(y.astype(_jnp.float32) - loss_target)
        return 0.5 * _jnp.sum(_jnp.mean(err, axis=-1)) if err.ndim else 0.5 * err


def _adamw(w, g, m, v):
    m = ADAM_B1 * m + (1.0 - ADAM_B1) * g
    v = ADAM_B2 * v + (1.0 - ADAM_B2) * _jnp.square(g)
    m_hat = m / (1.0 - ADAM_B1 ** ADAM_STEP)
    v_hat = v / (1.0 - ADAM_B2 ** ADAM_STEP)
    delta = -ADAM_LR * (m_hat / (_jnp.sqrt(v_hat) + ADAM_EPS) + ADAM_WD * w)
    return delta, m, v


def reference(x, mix_norm, ab_w_in, ab_rel_bias, ab_conv_w, ab_w_out, c_w_in, c_ln_g, c_ln_b, c_w_s, c_b_s, c_w_out, ffn_norm, ffn_w_gate, ffn_w_up, ffn_w_down, final_norm, loss_target, m_mix_norm, m_ab_w_in, m_ab_rel_bias, m_ab_conv_w, m_ab_w_out, m_c_w_in, m_c_ln_g, m_c_ln_b, m_c_w_s, m_c_b_s, m_c_w_out, m_ffn_norm, m_ffn_w_gate, m_ffn_w_up, m_ffn_w_down, m_final_norm, v_mix_norm, v_ab_w_in, v_ab_rel_bias, v_ab_conv_w, v_ab_w_out, v_c_w_in, v_c_ln_g, v_c_ln_b, v_c_w_s, v_c_b_s, v_c_w_out, v_ffn_norm, v_ffn_w_gate, v_ffn_w_up, v_ffn_w_down, v_final_norm):
    given = dict(x=x, mix_norm=mix_norm, ab_w_in=ab_w_in, ab_rel_bias=ab_rel_bias, ab_conv_w=ab_conv_w, ab_w_out=ab_w_out, c_w_in=c_w_in, c_ln_g=c_ln_g, c_ln_b=c_ln_b, c_w_s=c_w_s, c_b_s=c_b_s, c_w_out=c_w_out, ffn_norm=ffn_norm, ffn_w_gate=ffn_w_gate, ffn_w_up=ffn_w_up, ffn_w_down=ffn_w_down, final_norm=final_norm, loss_target=loss_target, m_mix_norm=m_mix_norm, m_ab_w_in=m_ab_w_in, m_ab_rel_bias=m_ab_rel_bias, m_ab_conv_w=m_ab_conv_w, m_ab_w_out=m_ab_w_out, m_c_w_in=m_c_w_in, m_c_ln_g=m_c_ln_g, m_c_ln_b=m_c_ln_b, m_c_w_s=m_c_w_s, m_c_b_s=m_c_b_s, m_c_w_out=m_c_w_out, m_ffn_norm=m_ffn_norm, m_ffn_w_gate=m_ffn_w_gate, m_ffn_w_up=m_ffn_w_up, m_ffn_w_down=m_ffn_w_down, m_final_norm=m_final_norm, v_mix_norm=v_mix_norm, v_ab_w_in=v_ab_w_in, v_ab_rel_bias=v_ab_rel_bias, v_ab_conv_w=v_ab_conv_w, v_ab_w_out=v_ab_w_out, v_c_w_in=v_c_w_in, v_c_ln_g=v_c_ln_g, v_c_ln_b=v_c_ln_b, v_c_w_s=v_c_w_s, v_c_b_s=v_c_b_s, v_c_w_out=v_c_w_out, v_ffn_norm=v_ffn_norm, v_ffn_w_gate=v_ffn_w_gate, v_ffn_w_up=v_ffn_w_up, v_ffn_w_down=v_ffn_w_down, v_final_norm=v_final_norm)
    weights = {n: given[n] for n in TWIN_WEIGHTS}
    shared = {n: given[n] for n in SHARED_INPUTS}
    per_example = {n: given[n] for n in ['x']}
    grad_fn = _jax.value_and_grad(_loss, argnums=(0, 1))

    def one_microbatch(ex, loss_target):
        ex = dict(ex)
        diff = ex.pop(TWIN_DIFF_INPUT)
        return grad_fn(weights, diff, {**shared, **ex}, loss_target)

    if N_MICROBATCH == 1:
        loss, (grad_w, grad_x) = one_microbatch(per_example, given["loss_target"])
    else:
        def body(carry, xs):
            loss_sum, grad_sum = carry
            l_k, (gw_k, gx_k) = one_microbatch(xs[0], xs[1])
            with _jax.named_scope("update"):
                return (loss_sum + l_k, _jax.tree.map(_jnp.add, grad_sum, gw_k)), gx_k

        init = (_jnp.zeros((), _jnp.float32), _jax.tree.map(_jnp.zeros_like, weights))
        (loss, grad_w), grad_x = _jax.lax.scan(body, init, (per_example, given["loss_target"]))
    with _jax.named_scope("update"):
        delta_w, new_m, new_v = {}, {}, {}
        for n in TWIN_WEIGHTS:
            delta_w[n], new_m[n], new_v[n] = _adamw(weights[n], grad_w[n], given["m_" + n], given["v_" + n])
    return (loss, grad_x, *[grad_w[n] for n in TWIN_WEIGHTS], *[delta_w[n] for n in TWIN_WEIGHTS],
            *[new_m[n] for n in TWIN_WEIGHTS], *[new_v[n] for n in TWIN_WEIGHTS])
```

```python
import numpy as np
import jax
import jax.numpy as jnp
from jax import lax
from jax.experimental import pallas as pl
from jax.experimental.pallas import tpu as pltpu

D_MODEL = 2048
SEQ = 2048
DEPTH = 4
CHUNK = 64
A_HEAD_DIM = 128
A_LEFT_CHUNKS = 8
A_MAX_REL = 256
CONV_WIDTH = 3
C_BLOCK = 128
C_GROUPS = 8
EPS = 1e-6
NEG_INF = -1e30

ADAM_LR = 0.001
ADAM_B1 = 0.9
ADAM_B2 = 0.999
ADAM_EPS = 1e-08
ADAM_WD = 0.01
ADAM_STEP = 10

N_DEV = 8
N_CHIP = 4
LANE = 128
VMEM_LIMIT = 52 * 1024 * 1024

bf16 = jnp.bfloat16
f32 = jnp.float32
MESH = pl.DeviceIdType.MESH
ANY = pl.BlockSpec(memory_space=pl.ANY)


def _params(*sem):
    return pltpu.CompilerParams(dimension_semantics=sem, vmem_limit_bytes=VMEM_LIMIT)


def _perm(j):
    return (j % 2) * N_CHIP + j // 2


_DN = {"nn": (((1,), (0,)), ((), ())), "nt": (((1,), (1,)), ((), ())), "tn": (((0,), (0,)), ((), ()))}


def _matmul(name, mode, grid, operands, specs, pairs, n_acc, acc_shape, extras, extra_specs, out_shapes, out_specs,
            epilogue):
    nk = grid[2]
    n_op, n_ex, n_out = len(operands), len(extras), len(out_shapes)

    def body(*refs):
        ops = refs[:n_op]
        ex = refs[n_op:n_op + n_ex]
        outs = refs[n_op + n_ex:n_op + n_ex + n_out]
        accs = refs[n_op + n_ex + n_out:]
        k = pl.program_id(2)

        @pl.when(k == 0)
        def _():
            for acc in accs:
                acc[...] = jnp.zeros_like(acc)

        for p, (ia, ib) in enumerate(pairs):
            acc = accs[p % n_acc]
            acc[...] += lax.dot_general(ops[ia][...], ops[ib][...], _DN[mode], preferred_element_type=f32)

        @pl.when(k == nk - 1)
        def _():
            epilogue([acc[...] for acc in accs], ex, outs)

    return pl.pallas_call(
        body, grid=grid, in_specs=list(specs) + list(extra_specs), out_specs=list(out_specs),
        out_shape=list(out_shapes), scratch_shapes=[pltpu.VMEM(acc_shape, f32)] * n_acc,
        compiler_params=_params("parallel", "parallel", "arbitrary"), name=name)(*operands, *extras)


def _store(dtype):
    def ep(accs, ex, outs):
        for a, o in zip(accs, outs):
            o[...] = a.astype(dtype)
    return ep


def _mm_cols(name, h, wg, out_dtype):
    t, kd = h.shape
    n8 = wg.shape[2]
    tm = min(t, 512)
    return _matmul(
        name, "nn", (t // tm, N_DEV, 1), [h, wg],
        [pl.BlockSpec((tm, kd), lambda i, j, k: (i, 0)), pl.BlockSpec((None, kd, n8), lambda i, j, k: (j, 0, 0))],
        [(0, 1)], 1, (tm, n8), [], [], [jax.ShapeDtypeStruct((t, N_DEV * n8), out_dtype)],
        [pl.BlockSpec((tm, n8), lambda i, j, k: (i, j))], _store(out_dtype))[0]


def _mm_rows_res(name, a, w, res):
    t, kd = a.shape
    n = w.shape[1]
    tm, tn = min(t, 512), min(n, 1024)

    def ep(accs, ex, outs):
        outs[0][...] = ex[0][...] + accs[0]

    return _matmul(
        name, "nn", (t // tm, n // tn, 1), [a, w],
        [pl.BlockSpec((tm, kd), lambda i, j, k: (i, 0)), pl.BlockSpec((kd, tn), lambda i, j, k: (0, j))],
        [(0, 1)], 1, (tm, tn), [res], [pl.BlockSpec((tm, tn), lambda i, j, k: (i, j))],
        [jax.ShapeDtypeStruct((t, n), f32)], [pl.BlockSpec((tm, tn), lambda i, j, k: (i, j))], ep)[0]


def _mm_nt(name, a, w, out_dtype):
    t, n = a.shape
    kd = w.shape[0]
    tm, tn = min(t, 512), min(kd, 1024)
    return _matmul(
        name, "nt", (t // tm, kd // tn, 1), [a, w],
        [pl.BlockSpec((tm, n), lambda i, j, k: (i, 0)), pl.BlockSpec((tn, n), lambda i, j, k: (j, 0))],
        [(0, 1)], 1, (tm, tn), [], [], [jax.ShapeDtypeStruct((t, kd), out_dtype)],
        [pl.BlockSpec((tm, tn), lambda i, j, k: (i, j))], _store(out_dtype))[0]


def _mm_nt_cols(name, da, wg):
    t = da.shape[0]
    kd, n8 = wg.shape[1], wg.shape[2]
    tm = min(t, 512)
    return _matmul(
        name, "nt", (t // tm, 1, N_DEV), [da, wg],
        [pl.BlockSpec((tm, n8), lambda i, j, k: (i, k)), pl.BlockSpec((None, kd, n8), lambda i, j, k: (k, 0, 0))],
        [(0, 1)], 1, (tm, kd), [], [], [jax.ShapeDtypeStruct((t, kd), f32)],
        [pl.BlockSpec((tm, kd), lambda i, j, k: (i, 0))], _store(f32))[0]


def _mm_tn_cols(name, h, da):
    t, kd = h.shape
    n8 = da.shape[1] // N_DEV
    tmk, tk = min(kd, 1024), min(t, 1024)
    return _matmul(
        name, "tn", (kd // tmk, N_DEV, t // tk), [h, da],
        [pl.BlockSpec((tk, tmk), lambda i, j, k: (k, i)), pl.BlockSpec((tk, n8), lambda i, j, k: (k, j))],
        [(0, 1)], 1, (tmk, n8), [], [], [jax.ShapeDtypeStruct((N_DEV, kd, n8), bf16)],
        [pl.BlockSpec((None, tmk, n8), lambda i, j, k: (_perm(j), i, 0))], _store(bf16))[0]


def _mm_tn_rows(name, a, dx):
    t, kf = a.shape
    r8 = kf // N_DEV
    n = dx.shape[1]
    return _matmul(
        name, "tn", (N_DEV, 1, 1), [a, dx],
        [pl.BlockSpec((t, r8), lambda i, j, k: (0, i)), pl.BlockSpec((t, n), lambda i, j, k: (0, 0))],
        [(0, 1)], 1, (r8, n), [], [], [jax.ShapeDtypeStruct((N_DEV, r8, n), bf16)],
        [pl.BlockSpec((None, r8, n), lambda i, j, k: (_perm(i), 0, 0))], _store(bf16))[0]


def _ffn_in(h2, wg, wu):
    t, kd = h2.shape
    f8 = wg.shape[2]
    tm = min(t, 512)

    def ep(accs, ex, outs):
        g, u = accs
        outs[0][...] = g.astype(bf16)
        outs[1][...] = u.astype(bf16)
        outs[2][...] = (g * jax.nn.sigmoid(g) * u).astype(bf16)

    wspec = pl.BlockSpec((None, kd, f8), lambda i, j, k: (j, 0, 0))
    ospec = pl.BlockSpec((None, tm, f8), lambda i, j, k: (j, i, 0))
    return _matmul(
        "ffn_in", "nn", (t // tm, N_DEV, 1), [h2, wg, wu],
        [pl.BlockSpec((tm, kd), lambda i, j, k: (i, 0)), wspec, wspec], [(0, 1), (0, 2)], 2, (tm, f8), [], [],
        [jax.ShapeDtypeStruct((N_DEV, t, f8), bf16)] * 3, [ospec] * 3, ep)


def _ffn_down(act, wd, res):
    _, t, f8 = act.shape
    n = wd.shape[2]
    tm = min(t, 512)

    def ep(accs, ex, outs):
        outs[0][...] = ex[0][...] + accs[0]

    return _matmul(
        "ffn_down", "nn", (t // tm, 1, N_DEV), [act, wd],
        [pl.BlockSpec((None, tm, f8), lambda i, j, k: (k, i, 0)), pl.BlockSpec((None, f8, n), lambda i, j, k: (k, 0, 0))],
        [(0, 1)], 1, (tm, n), [res], [pl.BlockSpec((tm, n), lambda i, j, k: (i, 0))],
        [jax.ShapeDtypeStruct((t, n), f32)], [pl.BlockSpec((tm, n), lambda i, j, k: (i, 0))], ep)[0]


def _ffn_bwd_act(dxb, wd, g, u):
    t, n = dxb.shape
    f8 = wd.shape[1]
    tm = min(t, 512)

    def ep(accs, ex, outs):
        dact = accs[0]
        gv = ex[0][...].astype(f32)
        uv = ex[1][...].astype(f32)
        sg = jax.nn.sigmoid(gv)
        silu = gv * sg
        outs[0][...] = (dact * uv * (sg * (1.0 + gv * (1.0 - sg)))).astype(bf16)
        outs[1][...] = (dact * silu).astype(bf16)
        outs[2][...] = (silu * uv).astype(bf16)

    bspec = pl.BlockSpec((None, tm, f8), lambda i, j, k: (j, i, 0))
    return _matmul(
        "ffn_bwd_act", "nt", (t // tm, N_DEV, 1), [dxb, wd],
        [pl.BlockSpec((tm, n), lambda i, j, k: (i, 0)), pl.BlockSpec((None, f8, n), lambda i, j, k: (j, 0, 0))],
        [(0, 1)], 1, (tm, f8), [g, u], [bspec, bspec], [jax.ShapeDtypeStruct((N_DEV, t, f8), bf16)] * 3, [bspec] * 3, ep)


def _ffn_dwd(act, dxb):
    _, t, f8 = act.shape
    n = dxb.shape[1]
    tk = min(t, 1024)
    return _matmul(
        "ffn_dwd", "tn", (N_DEV, 1, t // tk), [act, dxb],
        [pl.BlockSpec((None, tk, f8), lambda i, j, k: (i, k, 0)), pl.BlockSpec((tk, n), lambda i, j, k: (k, 0))],
        [(0, 1)], 1, (f8, n), [], [], [jax.ShapeDtypeStruct((N_DEV, f8, n), bf16)],
        [pl.BlockSpec((None, f8, n), lambda i, j, k: (_perm(i), 0, 0))], _store(bf16))[0]


def _ffn_dwgu(h2, dg, du):
    t, kd = h2.shape
    f8 = dg.shape[2]
    tmk, tk = min(kd, 1024), min(t, 1024)
    bspec = pl.BlockSpec((None, tk, f8), lambda i, j, k: (j, k, 0))
    ospec = pl.BlockSpec((None, tmk, f8), lambda i, j, k: (_perm(j), i, 0))
    return _matmul(
        "ffn_dwgu", "tn", (kd // tmk, N_DEV, t // tk), [h2, dg, du],
        [pl.BlockSpec((tk, tmk), lambda i, j, k: (k, i)), bspec, bspec], [(0, 1), (0, 2)], 2, (tmk, f8), [], [],
        [jax.ShapeDtypeStruct((N_DEV, kd, f8), bf16)] * 2, [ospec] * 2, _store(bf16))


def _ffn_dh(dg, du, wg, wu):
    _, t, f8 = dg.shape
    kd = wg.shape[1]
    tm = min(t, 512)
    aspec = pl.BlockSpec((None, tm, f8), lambda i, j, k: (k, i, 0))
    wspec = pl.BlockSpec((None, kd, f8), lambda i, j, k: (k, 0, 0))
    return _matmul(
        "ffn_dh", "nt", (t // tm, 1, N_DEV), [dg, du, wg, wu], [aspec, aspec, wspec, wspec], [(0, 2), (1, 3)], 1,
        (tm, kd), [], [], [jax.ShapeDtypeStruct((t, kd), f32)], [pl.BlockSpec((tm, kd), lambda i, j, k: (i, 0))],
        _store(f32))[0]


def _rms_fwd(x, g):
    t, d = x.shape
    tm = min(t, 256)

    def body(x_ref, g_ref, o_ref):
        xv = x_ref[...]
        r = lax.rsqrt(jnp.mean(xv * xv, axis=-1, keepdims=True) + EPS)
        o_ref[...] = (xv * r * g_ref[...]).astype(bf16)

    return pl.pallas_call(
        body, grid=(t // tm,), in_specs=[pl.BlockSpec((tm, d), lambda i: (i, 0)), pl.BlockSpec((1, d), lambda i: (0, 0))],
        out_specs=pl.BlockSpec((tm, d), lambda i: (i, 0)), out_shape=jax.ShapeDtypeStruct((t, d), bf16),
        compiler_params=_params("parallel"), name="rms_fwd")(x, g.reshape(1, d))


def _rms_bwd(x, g, dh, dres):
    t, d = x.shape
    tm = min(t, 256)

    def body(x_ref, g_ref, dh_ref, dres_ref, dx_ref, dxb_ref, dg_ref):
        xv = x_ref[...]
        dy = dh_ref[...].astype(f32)
        r = lax.rsqrt(jnp.mean(xv * xv, axis=-1, keepdims=True) + EPS)
        gy = dy * g_ref[...]
        dot = jnp.mean(xv * gy, axis=-1, keepdims=True)
        dx = dres_ref[...] + r * gy - xv * (r * r * r * dot)
        dx_ref[...] = dx
        dxb_ref[...] = dx.astype(bf16)

        @pl.when(pl.program_id(0) == 0)
        def _():
            dg_ref[...] = jnp.zeros_like(dg_ref)

        dg_ref[...] += jnp.sum(dy * xv * r, axis=0, keepdims=True)

    row = pl.BlockSpec((tm, d), lambda i: (i, 0))
    vec = pl.BlockSpec((1, d), lambda i: (0, 0))
    return pl.pallas_call(
        body, grid=(t // tm,), in_specs=[row, vec, row, row], out_specs=[row, row, vec],
        out_shape=[jax.ShapeDtypeStruct((t, d), f32), jax.ShapeDtypeStruct((t, d), bf16), jax.ShapeDtypeStruct((1, d), f32)],
        compiler_params=_params("arbitrary"), name="rms_bwd")(x, g.reshape(1, d), dh, dres)


def _loss_head(x, g, target):
    t, d = x.shape
    tm = min(t, 256)

    def body(x_ref, g_ref, t_ref, loss_ref, dx_ref, dxb_ref, dg_ref):
        xv = x_ref[...]
        r = lax.rsqrt(jnp.mean(xv * xv, axis=-1, keepdims=True) + EPS)
        xn = xv * r
        err = xn * g_ref[...] - t_ref[...]
        dy = err * (1.0 / d)
        gy = dy * g_ref[...]
        dot = jnp.mean(xv * gy, axis=-1, keepdims=True)
        dx = r * gy - xv * (r * r * r * dot)
        dx_ref[...] = dx
        dxb_ref[...] = dx.astype(bf16)

        @pl.when(pl.program_id(0) == 0)
        def _():
            dg_ref[...] = jnp.zeros_like(dg_ref)
            loss_ref[...] = jnp.zeros_like(loss_ref)

        dg_ref[...] += jnp.sum(dy * xn, axis=0, keepdims=True)
        loss_ref[...] += 0.5 * jnp.sum(jnp.sum(err * err, axis=-1, keepdims=True) * (1.0 / d), axis=0, keepdims=True)

    row = pl.BlockSpec((tm, d), lambda i: (i, 0))
    vec = pl.BlockSpec((1, d), lambda i: (0, 0))
    one = pl.BlockSpec((1, 1), lambda i: (0, 0))
    return pl.pallas_call(
        body, grid=(t // tm,), in_specs=[row, vec, row], out_specs=[one, row, row, vec],
        out_shape=[jax.ShapeDtypeStruct((1, 1), f32), jax.ShapeDtypeStruct((t, d), f32),
                   jax.ShapeDtypeStruct((t, d), bf16), jax.ShapeDtypeStruct((1, d), f32)],
        compiler_params=_params("arbitrary"), name="loss_head")(x, g.reshape(1, d), target)


def _attn_consts():
    qt, kw = 2 * CHUNK, (A_LEFT_CHUNKS + 2) * CHUNK
    r = np.arange(qt)[:, None]
    kc = np.arange(kw)[None, :]
    rel = np.clip(r + A_LEFT_CHUNKS * CHUNK - kc, -A_MAX_REL, A_MAX_REL) + A_MAX_REL
    dchunk = kc // CHUNK - r // CHUNK
    valid = (dchunk >= 0) & (dchunk <= A_LEFT_CHUNKS)
    m = np.arange(kw + qt)
    relidx = np.clip(A_LEFT_CHUNKS * CHUNK - (m - (qt - 1)), -A_MAX_REL, A_MAX_REL) + A_MAX_REL
    onehot = np.zeros((kw + qt, 2 * A_MAX_REL + 1), np.float32)
    onehot[m, relidx] = 1.0
    return qt, kw, rel, valid, onehot


def _bias_table(rel_bias):
    _, _, rel, valid, _ = _attn_consts()
    return jnp.where(jnp.asarray(valid)[None], rel_bias[:, jnp.asarray(rel)], NEG_INF).astype(f32)


def _bias_table_grad(dtab):
    qt, kw, _, _, onehot = _attn_consts()
    h = dtab.shape[0]
    w = kw + qt
    xp = jnp.pad(dtab[:, ::-1, :], ((0, 0), (0, 0), (0, w + 1 - kw)))
    skew = xp.reshape(h, qt * (w + 1))[:, :qt * w].reshape(h, qt, w)
    de = jnp.sum(skew, axis=1)
    return jnp.dot(de, jnp.asarray(onehot), precision=lax.Precision.HIGHEST)


def _attn_scores(q_ref, kpad, btab_ref, r0, qt, kw, pad):
    qv = q_ref[pl.ds(r0, qt), :]
    kwin = kpad[pl.ds(r0, kw), :]
    s = lax.dot_general(qv, kwin, _DN["nt"], preferred_element_type=f32) * (A_HEAD_DIM ** -0.5) + btab_ref[...]
    kcol = lax.broadcasted_iota(jnp.int32, (qt, kw), 1)
    s = jnp.where(r0 + kcol >= pad, s, NEG_INF)
    p = jnp.exp(s - jnp.max(s, axis=-1, keepdims=True))
    return qv, kwin, p / jnp.sum(p, axis=-1, keepdims=True)


def _attn_fwd(proj, btab, heads):
    t = proj.shape[0]
    qt, kw = btab.shape[1], btab.shape[2]
    pad = kw - qt

    def body(q_ref, k_ref, v_ref, btab_ref, o_ref, kpad, vpad):
        zeros = jnp.zeros((pad, A_HEAD_DIM), bf16)
        kpad[pl.ds(0, pad), :] = zeros
        vpad[pl.ds(0, pad), :] = zeros
        kpad[pl.ds(pad, t), :] = k_ref[...]
        vpad[pl.ds(pad, t), :] = v_ref[...]

        def tile(i, carry):
            r0 = pl.multiple_of(i * qt, qt)
            _, _, p = _attn_scores(q_ref, kpad, btab_ref, r0, qt, kw, pad)
            o = lax.dot_general(p.astype(bf16), vpad[pl.ds(r0, kw), :], _DN["nn"], preferred_element_type=f32)
            o_ref[pl.ds(r0, qt), :] = o.astype(bf16)
            return carry

        lax.fori_loop(0, t // qt, tile, 0)

    col = lambda off: pl.BlockSpec((t, A_HEAD_DIM), lambda h, off=off: (0, off + h))
    return pl.pallas_call(
        body, grid=(heads,),
        in_specs=[col(0), col(heads), col(2 * heads), pl.BlockSpec((None, qt, kw), lambda h: (h, 0, 0))],
        out_specs=col(0), out_shape=jax.ShapeDtypeStruct((t, heads * A_HEAD_DIM), bf16),
        scratch_shapes=[pltpu.VMEM((t + pad, A_HEAD_DIM), bf16)] * 2,
        compiler_params=_params("parallel"), name="attn_fwd")(proj, proj, proj, btab)


def _attn_bwd(proj, dmix, btab, heads):
    t = proj.shape[0]
    qt, kw = btab.shape[1], btab.shape[2]
    pad = kw - qt
    scale = A_HEAD_DIM ** -0.5

    def body(q_ref, k_ref, v_ref, do_ref, btab_ref, dq_ref, dk_ref, dv_ref, dtab_ref, kpad, vpad, dkacc, dvacc):
        zeros = jnp.zeros((pad, A_HEAD_DIM), bf16)
        kpad[pl.ds(0, pad), :] = zeros
        vpad[pl.ds(0, pad), :] = zeros
        kpad[pl.ds(pad, t), :] = k_ref[...]
        vpad[pl.ds(pad, t), :] = v_ref[...]
        dkacc[...] = jnp.zeros_like(dkacc)
        dvacc[...] = jnp.zeros_like(dvacc)
        dtab_ref[...] = jnp.zeros_like(dtab_ref)

        def tile(i, carry):
            r0 = pl.multiple_of(i * qt, qt)
            qv, kwin, p = _attn_scores(q_ref, kpad, btab_ref, r0, qt, kw, pad)
            dov = do_ref[pl.ds(r0, qt), :]
            dp = lax.dot_general(dov, vpad[pl.ds(r0, kw), :], _DN["nt"], preferred_element_type=f32)
            ds = p * (dp - jnp.sum(p * dp, axis=-1, keepdims=True))
            dtab_ref[...] += ds
            dsb = ds.astype(bf16)
            dq = lax.dot_general(dsb, kwin, _DN["nn"], preferred_element_type=f32) * scale
            dq_ref[pl.ds(r0, qt), :] = dq.astype(bf16)
            dkacc[pl.ds(r0, kw), :] += lax.dot_general(dsb, qv, _DN["tn"], preferred_element_type=f32) * scale
            dvacc[pl.ds(r0, kw), :] += lax.dot_general(p.astype(bf16), dov, _DN["tn"], preferred_element_type=f32)
            return carry

        lax.fori_loop(0, t // qt, tile, 0)
        dk_ref[...] = dkacc[pl.ds(pad, t), :].astype(bf16)
        dv_ref[...] = dvacc[pl.ds(pad, t), :].astype(bf16)

    col = lambda off: pl.BlockSpec((t, A_HEAD_DIM), lambda h, off=off: (0, off + h))
    tab = pl.BlockSpec((None, qt, kw), lambda h: (h, 0, 0))
    wide = jax.ShapeDtypeStruct((t, heads * A_HEAD_DIM), bf16)
    return pl.pallas_call(
        body, grid=(heads,), in_specs=[col(0), col(heads), col(2 * heads), col(0), tab],
        out_specs=[col(0), col(0), col(0), tab],
        out_shape=[wide, wide, wide, jax.ShapeDtypeStruct((heads, qt, kw), f32)],
        scratch_shapes=[pltpu.VMEM((t + pad, A_HEAD_DIM), bf16)] * 2 + [pltpu.VMEM((t + pad, A_HEAD_DIM), f32)] * 2,
        compiler_params=_params("parallel"), name="attn_bwd")(proj, proj, proj, dmix, btab)


def _shift_down(z, k):
    rows = lax.broadcasted_iota(jnp.int32, z.shape, 0)
    return jnp.where(rows >= k, pltpu.roll(z, k, 0), 0.0)


def _shift_up(z, k):
    t = z.shape[0]
    rows = lax.broadcasted_iota(jnp.int32, z.shape, 0)
    return jnp.where(rows < t - k, pltpu.roll(z, t - k, 0), 0.0)


def _conv_fwd(proj, conv_w, a_blocks, b_blocks):
    t = proj.shape[0]

    def body(b_ref, c_ref, h_ref, w_ref, o_ref):
        z = c_ref[...].astype(f32) * h_ref[...].astype(f32)
        w = w_ref[...]
        y = w[0:1, :] * _shift_down(z, 2) + w[1:2, :] * _shift_down(z, 1) + w[2:3, :] * z
        o_ref[...] = (b_ref[...].astype(f32) * y).astype(bf16)

    col = lambda off: pl.BlockSpec((t, LANE), lambda i, off=off: (0, off + i))
    return pl.pallas_call(
        body, grid=(b_blocks,),
        in_specs=[col(3 * a_blocks), col(3 * a_blocks + b_blocks), col(3 * a_blocks + 2 * b_blocks),
                  pl.BlockSpec((CONV_WIDTH, LANE), lambda i: (0, i))],
        out_specs=col(0), out_shape=jax.ShapeDtypeStruct((t, b_blocks * LANE), bf16),
        compiler_params=_params("parallel"), name="conv_fwd")(proj, proj, proj, conv_w)


def _conv_bwd(proj, dmix, conv_w, a_blocks, b_blocks):
    t = proj.shape[0]

    def body(b_ref, c_ref, h_ref, do_ref, w_ref, db_ref, dc_ref, dh_ref, dw_ref):
        bv, cv, hv = b_ref[...].astype(f32), c_ref[...].astype(f32), h_ref[...].astype(f32)
        w = w_ref[...]
        z = cv * hv
        z1, z2 = _shift_down(z, 1), _shift_down(z, 2)
        y = w[0:1, :] * z2 + w[1:2, :] * z1 + w[2:3, :] * z
        dov = do_ref[...].astype(f32)
        db_ref[...] = (dov * y).astype(bf16)
        dy = dov * bv
        dz = w[2:3, :] * dy + w[1:2, :] * _shift_up(dy, 1) + w[0:1, :] * _shift_up(dy, 2)
        dc_ref[...] = (dz * hv).astype(bf16)
        dh_ref[...] = (dz * cv).astype(bf16)
        dw_ref[0:1, :] = jnp.sum(dy * z2, axis=0, keepdims=True)
        dw_ref[1:2, :] = jnp.sum(dy * z1, axis=0, keepdims=True)
        dw_ref[2:3, :] = jnp.sum(dy * z, axis=0, keepdims=True)

    col = lambda off: pl.BlockSpec((t, LANE), lambda i, off=off: (0, off + i))
    wspec = pl.BlockSpec((CONV_WIDTH, LANE), lambda i: (0, i))
    wide = jax.ShapeDtypeStruct((t, b_blocks * LANE), bf16)
    return pl.pallas_call(
        body, grid=(b_blocks,),
        in_specs=[col(3 * a_blocks), col(3 * a_blocks + b_blocks), col(3 * a_blocks + 2 * b_blocks), col(a_blocks), wspec],
        out_specs=[col(0), col(0), col(0), wspec],
        out_shape=[wide, wide, wide, jax.ShapeDtypeStruct((CONV_WIDTH, b_blocks * LANE), f32)],
        compiler_params=_params("parallel"), name="conv_bwd")(proj, proj, proj, dmix, conv_w)


_RSQRT2 = 0.7071067811865476
_RSQRT2PI = 0.3989422804014327


def _gelu(x):
    return 0.5 * x * (1.0 + lax.erf(x * _RSQRT2))


def _gelu_grad(x):
    return 0.5 * (1.0 + lax.erf(x * _RSQRT2)) + x * jnp.exp(-0.5 * x * x) * _RSQRT2PI


def _sgu_common(a_ref, lg_ref, lb_ref, cw):
    av = a_ref[...]
    u = _gelu(av[:, :cw])
    v = _gelu(av[:, cw:])
    mu = jnp.mean(v, axis=-1, keepdims=True)
    xc = v - mu
    rstd = lax.rsqrt(jnp.mean(xc * xc, axis=-1, keepdims=True) + EPS)
    xhat = xc * rstd
    vln = xhat * lg_ref[...] + lb_ref[...]
    pos_t = lax.broadcasted_iota(jnp.int32, (C_BLOCK, C_BLOCK), 0) // CHUNK
    pos_s = lax.broadcasted_iota(jnp.int32, (C_BLOCK, C_BLOCK), 1) // CHUNK
    return av, u, xhat, rstd, vln, pos_s <= pos_t


def _sgu_fwd(a, ln_g, ln_b, w_s, bs_t):
    t, cw2 = a.shape
    cw = cw2 // 2
    groups = w_s.shape[0]
    cg = cw // groups

    def body(a_ref, lg_ref, lb_ref, ws_ref, bs_ref, m_ref):
        _, u, _, _, vln, mask = _sgu_common(a_ref, lg_ref, lb_ref, cw)
        vb = vln.astype(bf16)
        for g in range(groups):
            sl = slice(g * cg, (g + 1) * cg)
            wm = jnp.where(mask, ws_ref[g], 0.0).astype(bf16)
            s = lax.dot_general(wm, vb[:, sl], _DN["nn"], preferred_element_type=f32) + bs_ref[:, g:g + 1]
            m_ref[:, sl] = (u[:, sl] * s).astype(bf16)

    vec = pl.BlockSpec((1, cw), lambda n: (0, 0))
    return pl.pallas_call(
        body, grid=(t // C_BLOCK,),
        in_specs=[pl.BlockSpec((C_BLOCK, cw2), lambda n: (n, 0)), vec, vec,
                  pl.BlockSpec((groups, C_BLOCK, C_BLOCK), lambda n: (0, 0, 0)),
                  pl.BlockSpec((C_BLOCK, groups), lambda n: (0, 0))],
        out_specs=pl.BlockSpec((C_BLOCK, cw), lambda n: (n, 0)), out_shape=jax.ShapeDtypeStruct((t, cw), bf16),
        compiler_params=_params("parallel"), name="sgu_fwd")(a, ln_g.reshape(1, cw), ln_b.reshape(1, cw), w_s, bs_t)


def _sgu_bwd(a, dm, ln_g, ln_b, w_s, bs_t):
    t, cw2 = a.shape
    cw = cw2 // 2
    groups = w_s.shape[0]
    cg = cw // groups

    def body(a_ref, dm_ref, lg_ref, lb_ref, ws_ref, bs_ref, da_ref, dws_ref, dbs_ref, dlg_ref, dlb_ref, dvln):
        @pl.when(pl.program_id(0) == 0)
        def _():
            dws_ref[...] = jnp.zeros_like(dws_ref)
            dbs_ref[...] = jnp.zeros_like(dbs_ref)
            dlg_ref[...] = jnp.zeros_like(dlg_ref)
            dlb_ref[...] = jnp.zeros_like(dlb_ref)

        av, u, xhat, rstd, vln, mask = _sgu_common(a_ref, lg_ref, lb_ref, cw)
        vb = vln.astype(bf16)
        lane = lax.broadcasted_iota(jnp.int32, (C_BLOCK, groups), 1)
        dbs = jnp.zeros((C_BLOCK, groups), f32)
        for g in range(groups):
            sl = slice(g * cg, (g + 1) * cg)
            wm = jnp.where(mask, ws_ref[g], 0.0).astype(bf16)
            s = lax.dot_general(wm, vb[:, sl], _DN["nn"], preferred_element_type=f32) + bs_ref[:, g:g + 1]
            dmg = dm_ref[:, sl].astype(f32)
            da_ref[:, sl] = (dmg * s * _gelu_grad(av[:, sl])).astype(bf16)
            dsg = dmg * u[:, sl]
            dbs = dbs + jnp.where(lane == g, jnp.sum(dsg, axis=-1, keepdims=True), 0.0)
            dsb = dsg.astype(bf16)
            dws_ref[g] += jnp.where(mask, lax.dot_general(dsb, vb[:, sl], _DN["nt"], preferred_element_type=f32), 0.0)
            dvln[:, sl] = lax.dot_general(wm, dsb, _DN["tn"], preferred_element_type=f32)
        dbs_ref[...] += dbs
        dv = dvln[...]
        dlg_ref[...] += jnp.sum(dv * xhat, axis=0, keepdims=True)
        dlb_ref[...] += jnp.sum(dv, axis=0, keepdims=True)
        dxh = dv * lg_ref[...]
        dvv = rstd * (dxh - jnp.mean(dxh, axis=-1, keepdims=True) - xhat * jnp.mean(dxh * xhat, axis=-1, keepdims=True))
        da_ref[:, cw:] = (dvv * _gelu_grad(av[:, cw:])).astype(bf16)

    vec = pl.BlockSpec((1, cw), lambda n: (0, 0))
    wsp = pl.BlockSpec((groups, C_BLOCK, C_BLOCK), lambda n: (0, 0, 0))
    bsp = pl.BlockSpec((C_BLOCK, groups), lambda n: (0, 0))
    return pl.pallas_call(
        body, grid=(t // C_BLOCK,),
        in_specs=[pl.BlockSpec((C_BLOCK, cw2), lambda n: (n, 0)), pl.BlockSpec((C_BLOCK, cw), lambda n: (n, 0)), vec, vec, wsp, bsp],
        out_specs=[pl.BlockSpec((C_BLOCK, cw2), lambda n: (n, 0)), wsp, bsp, vec, vec],
        out_shape=[jax.ShapeDtypeStruct((t, cw2), bf16), jax.ShapeDtypeStruct(w_s.shape, f32),
                   jax.ShapeDtypeStruct(bs_t.shape, f32), jax.ShapeDtypeStruct((1, cw), f32), jax.ShapeDtypeStruct((1, cw), f32)],
        scratch_shapes=[pltpu.VMEM((C_BLOCK, cw), f32)],
        compiler_params=_params("arbitrary"), name="sgu_bwd")(a, dm, ln_g.reshape(1, cw), ln_b.reshape(1, cw), w_s, bs_t)


def _place():
    x, y, c = lax.axis_index("x"), lax.axis_index("y"), lax.axis_index("c")
    return x, y, c, [(1 - x, y), (x, 1 - y), (1 - x, 1 - y)]


def _gather_copies(n, ins, outs, send_sems, recv_sems, local_sems):
    x, y, c, chips = _place()
    sibling = (x, y, 1 - c)

    def slot(px, py, pc):
        return 4 * px + 2 * py + pc

    def copy(i, k, block, to, src=None):
        dst = outs[i].at[slot(*block)]
        return pltpu.make_async_remote_copy(src_ref=dst if src is None else src, dst_ref=dst, send_sem=send_sems.at[i, k],
                                            recv_sem=recv_sems.at[i, k], device_id=to, device_id_type=MESH)

    started = []
    for i in range(n):
        mine = pltpu.make_async_copy(ins[i], outs[i].at[slot(x, y, c)], local_sems.at[i])
        mine.start()
        started.append(mine)
    sends = []
    for i in range(n):
        sends.append(copy(i, 0, (x, y, c), sibling, src=ins[i]))
        sends += [copy(i, 1 + j, (x, y, c), (*chip, c), src=ins[i]) for j, chip in enumerate(chips)]
    for cp in sends:
        cp.start()
    for i in range(n):
        for j, chip in enumerate(chips):
            copy(i, 1 + j, (*chip, c), (x, y, c)).wait_recv()
            fwd = copy(i, 4 + j, (*chip, c), sibling)
            fwd.start()
            sends.append(fwd)
    for i in range(n):
        copy(i, 0, sibling, (x, y, c)).wait_recv()
        for j, chip in enumerate(chips):
            copy(i, 4 + j, (*chip, 1 - c), (x, y, c)).wait_recv()
    for cp in sends:
        cp.wait_send()
    for mine in started:
        mine.wait()


def _all_gather(name, shards):
    n = len(shards)

    def body(*refs):
        _gather_copies(n, refs[:n], refs[n:2 * n], *refs[2 * n:])

    return pl.pallas_call(
        body, in_specs=[ANY] * n, out_specs=[ANY] * n,
        out_shape=[jax.ShapeDtypeStruct((N_DEV,) + s.shape, s.dtype) for s in shards],
        scratch_shapes=[pltpu.SemaphoreType.DMA((n, 7)), pltpu.SemaphoreType.DMA((n, 7)), pltpu.SemaphoreType.DMA((n,))],
        name=name)(*shards)


def _rs_sibling(name, grads):
    n = len(grads)

    def body(*refs):
        g, own, got = refs[:n], refs[n:2 * n], refs[2 * n:3 * n]
        send_sems, recv_sems, local_sems = refs[3 * n:]
        x, y, c, _ = _place()
        cps = []
        for i in range(n):
            loc = pltpu.make_async_copy(g[i].at[pl.ds(N_CHIP * c, N_CHIP)], own[i], local_sems.at[i])
            loc.start()
            rem = pltpu.make_async_remote_copy(src_ref=g[i].at[pl.ds(N_CHIP * (1 - c), N_CHIP)], dst_ref=got[i],
                                               send_sem=send_sems.at[i], recv_sem=recv_sems.at[i],
                                               device_id=(x, y, 1 - c), device_id_type=MESH)
            rem.start()
            cps.append((loc, rem))
        for loc, rem in cps:
            rem.wait()
            loc.wait()

    half = [jax.ShapeDtypeStruct((N_CHIP,) + g.shape[1:], g.dtype) for g in grads]
    outs = pl.pallas_call(
        body, in_specs=[ANY] * n, out_specs=[ANY] * (2 * n), out_shape=half + half,
        scratch_shapes=[pltpu.SemaphoreType.DMA((n,)), pltpu.SemaphoreType.DMA((n,)), pltpu.SemaphoreType.DMA((n,))],
        name=name)(*grads)
    return outs[:n], outs[n:]


def _pair_sum(own, got):
    shape = own.shape
    cols = shape[-1]
    rows = own.size // cols
    tr = 512 if rows % 512 == 0 else 256 if rows % 256 == 0 else 8

    def body(a_ref, b_ref, o_ref):
        o_ref[...] = (a_ref[...].astype(f32) + b_ref[...].astype(f32)).astype(bf16)

    spec = pl.BlockSpec((tr, cols), lambda i: (i, 0))
    out = pl.pallas_call(body, grid=(rows // tr,), in_specs=[spec, spec], out_specs=spec,
                         out_shape=jax.ShapeDtypeStruct((rows, cols), bf16), compiler_params=_params("parallel"),
                         name="pair_sum")(own.reshape(rows, cols), got.reshape(rows, cols))
    return out.reshape(shape)


def _rs_chips(name, parts):
    n = len(parts)

    def body(*refs):
        p, r = refs[:n], refs[n:2 * n]
        send_sems, recv_sems, local_sems = refs[2 * n:]
        x, y, c, chips = _place()
        q = 2 * x + y
        cps = []
        for i in range(n):
            loc = pltpu.make_async_copy(p[i].at[q], r[i].at[q], local_sems.at[i])
            loc.start()
            cps.append(loc)
        rems = []
        for i in range(n):
            for k, (px, py) in enumerate(chips):
                rem = pltpu.make_async_remote_copy(src_ref=p[i].at[2 * px + py], dst_ref=r[i].at[q],
                                                   send_sem=send_sems.at[i, k], recv_sem=recv_sems.at[i, k],
                                                   device_id=(px, py, c), device_id_type=MESH)
                rem.start()
                rems.append(rem)
        for rem in rems:
            rem.wait()
        for loc in cps:
            loc.wait()

    return pl.pallas_call(
        body, in_specs=[ANY] * n, out_specs=[ANY] * n, out_shape=[jax.ShapeDtypeStruct(p.shape, p.dtype) for p in parts],
        scratch_shapes=[pltpu.SemaphoreType.DMA((n, 3)), pltpu.SemaphoreType.DMA((n, 3)), pltpu.SemaphoreType.DMA((n,))],
        name=name)(*parts)


def _reduce_scatter(tag, grads):
    own, got = _rs_sibling("rs_sibling_" + tag, grads)
    parts = [_pair_sum(a, b) for a, b in zip(own, got)]
    return _rs_chips("rs_chips_" + tag, parts)


def _gather_small(name, packed, reduce):
    rows = packed.shape[0]

    def body(x_ref, o_ref, buf, send_sems, recv_sems, local_sems):
        _gather_copies(1, [x_ref], [buf], send_sems, recv_sems, local_sems)
        if reduce:
            acc = buf[0]
            for j in range(1, N_DEV):
                acc = acc + buf[j]
            o_ref[...] = acc
        else:
            o_ref[...] = buf[...]

    vm = pl.BlockSpec(memory_space=pltpu.VMEM)
    return pl.pallas_call(
        body, in_specs=[vm], out_specs=vm,
        out_shape=jax.ShapeDtypeStruct((rows, LANE) if reduce else (N_DEV, rows, LANE), f32),
        scratch_shapes=[pltpu.VMEM((N_DEV, rows, LANE), f32), pltpu.SemaphoreType.DMA((1, 7)), pltpu.SemaphoreType.DMA((1, 7)),
                        pltpu.SemaphoreType.DMA((1,))],
        compiler_params=pltpu.CompilerParams(vmem_limit_bytes=VMEM_LIMIT), name=name)(packed)


def _pack(arrs):
    flat = jnp.concatenate([a.reshape(-1).astype(f32) for a in arrs])
    rows = -(-flat.shape[0] // (8 * LANE)) * 8
    return jnp.pad(flat, (0, rows * LANE - flat.shape[0])).reshape(rows, LANE)


def _unpack(buf, shapes):
    flat = buf.reshape(-1)
    out, off = [], 0
    for s in shapes:
        n = int(np.prod(s))
        out.append(flat[off:off + n].reshape(s))
        off += n
    return out


def _adam_math(w, g, m, v):
    m2 = ADAM_B1 * m + (1.0 - ADAM_B1) * g
    v2 = ADAM_B2 * v + (1.0 - ADAM_B2) * (g * g)
    m_hat = m2 / (1.0 - ADAM_B1 ** ADAM_STEP)
    v_hat = v2 / (1.0 - ADAM_B2 ** ADAM_STEP)
    delta = -ADAM_LR * (m_hat / (jnp.sqrt(v_hat) + ADAM_EPS) + ADAM_WD * w)
    return delta, m2, v2


def _adam_big(w, m, v, parts):
    rows, cols = w.shape
    tr = 256 if rows % 256 == 0 else 64 if rows % 64 == 0 else 8

    def body(w_ref, m_ref, v_ref, p_ref, g_ref, d_ref, m2_ref, v2_ref):
        g = p_ref[0].astype(f32)
        for q in range(1, N_CHIP):
            g = g + p_ref[q].astype(f32)
        delta, m2, v2 = _adam_math(w_ref[...], g, m_ref[...], v_ref[...])
        g_ref[...] = g
        d_ref[...] = delta
        m2_ref[...] = m2
        v2_ref[...] = v2

    spec = pl.BlockSpec((tr, cols), lambda i: (i, 0))
    out = jax.ShapeDtypeStruct((rows, cols), f32)
    return pl.pallas_call(
        body, grid=(rows // tr,), in_specs=[spec, spec, spec, pl.BlockSpec((N_CHIP, tr, cols), lambda i: (0, i, 0))],
        out_specs=[spec] * 4, out_shape=[out] * 4, compiler_params=_params("parallel"), name="adam_big")(w, m, v, parts)


def _adam_small(w, g, m, v):
    rows = w.shape[0]

    def body(w_ref, g_ref, m_ref, v_ref, d_ref, m2_ref, v2_ref):
        delta, m2, v2 = _adam_math(w_ref[...], g_ref[...], m_ref[...], v_ref[...])
        d_ref[...] = delta
        m2_ref[...] = m2
        v2_ref[...] = v2

    out = jax.ShapeDtypeStruct((rows, LANE), f32)
    return pl.pallas_call(body, out_shape=[out] * 3, name="adam_small")(w, g, m, v)


def kernel(x, mix_norm, ab_w_in, ab_rel_bias, ab_conv_w, ab_w_out, c_w_in, c_ln_g, c_ln_b, c_w_s, c_b_s, c_w_out, ffn_norm, ffn_w_gate, ffn_w_up, ffn_w_down, final_norm, loss_target, m_mix_norm, m_ab_w_in, m_ab_rel_bias, m_ab_conv_w, m_ab_w_out, m_c_w_in, m_c_ln_g, m_c_ln_b, m_c_w_s, m_c_b_s, m_c_w_out, m_ffn_norm, m_ffn_w_gate, m_ffn_w_up, m_ffn_w_down, m_final_norm, v_mix_norm, v_ab_w_in, v_ab_rel_bias, v_ab_conv_w, v_ab_w_out, v_c_w_in, v_c_ln_g, v_c_ln_b, v_c_w_s, v_c_b_s, v_c_w_out, v_ffn_norm, v_ffn_w_gate, v_ffn_w_up, v_ffn_w_down, v_final_norm):
    d = D_MODEL
    a_width = d // 2
    heads = a_width // A_HEAD_DIM
    a_blocks = a_width // LANE
    b_blocks = (d - a_width) // LANE
    n_even, n_odd = (DEPTH + 1) // 2, DEPTH // 2
    me = 4 * lax.axis_index("x") + 2 * lax.axis_index("y") + lax.axis_index("c")

    weights = dict(mix_norm=mix_norm, ab_w_in=ab_w_in, ab_rel_bias=ab_rel_bias, ab_conv_w=ab_conv_w, ab_w_out=ab_w_out,
                   c_w_in=c_w_in, c_ln_g=c_ln_g, c_ln_b=c_ln_b, c_w_s=c_w_s, c_b_s=c_b_s, c_w_out=c_w_out,
                   ffn_norm=ffn_norm, ffn_w_gate=ffn_w_gate, ffn_w_up=ffn_w_up, ffn_w_down=ffn_w_down, final_norm=final_norm)
    mom_m = dict(mix_norm=m_mix_norm, ab_w_in=m_ab_w_in, ab_rel_bias=m_ab_rel_bias, ab_conv_w=m_ab_conv_w, ab_w_out=m_ab_w_out,
                 c_w_in=m_c_w_in, c_ln_g=m_c_ln_g, c_ln_b=m_c_ln_b, c_w_s=m_c_w_s, c_b_s=m_c_b_s, c_w_out=m_c_w_out,
                 ffn_norm=m_ffn_norm, ffn_w_gate=m_ffn_w_gate, ffn_w_up=m_ffn_w_up, ffn_w_down=m_ffn_w_down, final_norm=m_final_norm)
    mom_v = dict(mix_norm=v_mix_norm, ab_w_in=v_ab_w_in, ab_rel_bias=v_ab_rel_bias, ab_conv_w=v_ab_conv_w, ab_w_out=v_ab_w_out,
                 c_w_in=v_c_w_in, c_ln_g=v_c_ln_g, c_ln_b=v_c_ln_b, c_w_s=v_c_w_s, c_b_s=v_c_b_s, c_w_out=v_c_w_out,
                 ffn_norm=v_ffn_norm, ffn_w_gate=v_ffn_w_gate, ffn_w_up=v_ffn_w_up, ffn_w_down=v_ffn_w_down, final_norm=v_final_norm)
    order = list(weights)

    sharded_small = [ab_conv_w, c_ln_g, c_ln_b]
    gathered = _gather_small("gather_small", _pack(sharded_small), reduce=False)
    conv_parts, lng_parts, lnb_parts = [], [], []
    for j in range(N_DEV):
        cw_j, lg_j, lb_j = _unpack(gathered[j], [a.shape for a in sharded_small])
        conv_parts.append(cw_j)
        lng_parts.append(lg_j)
        lnb_parts.append(lb_j)
    conv_full = jnp.concatenate(conv_parts, axis=-1)
    lng_full = jnp.concatenate(lng_parts, axis=-1)
    lnb_full = jnp.concatenate(lnb_parts, axis=-1)

    def gather_layer(layer):
        i = layer // 2
        if layer % 2 == 0:
            mix = _all_gather("gather_ab", [ab_w_in[i].astype(bf16), ab_w_out[i].astype(bf16)])
        else:
            mix = _all_gather("gather_c", [c_w_in[i].astype(bf16), c_w_out[i].astype(bf16)])
        ffn = _all_gather("gather_ffn", [ffn_w_gate[layer].astype(bf16), ffn_w_up[layer].astype(bf16),
                                         ffn_w_down[layer].astype(bf16)])
        return mix, ffn

    xs = x[0]
    tgt = loss_target[0]
    saved = []
    for layer in range(DEPTH):
        i = layer // 2
        (w_in_g, w_out_g), (wg_g, wu_g, wd_g) = gather_layer(layer)
        w_out_full = w_out_g.reshape(-1, w_out_g.shape[-1])
        h = _rms_fwd(xs, mix_norm[layer])
        if layer % 2 == 0:
            proj = _mm_cols("ab_proj", h, w_in_g, bf16)
            btab = _bias_table(ab_rel_bias[i])
            attn = _attn_fwd(proj, btab, heads)
            conv = _conv_fwd(proj, conv_full[i], a_blocks, b_blocks)
            mixed = jnp.concatenate([attn, conv], axis=-1)
            ctx = (proj, btab)
        else:
            proj = _mm_cols("c_proj", h, w_in_g, f32)
            bs_t = jnp.transpose(c_b_s[i])
            mixed = _sgu_fwd(proj, lng_full[i], lnb_full[i], c_w_s[i], bs_t)
            ctx = (proj, bs_t)
        x1 = _mm_rows_res("mix_out", mixed, w_out_full, xs)
        h2 = _rms_fwd(x1, ffn_norm[layer])
        g_act, u_act, act = _ffn_in(h2, wg_g, wu_g)
        x2 = _ffn_down(act, wd_g, x1)
        saved.append((xs, h, ctx, mixed, x1, h2, g_act, u_act, w_in_g, w_out_full, wg_g, wu_g, wd_g))
        xs = x2

    loss_part, dx, dxb, d_final = _loss_head(xs, final_norm, tgt)
    loss = lax.psum(loss_part[0, 0], ("x", "y", "c"))

    big_grads = {}
    small = {k: [None] * weights[k].shape[0] for k in ("mix_norm", "ffn_norm", "ab_rel_bias", "ab_conv_w", "c_ln_g", "c_ln_b",
                                                       "c_w_s", "c_b_s")}
    for layer in reversed(range(DEPTH)):
        i = layer // 2
        xs, h, ctx, mixed, x1, h2, g_act, u_act, w_in_g, w_out_full, wg_g, wu_g, wd_g = saved[layer]
        dg, du, act = _ffn_bwd_act(dxb, wd_g, g_act, u_act)
        dwd = _ffn_dwd(act, dxb)
        dwg, dwu = _ffn_dwgu(h2, dg, du)
        dh2 = _ffn_dh(dg, du, wg_g, wu_g)
        dx, dxb, dgn = _rms_bwd(x1, ffn_norm[layer], dh2, dx)
        small["ffn_norm"][layer] = dgn[0]
        rg, ru, rd = _reduce_scatter("ffn", [dwg, dwu, dwd])
        big_grads[("ffn_w_gate", layer)] = rg
        big_grads[("ffn_w_up", layer)] = ru
        big_grads[("ffn_w_down", layer)] = rd
        dmixed = _mm_nt("mix_out_bwd", dxb, w_out_full, bf16)
        dwout = _mm_tn_rows("mix_out_dw", mixed, dxb)
        if layer % 2 == 0:
            proj, btab = ctx
            dq, dk, dv, dtab = _attn_bwd(proj, dmixed, btab, heads)
            db, dc, dhv, dcw = _conv_bwd(proj, dmixed, conv_full[i], a_blocks, b_blocks)
            dproj = jnp.concatenate([dq, dk, dv, db, dc, dhv], axis=-1)
            small["ab_rel_bias"][i] = _bias_table_grad(dtab)
            small["ab_conv_w"][i] = dcw
            names = ("ab_w_in", "ab_w_out")
            tag = "ab"
        else:
            proj, bs_t = ctx
            dproj, dws, dbs_t, dlg, dlb = _sgu_bwd(proj, dmixed, lng_full[i], lnb_full[i], c_w_s[i], bs_t)
            small["c_w_s"][i] = dws
            small["c_b_s"][i] = jnp.transpose(dbs_t)
            small["c_ln_g"][i] = dlg[0]
            small["c_ln_b"][i] = dlb[0]
            names = ("c_w_in", "c_w_out")
            tag = "c"
        dwin = _mm_tn_cols(tag + "_proj_dw", h, dproj)
        dh = _mm_nt_cols(tag + "_proj_bwd", dproj, w_in_g)
        dx, dxb, dgm = _rms_bwd(xs, mix_norm[layer], dh, dx)
        small["mix_norm"][layer] = dgm[0]
        rin, rout = _reduce_scatter(tag, [dwin, dwout])
        big_grads[(names[0], i)] = rin
        big_grads[(names[1], i)] = rout
    grad_x = dx[None]

    small_names = ["mix_norm", "ffn_norm", "ab_rel_bias", "ab_conv_w", "c_ln_g", "c_ln_b", "c_w_s", "c_b_s"]
    small_full = [jnp.stack(small[k]) for k in small_names] + [d_final[0]]
    summed = _unpack(_gather_small("reduce_small", _pack(small_full), reduce=True), [a.shape for a in small_full])
    small_grads = dict(zip(small_names + ["final_norm"], summed))
    for k in ("ab_conv_w", "c_ln_g", "c_ln_b"):
        width = weights[k].shape[-1]
        small_grads[k] = lax.dynamic_slice_in_dim(small_grads[k], me * width, width, axis=-1)
    small_order = [k for k in order if k in small_grads]
    shapes = [weights[k].shape for k in small_order]
    d_s, m_s, v_s = _adam_small(_pack([weights[k] for k in small_order]), _pack([small_grads[k] for k in small_order]),
                                _pack([mom_m[k] for k in small_order]), _pack([mom_v[k] for k in small_order]))
    grads, deltas, new_m, new_v = dict(small_grads), {}, {}, {}
    for k, dd, mm, vv in zip(small_order, _unpack(d_s, shapes), _unpack(m_s, shapes), _unpack(v_s, shapes)):
        deltas[k], new_m[k], new_v[k] = dd, mm, vv

    for k in order:
        if k in small_grads:
            continue
        per_layer = []
        for li in range(weights[k].shape[0]):
            shard = weights[k][li]
            parts = big_grads[(k, li)].reshape((N_CHIP,) + shard.shape)
            per_layer.append(_adam_big(shard, mom_m[k][li], mom_v[k][li], parts))
        grads[k], deltas[k], new_m[k], new_v[k] = (jnp.stack([p[n] for p in per_layer]) for n in range(4))

    return (loss, grad_x, *[grads[k] for k in order], *[deltas[k] for k in order], *[new_m[k] for k in order],
            *[new_v[k] for k in order])
```

```python
import numpy as np
import jax
import jax.numpy as jnp
from jax import lax
from jax.experimental import pallas as pl
from jax.experimental.pallas import tpu as pltpu

D_MODEL = 2048
SEQ = 2048
DEPTH = 4
CHUNK = 64
A_HEAD_DIM = 128
A_LEFT_CHUNKS = 8
A_MAX_REL = 256
CONV_WIDTH = 3
C_BLOCK = 128
C_GROUPS = 8
EPS = 1e-6
NEG_INF = -1e30

ADAM_LR = 0.001
ADAM_B1 = 0.9
ADAM_B2 = 0.999
ADAM_EPS = 1e-08
ADAM_WD = 0.01
ADAM_STEP = 10

N_DEV = 8
N_CHIP = 4
RS_SPLIT = 4
LANE = 128
VMEM_LIMIT = 52 * 1024 * 1024

bf16 = jnp.bfloat16
f32 = jnp.float32
MESH = pl.DeviceIdType.MESH
ANY = pl.BlockSpec(memory_space=pl.ANY)


def _params(*sem):
    return pltpu.CompilerParams(dimension_semantics=sem, vmem_limit_bytes=VMEM_LIMIT)


def _perm(j):
    return (j % 2) * N_CHIP + j // 2


_DN = {"nn": (((1,), (0,)), ((), ())), "nt": (((1,), (1,)), ((), ())), "tn": (((0,), (0,)), ((), ()))}


def _matmul(name, mode, grid, operands, specs, pairs, n_acc, acc_shape, extras, extra_specs, out_shapes, out_specs,
            epilogue):
    nk = grid[2]
    n_op, n_ex, n_out = len(operands), len(extras), len(out_shapes)

    def body(*refs):
        ops = refs[:n_op]
        ex = refs[n_op:n_op + n_ex]
        outs = refs[n_op + n_ex:n_op + n_ex + n_out]
        accs = refs[n_op + n_ex + n_out:]
        k = pl.program_id(2)

        @pl.when(k == 0)
        def _():
            for acc in accs:
                acc[...] = jnp.zeros_like(acc)

        for p, (ia, ib) in enumerate(pairs):
            acc = accs[p % n_acc]
            acc[...] += lax.dot_general(ops[ia][...], ops[ib][...], _DN[mode], preferred_element_type=f32)

        @pl.when(k == nk - 1)
        def _():
            epilogue([acc[...] for acc in accs], ex, outs)

    return pl.pallas_call(
        body, grid=grid, in_specs=list(specs) + list(extra_specs), out_specs=list(out_specs),
        out_shape=list(out_shapes), scratch_shapes=[pltpu.VMEM(acc_shape, f32)] * n_acc,
        compiler_params=_params("parallel", "parallel", "arbitrary"), name=name)(*operands, *extras)


def _store(dtype):
    def ep(accs, ex, outs):
        for a, o in zip(accs, outs):
            o[...] = a.astype(dtype)
    return ep


def _mm_cols(name, h, wg, out_dtype):
    t, kd = h.shape
    n8 = wg.shape[2]
    tm = min(t, 512)
    return _matmul(
        name, "nn", (t // tm, N_DEV, 1), [h, wg],
        [pl.BlockSpec((tm, kd), lambda i, j, k: (i, 0)), pl.BlockSpec((None, kd, n8), lambda i, j, k: (j, 0, 0))],
        [(0, 1)], 1, (tm, n8), [], [], [jax.ShapeDtypeStruct((t, N_DEV * n8), out_dtype)],
        [pl.BlockSpec((tm, n8), lambda i, j, k: (i, j))], _store(out_dtype))[0]


def _mm_rows_res(name, a, w, res):
    t, kd = a.shape
    n = w.shape[1]
    tm, tn = min(t, 512), min(n, 1024)

    def ep(accs, ex, outs):
        outs[0][...] = ex[0][...] + accs[0]

    return _matmul(
        name, "nn", (t // tm, n // tn, 1), [a, w],
        [pl.BlockSpec((tm, kd), lambda i, j, k: (i, 0)), pl.BlockSpec((kd, tn), lambda i, j, k: (0, j))],
        [(0, 1)], 1, (tm, tn), [res], [pl.BlockSpec((tm, tn), lambda i, j, k: (i, j))],
        [jax.ShapeDtypeStruct((t, n), f32)], [pl.BlockSpec((tm, tn), lambda i, j, k: (i, j))], ep)[0]


def _mm_nt(name, a, w, out_dtype):
    t, n = a.shape
    kd = w.shape[0]
    tm, tn = min(t, 512), min(kd, 1024)
    return _matmul(
        name, "nt", (t // tm, kd // tn, 1), [a, w],
        [pl.BlockSpec((tm, n), lambda i, j, k: (i, 0)), pl.BlockSpec((tn, n), lambda i, j, k: (j, 0))],
        [(0, 1)], 1, (tm, tn), [], [], [jax.ShapeDtypeStruct((t, kd), out_dtype)],
        [pl.BlockSpec((tm, tn), lambda i, j, k: (i, j))], _store(out_dtype))[0]


def _mm_nt_cols(name, da, wg):
    t = da.shape[0]
    kd, n8 = wg.shape[1], wg.shape[2]
    tm = min(t, 512)
    return _matmul(
        name, "nt", (t // tm, 1, N_DEV), [da, wg],
        [pl.BlockSpec((tm, n8), lambda i, j, k: (i, k)), pl.BlockSpec((None, kd, n8), lambda i, j, k: (k, 0, 0))],
        [(0, 1)], 1, (tm, kd), [], [], [jax.ShapeDtypeStruct((t, kd), f32)],
        [pl.BlockSpec((tm, kd), lambda i, j, k: (i, 0))], _store(f32))[0]


def _mm_tn_cols(name, h, da):
    t, kd = h.shape
    n8 = da.shape[1] // N_DEV
    tmk, tk = min(kd, 1024), min(t, 1024)
    return _matmul(
        name, "tn", (kd // tmk, N_DEV, t // tk), [h, da],
        [pl.BlockSpec((tk, tmk), lambda i, j, k: (k, i)), pl.BlockSpec((tk, n8), lambda i, j, k: (k, j))],
        [(0, 1)], 1, (tmk, n8), [], [], [jax.ShapeDtypeStruct((N_DEV, kd, n8), bf16)],
        [pl.BlockSpec((None, tmk, n8), lambda i, j, k: (_perm(j), i, 0))], _store(bf16))[0]


def _mm_tn_rows(name, a, dx):
    t, kf = a.shape
    r8 = kf // N_DEV
    n = dx.shape[1]
    return _matmul(
        name, "tn", (N_DEV, 1, 1), [a, dx],
        [pl.BlockSpec((t, r8), lambda i, j, k: (0, i)), pl.BlockSpec((t, n), lambda i, j, k: (0, 0))],
        [(0, 1)], 1, (r8, n), [], [], [jax.ShapeDtypeStruct((N_DEV, r8, n), bf16)],
        [pl.BlockSpec((None, r8, n), lambda i, j, k: (_perm(i), 0, 0))], _store(bf16))[0]


def _ffn_in(h2, wg, wu):
    t, kd = h2.shape
    f8 = wg.shape[2]
    tm = min(t, 512)

    def ep(accs, ex, outs):
        g, u = accs
        outs[0][...] = g.astype(bf16)
        outs[1][...] = u.astype(bf16)
        outs[2][...] = (g * jax.nn.sigmoid(g) * u).astype(bf16)

    wspec = pl.BlockSpec((None, kd, f8), lambda i, j, k: (j, 0, 0))
    ospec = pl.BlockSpec((None, tm, f8), lambda i, j, k: (j, i, 0))
    return _matmul(
        "ffn_in", "nn", (t // tm, N_DEV, 1), [h2, wg, wu],
        [pl.BlockSpec((tm, kd), lambda i, j, k: (i, 0)), wspec, wspec], [(0, 1), (0, 2)], 2, (tm, f8), [], [],
        [jax.ShapeDtypeStruct((N_DEV, t, f8), bf16)] * 3, [ospec] * 3, ep)


def _ffn_down(act, wd, res):
    _, t, f8 = act.shape
    n = wd.shape[2]
    tm = min(t, 512)

    def ep(accs, ex, outs):
        outs[0][...] = ex[0][...] + accs[0]

    return _matmul(
        "ffn_down", "nn", (t // tm, 1, N_DEV), [act, wd],
        [pl.BlockSpec((None, tm, f8), lambda i, j, k: (k, i, 0)), pl.BlockSpec((None, f8, n), lambda i, j, k: (k, 0, 0))],
        [(0, 1)], 1, (tm, n), [res], [pl.BlockSpec((tm, n), lambda i, j, k: (i, 0))],
        [jax.ShapeDtypeStruct((t, n), f32)], [pl.BlockSpec((tm, n), lambda i, j, k: (i, 0))], ep)[0]


def _ffn_bwd_act(dxb, wd, g, u):
    t, n = dxb.shape
    f8 = wd.shape[1]
    tm = min(t, 512)

    def ep(accs, ex, outs):
        dact = accs[0]
        gv = ex[0][...].astype(f32)
        uv = ex[1][...].astype(f32)
        sg = jax.nn.sigmoid(gv)
        silu = gv * sg
        outs[0][...] = (dact * uv * (sg * (1.0 + gv * (1.0 - sg)))).astype(bf16)
        outs[1][...] = (dact * silu).astype(bf16)
        outs[2][...] = (silu * uv).astype(bf16)

    bspec = pl.BlockSpec((None, tm, f8), lambda i, j, k: (j, i, 0))
    return _matmul(
        "ffn_bwd_act", "nt", (t // tm, N_DEV, 1), [dxb, wd],
        [pl.BlockSpec((tm, n), lambda i, j, k: (i, 0)), pl.BlockSpec((None, f8, n), lambda i, j, k: (j, 0, 0))],
        [(0, 1)], 1, (tm, f8), [g, u], [bspec, bspec], [jax.ShapeDtypeStruct((N_DEV, t, f8), bf16)] * 3, [bspec] * 3, ep)


def _ffn_dwd(act, dxb):
    _, t, f8 = act.shape
    n = dxb.shape[1]
    tk = min(t, 1024)
    return _matmul(
        "ffn_dwd", "tn", (N_DEV, 1, t // tk), [act, dxb],
        [pl.BlockSpec((None, tk, f8), lambda i, j, k: (i, k, 0)), pl.BlockSpec((tk, n), lambda i, j, k: (k, 0))],
        [(0, 1)], 1, (f8, n), [], [], [jax.ShapeDtypeStruct((N_DEV, f8, n), bf16)],
        [pl.BlockSpec((None, f8, n), lambda i, j, k: (_perm(i), 0, 0))], _store(bf16))[0]


def _ffn_dwgu(h2, dg, du):
    t, kd = h2.shape
    f8 = dg.shape[2]
    tmk, tk = min(kd, 1024), min(t, 1024)
    bspec = pl.BlockSpec((None, tk, f8), lambda i, j, k: (j, k, 0))
    ospec = pl.BlockSpec((None, tmk, f8), lambda i, j, k: (_perm(j), i, 0))
    return _matmul(
        "ffn_dwgu", "tn", (kd // tmk, N_DEV, t // tk), [h2, dg, du],
        [pl.BlockSpec((tk, tmk), lambda i, j, k: (k, i)), bspec, bspec], [(0, 1), (0, 2)], 2, (tmk, f8), [], [],
        [jax.ShapeDtypeStruct((N_DEV, kd, f8), bf16)] * 2, [ospec] * 2, _store(bf16))


def _ffn_dh(dg, du, wg, wu):
    _, t, f8 = dg.shape
    kd = wg.shape[1]
    tm = min(t, 512)
    aspec = pl.BlockSpec((None, tm, f8), lambda i, j, k: (k, i, 0))
    wspec = pl.BlockSpec((None, kd, f8), lambda i, j, k: (k, 0, 0))
    return _matmul(
        "ffn_dh", "nt", (t // tm, 1, N_DEV), [dg, du, wg, wu], [aspec, aspec, wspec, wspec], [(0, 2), (1, 3)], 1,
        (tm, kd), [], [], [jax.ShapeDtypeStruct((t, kd), f32)], [pl.BlockSpec((tm, kd), lambda i, j, k: (i, 0))],
        _store(f32))[0]


def _rms_fwd(x, g):
    t, d = x.shape
    tm = min(t, 256)

    def body(x_ref, g_ref, o_ref):
        xv = x_ref[...]
        r = lax.rsqrt(jnp.mean(xv * xv, axis=-1, keepdims=True) + EPS)
        o_ref[...] = (xv * r * g_ref[...]).astype(bf16)

    return pl.pallas_call(
        body, grid=(t // tm,), in_specs=[pl.BlockSpec((tm, d), lambda i: (i, 0)), pl.BlockSpec((1, d), lambda i: (0, 0))],
        out_specs=pl.BlockSpec((tm, d), lambda i: (i, 0)), out_shape=jax.ShapeDtypeStruct((t, d), bf16),
        compiler_params=_params("parallel"), name="rms_fwd")(x, g.reshape(1, d))


def _rms_bwd(x, g, dh, dres):
    t, d = x.shape
    tm = min(t, 256)

    def body(x_ref, g_ref, dh_ref, dres_ref, dx_ref, dxb_ref, dg_ref):
        xv = x_ref[...]
        dy = dh_ref[...].astype(f32)
        r = lax.rsqrt(jnp.mean(xv * xv, axis=-1, keepdims=True) + EPS)
        gy = dy * g_ref[...]
        dot = jnp.mean(xv * gy, axis=-1, keepdims=True)
        dx = dres_ref[...] + r * gy - xv * (r * r * r * dot)
        dx_ref[...] = dx
        dxb_ref[...] = dx.astype(bf16)

        @pl.when(pl.program_id(0) == 0)
        def _():
            dg_ref[...] = jnp.zeros_like(dg_ref)

        dg_ref[...] += jnp.sum(dy * xv * r, axis=0, keepdims=True)

    row = pl.BlockSpec((tm, d), lambda i: (i, 0))
    vec = pl.BlockSpec((1, d), lambda i: (0, 0))
    return pl.pallas_call(
        body, grid=(t // tm,), in_specs=[row, vec, row, row], out_specs=[row, row, vec],
        out_shape=[jax.ShapeDtypeStruct((t, d), f32), jax.ShapeDtypeStruct((t, d), bf16), jax.ShapeDtypeStruct((1, d), f32)],
        compiler_params=_params("arbitrary"), name="rms_bwd")(x, g.reshape(1, d), dh, dres)


def _loss_head(x, g, target):
    t, d = x.shape
    tm = min(t, 256)

    def body(x_ref, g_ref, t_ref, loss_ref, dx_ref, dxb_ref, dg_ref):
        xv = x_ref[...]
        r = lax.rsqrt(jnp.mean(xv * xv, axis=-1, keepdims=True) + EPS)
        xn = xv * r
        err = xn * g_ref[...] - t_ref[...]
        dy = err * (1.0 / d)
        gy = dy * g_ref[...]
        dot = jnp.mean(xv * gy, axis=-1, keepdims=True)
        dx = r * gy - xv * (r * r * r * dot)
        dx_ref[...] = dx
        dxb_ref[...] = dx.astype(bf16)

        @pl.when(pl.program_id(0) == 0)
        def _():
            dg_ref[...] = jnp.zeros_like(dg_ref)
            loss_ref[...] = jnp.zeros_like(loss_ref)

        dg_ref[...] += jnp.sum(dy * xn, axis=0, keepdims=True)
        loss_ref[...] += 0.5 * jnp.sum(jnp.sum(err * err, axis=-1, keepdims=True) * (1.0 / d), axis=0, keepdims=True)

    row = pl.BlockSpec((tm, d), lambda i: (i, 0))
    vec = pl.BlockSpec((1, d), lambda i: (0, 0))
    one = pl.BlockSpec((1, 1), lambda i: (0, 0))
    return pl.pallas_call(
        body, grid=(t // tm,), in_specs=[row, vec, row], out_specs=[one, row, row, vec],
        out_shape=[jax.ShapeDtypeStruct((1, 1), f32), jax.ShapeDtypeStruct((t, d), f32),
                   jax.ShapeDtypeStruct((t, d), bf16), jax.ShapeDtypeStruct((1, d), f32)],
        compiler_params=_params("arbitrary"), name="loss_head")(x, g.reshape(1, d), target)


def _attn_consts():
    qt, kw = 2 * CHUNK, (A_LEFT_CHUNKS + 2) * CHUNK
    r = np.arange(qt)[:, None]
    kc = np.arange(kw)[None, :]
    rel = np.clip(r + A_LEFT_CHUNKS * CHUNK - kc, -A_MAX_REL, A_MAX_REL) + A_MAX_REL
    dchunk = kc // CHUNK - r // CHUNK
    valid = (dchunk >= 0) & (dchunk <= A_LEFT_CHUNKS)
    m = np.arange(kw + qt)
    relidx = np.clip(A_LEFT_CHUNKS * CHUNK - (m - (qt - 1)), -A_MAX_REL, A_MAX_REL) + A_MAX_REL
    onehot = np.zeros((kw + qt, 2 * A_MAX_REL + 1), np.float32)
    onehot[m, relidx] = 1.0
    return qt, kw, rel, valid, onehot


def _bias_table(rel_bias):
    qt, kw, _, valid, onehot = _attn_consts()
    h = rel_bias.shape[0]
    w = kw + qt
    relidx = np.argmax(onehot, axis=1)
    e = jnp.roll(jnp.take(rel_bias, jnp.asarray(relidx), axis=1), -(qt - 1), axis=1)
    rows = jnp.broadcast_to(e[:, None, :], (h, qt, w)).reshape(h, qt * w)
    skew = rows[:, :qt * (w - 1)].reshape(h, qt, w - 1)[:, :, :kw]
    return jnp.where(jnp.asarray(valid)[None], skew, NEG_INF).astype(f32)


def _bias_table_grad(dtab):
    qt, kw, _, _, onehot = _attn_consts()
    h = dtab.shape[0]
    w = kw + qt
    xp = jnp.pad(dtab[:, ::-1, :], ((0, 0), (0, 0), (0, w + 1 - kw)))
    skew = xp.reshape(h, qt * (w + 1))[:, :qt * w].reshape(h, qt, w)
    de = jnp.sum(skew, axis=1)
    return jnp.dot(de, jnp.asarray(onehot), precision=lax.Precision.HIGHEST)


def _attn_scores(q_ref, kpad, btab_ref, r0, qt, kw, pad):
    qv = q_ref[pl.ds(r0, qt), :]
    kwin = kpad[pl.ds(r0, kw), :]
    s = lax.dot_general(qv, kwin, _DN["nt"], preferred_element_type=f32) * (A_HEAD_DIM ** -0.5) + btab_ref[...]
    kcol = lax.broadcasted_iota(jnp.int32, (qt, kw), 1)
    s = jnp.where(r0 + kcol >= pad, s, NEG_INF)
    p = jnp.exp(s - jnp.max(s, axis=-1, keepdims=True))
    return qv, kwin, p / jnp.sum(p, axis=-1, keepdims=True)


def _attn_fwd(proj, btab, heads):
    t = proj.shape[0]
    qt, kw = btab.shape[1], btab.shape[2]
    pad = kw - qt

    def body(q_ref, k_ref, v_ref, btab_ref, o_ref, kpad, vpad):
        zeros = jnp.zeros((pad, A_HEAD_DIM), bf16)
        kpad[pl.ds(0, pad), :] = zeros
        vpad[pl.ds(0, pad), :] = zeros
        kpad[pl.ds(pad, t), :] = k_ref[...]
        vpad[pl.ds(pad, t), :] = v_ref[...]

        def tile(i, carry):
            r0 = pl.multiple_of(i * qt, qt)
            _, _, p = _attn_scores(q_ref, kpad, btab_ref, r0, qt, kw, pad)
            o = lax.dot_general(p.astype(bf16), vpad[pl.ds(r0, kw), :], _DN["nn"], preferred_element_type=f32)
            o_ref[pl.ds(r0, qt), :] = o.astype(bf16)
            return carry

        lax.fori_loop(0, t // qt, tile, 0)

    col = lambda off: pl.BlockSpec((t, A_HEAD_DIM), lambda h, off=off: (0, off + h))
    return pl.pallas_call(
        body, grid=(heads,),
        in_specs=[col(0), col(heads), col(2 * heads), pl.BlockSpec((None, qt, kw), lambda h: (h, 0, 0))],
        out_specs=col(0), out_shape=jax.ShapeDtypeStruct((t, heads * A_HEAD_DIM), bf16),
        scratch_shapes=[pltpu.VMEM((t + pad, A_HEAD_DIM), bf16)] * 2,
        compiler_params=_params("parallel"), name="attn_fwd")(proj, proj, proj, btab)


def _attn_bwd(proj, dmix, btab, heads):
    t = proj.shape[0]
    qt, kw = btab.shape[1], btab.shape[2]
    pad = kw - qt
    scale = A_HEAD_DIM ** -0.5

    def body(q_ref, k_ref, v_ref, do_ref, btab_ref, dq_ref, dk_ref, dv_ref, dtab_ref, kpad, vpad, dkacc, dvacc):
        zeros = jnp.zeros((pad, A_HEAD_DIM), bf16)
        kpad[pl.ds(0, pad), :] = zeros
        vpad[pl.ds(0, pad), :] = zeros
        kpad[pl.ds(pad, t), :] = k_ref[...]
        vpad[pl.ds(pad, t), :] = v_ref[...]
        dkacc[...] = jnp.zeros_like(dkacc)
        dvacc[...] = jnp.zeros_like(dvacc)
        dtab_ref[...] = jnp.zeros_like(dtab_ref)

        def tile(i, carry):
            r0 = pl.multiple_of(i * qt, qt)
            qv, kwin, p = _attn_scores(q_ref, kpad, btab_ref, r0, qt, kw, pad)
            dov = do_ref[pl.ds(r0, qt), :]
            dp = lax.dot_general(dov, vpad[pl.ds(r0, kw), :], _DN["nt"], preferred_element_type=f32)
            ds = p * (dp - jnp.sum(p * dp, axis=-1, keepdims=True))
            dtab_ref[...] += ds
            dsb = ds.astype(bf16)
            dq = lax.dot_general(dsb, kwin, _DN["nn"], preferred_element_type=f32) * scale
            dq_ref[pl.ds(r0, qt), :] = dq.astype(bf16)
            dkacc[pl.ds(r0, kw), :] += lax.dot_general(dsb, qv, _DN["tn"], preferred_element_type=f32) * scale
            dvacc[pl.ds(r0, kw), :] += lax.dot_general(p.astype(bf16), dov, _DN["tn"], preferred_element_type=f32)
            return carry

        lax.fori_loop(0, t // qt, tile, 0)
        dk_ref[...] = dkacc[pl.ds(pad, t), :].astype(bf16)
        dv_ref[...] = dvacc[pl.ds(pad, t), :].astype(bf16)

    col = lambda off: pl.BlockSpec((t, A_HEAD_DIM), lambda h, off=off: (0, off + h))
    tab = pl.BlockSpec((None, qt, kw), lambda h: (h, 0, 0))
    wide = jax.ShapeDtypeStruct((t, heads * A_HEAD_DIM), bf16)
    return pl.pallas_call(
        body, grid=(heads,), in_specs=[col(0), col(heads), col(2 * heads), col(0), tab],
        out_specs=[col(0), col(0), col(0), tab],
        out_shape=[wide, wide, wide, jax.ShapeDtypeStruct((heads, qt, kw), f32)],
        scratch_shapes=[pltpu.VMEM((t + pad, A_HEAD_DIM), bf16)] * 2 + [pltpu.VMEM((t + pad, A_HEAD_DIM), f32)] * 2,
        compiler_params=_params("parallel"), name="attn_bwd")(proj, proj, proj, dmix, btab)


def _shift_down(z, k):
    rows = lax.broadcasted_iota(jnp.int32, z.shape, 0)
    return jnp.where(rows >= k, pltpu.roll(z, k, 0), 0.0)


def _shift_up(z, k):
    t = z.shape[0]
    rows = lax.broadcasted_iota(jnp.int32, z.shape, 0)
    return jnp.where(rows < t - k, pltpu.roll(z, t - k, 0), 0.0)


def _conv_fwd(proj, conv_w, a_blocks, b_blocks):
    t = proj.shape[0]

    def body(b_ref, c_ref, h_ref, w_ref, o_ref):
        z = c_ref[...].astype(f32) * h_ref[...].astype(f32)
        w = w_ref[...]
        y = w[0:1, :] * _shift_down(z, 2) + w[1:2, :] * _shift_down(z, 1) + w[2:3, :] * z
        o_ref[...] = (b_ref[...].astype(f32) * y).astype(bf16)

    col = lambda off: pl.BlockSpec((t, LANE), lambda i, off=off: (0, off + i))
    return pl.pallas_call(
        body, grid=(b_blocks,),
        in_specs=[col(3 * a_blocks), col(3 * a_blocks + b_blocks), col(3 * a_blocks + 2 * b_blocks),
                  pl.BlockSpec((CONV_WIDTH, LANE), lambda i: (0, i))],
        out_specs=col(0), out_shape=jax.ShapeDtypeStruct((t, b_blocks * LANE), bf16),
        compiler_params=_params("parallel"), name="conv_fwd")(proj, proj, proj, conv_w)


def _conv_bwd(proj, dmix, conv_w, a_blocks, b_blocks):
    t = proj.shape[0]

    def body(b_ref, c_ref, h_ref, do_ref, w_ref, db_ref, dc_ref, dh_ref, dw_ref):
        bv, cv, hv = b_ref[...].astype(f32), c_ref[...].astype(f32), h_ref[...].astype(f32)
        w = w_ref[...]
        z = cv * hv
        z1, z2 = _shift_down(z, 1), _shift_down(z, 2)
        y = w[0:1, :] * z2 + w[1:2, :] * z1 + w[2:3, :] * z
        dov = do_ref[...].astype(f32)
        db_ref[...] = (dov * y).astype(bf16)
        dy = dov * bv
        dz = w[2:3, :] * dy + w[1:2, :] * _shift_up(dy, 1) + w[0:1, :] * _shift_up(dy, 2)
        dc_ref[...] = (dz * hv).astype(bf16)
        dh_ref[...] = (dz * cv).astype(bf16)
        dw_ref[0:1, :] = jnp.sum(dy * z2, axis=0, keepdims=True)
        dw_ref[1:2, :] = jnp.sum(dy * z1, axis=0, keepdims=True)
        dw_ref[2:3, :] = jnp.sum(dy * z, axis=0, keepdims=True)

    col = lambda off: pl.BlockSpec((t, LANE), lambda i, off=off: (0, off + i))
    wspec = pl.BlockSpec((CONV_WIDTH, LANE), lambda i: (0, i))
    wide = jax.ShapeDtypeStruct((t, b_blocks * LANE), bf16)
    return pl.pallas_call(
        body, grid=(b_blocks,),
        in_specs=[col(3 * a_blocks), col(3 * a_blocks + b_blocks), col(3 * a_blocks + 2 * b_blocks), col(a_blocks), wspec],
        out_specs=[col(0), col(0), col(0), wspec],
        out_shape=[wide, wide, wide, jax.ShapeDtypeStruct((CONV_WIDTH, b_blocks * LANE), f32)],
        compiler_params=_params("parallel"), name="conv_bwd")(proj, proj, proj, dmix, conv_w)


_RSQRT2 = 0.7071067811865476
_RSQRT2PI = 0.3989422804014327


def _gelu(x):
    return 0.5 * x * (1.0 + lax.erf(x * _RSQRT2))


def _gelu_grad(x):
    return 0.5 * (1.0 + lax.erf(x * _RSQRT2)) + x * jnp.exp(-0.5 * x * x) * _RSQRT2PI


def _sgu_common(a_ref, lg_ref, lb_ref, cw):
    av = a_ref[...]
    u = _gelu(av[:, :cw])
    v = _gelu(av[:, cw:])
    mu = jnp.mean(v, axis=-1, keepdims=True)
    xc = v - mu
    rstd = lax.rsqrt(jnp.mean(xc * xc, axis=-1, keepdims=True) + EPS)
    xhat = xc * rstd
    vln = xhat * lg_ref[...] + lb_ref[...]
    pos_t = lax.broadcasted_iota(jnp.int32, (C_BLOCK, C_BLOCK), 0) // CHUNK
    pos_s = lax.broadcasted_iota(jnp.int32, (C_BLOCK, C_BLOCK), 1) // CHUNK
    return av, u, xhat, rstd, vln, pos_s <= pos_t


def _sgu_fwd(a, ln_g, ln_b, w_s, bs_t):
    t, cw2 = a.shape
    cw = cw2 // 2
    groups = w_s.shape[0]
    cg = cw // groups

    def body(a_ref, lg_ref, lb_ref, ws_ref, bs_ref, m_ref):
        _, u, _, _, vln, mask = _sgu_common(a_ref, lg_ref, lb_ref, cw)
        vb = vln.astype(bf16)
        for g in range(groups):
            sl = slice(g * cg, (g + 1) * cg)
            wm = jnp.where(mask, ws_ref[g], 0.0).astype(bf16)
            s = lax.dot_general(wm, vb[:, sl], _DN["nn"], preferred_element_type=f32) + bs_ref[:, g:g + 1]
            m_ref[:, sl] = (u[:, sl] * s).astype(bf16)

    vec = pl.BlockSpec((1, cw), lambda n: (0, 0))
    return pl.pallas_call(
        body, grid=(t // C_BLOCK,),
        in_specs=[pl.BlockSpec((C_BLOCK, cw2), lambda n: (n, 0)), vec, vec,
                  pl.BlockSpec((groups, C_BLOCK, C_BLOCK), lambda n: (0, 0, 0)),
                  pl.BlockSpec((C_BLOCK, groups), lambda n: (0, 0))],
        out_specs=pl.BlockSpec((C_BLOCK, cw), lambda n: (n, 0)), out_shape=jax.ShapeDtypeStruct((t, cw), bf16),
        compiler_params=_params("parallel"), name="sgu_fwd")(a, ln_g.reshape(1, cw), ln_b.reshape(1, cw), w_s, bs_t)


def _sgu_bwd(a, dm, ln_g, ln_b, w_s, bs_t):
    t, cw2 = a.shape
    cw = cw2 // 2
    groups = w_s.shape[0]
    cg = cw // groups

    def body(a_ref, dm_ref, lg_ref, lb_ref, ws_ref, bs_ref, da_ref, dws_ref, dbs_ref, dlg_ref, dlb_ref, dvln):
        @pl.when(pl.program_id(0) == 0)
        def _():
            dws_ref[...] = jnp.zeros_like(dws_ref)
            dbs_ref[...] = jnp.zeros_like(dbs_ref)
            dlg_ref[...] = jnp.zeros_like(dlg_ref)
            dlb_ref[...] = jnp.zeros_like(dlb_ref)

        av, u, xhat, rstd, vln, mask = _sgu_common(a_ref, lg_ref, lb_ref, cw)
        vb = vln.astype(bf16)
        lane = lax.broadcasted_iota(jnp.int32, (C_BLOCK, groups), 1)
        dbs = jnp.zeros((C_BLOCK, groups), f32)
        for g in range(groups):
            sl = slice(g * cg, (g + 1) * cg)
            wm = jnp.where(mask, ws_ref[g], 0.0).astype(bf16)
            s = lax.dot_general(wm, vb[:, sl], _DN["nn"], preferred_element_type=f32) + bs_ref[:, g:g + 1]
            dmg = dm_ref[:, sl].astype(f32)
            da_ref[:, sl] = (dmg * s * _gelu_grad(av[:, sl])).astype(bf16)
            dsg = dmg * u[:, sl]
            dbs = dbs + jnp.where(lane == g, jnp.sum(dsg, axis=-1, keepdims=True), 0.0)
            dsb = dsg.astype(bf16)
            dws_ref[g] += jnp.where(mask, lax.dot_general(dsb, vb[:, sl], _DN["nt"], preferred_element_type=f32), 0.0)
            dvln[:, sl] = lax.dot_general(wm, dsb, _DN["tn"], preferred_element_type=f32)
        dbs_ref[...] += dbs
        dv = dvln[...]
        dlg_ref[...] += jnp.sum(dv * xhat, axis=0, keepdims=True)
        dlb_ref[...] += jnp.sum(dv, axis=0, keepdims=True)
        dxh = dv * lg_ref[...]
        dvv = rstd * (dxh - jnp.mean(dxh, axis=-1, keepdims=True) - xhat * jnp.mean(dxh * xhat, axis=-1, keepdims=True))
        da_ref[:, cw:] = (dvv * _gelu_grad(av[:, cw:])).astype(bf16)

    vec = pl.BlockSpec((1, cw), lambda n: (0, 0))
    wsp = pl.BlockSpec((groups, C_BLOCK, C_BLOCK), lambda n: (0, 0, 0))
    bsp = pl.BlockSpec((C_BLOCK, groups), lambda n: (0, 0))
    return pl.pallas_call(
        body, grid=(t // C_BLOCK,),
        in_specs=[pl.BlockSpec((C_BLOCK, cw2), lambda n: (n, 0)), pl.BlockSpec((C_BLOCK, cw), lambda n: (n, 0)), vec, vec, wsp, bsp],
        out_specs=[pl.BlockSpec((C_BLOCK, cw2), lambda n: (n, 0)), wsp, bsp, vec, vec],
        out_shape=[jax.ShapeDtypeStruct((t, cw2), bf16), jax.ShapeDtypeStruct(w_s.shape, f32),
                   jax.ShapeDtypeStruct(bs_t.shape, f32), jax.ShapeDtypeStruct((1, cw), f32), jax.ShapeDtypeStruct((1, cw), f32)],
        scratch_shapes=[pltpu.VMEM((C_BLOCK, cw), f32)],
        compiler_params=_params("arbitrary"), name="sgu_bwd")(a, dm, ln_g.reshape(1, cw), ln_b.reshape(1, cw), w_s, bs_t)


def _place():
    x, y, c = lax.axis_index("x"), lax.axis_index("y"), lax.axis_index("c")
    return x, y, c, [(1 - x, y), (x, 1 - y), (1 - x, 1 - y)]


def _gather_copies(n, ins, outs, send_sems, recv_sems, local_sems):
    x, y, c, chips = _place()
    sibling = (x, y, 1 - c)

    def slot(px, py, pc):
        return 4 * px + 2 * py + pc

    def copy(i, k, block, to, src=None):
        dst = outs[i].at[slot(*block)]
        return pltpu.make_async_remote_copy(src_ref=dst if src is None else src, dst_ref=dst, send_sem=send_sems.at[i, k],
                                            recv_sem=recv_sems.at[i, k], device_id=to, device_id_type=MESH)

    started = []
    for i in range(n):
        mine = pltpu.make_async_copy(ins[i], outs[i].at[slot(x, y, c)], local_sems.at[i])
        mine.start()
        started.append(mine)
    sends = []
    for i in range(n):
        sends.append(copy(i, 0, (x, y, c), sibling, src=ins[i]))
        sends += [copy(i, 1 + j, (x, y, c), (*chip, c), src=ins[i]) for j, chip in enumerate(chips)]
    for cp in sends:
        cp.start()
    for i in range(n):
        for j, chip in enumerate(chips):
            copy(i, 1 + j, (*chip, c), (x, y, c)).wait_recv()
            fwd = copy(i, 4 + j, (*chip, c), sibling)
            fwd.start()
            sends.append(fwd)
    for i in range(n):
        copy(i, 0, sibling, (x, y, c)).wait_recv()
        for j, chip in enumerate(chips):
            copy(i, 4 + j, (*chip, 1 - c), (x, y, c)).wait_recv()
    for cp in sends:
        cp.wait_send()
    for mine in started:
        mine.wait()


def _all_gather(name, shards):
    n = len(shards)

    def body(*refs):
        _gather_copies(n, refs[:n], refs[n:2 * n], *refs[2 * n:])

    return pl.pallas_call(
        body, in_specs=[ANY] * n, out_specs=[ANY] * n,
        out_shape=[jax.ShapeDtypeStruct((N_DEV,) + s.shape, s.dtype) for s in shards],
        scratch_shapes=[pltpu.SemaphoreType.DMA((n, 7)), pltpu.SemaphoreType.DMA((n, 7)), pltpu.SemaphoreType.DMA((n,))],
        name=name)(*shards)


def _rs_sibling(name, grads):
    n = len(grads)
    pieces = N_CHIP * RS_SPLIT

    def body(*refs):
        g, got = refs[:n], refs[n:2 * n]
        send_sems, recv_sems = refs[2 * n:]
        x, y, c, _ = _place()
        cps = []
        for i in range(n):
            rc = g[i].shape[1] // RS_SPLIT
            for q in range(N_CHIP):
                for s in range(RS_SPLIT):
                    rows = pl.ds(s * rc, rc)
                    k = q * RS_SPLIT + s
                    cps.append(pltpu.make_async_remote_copy(
                        src_ref=g[i].at[N_CHIP * (1 - c) + q, rows], dst_ref=got[i].at[q, rows],
                        send_sem=send_sems.at[i, k], recv_sem=recv_sems.at[i, k],
                        device_id=(x, y, 1 - c), device_id_type=MESH))
        for cp in cps:
            cp.start()
        for cp in cps:
            cp.wait()

    half = [jax.ShapeDtypeStruct((N_CHIP,) + g.shape[1:], g.dtype) for g in grads]
    return pl.pallas_call(
        body, in_specs=[ANY] * n, out_specs=[ANY] * n, out_shape=half,
        scratch_shapes=[pltpu.SemaphoreType.DMA((n, pieces)), pltpu.SemaphoreType.DMA((n, pieces))],
        name=name)(*grads)


def _pair_sum(g, got, core):
    _, rows, cols = g.shape
    tr = 512 if rows % 512 == 0 else rows

    def body(c_ref, a_ref, b_ref, o_ref):
        o_ref[...] = (a_ref[...].astype(f32) + b_ref[...].astype(f32)).astype(bf16)

    spec = pl.BlockSpec((None, tr, cols), lambda q, i, c_ref: (q, i, 0))
    return pl.pallas_call(
        body,
        grid_spec=pltpu.PrefetchScalarGridSpec(
            num_scalar_prefetch=1, grid=(N_CHIP, rows // tr),
            in_specs=[pl.BlockSpec((None, tr, cols), lambda q, i, c_ref: (N_CHIP * c_ref[0] + q, i, 0)), spec],
            out_specs=spec),
        out_shape=jax.ShapeDtypeStruct((N_CHIP, rows, cols), bf16), compiler_params=_params("parallel", "parallel"),
        name="pair_sum")(core, g, got)


def _rs_chips(name, parts):
    n = len(parts)

    def body(*refs):
        p, r = refs[:n], refs[n:2 * n]
        send_sems, recv_sems, local_sems = refs[2 * n:]
        x, y, c, chips = _place()
        q = 2 * x + y
        cps = []
        for i in range(n):
            loc = pltpu.make_async_copy(p[i].at[q], r[i].at[q], local_sems.at[i])
            loc.start()
            cps.append(loc)
        rems = []
        for i in range(n):
            for k, (px, py) in enumerate(chips):
                rem = pltpu.make_async_remote_copy(src_ref=p[i].at[2 * px + py], dst_ref=r[i].at[q],
                                                   send_sem=send_sems.at[i, k], recv_sem=recv_sems.at[i, k],
                                                   device_id=(px, py, c), device_id_type=MESH)
                rem.start()
                rems.append(rem)
        for rem in rems:
            rem.wait()
        for loc in cps:
            loc.wait()

    return pl.pallas_call(
        body, in_specs=[ANY] * n, out_specs=[ANY] * n, out_shape=[jax.ShapeDtypeStruct(p.shape, p.dtype) for p in parts],
        scratch_shapes=[pltpu.SemaphoreType.DMA((n, 3)), pltpu.SemaphoreType.DMA((n, 3)), pltpu.SemaphoreType.DMA((n,))],
        name=name)(*parts)


def _reduce_scatter(tag, grads, core):
    got = _rs_sibling("rs_sibling_" + tag, grads)
    parts = [_pair_sum(a, b, core) for a, b in zip(grads, got)]
    return _rs_chips("rs_chips_" + tag, parts)


def _gather_small(name, packed, reduce):
    rows = packed.shape[0]

    def body(x_ref, o_ref, buf, send_sems, recv_sems, local_sems):
        _gather_copies(1, [x_ref], [buf], send_sems, recv_sems, local_sems)
        if reduce:
            acc = buf[0]
            for j in range(1, N_DEV):
                acc = acc + buf[j]
            o_ref[...] = acc
        else:
            o_ref[...] = buf[...]

    vm = pl.BlockSpec(memory_space=pltpu.VMEM)
    return pl.pallas_call(
        body, in_specs=[vm], out_specs=vm,
        out_shape=jax.ShapeDtypeStruct((rows, LANE) if reduce else (N_DEV, rows, LANE), f32),
        scratch_shapes=[pltpu.VMEM((N_DEV, rows, LANE), f32), pltpu.SemaphoreType.DMA((1, 7)), pltpu.SemaphoreType.DMA((1, 7)),
                        pltpu.SemaphoreType.DMA((1,))],
        compiler_params=pltpu.CompilerParams(vmem_limit_bytes=VMEM_LIMIT), name=name)(packed)


def _pack(arrs):
    flat = jnp.concatenate([a.reshape(-1).astype(f32) for a in arrs])
    rows = -(-flat.shape[0] // (8 * LANE)) * 8
    return jnp.pad(flat, (0, rows * LANE - flat.shape[0])).reshape(rows, LANE)


def _unpack(buf, shapes):
    flat = buf.reshape(-1)
    out, off = [], 0
    for s in shapes:
        n = int(np.prod(s))
        out.append(flat[off:off + n].reshape(s))
        off += n
    return out


def _adam_math(w, g, m, v):
    m2 = ADAM_B1 * m + (1.0 - ADAM_B1) * g
    v2 = ADAM_B2 * v + (1.0 - ADAM_B2) * (g * g)
    m_hat = m2 / (1.0 - ADAM_B1 ** ADAM_STEP)
    v_hat = v2 / (1.0 - ADAM_B2 ** ADAM_STEP)
    delta = -ADAM_LR * (m_hat / (jnp.sqrt(v_hat) + ADAM_EPS) + ADAM_WD * w)
    return delta, m2, v2


def _adam_big(name, w, m, v, parts):
    layers, rows, cols = w.shape
    tr = 256 if rows % 256 == 0 else 64 if rows % 64 == 0 else 8

    def body(w_ref, m_ref, v_ref, *rest):
        p_refs = rest[:layers]
        g_ref, d_ref, m2_ref, v2_ref = rest[layers:]
        for li in range(layers):
            @pl.when(pl.program_id(0) == li)
            def _(li=li):
                g = p_refs[li][0].astype(f32)
                for q in range(1, N_CHIP):
                    g = g + p_refs[li][q].astype(f32)
                delta, m2, v2 = _adam_math(w_ref[...], g, m_ref[...], v_ref[...])
                g_ref[...] = g
                d_ref[...] = delta
                m2_ref[...] = m2
                v2_ref[...] = v2

    spec = pl.BlockSpec((None, tr, cols), lambda l, i: (l, i, 0))
    pspecs = [pl.BlockSpec((N_CHIP, tr, cols), lambda l, i, li=li: (0, jnp.where(l == li, i, 0), 0)) for li in range(layers)]
    out = jax.ShapeDtypeStruct((layers, rows, cols), f32)
    return pl.pallas_call(
        body, grid=(layers, rows // tr), in_specs=[spec, spec, spec] + pspecs, out_specs=[spec] * 4, out_shape=[out] * 4,
        compiler_params=_params("arbitrary", "arbitrary"), name=name)(w, m, v, *parts)


def _adam_small(w, g, m, v):
    rows = w.shape[0]

    def body(w_ref, g_ref, m_ref, v_ref, d_ref, m2_ref, v2_ref):
        delta, m2, v2 = _adam_math(w_ref[...], g_ref[...], m_ref[...], v_ref[...])
        d_ref[...] = delta
        m2_ref[...] = m2
        v2_ref[...] = v2

    out = jax.ShapeDtypeStruct((rows, LANE), f32)
    return pl.pallas_call(body, out_shape=[out] * 3, name="adam_small")(w, g, m, v)


def kernel(x, mix_norm, ab_w_in, ab_rel_bias, ab_conv_w, ab_w_out, c_w_in, c_ln_g, c_ln_b, c_w_s, c_b_s, c_w_out, ffn_norm, ffn_w_gate, ffn_w_up, ffn_w_down, final_norm, loss_target, m_mix_norm, m_ab_w_in, m_ab_rel_bias, m_ab_conv_w, m_ab_w_out, m_c_w_in, m_c_ln_g, m_c_ln_b, m_c_w_s, m_c_b_s, m_c_w_out, m_ffn_norm, m_ffn_w_gate, m_ffn_w_up, m_ffn_w_down, m_final_norm, v_mix_norm, v_ab_w_in, v_ab_rel_bias, v_ab_conv_w, v_ab_w_out, v_c_w_in, v_c_ln_g, v_c_ln_b, v_c_w_s, v_c_b_s, v_c_w_out, v_ffn_norm, v_ffn_w_gate, v_ffn_w_up, v_ffn_w_down, v_final_norm):
    d = D_MODEL
    a_width = d // 2
    heads = a_width // A_HEAD_DIM
    a_blocks = a_width // LANE
    b_blocks = (d - a_width) // LANE
    n_even, n_odd = (DEPTH + 1) // 2, DEPTH // 2
    me = 4 * lax.axis_index("x") + 2 * lax.axis_index("y") + lax.axis_index("c")
    core = lax.axis_index("c").astype(jnp.int32).reshape(1)

    weights = dict(mix_norm=mix_norm, ab_w_in=ab_w_in, ab_rel_bias=ab_rel_bias, ab_conv_w=ab_conv_w, ab_w_out=ab_w_out,
                   c_w_in=c_w_in, c_ln_g=c_ln_g, c_ln_b=c_ln_b, c_w_s=c_w_s, c_b_s=c_b_s, c_w_out=c_w_out,
                   ffn_norm=ffn_norm, ffn_w_gate=ffn_w_gate, ffn_w_up=ffn_w_up, ffn_w_down=ffn_w_down, final_norm=final_norm)
    mom_m = dict(mix_norm=m_mix_norm, ab_w_in=m_ab_w_in, ab_rel_bias=m_ab_rel_bias, ab_conv_w=m_ab_conv_w, ab_w_out=m_ab_w_out,
                 c_w_in=m_c_w_in, c_ln_g=m_c_ln_g, c_ln_b=m_c_ln_b, c_w_s=m_c_w_s, c_b_s=m_c_b_s, c_w_out=m_c_w_out,
                 ffn_norm=m_ffn_norm, ffn_w_gate=m_ffn_w_gate, ffn_w_up=m_ffn_w_up, ffn_w_down=m_ffn_w_down, final_norm=m_final_norm)
    mom_v = dict(mix_norm=v_mix_norm, ab_w_in=v_ab_w_in, ab_rel_bias=v_ab_rel_bias, ab_conv_w=v_ab_conv_w, ab_w_out=v_ab_w_out,
                 c_w_in=v_c_w_in, c_ln_g=v_c_ln_g, c_ln_b=v_c_ln_b, c_w_s=v_c_w_s, c_b_s=v_c_b_s, c_w_out=v_c_w_out,
                 ffn_norm=v_ffn_norm, ffn_w_gate=v_ffn_w_gate, ffn_w_up=v_ffn_w_up, ffn_w_down=v_ffn_w_down, final_norm=v_final_norm)
    order = list(weights)

    sharded_small = [ab_conv_w, c_ln_g, c_ln_b]
    gathered = _gather_small("gather_small", _pack(sharded_small), reduce=False)
    conv_parts, lng_parts, lnb_parts = [], [], []
    for j in range(N_DEV):
        cw_j, lg_j, lb_j = _unpack(gathered[j], [a.shape for a in sharded_small])
        conv_parts.append(cw_j)
        lng_parts.append(lg_j)
        lnb_parts.append(lb_j)
    conv_full = jnp.concatenate(conv_parts, axis=-1)
    lng_full = jnp.concatenate(lng_parts, axis=-1)
    lnb_full = jnp.concatenate(lnb_parts, axis=-1)

    def gather_layer(layer):
        i = layer // 2
        if layer % 2 == 0:
            mix = _all_gather("gather_ab", [ab_w_in[i].astype(bf16), ab_w_out[i].astype(bf16)])
        else:
            mix = _all_gather("gather_c", [c_w_in[i].astype(bf16), c_w_out[i].astype(bf16)])
        ffn = _all_gather("gather_ffn", [ffn_w_gate[layer].astype(bf16), ffn_w_up[layer].astype(bf16),
                                         ffn_w_down[layer].astype(bf16)])
        return mix, ffn

    xs = x[0]
    tgt = loss_target[0]
    saved = []
    for layer in range(DEPTH):
        i = layer // 2
        (w_in_g, w_out_g), (wg_g, wu_g, wd_g) = gather_layer(layer)
        w_out_full = w_out_g.reshape(-1, w_out_g.shape[-1])
        h = _rms_fwd(xs, mix_norm[layer])
        if layer % 2 == 0:
            proj = _mm_cols("ab_proj", h, w_in_g, bf16)
            btab = _bias_table(ab_rel_bias[i])
            attn = _attn_fwd(proj, btab, heads)
            conv = _conv_fwd(proj, conv_full[i], a_blocks, b_blocks)
            mixed = jnp.concatenate([attn, conv], axis=-1)
            ctx = (proj, btab)
        else:
            proj = _mm_cols("c_proj", h, w_in_g, f32)
            bs_t = jnp.transpose(c_b_s[i])
            mixed = _sgu_fwd(proj, lng_full[i], lnb_full[i], c_w_s[i], bs_t)
            ctx = (proj, bs_t)
        x1 = _mm_rows_res("mix_out", mixed, w_out_full, xs)
        h2 = _rms_fwd(x1, ffn_norm[layer])
        g_act, u_act, act = _ffn_in(h2, wg_g, wu_g)
        x2 = _ffn_down(act, wd_g, x1)
        saved.append((xs, h, ctx, mixed, x1, h2, g_act, u_act, w_in_g, w_out_full, wg_g, wu_g, wd_g))
        xs = x2

    loss_part, dx, dxb, d_final = _loss_head(xs, final_norm, tgt)
    loss = lax.psum(loss_part[0, 0], ("x", "y", "c"))

    big_grads = {}
    small = {k: [None] * weights[k].shape[0] for k in ("mix_norm", "ffn_norm", "ab_rel_bias", "ab_conv_w", "c_ln_g", "c_ln_b",
                                                       "c_w_s", "c_b_s")}
    for layer in reversed(range(DEPTH)):
        i = layer // 2
        xs, h, ctx, mixed, x1, h2, g_act, u_act, w_in_g, w_out_full, wg_g, wu_g, wd_g = saved[layer]
        dg, du, act = _ffn_bwd_act(dxb, wd_g, g_act, u_act)
        dwd = _ffn_dwd(act, dxb)
        dwg, dwu = _ffn_dwgu(h2, dg, du)
        dh2 = _ffn_dh(dg, du, wg_g, wu_g)
        dx, dxb, dgn = _rms_bwd(x1, ffn_norm[layer], dh2, dx)
        small["ffn_norm"][layer] = dgn[0]
        rg, ru, rd = _reduce_scatter("ffn", [dwg, dwu, dwd], core)
        big_grads[("ffn_w_gate", layer)] = rg
        big_grads[("ffn_w_up", layer)] = ru
        big_grads[("ffn_w_down", layer)] = rd
        dmixed = _mm_nt("mix_out_bwd", dxb, w_out_full, bf16)
        dwout = _mm_tn_rows("mix_out_dw", mixed, dxb)
        if layer % 2 == 0:
            proj, btab = ctx
            dq, dk, dv, dtab = _attn_bwd(proj, dmixed, btab, heads)
            db, dc, dhv, dcw = _conv_bwd(proj, dmixed, conv_full[i], a_blocks, b_blocks)
            dproj = jnp.concatenate([dq, dk, dv, db, dc, dhv], axis=-1)
            small["ab_rel_bias"][i] = _bias_table_grad(dtab)
            small["ab_conv_w"][i] = dcw
            names = ("ab_w_in", "ab_w_out")
            tag = "ab"
        else:
            proj, bs_t = ctx
            dproj, dws, dbs_t, dlg, dlb = _sgu_bwd(proj, dmixed, lng_full[i], lnb_full[i], c_w_s[i], bs_t)
            small["c_w_s"][i] = dws
            small["c_b_s"][i] = jnp.transpose(dbs_t)
            small["c_ln_g"][i] = dlg[0]
            small["c_ln_b"][i] = dlb[0]
            names = ("c_w_in", "c_w_out")
            tag = "c"
        dwin = _mm_tn_cols(tag + "_proj_dw", h, dproj)
        dh = _mm_nt_cols(tag + "_proj_bwd", dproj, w_in_g)
        dx, dxb, dgm = _rms_bwd(xs, mix_norm[layer], dh, dx)
        small["mix_norm"][layer] = dgm[0]
        rin, rout = _reduce_scatter(tag, [dwin, dwout], core)
        big_grads[(names[0], i)] = rin
        big_grads[(names[1], i)] = rout
    grad_x = dx[None]

    small_names = ["mix_norm", "ffn_norm", "ab_rel_bias", "ab_conv_w", "c_ln_g", "c_ln_b", "c_w_s", "c_b_s"]
    small_full = [jnp.stack(small[k]) for k in small_names] + [d_final[0]]
    summed = _unpack(_gather_small("reduce_small", _pack(small_full), reduce=True), [a.shape for a in small_full])
    small_grads = dict(zip(small_names + ["final_norm"], summed))
    for k in ("ab_conv_w", "c_ln_g", "c_ln_b"):
        width = weights[k].shape[-1]
        small_grads[k] = lax.dynamic_slice_in_dim(small_grads[k], me * width, width, axis=-1)
    small_order = [k for k in order if k in small_grads]
    shapes = [weights[k].shape for k in small_order]
    d_s, m_s, v_s = _adam_small(_pack([weights[k] for k in small_order]), _pack([small_grads[k] for k in small_order]),
                                _pack([mom_m[k] for k in small_order]), _pack([mom_v[k] for k in small_order]))
    grads, deltas, new_m, new_v = dict(small_grads), {}, {}, {}
    for k, dd, mm, vv in zip(small_order, _unpack(d_s, shapes), _unpack(m_s, shapes), _unpack(v_s, shapes)):
        deltas[k], new_m[k], new_v[k] = dd, mm, vv

    for k in order:
        if k in small_grads:
            continue
        parts = [big_grads[(k, li)] for li in range(weights[k].shape[0])]
        grads[k], deltas[k], new_m[k], new_v[k] = _adam_big("adam_" + k, weights[k], mom_m[k], mom_v[k], parts)

    return (loss, grad_x, *[grads[k] for k in order], *[deltas[k] for k in order], *[new_m[k] for k in order],
            *[new_v[k] for k in order])
```

```python
import numpy as np
import jax
import jax.numpy as jnp
from jax import lax
from jax.experimental import pallas as pl
from jax.experimental.pallas import tpu as pltpu

D_MODEL = 2048
SEQ = 2048
DEPTH = 4
CHUNK = 64
A_HEAD_DIM = 128
A_LEFT_CHUNKS = 8
A_MAX_REL = 256
CONV_WIDTH = 3
C_BLOCK = 128
C_GROUPS = 8
EPS = 1e-6
NEG_INF = -1e30

ADAM_LR = 0.001
ADAM_B1 = 0.9
ADAM_B2 = 0.999
ADAM_EPS = 1e-08
ADAM_WD = 0.01
ADAM_STEP = 10

N_DEV = 8
N_CHIP = 4
RS_SPLIT = 4
D2D_SPLIT = 2
LANE = 128
VMEM_LIMIT = 52 * 1024 * 1024

bf16 = jnp.bfloat16
f32 = jnp.float32
MESH = pl.DeviceIdType.MESH
ANY = pl.BlockSpec(memory_space=pl.ANY)


def _params(*sem):
    return pltpu.CompilerParams(dimension_semantics=sem, vmem_limit_bytes=VMEM_LIMIT)


def _perm(j):
    return (j % 2) * N_CHIP + j // 2


_DN = {"nn": (((1,), (0,)), ((), ())), "nt": (((1,), (1,)), ((), ())), "tn": (((0,), (0,)), ((), ()))}


def _matmul(name, mode, grid, operands, specs, pairs, n_acc, acc_shape, extras, extra_specs, out_shapes, out_specs,
            epilogue):
    nk = grid[2]
    n_op, n_ex, n_out = len(operands), len(extras), len(out_shapes)

    def body(*refs):
        ops = refs[:n_op]
        ex = refs[n_op:n_op + n_ex]
        outs = refs[n_op + n_ex:n_op + n_ex + n_out]
        accs = refs[n_op + n_ex + n_out:]
        k = pl.program_id(2)

        @pl.when(k == 0)
        def _():
            for acc in accs:
                acc[...] = jnp.zeros_like(acc)

        for p, (ia, ib) in enumerate(pairs):
            acc = accs[p % n_acc]
            acc[...] += lax.dot_general(ops[ia][...], ops[ib][...], _DN[mode], preferred_element_type=f32)

        @pl.when(k == nk - 1)
        def _():
            epilogue([acc[...] for acc in accs], ex, outs)

    return pl.pallas_call(
        body, grid=grid, in_specs=list(specs) + list(extra_specs), out_specs=list(out_specs),
        out_shape=list(out_shapes), scratch_shapes=[pltpu.VMEM(acc_shape, f32)] * n_acc,
        compiler_params=_params("parallel", "parallel", "arbitrary"), name=name)(*operands, *extras)


def _store(dtype):
    def ep(accs, ex, outs):
        for a, o in zip(accs, outs):
            o[...] = a.astype(dtype)
    return ep


def _mm_cols(name, h, wg, out_dtype):
    t, kd = h.shape
    n8 = wg.shape[2]
    tm = min(t, 512)
    return _matmul(
        name, "nn", (t // tm, N_DEV, 1), [h, wg],
        [pl.BlockSpec((tm, kd), lambda i, j, k: (i, 0)), pl.BlockSpec((None, kd, n8), lambda i, j, k: (j, 0, 0))],
        [(0, 1)], 1, (tm, n8), [], [], [jax.ShapeDtypeStruct((t, N_DEV * n8), out_dtype)],
        [pl.BlockSpec((tm, n8), lambda i, j, k: (i, j))], _store(out_dtype))[0]


def _mm_rows_res(name, a, w, res):
    t, kd = a.shape
    n = w.shape[1]
    tm, tn = min(t, 512), min(n, 1024)

    def ep(accs, ex, outs):
        outs[0][...] = ex[0][...] + accs[0]

    return _matmul(
        name, "nn", (t // tm, n // tn, 1), [a, w],
        [pl.BlockSpec((tm, kd), lambda i, j, k: (i, 0)), pl.BlockSpec((kd, tn), lambda i, j, k: (0, j))],
        [(0, 1)], 1, (tm, tn), [res], [pl.BlockSpec((tm, tn), lambda i, j, k: (i, j))],
        [jax.ShapeDtypeStruct((t, n), f32)], [pl.BlockSpec((tm, tn), lambda i, j, k: (i, j))], ep)[0]


def _mm_nt(name, a, w, out_dtype):
    t, n = a.shape
    kd = w.shape[0]
    tm, tn = min(t, 512), min(kd, 1024)
    return _matmul(
        name, "nt", (t // tm, kd // tn, 1), [a, w],
        [pl.BlockSpec((tm, n), lambda i, j, k: (i, 0)), pl.BlockSpec((tn, n), lambda i, j, k: (j, 0))],
        [(0, 1)], 1, (tm, tn), [], [], [jax.ShapeDtypeStruct((t, kd), out_dtype)],
        [pl.BlockSpec((tm, tn), lambda i, j, k: (i, j))], _store(out_dtype))[0]


def _mm_nt_cols(name, da, wg):
    t = da.shape[0]
    kd, n8 = wg.shape[1], wg.shape[2]
    tm = min(t, 512)
    return _matmul(
        name, "nt", (t // tm, 1, N_DEV), [da, wg],
        [pl.BlockSpec((tm, n8), lambda i, j, k: (i, k)), pl.BlockSpec((None, kd, n8), lambda i, j, k: (k, 0, 0))],
        [(0, 1)], 1, (tm, kd), [], [], [jax.ShapeDtypeStruct((t, kd), f32)],
        [pl.BlockSpec((tm, kd), lambda i, j, k: (i, 0))], _store(f32))[0]


def _mm_tn_cols(name, h, da):
    t, kd = h.shape
    n8 = da.shape[1] // N_DEV
    tmk, tk = min(kd, 1024), min(t, 1024)
    return _matmul(
        name, "tn", (kd // tmk, N_DEV, t // tk), [h, da],
        [pl.BlockSpec((tk, tmk), lambda i, j, k: (k, i)), pl.BlockSpec((tk, n8), lambda i, j, k: (k, j))],
        [(0, 1)], 1, (tmk, n8), [], [], [jax.ShapeDtypeStruct((N_DEV, kd, n8), bf16)],
        [pl.BlockSpec((None, tmk, n8), lambda i, j, k: (_perm(j), i, 0))], _store(bf16))[0]


def _mm_tn_rows(name, a, dx):
    t, kf = a.shape
    r8 = kf // N_DEV
    n = dx.shape[1]
    return _matmul(
        name, "tn", (N_DEV, 1, 1), [a, dx],
        [pl.BlockSpec((t, r8), lambda i, j, k: (0, i)), pl.BlockSpec((t, n), lambda i, j, k: (0, 0))],
        [(0, 1)], 1, (r8, n), [], [], [jax.ShapeDtypeStruct((N_DEV, r8, n), bf16)],
        [pl.BlockSpec((None, r8, n), lambda i, j, k: (_perm(i), 0, 0))], _store(bf16))[0]


def _ffn_in(h2, wg, wu):
    t, kd = h2.shape
    f8 = wg.shape[2]
    tm = min(t, 512)

    def ep(accs, ex, outs):
        g, u = accs
        outs[0][...] = g.astype(bf16)
        outs[1][...] = u.astype(bf16)
        outs[2][...] = (g * jax.nn.sigmoid(g) * u).astype(bf16)

    wspec = pl.BlockSpec((None, kd, f8), lambda i, j, k: (j, 0, 0))
    ospec = pl.BlockSpec((None, tm, f8), lambda i, j, k: (j, i, 0))
    return _matmul(
        "ffn_in", "nn", (t // tm, N_DEV, 1), [h2, wg, wu],
        [pl.BlockSpec((tm, kd), lambda i, j, k: (i, 0)), wspec, wspec], [(0, 1), (0, 2)], 2, (tm, f8), [], [],
        [jax.ShapeDtypeStruct((N_DEV, t, f8), bf16)] * 3, [ospec] * 3, ep)


def _ffn_down(act, wd, res):
    _, t, f8 = act.shape
    n = wd.shape[2]
    tm = min(t, 512)

    def ep(accs, ex, outs):
        outs[0][...] = ex[0][...] + accs[0]

    return _matmul(
        "ffn_down", "nn", (t // tm, 1, N_DEV), [act, wd],
        [pl.BlockSpec((None, tm, f8), lambda i, j, k: (k, i, 0)), pl.BlockSpec((None, f8, n), lambda i, j, k: (k, 0, 0))],
        [(0, 1)], 1, (tm, n), [res], [pl.BlockSpec((tm, n), lambda i, j, k: (i, 0))],
        [jax.ShapeDtypeStruct((t, n), f32)], [pl.BlockSpec((tm, n), lambda i, j, k: (i, 0))], ep)[0]


def _ffn_bwd_act(dxb, wd, g, u):
    t, n = dxb.shape
    f8 = wd.shape[1]
    tm = min(t, 512)

    def ep(accs, ex, outs):
        dact = accs[0]
        gv = ex[0][...].astype(f32)
        uv = ex[1][...].astype(f32)
        sg = jax.nn.sigmoid(gv)
        silu = gv * sg
        outs[0][...] = (dact * uv * (sg * (1.0 + gv * (1.0 - sg)))).astype(bf16)
        outs[1][...] = (dact * silu).astype(bf16)
        outs[2][...] = (silu * uv).astype(bf16)

    bspec = pl.BlockSpec((None, tm, f8), lambda i, j, k: (j, i, 0))
    return _matmul(
        "ffn_bwd_act", "nt", (t // tm, N_DEV, 1), [dxb, wd],
        [pl.BlockSpec((tm, n), lambda i, j, k: (i, 0)), pl.BlockSpec((None, f8, n), lambda i, j, k: (j, 0, 0))],
        [(0, 1)], 1, (tm, f8), [g, u], [bspec, bspec], [jax.ShapeDtypeStruct((N_DEV, t, f8), bf16)] * 3, [bspec] * 3, ep)


def _ffn_dwd(act, dxb):
    _, t, f8 = act.shape
    n = dxb.shape[1]
    tk = min(t, 1024)
    return _matmul(
        "ffn_dwd", "tn", (N_DEV, 1, t // tk), [act, dxb],
        [pl.BlockSpec((None, tk, f8), lambda i, j, k: (i, k, 0)), pl.BlockSpec((tk, n), lambda i, j, k: (k, 0))],
        [(0, 1)], 1, (f8, n), [], [], [jax.ShapeDtypeStruct((N_DEV, f8, n), bf16)],
        [pl.BlockSpec((None, f8, n), lambda i, j, k: (_perm(i), 0, 0))], _store(bf16))[0]


def _ffn_dwgu(h2, dg, du):
    t, kd = h2.shape
    f8 = dg.shape[2]
    tmk, tk = min(kd, 1024), min(t, 1024)
    bspec = pl.BlockSpec((None, tk, f8), lambda i, j, k: (j, k, 0))
    ospec = pl.BlockSpec((None, tmk, f8), lambda i, j, k: (_perm(j), i, 0))
    return _matmul(
        "ffn_dwgu", "tn", (kd // tmk, N_DEV, t // tk), [h2, dg, du],
        [pl.BlockSpec((tk, tmk), lambda i, j, k: (k, i)), bspec, bspec], [(0, 1), (0, 2)], 2, (tmk, f8), [], [],
        [jax.ShapeDtypeStruct((N_DEV, kd, f8), bf16)] * 2, [ospec] * 2, _store(bf16))


def _ffn_dh(dg, du, wg, wu):
    _, t, f8 = dg.shape
    kd = wg.shape[1]
    tm = min(t, 512)
    aspec = pl.BlockSpec((None, tm, f8), lambda i, j, k: (k, i, 0))
    wspec = pl.BlockSpec((None, kd, f8), lambda i, j, k: (k, 0, 0))
    return _matmul(
        "ffn_dh", "nt", (t // tm, 1, N_DEV), [dg, du, wg, wu], [aspec, aspec, wspec, wspec], [(0, 2), (1, 3)], 1,
        (tm, kd), [], [], [jax.ShapeDtypeStruct((t, kd), f32)], [pl.BlockSpec((tm, kd), lambda i, j, k: (i, 0))],
        _store(f32))[0]


def _rms_fwd(x, g):
    t, d = x.shape
    tm = min(t, 256)

    def body(x_ref, g_ref, o_ref):
        xv = x_ref[...]
        r = lax.rsqrt(jnp.mean(xv * xv, axis=-1, keepdims=True) + EPS)
        o_ref[...] = (xv * r * g_ref[...]).astype(bf16)

    return pl.pallas_call(
        body, grid=(t // tm,), in_specs=[pl.BlockSpec((tm, d), lambda i: (i, 0)), pl.BlockSpec((1, d), lambda i: (0, 0))],
        out_specs=pl.BlockSpec((tm, d), lambda i: (i, 0)), out_shape=jax.ShapeDtypeStruct((t, d), bf16),
        compiler_params=_params("parallel"), name="rms_fwd")(x, g.reshape(1, d))


def _rms_bwd(x, g, dh, dres):
    t, d = x.shape
    tm = min(t, 256)

    def body(x_ref, g_ref, dh_ref, dres_ref, dx_ref, dxb_ref, dg_ref):
        xv = x_ref[...]
        dy = dh_ref[...].astype(f32)
        r = lax.rsqrt(jnp.mean(xv * xv, axis=-1, keepdims=True) + EPS)
        gy = dy * g_ref[...]
        dot = jnp.mean(xv * gy, axis=-1, keepdims=True)
        dx = dres_ref[...] + r * gy - xv * (r * r * r * dot)
        dx_ref[...] = dx
        dxb_ref[...] = dx.astype(bf16)

        @pl.when(pl.program_id(0) == 0)
        def _():
            dg_ref[...] = jnp.zeros_like(dg_ref)

        dg_ref[...] += jnp.sum(dy * xv * r, axis=0, keepdims=True)

    row = pl.BlockSpec((tm, d), lambda i: (i, 0))
    vec = pl.BlockSpec((1, d), lambda i: (0, 0))
    return pl.pallas_call(
        body, grid=(t // tm,), in_specs=[row, vec, row, row], out_specs=[row, row, vec],
        out_shape=[jax.ShapeDtypeStruct((t, d), f32), jax.ShapeDtypeStruct((t, d), bf16), jax.ShapeDtypeStruct((1, d), f32)],
        compiler_params=_params("arbitrary"), name="rms_bwd")(x, g.reshape(1, d), dh, dres)


def _loss_head(x, g, target):
    t, d = x.shape
    tm = min(t, 256)

    def body(x_ref, g_ref, t_ref, loss_ref, dx_ref, dxb_ref, dg_ref):
        xv = x_ref[...]
        r = lax.rsqrt(jnp.mean(xv * xv, axis=-1, keepdims=True) + EPS)
        xn = xv * r
        err = xn * g_ref[...] - t_ref[...]
        dy = err * (1.0 / d)
        gy = dy * g_ref[...]
        dot = jnp.mean(xv * gy, axis=-1, keepdims=True)
        dx = r * gy - xv * (r * r * r * dot)
        dx_ref[...] = dx
        dxb_ref[...] = dx.astype(bf16)

        @pl.when(pl.program_id(0) == 0)
        def _():
            dg_ref[...] = jnp.zeros_like(dg_ref)
            loss_ref[...] = jnp.zeros_like(loss_ref)

        dg_ref[...] += jnp.sum(dy * xn, axis=0, keepdims=True)
        loss_ref[...] += 0.5 * jnp.sum(jnp.sum(err * err, axis=-1, keepdims=True) * (1.0 / d), axis=0, keepdims=True)

    row = pl.BlockSpec((tm, d), lambda i: (i, 0))
    vec = pl.BlockSpec((1, d), lambda i: (0, 0))
    one = pl.BlockSpec((1, 1), lambda i: (0, 0))
    return pl.pallas_call(
        body, grid=(t // tm,), in_specs=[row, vec, row], out_specs=[one, row, row, vec],
        out_shape=[jax.ShapeDtypeStruct((1, 1), f32), jax.ShapeDtypeStruct((t, d), f32),
                   jax.ShapeDtypeStruct((t, d), bf16), jax.ShapeDtypeStruct((1, d), f32)],
        compiler_params=_params("arbitrary"), name="loss_head")(x, g.reshape(1, d), target)


def _attn_consts():
    qt, kw = 2 * CHUNK, (A_LEFT_CHUNKS + 2) * CHUNK
    r = np.arange(qt)[:, None]
    kc = np.arange(kw)[None, :]
    rel = np.clip(r + A_LEFT_CHUNKS * CHUNK - kc, -A_MAX_REL, A_MAX_REL) + A_MAX_REL
    dchunk = kc // CHUNK - r // CHUNK
    valid = (dchunk >= 0) & (dchunk <= A_LEFT_CHUNKS)
    m = np.arange(kw + qt)
    relidx = np.clip(A_LEFT_CHUNKS * CHUNK - (m - (qt - 1)), -A_MAX_REL, A_MAX_REL) + A_MAX_REL
    onehot = np.zeros((kw + qt, 2 * A_MAX_REL + 1), np.float32)
    onehot[m, relidx] = 1.0
    return qt, kw, rel, valid, onehot


def _bias_table(rel_bias):
    qt, kw, _, valid, onehot = _attn_consts()
    h = rel_bias.shape[0]
    w = kw + qt
    relidx = np.argmax(onehot, axis=1)
    e = jnp.roll(jnp.take(rel_bias, jnp.asarray(relidx), axis=1), -(qt - 1), axis=1)
    rows = jnp.broadcast_to(e[:, None, :], (h, qt, w)).reshape(h, qt * w)
    skew = rows[:, :qt * (w - 1)].reshape(h, qt, w - 1)[:, :, :kw]
    return jnp.where(jnp.asarray(valid)[None], skew, NEG_INF).astype(f32)


def _bias_table_grad(dtab):
    qt, kw, _, _, onehot = _attn_consts()
    h = dtab.shape[0]
    w = kw + qt
    xp = jnp.pad(dtab[:, ::-1, :], ((0, 0), (0, 0), (0, w + 1 - kw)))
    skew = xp.reshape(h, qt * (w + 1))[:, :qt * w].reshape(h, qt, w)
    de = jnp.sum(skew, axis=1)
    return jnp.dot(de, jnp.asarray(onehot), precision=lax.Precision.HIGHEST)


def _attn_scores(q_ref, kpad, btab_ref, r0, qt, kw, pad):
    qv = q_ref[pl.ds(r0, qt), :]
    kwin = kpad[pl.ds(r0, kw), :]
    s = lax.dot_general(qv, kwin, _DN["nt"], preferred_element_type=f32) * (A_HEAD_DIM ** -0.5) + btab_ref[...]
    kcol = lax.broadcasted_iota(jnp.int32, (qt, kw), 1)
    s = jnp.where(r0 + kcol >= pad, s, NEG_INF)
    p = jnp.exp(s - jnp.max(s, axis=-1, keepdims=True))
    return qv, kwin, p / jnp.sum(p, axis=-1, keepdims=True)


def _attn_fwd(proj, btab, heads):
    t = proj.shape[0]
    qt, kw = btab.shape[1], btab.shape[2]
    pad = kw - qt

    def body(q_ref, k_ref, v_ref, btab_ref, o_ref, kpad, vpad):
        zeros = jnp.zeros((pad, A_HEAD_DIM), bf16)
        kpad[pl.ds(0, pad), :] = zeros
        vpad[pl.ds(0, pad), :] = zeros
        kpad[pl.ds(pad, t), :] = k_ref[...]
        vpad[pl.ds(pad, t), :] = v_ref[...]

        def tile(i, carry):
            r0 = pl.multiple_of(i * qt, qt)
            _, _, p = _attn_scores(q_ref, kpad, btab_ref, r0, qt, kw, pad)
            o = lax.dot_general(p.astype(bf16), vpad[pl.ds(r0, kw), :], _DN["nn"], preferred_element_type=f32)
            o_ref[pl.ds(r0, qt), :] = o.astype(bf16)
            return carry

        lax.fori_loop(0, t // qt, tile, 0)

    col = lambda off: pl.BlockSpec((t, A_HEAD_DIM), lambda h, off=off: (0, off + h))
    return pl.pallas_call(
        body, grid=(heads,),
        in_specs=[col(0), col(heads), col(2 * heads), pl.BlockSpec((None, qt, kw), lambda h: (h, 0, 0))],
        out_specs=col(0), out_shape=jax.ShapeDtypeStruct((t, heads * A_HEAD_DIM), bf16),
        scratch_shapes=[pltpu.VMEM((t + pad, A_HEAD_DIM), bf16)] * 2,
        compiler_params=_params("parallel"), name="attn_fwd")(proj, proj, proj, btab)


def _attn_bwd(proj, dmix, btab, heads):
    t = proj.shape[0]
    qt, kw = btab.shape[1], btab.shape[2]
    pad = kw - qt
    scale = A_HEAD_DIM ** -0.5

    def body(q_ref, k_ref, v_ref, do_ref, btab_ref, dq_ref, dk_ref, dv_ref, dtab_ref, kpad, vpad, dkacc, dvacc):
        zeros = jnp.zeros((pad, A_HEAD_DIM), bf16)
        kpad[pl.ds(0, pad), :] = zeros
        vpad[pl.ds(0, pad), :] = zeros
        kpad[pl.ds(pad, t), :] = k_ref[...]
        vpad[pl.ds(pad, t), :] = v_ref[...]
        dkacc[...] = jnp.zeros_like(dkacc)
        dvacc[...] = jnp.zeros_like(dvacc)
        dtab_ref[...] = jnp.zeros_like(dtab_ref)

        def tile(i, carry):
            r0 = pl.multiple_of(i * qt, qt)
            qv, kwin, p = _attn_scores(q_ref, kpad, btab_ref, r0, qt, kw, pad)
            dov = do_ref[pl.ds(r0, qt), :]
            dp = lax.dot_general(dov, vpad[pl.ds(r0, kw), :], _DN["nt"], preferred_element_type=f32)
            ds = p * (dp - jnp.sum(p * dp, axis=-1, keepdims=True))
            dtab_ref[...] += ds
            dsb = ds.astype(bf16)
            dq = lax.dot_general(dsb, kwin, _DN["nn"], preferred_element_type=f32) * scale
            dq_ref[pl.ds(r0, qt), :] = dq.astype(bf16)
            dkacc[pl.ds(r0, kw), :] += lax.dot_general(dsb, qv, _DN["tn"], preferred_element_type=f32) * scale
            dvacc[pl.ds(r0, kw), :] += lax.dot_general(p.astype(bf16), dov, _DN["tn"], preferred_element_type=f32)
            return carry

        lax.fori_loop(0, t // qt, tile, 0)
        dk_ref[...] = dkacc[pl.ds(pad, t), :].astype(bf16)
        dv_ref[...] = dvacc[pl.ds(pad, t), :].astype(bf16)

    col = lambda off: pl.BlockSpec((t, A_HEAD_DIM), lambda h, off=off: (0, off + h))
    tab = pl.BlockSpec((None, qt, kw), lambda h: (h, 0, 0))
    wide = jax.ShapeDtypeStruct((t, heads * A_HEAD_DIM), bf16)
    return pl.pallas_call(
        body, grid=(heads,), in_specs=[col(0), col(heads), col(2 * heads), col(0), tab],
        out_specs=[col(0), col(0), col(0), tab],
        out_shape=[wide, wide, wide, jax.ShapeDtypeStruct((heads, qt, kw), f32)],
        scratch_shapes=[pltpu.VMEM((t + pad, A_HEAD_DIM), bf16)] * 2 + [pltpu.VMEM((t + pad, A_HEAD_DIM), f32)] * 2,
        compiler_params=_params("parallel"), name="attn_bwd")(proj, proj, proj, dmix, btab)


def _shift_down(z, k):
    rows = lax.broadcasted_iota(jnp.int32, z.shape, 0)
    return jnp.where(rows >= k, pltpu.roll(z, k, 0), 0.0)


def _shift_up(z, k):
    t = z.shape[0]
    rows = lax.broadcasted_iota(jnp.int32, z.shape, 0)
    return jnp.where(rows < t - k, pltpu.roll(z, t - k, 0), 0.0)


def _conv_fwd(proj, conv_w, a_blocks, b_blocks):
    t = proj.shape[0]

    def body(b_ref, c_ref, h_ref, w_ref, o_ref):
        z = c_ref[...].astype(f32) * h_ref[...].astype(f32)
        w = w_ref[...]
        y = w[0:1, :] * _shift_down(z, 2) + w[1:2, :] * _shift_down(z, 1) + w[2:3, :] * z
        o_ref[...] = (b_ref[...].astype(f32) * y).astype(bf16)

    col = lambda off: pl.BlockSpec((t, LANE), lambda i, off=off: (0, off + i))
    return pl.pallas_call(
        body, grid=(b_blocks,),
        in_specs=[col(3 * a_blocks), col(3 * a_blocks + b_blocks), col(3 * a_blocks + 2 * b_blocks),
                  pl.BlockSpec((CONV_WIDTH, LANE), lambda i: (0, i))],
        out_specs=col(0), out_shape=jax.ShapeDtypeStruct((t, b_blocks * LANE), bf16),
        compiler_params=_params("parallel"), name="conv_fwd")(proj, proj, proj, conv_w)


def _conv_bwd(proj, dmix, conv_w, a_blocks, b_blocks):
    t = proj.shape[0]

    def body(b_ref, c_ref, h_ref, do_ref, w_ref, db_ref, dc_ref, dh_ref, dw_ref):
        bv, cv, hv = b_ref[...].astype(f32), c_ref[...].astype(f32), h_ref[...].astype(f32)
        w = w_ref[...]
        z = cv * hv
        z1, z2 = _shift_down(z, 1), _shift_down(z, 2)
        y = w[0:1, :] * z2 + w[1:2, :] * z1 + w[2:3, :] * z
        dov = do_ref[...].astype(f32)
        db_ref[...] = (dov * y).astype(bf16)
        dy = dov * bv
        dz = w[2:3, :] * dy + w[1:2, :] * _shift_up(dy, 1) + w[0:1, :] * _shift_up(dy, 2)
        dc_ref[...] = (dz * hv).astype(bf16)
        dh_ref[...] = (dz * cv).astype(bf16)
        dw_ref[0:1, :] = jnp.sum(dy * z2, axis=0, keepdims=True)
        dw_ref[1:2, :] = jnp.sum(dy * z1, axis=0, keepdims=True)
        dw_ref[2:3, :] = jnp.sum(dy * z, axis=0, keepdims=True)

    col = lambda off: pl.BlockSpec((t, LANE), lambda i, off=off: (0, off + i))
    wspec = pl.BlockSpec((CONV_WIDTH, LANE), lambda i: (0, i))
    wide = jax.ShapeDtypeStruct((t, b_blocks * LANE), bf16)
    return pl.pallas_call(
        body, grid=(b_blocks,),
        in_specs=[col(3 * a_blocks), col(3 * a_blocks + b_blocks), col(3 * a_blocks + 2 * b_blocks), col(a_blocks), wspec],
        out_specs=[col(0), col(0), col(0), wspec],
        out_shape=[wide, wide, wide, jax.ShapeDtypeStruct((CONV_WIDTH, b_blocks * LANE), f32)],
        compiler_params=_params("parallel"), name="conv_bwd")(proj, proj, proj, dmix, conv_w)


_RSQRT2 = 0.7071067811865476
_RSQRT2PI = 0.3989422804014327


def _gelu(x):
    return 0.5 * x * (1.0 + lax.erf(x * _RSQRT2))


def _gelu_grad(x):
    return 0.5 * (1.0 + lax.erf(x * _RSQRT2)) + x * jnp.exp(-0.5 * x * x) * _RSQRT2PI


def _sgu_common(a_ref, lg_ref, lb_ref, cw):
    av = a_ref[...]
    u = _gelu(av[:, :cw])
    v = _gelu(av[:, cw:])
    mu = jnp.mean(v, axis=-1, keepdims=True)
    xc = v - mu
    rstd = lax.rsqrt(jnp.mean(xc * xc, axis=-1, keepdims=True) + EPS)
    xhat = xc * rstd
    vln = xhat * lg_ref[...] + lb_ref[...]
    pos_t = lax.broadcasted_iota(jnp.int32, (C_BLOCK, C_BLOCK), 0) // CHUNK
    pos_s = lax.broadcasted_iota(jnp.int32, (C_BLOCK, C_BLOCK), 1) // CHUNK
    return av, u, xhat, rstd, vln, pos_s <= pos_t


def _sgu_fwd(a, ln_g, ln_b, w_s, bs_t):
    t, cw2 = a.shape
    cw = cw2 // 2
    groups = w_s.shape[0]
    cg = cw // groups

    def body(a_ref, lg_ref, lb_ref, ws_ref, bs_ref, m_ref):
        _, u, _, _, vln, mask = _sgu_common(a_ref, lg_ref, lb_ref, cw)
        vb = vln.astype(bf16)
        for g in range(groups):
            sl = slice(g * cg, (g + 1) * cg)
            wm = jnp.where(mask, ws_ref[g], 0.0).astype(bf16)
            s = lax.dot_general(wm, vb[:, sl], _DN["nn"], preferred_element_type=f32) + bs_ref[:, g:g + 1]
            m_ref[:, sl] = (u[:, sl] * s).astype(bf16)

    vec = pl.BlockSpec((1, cw), lambda n: (0, 0))
    return pl.pallas_call(
        body, grid=(t // C_BLOCK,),
        in_specs=[pl.BlockSpec((C_BLOCK, cw2), lambda n: (n, 0)), vec, vec,
                  pl.BlockSpec((groups, C_BLOCK, C_BLOCK), lambda n: (0, 0, 0)),
                  pl.BlockSpec((C_BLOCK, groups), lambda n: (0, 0))],
        out_specs=pl.BlockSpec((C_BLOCK, cw), lambda n: (n, 0)), out_shape=jax.ShapeDtypeStruct((t, cw), bf16),
        compiler_params=_params("parallel"), name="sgu_fwd")(a, ln_g.reshape(1, cw), ln_b.reshape(1, cw), w_s, bs_t)


def _sgu_bwd(a, dm, ln_g, ln_b, w_s, bs_t):
    t, cw2 = a.shape
    cw = cw2 // 2
    groups = w_s.shape[0]
    cg = cw // groups

    def body(a_ref, dm_ref, lg_ref, lb_ref, ws_ref, bs_ref, da_ref, dws_ref, dbs_ref, dlg_ref, dlb_ref, dvln):
        @pl.when(pl.program_id(0) == 0)
        def _():
            dws_ref[...] = jnp.zeros_like(dws_ref)
            dbs_ref[...] = jnp.zeros_like(dbs_ref)
            dlg_ref[...] = jnp.zeros_like(dlg_ref)
            dlb_ref[...] = jnp.zeros_like(dlb_ref)

        av, u, xhat, rstd, vln, mask = _sgu_common(a_ref, lg_ref, lb_ref, cw)
        vb = vln.astype(bf16)
        lane = lax.broadcasted_iota(jnp.int32, (C_BLOCK, groups), 1)
        dbs = jnp.zeros((C_BLOCK, groups), f32)
        for g in range(groups):
            sl = slice(g * cg, (g + 1) * cg)
            wm = jnp.where(mask, ws_ref[g], 0.0).astype(bf16)
            s = lax.dot_general(wm, vb[:, sl], _DN["nn"], preferred_element_type=f32) + bs_ref[:, g:g + 1]
            dmg = dm_ref[:, sl].astype(f32)
            da_ref[:, sl] = (dmg * s * _gelu_grad(av[:, sl])).astype(bf16)
            dsg = dmg * u[:, sl]
            dbs = dbs + jnp.where(lane == g, jnp.sum(dsg, axis=-1, keepdims=True), 0.0)
            dsb = dsg.astype(bf16)
            dws_ref[g] += jnp.where(mask, lax.dot_general(dsb, vb[:, sl], _DN["nt"], preferred_element_type=f32), 0.0)
            dvln[:, sl] = lax.dot_general(wm, dsb, _DN["tn"], preferred_element_type=f32)
        dbs_ref[...] += dbs
        dv = dvln[...]
        dlg_ref[...] += jnp.sum(dv * xhat, axis=0, keepdims=True)
        dlb_ref[...] += jnp.sum(dv, axis=0, keepdims=True)
        dxh = dv * lg_ref[...]
        dvv = rstd * (dxh - jnp.mean(dxh, axis=-1, keepdims=True) - xhat * jnp.mean(dxh * xhat, axis=-1, keepdims=True))
        da_ref[:, cw:] = (dvv * _gelu_grad(av[:, cw:])).astype(bf16)

    vec = pl.BlockSpec((1, cw), lambda n: (0, 0))
    wsp = pl.BlockSpec((groups, C_BLOCK, C_BLOCK), lambda n: (0, 0, 0))
    bsp = pl.BlockSpec((C_BLOCK, groups), lambda n: (0, 0))
    return pl.pallas_call(
        body, grid=(t // C_BLOCK,),
        in_specs=[pl.BlockSpec((C_BLOCK, cw2), lambda n: (n, 0)), pl.BlockSpec((C_BLOCK, cw), lambda n: (n, 0)), vec, vec, wsp, bsp],
        out_specs=[pl.BlockSpec((C_BLOCK, cw2), lambda n: (n, 0)), wsp, bsp, vec, vec],
        out_shape=[jax.ShapeDtypeStruct((t, cw2), bf16), jax.ShapeDtypeStruct(w_s.shape, f32),
                   jax.ShapeDtypeStruct(bs_t.shape, f32), jax.ShapeDtypeStruct((1, cw), f32), jax.ShapeDtypeStruct((1, cw), f32)],
        scratch_shapes=[pltpu.VMEM((C_BLOCK, cw), f32)],
        compiler_params=_params("arbitrary"), name="sgu_bwd")(a, dm, ln_g.reshape(1, cw), ln_b.reshape(1, cw), w_s, bs_t)


HBM = pl.BlockSpec(memory_space=pltpu.HBM)
SEM = pl.BlockSpec(memory_space=pltpu.SEMAPHORE)
EFFECT = pltpu.SideEffectType.DATAFLOW_SIDE_EFFECTING


def _place():
    x, y, c = lax.axis_index("x"), lax.axis_index("y"), lax.axis_index("c")
    return x, y, c, [(1 - x, y), (x, 1 - y), (1 - x, 1 - y)]


def _remote(src, dst, send_sems, recv_sems, k, to):
    return pltpu.make_async_remote_copy(src_ref=src, dst_ref=dst, send_sem=send_sems.at[k], recv_sem=recv_sems.at[k],
                                        device_id=to, device_id_type=MESH)


def _split_start(name, arrays, plan, n_copies, after):
    n = len(arrays)

    def body(*refs):
        send_sems, recv_sems, token = refs[n + 1], refs[n + 2], refs[-1]
        for cp in plan(refs[:n], send_sems, recv_sems):
            cp.start()
        token[...] = jnp.zeros_like(token)

    out = pl.pallas_call(
        body, name=name,
        out_shape=(pltpu.SemaphoreType.DMA((n_copies,)), pltpu.SemaphoreType.DMA((n_copies,)),
                   *[pltpu.HBM(a.shape, a.dtype) for a in arrays], jax.ShapeDtypeStruct((8, LANE), f32)),
        in_specs=[HBM] * n + [ANY], out_specs=(SEM, SEM, *[HBM] * n, pl.BlockSpec(memory_space=pltpu.VMEM)),
        input_output_aliases={i: 2 + i for i in range(n)},
        compiler_params=pltpu.CompilerParams(has_side_effects=EFFECT),
    )(*[pltpu.with_memory_space_constraint(a, pltpu.HBM) for a in arrays], after)
    return (out[0], out[1]), list(out[2:2 + n])


def _split_wait(name, arrays, sems, plan, after):
    n = len(arrays)

    def body(*refs):
        for cp in plan(refs[:n], refs[n], refs[n + 1]):
            cp.wait()

    out = pl.pallas_call(
        body, name=name, out_shape=tuple(pltpu.HBM(a.shape, a.dtype) for a in arrays),
        in_specs=[HBM] * n + [SEM, SEM, ANY], out_specs=tuple([HBM] * n), input_output_aliases={i: i for i in range(n)},
        compiler_params=pltpu.CompilerParams(has_side_effects=EFFECT),
    )(*arrays, sems[0], sems[1], after)
    return list(out)


def _row_pieces(ref_rows, split):
    rc = ref_rows // split
    return [pl.ds(s * rc, rc) for s in range(split)]


def _plan_gather_chips(bufs, send_sems, recv_sems):
    x, y, c, chips = _place()
    me = 4 * x + 2 * y + c
    cps, k = [], 0
    for b in bufs:
        for rows in _row_pieces(b.shape[1], D2D_SPLIT):
            cps.append(_remote(b.at[me, rows], b.at[me, rows], send_sems, recv_sems, k, (x, y, 1 - c)))
            k += 1
        for px, py in chips:
            cps.append(_remote(b.at[me], b.at[me], send_sems, recv_sems, k, (px, py, c)))
            k += 1
    return cps


def _plan_gather_forward(bufs, send_sems, recv_sems):
    x, y, c, chips = _place()
    cps, k = [], 0
    for b in bufs:
        for px, py in chips:
            slot = 4 * px + 2 * py + c
            for rows in _row_pieces(b.shape[1], D2D_SPLIT):
                cps.append(_remote(b.at[slot, rows], b.at[slot, rows], send_sems, recv_sems, k, (x, y, 1 - c)))
                k += 1
    return cps


def _plan_rs_sibling(arrs, send_sems, recv_sems):
    n = len(arrs) // 2
    x, y, c, _ = _place()
    cps, k = [], 0
    for g, got in zip(arrs[:n], arrs[n:]):
        for q in range(N_CHIP):
            for rows in _row_pieces(g.shape[1], RS_SPLIT):
                cps.append(_remote(g.at[N_CHIP * (1 - c) + q, rows], got.at[q, rows], send_sems, recv_sems, k, (x, y, 1 - c)))
                k += 1
    return cps


def _plan_rs_chips(arrs, send_sems, recv_sems):
    n = len(arrs) // 2
    x, y, c, chips = _place()
    q = 2 * x + y
    cps, k = [], 0
    for p, r in zip(arrs[:n], arrs[n:]):
        for px, py in chips:
            cps.append(_remote(p.at[2 * px + py], r.at[q], send_sems, recv_sems, k, (px, py, c)))
            k += 1
    return cps


class _Gather:
    def __init__(self, tag, bufs):
        self.tag, self.bufs, self.n = tag, bufs, len(bufs)

    def start_chips(self, after):
        self.sems, self.bufs = _split_start("gather_chips_start_" + self.tag, self.bufs, _plan_gather_chips,
                                            self.n * (D2D_SPLIT + 3), after)

    def start_forward(self, after):
        self.bufs = _split_wait("gather_chips_wait_" + self.tag, self.bufs, self.sems, _plan_gather_chips, after)
        self.sems, self.bufs = _split_start("gather_fwd_start_" + self.tag, self.bufs, _plan_gather_forward,
                                            self.n * 3 * D2D_SPLIT, after)

    def ready(self, after):
        return _split_wait("gather_fwd_wait_" + self.tag, self.bufs, self.sems, _plan_gather_forward, after)


class _ReduceScatter:
    def __init__(self, tag, grads, core):
        self.tag, self.n = tag, len(grads)
        lands = [lax.empty((N_CHIP,) + g.shape[1:], g.dtype) for g in grads]
        self.sems, self.arrs = _split_start("rs_sibling_start_" + tag, list(grads) + lands, _plan_rs_sibling,
                                            self.n * N_CHIP * RS_SPLIT, core)

    def middle(self, after, core):
        arrs = _split_wait("rs_sibling_wait_" + self.tag, self.arrs, self.sems, _plan_rs_sibling, after)
        parts = [_pair_sum(g, got, core) for g, got in zip(arrs[:self.n], arrs[self.n:])]
        lands = [lax.empty(p.shape, p.dtype) for p in parts]
        self.sems, self.arrs = _split_start("rs_chips_start_" + self.tag, parts + lands, _plan_rs_chips, self.n * 3, core)

    def finish(self, after):
        arrs = _split_wait("rs_chips_wait_" + self.tag, self.arrs, self.sems, _plan_rs_chips, after)
        return list(zip(arrs[:self.n], arrs[self.n:]))


def _cast_into_slot(name, w, layer, me):
    _, rows, cols = w.shape
    tr = 256 if rows % 256 == 0 else rows

    def body(me_ref, w_ref, o_ref):
        o_ref[...] = w_ref[...].astype(bf16)

    return pl.pallas_call(
        body,
        grid_spec=pltpu.PrefetchScalarGridSpec(
            num_scalar_prefetch=1, grid=(rows // tr,),
            in_specs=[pl.BlockSpec((None, tr, cols), lambda i, me_ref: (layer, i, 0))],
            out_specs=pl.BlockSpec((None, tr, cols), lambda i, me_ref: (me_ref[0], i, 0))),
        out_shape=jax.ShapeDtypeStruct((N_DEV, rows, cols), bf16), compiler_params=_params("parallel"), name=name)(me, w)


def _pair_sum(g, got, core):
    _, rows, cols = g.shape
    tr = 512 if rows % 512 == 0 else rows

    def body(c_ref, a_ref, b_ref, o_ref):
        o_ref[...] = (a_ref[...].astype(f32) + b_ref[...].astype(f32)).astype(bf16)

    spec = pl.BlockSpec((None, tr, cols), lambda q, i, c_ref: (q, i, 0))
    return pl.pallas_call(
        body,
        grid_spec=pltpu.PrefetchScalarGridSpec(
            num_scalar_prefetch=1, grid=(N_CHIP, rows // tr),
            in_specs=[pl.BlockSpec((None, tr, cols), lambda q, i, c_ref: (N_CHIP * c_ref[0] + q, i, 0)), spec],
            out_specs=spec),
        out_shape=jax.ShapeDtypeStruct((N_CHIP, rows, cols), bf16), compiler_params=_params("parallel", "parallel"),
        name="pair_sum")(core, g, got)


def _gather_copies(n, ins, outs, send_sems, recv_sems, local_sems):
    x, y, c, chips = _place()
    sibling = (x, y, 1 - c)

    def slot(px, py, pc):
        return 4 * px + 2 * py + pc

    def copy(i, k, block, to, src=None):
        dst = outs[i].at[slot(*block)]
        return pltpu.make_async_remote_copy(src_ref=dst if src is None else src, dst_ref=dst, send_sem=send_sems.at[i, k],
                                            recv_sem=recv_sems.at[i, k], device_id=to, device_id_type=MESH)

    started = []
    for i in range(n):
        mine = pltpu.make_async_copy(ins[i], outs[i].at[slot(x, y, c)], local_sems.at[i])
        mine.start()
        started.append(mine)
    sends = []
    for i in range(n):
        sends.append(copy(i, 0, (x, y, c), sibling, src=ins[i]))
        sends += [copy(i, 1 + j, (x, y, c), (*chip, c), src=ins[i]) for j, chip in enumerate(chips)]
    for cp in sends:
        cp.start()
    for i in range(n):
        for j, chip in enumerate(chips):
            copy(i, 1 + j, (*chip, c), (x, y, c)).wait_recv()
            fwd = copy(i, 4 + j, (*chip, c), sibling)
            fwd.start()
            sends.append(fwd)
    for i in range(n):
        copy(i, 0, sibling, (x, y, c)).wait_recv()
        for j, chip in enumerate(chips):
            copy(i, 4 + j, (*chip, 1 - c), (x, y, c)).wait_recv()
    for cp in sends:
        cp.wait_send()
    for mine in started:
        mine.wait()


def _gather_small(name, packed, reduce):
    rows = packed.shape[0]

    def body(x_ref, o_ref, buf, send_sems, recv_sems, local_sems):
        _gather_copies(1, [x_ref], [buf], send_sems, recv_sems, local_sems)
        if reduce:
            acc = buf[0]
            for j in range(1, N_DEV):
                acc = acc + buf[j]
            o_ref[...] = acc
        else:
            o_ref[...] = buf[...]

    vm = pl.BlockSpec(memory_space=pltpu.VMEM)
    return pl.pallas_call(
        body, in_specs=[vm], out_specs=vm,
        out_shape=jax.ShapeDtypeStruct((rows, LANE) if reduce else (N_DEV, rows, LANE), f32),
        scratch_shapes=[pltpu.VMEM((N_DEV, rows, LANE), f32), pltpu.SemaphoreType.DMA((1, 7)), pltpu.SemaphoreType.DMA((1, 7)),
                        pltpu.SemaphoreType.DMA((1,))],
        compiler_params=pltpu.CompilerParams(vmem_limit_bytes=VMEM_LIMIT), name=name)(packed)


def _pack(arrs):
    flat = jnp.concatenate([a.reshape(-1).astype(f32) for a in arrs])
    rows = -(-flat.shape[0] // (8 * LANE)) * 8
    return jnp.pad(flat, (0, rows * LANE - flat.shape[0])).reshape(rows, LANE)


def _unpack(buf, shapes):
    flat = buf.reshape(-1)
    out, off = [], 0
    for s in shapes:
        n = int(np.prod(s))
        out.append(flat[off:off + n].reshape(s))
        off += n
    return out


def _adam_math(w, g, m, v):
    m2 = ADAM_B1 * m + (1.0 - ADAM_B1) * g
    v2 = ADAM_B2 * v + (1.0 - ADAM_B2) * (g * g)
    m_hat = m2 / (1.0 - ADAM_B1 ** ADAM_STEP)
    v_hat = v2 / (1.0 - ADAM_B2 ** ADAM_STEP)
    delta = -ADAM_LR * (m_hat / (jnp.sqrt(v_hat) + ADAM_EPS) + ADAM_WD * w)
    return delta, m2, v2


def _adam_big(name, w, m, v, parts, chip):
    layers, rows, cols = w.shape
    tr = 256 if rows % 256 == 0 else 64 if rows % 64 == 0 else 8

    def body(chip_ref, w_ref, m_ref, v_ref, *rest):
        p_refs = rest[:N_CHIP * layers]
        g_ref, d_ref, m2_ref, v2_ref = rest[N_CHIP * layers:]
        for li in range(layers):
            @pl.when(pl.program_id(0) == li)
            def _(li=li):
                g = p_refs[N_CHIP * li][...].astype(f32)
                for q in range(1, N_CHIP):
                    g = g + p_refs[N_CHIP * li + q][...].astype(f32)
                delta, m2, v2 = _adam_math(w_ref[...], g, m_ref[...], v_ref[...])
                g_ref[...] = g
                d_ref[...] = delta
                m2_ref[...] = m2
                v2_ref[...] = v2

    spec = pl.BlockSpec((None, tr, cols), lambda l, i, c_ref: (l, i, 0))
    pspecs, operands = [], []
    for li in range(layers):
        for q in range(N_CHIP):
            pspecs.append(pl.BlockSpec((None, tr, cols),
                                       lambda l, i, c_ref, li=li, q=q: ((c_ref[0] + q) % N_CHIP, jnp.where(l == li, i, 0), 0)))
            operands.append(parts[li][0] if q == 0 else parts[li][1])
    out = jax.ShapeDtypeStruct((layers, rows, cols), f32)
    return pl.pallas_call(
        body,
        grid_spec=pltpu.PrefetchScalarGridSpec(num_scalar_prefetch=1, grid=(layers, rows // tr),
                                               in_specs=[spec, spec, spec] + pspecs, out_specs=[spec] * 4),
        out_shape=[out] * 4, compiler_params=_params("arbitrary", "arbitrary"), name=name)(chip, w, m, v, *operands)


def _adam_small(w, g, m, v):
    rows = w.shape[0]

    def body(w_ref, g_ref, m_ref, v_ref, d_ref, m2_ref, v2_ref):
        delta, m2, v2 = _adam_math(w_ref[...], g_ref[...], m_ref[...], v_ref[...])
        d_ref[...] = delta
        m2_ref[...] = m2
        v2_ref[...] = v2

    out = jax.ShapeDtypeStruct((rows, LANE), f32)
    return pl.pallas_call(body, out_shape=[out] * 3, name="adam_small")(w, g, m, v)


def kernel(x, mix_norm, ab_w_in, ab_rel_bias, ab_conv_w, ab_w_out, c_w_in, c_ln_g, c_ln_b, c_w_s, c_b_s, c_w_out, ffn_norm, ffn_w_gate, ffn_w_up, ffn_w_down, final_norm, loss_target, m_mix_norm, m_ab_w_in, m_ab_rel_bias, m_ab_conv_w, m_ab_w_out, m_c_w_in, m_c_ln_g, m_c_ln_b, m_c_w_s, m_c_b_s, m_c_w_out, m_ffn_norm, m_ffn_w_gate, m_ffn_w_up, m_ffn_w_down, m_final_norm, v_mix_norm, v_ab_w_in, v_ab_rel_bias, v_ab_conv_w, v_ab_w_out, v_c_w_in, v_c_ln_g, v_c_ln_b, v_c_w_s, v_c_b_s, v_c_w_out, v_ffn_norm, v_ffn_w_gate, v_ffn_w_up, v_ffn_w_down, v_final_norm):
    d = D_MODEL
    a_width = d // 2
    heads = a_width // A_HEAD_DIM
    a_blocks = a_width // LANE
    b_blocks = (d - a_width) // LANE
    n_even, n_odd = (DEPTH + 1) // 2, DEPTH // 2
    me_s = 4 * lax.axis_index("x") + 2 * lax.axis_index("y") + lax.axis_index("c")
    me = me_s.astype(jnp.int32).reshape(1)
    core = lax.axis_index("c").astype(jnp.int32).reshape(1)
    chip = (2 * lax.axis_index("x") + lax.axis_index("y")).astype(jnp.int32).reshape(1)

    weights = dict(mix_norm=mix_norm, ab_w_in=ab_w_in, ab_rel_bias=ab_rel_bias, ab_conv_w=ab_conv_w, ab_w_out=ab_w_out,
                   c_w_in=c_w_in, c_ln_g=c_ln_g, c_ln_b=c_ln_b, c_w_s=c_w_s, c_b_s=c_b_s, c_w_out=c_w_out,
                   ffn_norm=ffn_norm, ffn_w_gate=ffn_w_gate, ffn_w_up=ffn_w_up, ffn_w_down=ffn_w_down, final_norm=final_norm)
    mom_m = dict(mix_norm=m_mix_norm, ab_w_in=m_ab_w_in, ab_rel_bias=m_ab_rel_bias, ab_conv_w=m_ab_conv_w, ab_w_out=m_ab_w_out,
                 c_w_in=m_c_w_in, c_ln_g=m_c_ln_g, c_ln_b=m_c_ln_b, c_w_s=m_c_w_s, c_b_s=m_c_b_s, c_w_out=m_c_w_out,
                 ffn_norm=m_ffn_norm, ffn_w_gate=m_ffn_w_gate, ffn_w_up=m_ffn_w_up, ffn_w_down=m_ffn_w_down, final_norm=m_final_norm)
    mom_v = dict(mix_norm=v_mix_norm, ab_w_in=v_ab_w_in, ab_rel_bias=v_ab_rel_bias, ab_conv_w=v_ab_conv_w, ab_w_out=v_ab_w_out,
                 c_w_in=v_c_w_in, c_ln_g=v_c_ln_g, c_ln_b=v_c_ln_b, c_w_s=v_c_w_s, c_b_s=v_c_b_s, c_w_out=v_c_w_out,
                 ffn_norm=v_ffn_norm, ffn_w_gate=v_ffn_w_gate, ffn_w_up=v_ffn_w_up, ffn_w_down=v_ffn_w_down, final_norm=v_final_norm)
    order = list(weights)

    sharded_small = [ab_conv_w, c_ln_g, c_ln_b]
    gathered = _gather_small("gather_small", _pack(sharded_small), reduce=False)
    conv_parts, lng_parts, lnb_parts = [], [], []
    for j in range(N_DEV):
        cw_j, lg_j, lb_j = _unpack(gathered[j], [a.shape for a in sharded_small])
        conv_parts.append(cw_j)
        lng_parts.append(lg_j)
        lnb_parts.append(lb_j)
    conv_full = jnp.concatenate(conv_parts, axis=-1)
    lng_full = jnp.concatenate(lng_parts, axis=-1)
    lnb_full = jnp.concatenate(lnb_parts, axis=-1)

    def cast(w, li):
        return _cast_into_slot("cast_slot", w, li, me)

    units = []
    for layer in range(DEPTH):
        i = layer // 2
        if layer % 2 == 0:
            units.append(_Gather("ab%d" % i, [cast(ab_w_in, i), cast(ab_w_out, i)]))
        else:
            units.append(_Gather("c%d" % i, [cast(c_w_in, i), cast(c_w_out, i)]))
        units.append(_Gather("ffn_in%d" % layer, [cast(ffn_w_gate, layer), cast(ffn_w_up, layer)]))
        units.append(_Gather("ffn_out%d" % layer, [cast(ffn_w_down, layer)]))
    cursor = [0]

    def next_weights(after):
        k = cursor[0]
        cursor[0] = k + 1
        units[k].start_forward(after)
        if k + 2 < len(units):
            units[k + 2].start_chips(after)
        return units[k].ready(after)

    xs = x[0]
    tgt = loss_target[0]
    units[0].start_chips(xs)
    units[1].start_chips(xs)
    saved = []
    for layer in range(DEPTH):
        i = layer // 2
        w_in_g, w_out_g = next_weights(xs)
        w_out_full = w_out_g.reshape(-1, w_out_g.shape[-1])
        h = _rms_fwd(xs, mix_norm[layer])
        if layer % 2 == 0:
            proj = _mm_cols("ab_proj", h, w_in_g, bf16)
            btab = _bias_table(ab_rel_bias[i])
            attn = _attn_fwd(proj, btab, heads)
            conv = _conv_fwd(proj, conv_full[i], a_blocks, b_blocks)
            mixed = jnp.concatenate([attn, conv], axis=-1)
            ctx = (proj, btab)
        else:
            proj = _mm_cols("c_proj", h, w_in_g, f32)
            bs_t = jnp.transpose(c_b_s[i])
            mixed = _sgu_fwd(proj, lng_full[i], lnb_full[i], c_w_s[i], bs_t)
            ctx = (proj, bs_t)
        x1 = _mm_rows_res("mix_out", mixed, w_out_full, xs)
        wg_g, wu_g = next_weights(x1)
        h2 = _rms_fwd(x1, ffn_norm[layer])
        g_act, u_act, act = _ffn_in(h2, wg_g, wu_g)
        (wd_g,) = next_weights(g_act)
        x2 = _ffn_down(act, wd_g, x1)
        saved.append((xs, h, ctx, mixed, x1, h2, g_act, u_act, w_in_g, w_out_full, wg_g, wu_g, wd_g))
        xs = x2

    loss_part, dx, dxb, d_final = _loss_head(xs, final_norm, tgt)
    loss = lax.psum(loss_part[0, 0], ("x", "y", "c"))

    scatters = {}
    small = {k: [None] * weights[k].shape[0] for k in ("mix_norm", "ffn_norm", "ab_rel_bias", "ab_conv_w", "c_ln_g", "c_ln_b",
                                                       "c_w_s", "c_b_s")}
    for layer in reversed(range(DEPTH)):
        i = layer // 2
        xs, h, ctx, mixed, x1, h2, g_act, u_act, w_in_g, w_out_full, wg_g, wu_g, wd_g = saved[layer]
        dg, du, act = _ffn_bwd_act(dxb, wd_g, g_act, u_act)
        dwd = _ffn_dwd(act, dxb)
        dwg, dwu = _ffn_dwgu(h2, dg, du)
        rs_ffn = _ReduceScatter("ffn%d" % layer, [dwg, dwu, dwd], core)
        dh2 = _ffn_dh(dg, du, wg_g, wu_g)
        dx, dxb, dgn = _rms_bwd(x1, ffn_norm[layer], dh2, dx)
        small["ffn_norm"][layer] = dgn[0]
        rs_ffn.middle(dxb, core)
        for pos, k in enumerate(("ffn_w_gate", "ffn_w_up", "ffn_w_down")):
            scatters[(k, layer)] = (rs_ffn, pos)
        dmixed = _mm_nt("mix_out_bwd", dxb, w_out_full, bf16)
        dwout = _mm_tn_rows("mix_out_dw", mixed, dxb)
        if layer % 2 == 0:
            proj, btab = ctx
            dq, dk, dv, dtab = _attn_bwd(proj, dmixed, btab, heads)
            db, dc, dhv, dcw = _conv_bwd(proj, dmixed, conv_full[i], a_blocks, b_blocks)
            dproj = jnp.concatenate([dq, dk, dv, db, dc, dhv], axis=-1)
            small["ab_rel_bias"][i] = _bias_table_grad(dtab)
            small["ab_conv_w"][i] = dcw
            names = ("ab_w_in", "ab_w_out")
            tag = "ab"
        else:
            proj, bs_t = ctx
            dproj, dws, dbs_t, dlg, dlb = _sgu_bwd(proj, dmixed, lng_full[i], lnb_full[i], c_w_s[i], bs_t)
            small["c_w_s"][i] = dws
            small["c_b_s"][i] = jnp.transpose(dbs_t)
            small["c_ln_g"][i] = dlg[0]
            small["c_ln_b"][i] = dlb[0]
            names = ("c_w_in", "c_w_out")
            tag = "c"
        dwin = _mm_tn_cols(tag + "_proj_dw", h, dproj)
        rs_mix = _ReduceScatter("%s%d" % (tag, i), [dwin, dwout], core)
        dh =_mm_nt_cols(tag + "_proj_bwd", dproj, w_in_g)
        dx, dxb, dgm = _rms_bwd(xs, mix_norm[layer], dh, dx)
        small["mix_norm"][layer] = dgm[0]
        rs_mix.middle(dxb, core)
        scatters[(names[0], i)] = (rs_mix, 0)
        scatters[(names[1], i)] = (rs_mix, 1)
    grad_x = dx[None]

    small_names = ["mix_norm", "ffn_norm", "ab_rel_bias", "ab_conv_w", "c_ln_g", "c_ln_b", "c_w_s", "c_b_s"]
    small_full = [jnp.stack(small[k]) for k in small_names] + [d_final[0]]
    summed = _unpack(_gather_small("reduce_small", _pack(small_full), reduce=True), [a.shape for a in small_full])
    small_grads = dict(zip(small_names + ["final_norm"], summed))
    for k in ("ab_conv_w", "c_ln_g", "c_ln_b"):
        width = weights[k].shape[-1]
        small_grads[k] = lax.dynamic_slice_in_dim(small_grads[k], me_s * width, width, axis=-1)
    small_order = [k for k in order if k in small_grads]
    shapes = [weights[k].shape for k in small_order]
    d_s, m_s, v_s = _adam_small(_pack([weights[k] for k in small_order]), _pack([small_grads[k] for k in small_order]),
                                _pack([mom_m[k] for k in small_order]), _pack([mom_v[k] for k in small_order]))
    grads, deltas, new_m, new_v = dict(small_grads), {}, {}, {}
    for k, dd, mm, vv in zip(small_order, _unpack(d_s, shapes), _unpack(m_s, shapes), _unpack(v_s, shapes)):
        deltas[k], new_m[k], new_v[k] = dd, mm, vv

    finished = {}
    last = d_s
    for k in ("c_w_in", "c_w_out", "ffn_w_gate", "ffn_w_up", "ffn_w_down", "ab_w_in", "ab_w_out"):
        parts = []
        for li in range(weights[k].shape[0]):
            rs, pos = scatters[(k, li)]
            if id(rs) not in finished:
                finished[id(rs)] = rs.finish(last)
            parts.append(finished[id(rs)][pos])
        grads[k], deltas[k], new_m[k], new_v[k] = _adam_big("adam_" + k, weights[k], mom_m[k], mom_v[k], parts, chip)
        last = deltas[k]

    return (loss, grad_x, *[grads[k] for k in order], *[deltas[k] for k in order], *[new_m[k] for k in order],
            *[new_v[k] for k in order])
```

```python
import numpy as np
import jax
import jax.numpy as jnp
from jax import lax
from jax.experimental import pallas as pl
from jax.experimental.pallas import tpu as pltpu

D_MODEL = 2048
SEQ = 2048
DEPTH = 4
CHUNK = 64
A_HEAD_DIM = 128
A_LEFT_CHUNKS = 8
A_MAX_REL = 256
CONV_WIDTH = 3
C_BLOCK = 128
C_GROUPS = 8
EPS = 1e-6
NEG_INF = -1e30

ADAM_LR = 0.001
ADAM_B1 = 0.9
ADAM_B2 = 0.999
ADAM_EPS = 1e-08
ADAM_WD = 0.01
ADAM_STEP = 10

N_DEV = 8
N_CHIP = 4
RS_SPLIT = 4
D2D_SPLIT = 2
LANE = 128
VMEM_LIMIT = 52 * 1024 * 1024

bf16 = jnp.bfloat16
f32 = jnp.float32
MESH = pl.DeviceIdType.MESH
ANY = pl.BlockSpec(memory_space=pl.ANY)


def _params(*sem):
    return pltpu.CompilerParams(dimension_semantics=sem, vmem_limit_bytes=VMEM_LIMIT)


def _perm(j):
    return (j % 2) * N_CHIP + j // 2


_DN = {"nn": (((1,), (0,)), ((), ())), "nt": (((1,), (1,)), ((), ())), "tn": (((0,), (0,)), ((), ()))}


def _matmul(name, mode, grid, operands, specs, pairs, n_acc, acc_shape, extras, extra_specs, out_shapes, out_specs,
            epilogue):
    nk = grid[2]
    n_op, n_ex, n_out = len(operands), len(extras), len(out_shapes)

    def body(*refs):
        ops = refs[:n_op]
        ex = refs[n_op:n_op + n_ex]
        outs = refs[n_op + n_ex:n_op + n_ex + n_out]
        accs = refs[n_op + n_ex + n_out:]
        k = pl.program_id(2)

        @pl.when(k == 0)
        def _():
            for acc in accs:
                acc[...] = jnp.zeros_like(acc)

        for p, (ia, ib) in enumerate(pairs):
            acc = accs[p % n_acc]
            acc[...] += lax.dot_general(ops[ia][...], ops[ib][...], _DN[mode], preferred_element_type=f32)

        @pl.when(k == nk - 1)
        def _():
            epilogue([acc[...] for acc in accs], ex, outs)

    return pl.pallas_call(
        body, grid=grid, in_specs=list(specs) + list(extra_specs), out_specs=list(out_specs),
        out_shape=list(out_shapes), scratch_shapes=[pltpu.VMEM(acc_shape, f32)] * n_acc,
        compiler_params=_params("parallel", "parallel", "arbitrary"), name=name)(*operands, *extras)


def _store(dtype):
    def ep(accs, ex, outs):
        for a, o in zip(accs, outs):
            o[...] = a.astype(dtype)
    return ep


def _mm_cols(name, h, wg, out_dtype, deps=()):
    t, kd = h.shape
    n8 = wg.shape[2]
    tm = min(t, 512)
    return _matmul(
        name, "nn", (t // tm, N_DEV, 1), [h, wg],
        [pl.BlockSpec((tm, kd), lambda i, j, k: (i, 0)), pl.BlockSpec((None, kd, n8), lambda i, j, k: (j, 0, 0))],
        [(0, 1)], 1, (tm, n8), list(deps), [ANY] * len(deps), [jax.ShapeDtypeStruct((t, N_DEV * n8), out_dtype)],
        [pl.BlockSpec((tm, n8), lambda i, j, k: (i, j))], _store(out_dtype))[0]


def _mm_rows_res(name, a, w, res):
    t, kd = a.shape
    n = w.shape[1]
    tm, tn = min(t, 512), min(n, 1024)

    def ep(accs, ex, outs):
        outs[0][...] = ex[0][...] + accs[0]

    return _matmul(
        name, "nn", (t // tm, n // tn, 1), [a, w],
        [pl.BlockSpec((tm, kd), lambda i, j, k: (i, 0)), pl.BlockSpec((kd, tn), lambda i, j, k: (0, j))],
        [(0, 1)], 1, (tm, tn), [res], [pl.BlockSpec((tm, tn), lambda i, j, k: (i, j))],
        [jax.ShapeDtypeStruct((t, n), f32)], [pl.BlockSpec((tm, tn), lambda i, j, k: (i, j))], ep)[0]


def _mm_nt(name, a, w, out_dtype, deps=()):
    t, n = a.shape
    kd = w.shape[0]
    tm, tn = min(t, 512), min(kd, 1024)
    return _matmul(
        name, "nt", (t // tm, kd // tn, 1), [a, w],
        [pl.BlockSpec((tm, n), lambda i, j, k: (i, 0)), pl.BlockSpec((tn, n), lambda i, j, k: (j, 0))],
        [(0, 1)], 1, (tm, tn), list(deps), [ANY] * len(deps), [jax.ShapeDtypeStruct((t, kd), out_dtype)],
        [pl.BlockSpec((tm, tn), lambda i, j, k: (i, j))], _store(out_dtype))[0]


def _mm_nt_cols(name, da, wg, deps=()):
    t = da.shape[0]
    kd, n8 = wg.shape[1], wg.shape[2]
    tm = min(t, 512)
    return _matmul(
        name, "nt", (t // tm, 1, N_DEV), [da, wg],
        [pl.BlockSpec((tm, n8), lambda i, j, k: (i, k)), pl.BlockSpec((None, kd, n8), lambda i, j, k: (k, 0, 0))],
        [(0, 1)], 1, (tm, kd), list(deps), [ANY] * len(deps), [jax.ShapeDtypeStruct((t, kd), f32)],
        [pl.BlockSpec((tm, kd), lambda i, j, k: (i, 0))], _store(f32))[0]


def _mm_tn_cols(name, h, da):
    t, kd = h.shape
    n8 = da.shape[1] // N_DEV
    tmk, tk = min(kd, 1024), min(t, 1024)
    return _matmul(
        name, "tn", (kd // tmk, N_DEV, t // tk), [h, da],
        [pl.BlockSpec((tk, tmk), lambda i, j, k: (k, i)), pl.BlockSpec((tk, n8), lambda i, j, k: (k, j))],
        [(0, 1)], 1, (tmk, n8), [], [], [jax.ShapeDtypeStruct((N_DEV, kd, n8), bf16)],
        [pl.BlockSpec((None, tmk, n8), lambda i, j, k: (_perm(j), i, 0))], _store(bf16))[0]


def _mm_tn_rows(name, a, dx):
    t, kf = a.shape
    r8 = kf // N_DEV
    n = dx.shape[1]
    return _matmul(
        name, "tn", (N_DEV, 1, 1), [a, dx],
        [pl.BlockSpec((t, r8), lambda i, j, k: (0, i)), pl.BlockSpec((t, n), lambda i, j, k: (0, 0))],
        [(0, 1)], 1, (r8, n), [], [], [jax.ShapeDtypeStruct((N_DEV, r8, n), bf16)],
        [pl.BlockSpec((None, r8, n), lambda i, j, k: (_perm(i), 0, 0))], _store(bf16))[0]


def _ffn_in(h2, wg_t, wu_t, deps=()):
    t, kd = h2.shape
    f8 = wg_t.shape[1]
    tm = min(t, 512)

    def ep(accs, ex, outs):
        g, u = accs
        outs[0][...] = g.astype(bf16)
        outs[1][...] = u.astype(bf16)
        outs[2][...] = (g * jax.nn.sigmoid(g) * u).astype(bf16)

    wspec = pl.BlockSpec((None, f8, kd), lambda i, j, k: (j, 0, 0))
    ospec = pl.BlockSpec((None, tm, f8), lambda i, j, k: (j, i, 0))
    return _matmul(
        "ffn_in", "nt", (t // tm, N_DEV, 1), [h2, wg_t, wu_t],
        [pl.BlockSpec((tm, kd), lambda i, j, k: (i, 0)), wspec, wspec], [(0, 1), (0, 2)], 2, (tm, f8), list(deps),
        [ANY] * len(deps), [jax.ShapeDtypeStruct((N_DEV, t, f8), bf16)] * 3, [ospec] * 3, ep)


def _ffn_down(act, wd, res, deps=()):
    _, t, f8 = act.shape
    n = wd.shape[2]
    tm = min(t, 512)

    def ep(accs, ex, outs):
        outs[0][...] = ex[0][...] + accs[0]

    return _matmul(
        "ffn_down", "nn", (t // tm, 1, N_DEV), [act, wd],
        [pl.BlockSpec((None, tm, f8), lambda i, j, k: (k, i, 0)), pl.BlockSpec((None, f8, n), lambda i, j, k: (k, 0, 0))],
        [(0, 1)], 1, (tm, n), [res] + list(deps), [pl.BlockSpec((tm, n), lambda i, j, k: (i, 0))] + [ANY] * len(deps),
        [jax.ShapeDtypeStruct((t, n), f32)], [pl.BlockSpec((tm, n), lambda i, j, k: (i, 0))], ep)[0]


def _ffn_bwd_act(dxb, wd, g, u, deps=()):
    t, n = dxb.shape
    f8 = wd.shape[1]
    tm = min(t, 512)

    def ep(accs, ex, outs):
        dact = accs[0]
        gv = ex[0][...].astype(f32)
        uv = ex[1][...].astype(f32)
        sg = jax.nn.sigmoid(gv)
        silu = gv * sg
        outs[0][...] = (dact * uv * (sg * (1.0 + gv * (1.0 - sg)))).astype(bf16)
        outs[1][...] = (dact * silu).astype(bf16)
        outs[2][...] = (silu * uv).astype(bf16)

    bspec = pl.BlockSpec((None, tm, f8), lambda i, j, k: (j, i, 0))
    return _matmul(
        "ffn_bwd_act", "nt", (t // tm, N_DEV, 1), [dxb, wd],
        [pl.BlockSpec((tm, n), lambda i, j, k: (i, 0)), pl.BlockSpec((None, f8, n), lambda i, j, k: (j, 0, 0))],
        [(0, 1)], 1, (tm, f8), [g, u] + list(deps), [bspec, bspec] + [ANY] * len(deps),
        [jax.ShapeDtypeStruct((N_DEV, t, f8), bf16)] * 3, [bspec] * 3, ep)


def _ffn_dwd(act, dxb):
    _, t, f8 = act.shape
    n = dxb.shape[1]
    tk = min(t, 1024)
    return _matmul(
        "ffn_dwd", "tn", (N_DEV, 1, t // tk), [act, dxb],
        [pl.BlockSpec((None, tk, f8), lambda i, j, k: (i, k, 0)), pl.BlockSpec((tk, n), lambda i, j, k: (k, 0))],
        [(0, 1)], 1, (f8, n), [], [], [jax.ShapeDtypeStruct((N_DEV, f8, n), bf16)],
        [pl.BlockSpec((None, f8, n), lambda i, j, k: (_perm(i), 0, 0))], _store(bf16))[0]


def _ffn_dwgu(h2, dg, du):
    t, kd = h2.shape
    f8 = dg.shape[2]
    tk = min(t, 1024)
    aspec = pl.BlockSpec((None, tk, f8), lambda i, j, k: (i, k, 0))
    ospec = pl.BlockSpec((None, f8, kd), lambda i, j, k: (_perm(i), 0, 0))
    return _matmul(
        "ffn_dwgu", "tn", (N_DEV, 1, t // tk), [dg, du, h2],
        [aspec, aspec, pl.BlockSpec((tk, kd), lambda i, j, k: (k, 0))], [(0, 2), (1, 2)], 2, (f8, kd), [], [],
        [jax.ShapeDtypeStruct((N_DEV, f8, kd), bf16)] * 2, [ospec] * 2, _store(bf16))


def _ffn_dh(dg, du, wg_t, wu_t, deps=()):
    _, t, f8 = dg.shape
    kd = wg_t.shape[2]
    tm = min(t, 512)
    aspec = pl.BlockSpec((None, tm, f8), lambda i, j, k: (k, i, 0))
    wspec = pl.BlockSpec((None, f8, kd), lambda i, j, k: (k, 0, 0))
    return _matmul(
        "ffn_dh", "nn", (t // tm, 1, N_DEV), [dg, du, wg_t, wu_t], [aspec, aspec, wspec, wspec], [(0, 2), (1, 3)], 1,
        (tm, kd), list(deps), [ANY] * len(deps), [jax.ShapeDtypeStruct((t, kd), f32)],
        [pl.BlockSpec((tm, kd), lambda i, j, k: (i, 0))], _store(f32))[0]


def _rms_fwd(x, g):
    t, d = x.shape
    tm = min(t, 256)

    def body(x_ref, g_ref, o_ref):
        xv = x_ref[...]
        r = lax.rsqrt(jnp.mean(xv * xv, axis=-1, keepdims=True) + EPS)
        o_ref[...] = (xv * r * g_ref[...]).astype(bf16)

    return pl.pallas_call(
        body, grid=(t // tm,), in_specs=[pl.BlockSpec((tm, d), lambda i: (i, 0)), pl.BlockSpec((1, d), lambda i: (0, 0))],
        out_specs=pl.BlockSpec((tm, d), lambda i: (i, 0)), out_shape=jax.ShapeDtypeStruct((t, d), bf16),
        compiler_params=_params("parallel"), name="rms_fwd")(x, g.reshape(1, d))


def _rms_bwd(x, g, dh, dres):
    t, d = x.shape
    tm = min(t, 256)

    def body(x_ref, g_ref, dh_ref, dres_ref, dx_ref, dxb_ref, dg_ref):
        xv = x_ref[...]
        dy = dh_ref[...].astype(f32)
        r = lax.rsqrt(jnp.mean(xv * xv, axis=-1, keepdims=True) + EPS)
        gy = dy * g_ref[...]
        dot = jnp.mean(xv * gy, axis=-1, keepdims=True)
        dx = dres_ref[...] + r * gy - xv * (r * r * r * dot)
        dx_ref[...] = dx
        dxb_ref[...] = dx.astype(bf16)

        @pl.when(pl.program_id(0) == 0)
        def _():
            dg_ref[...] = jnp.zeros_like(dg_ref)

        dg_ref[...] += jnp.sum(dy * xv * r, axis=0, keepdims=True)

    row = pl.BlockSpec((tm, d), lambda i: (i, 0))
    vec = pl.BlockSpec((1, d), lambda i: (0, 0))
    return pl.pallas_call(
        body, grid=(t // tm,), in_specs=[row, vec, row, row], out_specs=[row, row, vec],
        out_shape=[jax.ShapeDtypeStruct((t, d), f32), jax.ShapeDtypeStruct((t, d), bf16), jax.ShapeDtypeStruct((1, d), f32)],
        compiler_params=_params("arbitrary"), name="rms_bwd")(x, g.reshape(1, d), dh, dres)


def _loss_head(x, g, target):
    t, d = x.shape
    tm = min(t, 256)

    def body(x_ref, g_ref, t_ref, loss_ref, dx_ref, dxb_ref, dg_ref):
        xv = x_ref[...]
        r = lax.rsqrt(jnp.mean(xv * xv, axis=-1, keepdims=True) + EPS)
        xn = xv * r
        err = xn * g_ref[...] - t_ref[...]
        dy = err * (1.0 / d)
        gy = dy * g_ref[...]
        dot = jnp.mean(xv * gy, axis=-1, keepdims=True)
        dx = r * gy - xv * (r * r * r * dot)
        dx_ref[...] = dx
        dxb_ref[...] = dx.astype(bf16)

        @pl.when(pl.program_id(0) == 0)
        def _():
            dg_ref[...] = jnp.zeros_like(dg_ref)
            loss_ref[...] = jnp.zeros_like(loss_ref)

        dg_ref[...] += jnp.sum(dy * xn, axis=0, keepdims=True)
        loss_ref[...] += 0.5 * jnp.sum(jnp.sum(err * err, axis=-1, keepdims=True) * (1.0 / d), axis=0, keepdims=True)

    row = pl.BlockSpec((tm, d), lambda i: (i, 0))
    vec = pl.BlockSpec((1, d), lambda i: (0, 0))
    one = pl.BlockSpec((1, 1), lambda i: (0, 0))
    return pl.pallas_call(
        body, grid=(t // tm,), in_specs=[row, vec, row], out_specs=[one, row, row, vec],
        out_shape=[jax.ShapeDtypeStruct((1, 1), f32), jax.ShapeDtypeStruct((t, d), f32),
                   jax.ShapeDtypeStruct((t, d), bf16), jax.ShapeDtypeStruct((1, d), f32)],
        compiler_params=_params("arbitrary"), name="loss_head")(x, g.reshape(1, d), target)


def _attn_consts():
    qt, kw = 2 * CHUNK, (A_LEFT_CHUNKS + 2) * CHUNK
    r = np.arange(qt)[:, None]
    kc = np.arange(kw)[None, :]
    rel = np.clip(r + A_LEFT_CHUNKS * CHUNK - kc, -A_MAX_REL, A_MAX_REL) + A_MAX_REL
    dchunk = kc // CHUNK - r // CHUNK
    valid = (dchunk >= 0) & (dchunk <= A_LEFT_CHUNKS)
    m = np.arange(kw + qt)
    relidx = np.clip(A_LEFT_CHUNKS * CHUNK - (m - (qt - 1)), -A_MAX_REL, A_MAX_REL) + A_MAX_REL
    onehot = np.zeros((kw + qt, 2 * A_MAX_REL + 1), np.float32)
    onehot[m, relidx] = 1.0
    return qt, kw, rel, valid, onehot


def _bias_table(rel_bias):
    qt, kw, _, valid, onehot = _attn_consts()
    h = rel_bias.shape[0]
    w = kw + qt
    relidx = np.argmax(onehot, axis=1)
    e = jnp.roll(jnp.take(rel_bias, jnp.asarray(relidx), axis=1), -(qt - 1), axis=1)
    rows = jnp.broadcast_to(e[:, None, :], (h, qt, w)).reshape(h, qt * w)
    skew = rows[:, :qt * (w - 1)].reshape(h, qt, w - 1)[:, :, :kw]
    return jnp.where(jnp.asarray(valid)[None], skew, NEG_INF).astype(f32)


def _bias_table_grad(dtab):
    qt, kw, _, _, onehot = _attn_consts()
    h = dtab.shape[0]
    w = kw + qt
    xp = jnp.pad(dtab[:, ::-1, :], ((0, 0), (0, 0), (0, w + 1 - kw)))
    skew = xp.reshape(h, qt * (w + 1))[:, :qt * w].reshape(h, qt, w)
    de = jnp.sum(skew, axis=1)
    return jnp.dot(de, jnp.asarray(onehot), precision=lax.Precision.HIGHEST)


def _attn_scores(q_ref, kpad, btab_ref, r0, qt, kw, pad):
    qv = q_ref[pl.ds(r0, qt), :]
    kwin = kpad[pl.ds(r0, kw), :]
    s = lax.dot_general(qv, kwin, _DN["nt"], preferred_element_type=f32) * (A_HEAD_DIM ** -0.5) + btab_ref[...]
    kcol = lax.broadcasted_iota(jnp.int32, (qt, kw), 1)
    s = jnp.where(r0 + kcol >= pad, s, NEG_INF)
    p = jnp.exp(s - jnp.max(s, axis=-1, keepdims=True))
    return qv, kwin, p / jnp.sum(p, axis=-1, keepdims=True)


def _attn_fwd(proj, btab, heads):
    t = proj.shape[0]
    qt, kw = btab.shape[1], btab.shape[2]
    pad = kw - qt

    def body(q_ref, k_ref, v_ref, btab_ref, o_ref, kpad, vpad):
        zeros = jnp.zeros((pad, A_HEAD_DIM), bf16)
        kpad[pl.ds(0, pad), :] = zeros
        vpad[pl.ds(0, pad), :] = zeros
        kpad[pl.ds(pad, t), :] = k_ref[...]
        vpad[pl.ds(pad, t), :] = v_ref[...]

        def tile(i, carry):
            r0 = pl.multiple_of(i * qt, qt)
            _, _, p = _attn_scores(q_ref, kpad, btab_ref, r0, qt, kw, pad)
            o = lax.dot_general(p.astype(bf16), vpad[pl.ds(r0, kw), :], _DN["nn"], preferred_element_type=f32)
            o_ref[pl.ds(r0, qt), :] = o.astype(bf16)
            return carry

        lax.fori_loop(0, t // qt, tile, 0)

    col = lambda off: pl.BlockSpec((t, A_HEAD_DIM), lambda h, off=off: (0, off + h))
    return pl.pallas_call(
        body, grid=(heads,),
        in_specs=[col(0), col(heads), col(2 * heads), pl.BlockSpec((None, qt, kw), lambda h: (h, 0, 0))],
        out_specs=col(0), out_shape=jax.ShapeDtypeStruct((t, heads * A_HEAD_DIM), bf16),
        scratch_shapes=[pltpu.VMEM((t + pad, A_HEAD_DIM), bf16)] * 2,
        compiler_params=_params("parallel"), name="attn_fwd")(proj, proj, proj, btab)


def _attn_bwd(proj, dmix, btab, heads):
    t = proj.shape[0]
    qt, kw = btab.shape[1], btab.shape[2]
    pad = kw - qt
    scale = A_HEAD_DIM ** -0.5

    def body(q_ref, k_ref, v_ref, do_ref, btab_ref, dq_ref, dk_ref, dv_ref, dtab_ref, kpad, vpad, dkacc, dvacc):
        zeros = jnp.zeros((pad, A_HEAD_DIM), bf16)
        kpad[pl.ds(0, pad), :] = zeros
        vpad[pl.ds(0, pad), :] = zeros
        kpad[pl.ds(pad, t), :] = k_ref[...]
        vpad[pl.ds(pad, t), :] = v_ref[...]
        dkacc[...] = jnp.zeros_like(dkacc)
        dvacc[...] = jnp.zeros_like(dvacc)
        dtab_ref[...] = jnp.zeros_like(dtab_ref)

        def tile(i, carry):
            r0 = pl.multiple_of(i * qt, qt)
            qv, kwin, p = _attn_scores(q_ref, kpad, btab_ref, r0, qt, kw, pad)
            dov = do_ref[pl.ds(r0, qt), :]
            dp = lax.dot_general(dov, vpad[pl.ds(r0, kw), :], _DN["nt"], preferred_element_type=f32)
            ds = p * (dp - jnp.sum(p * dp, axis=-1, keepdims=True))
            dtab_ref[...] += ds
            dsb = ds.astype(bf16)
            dq = lax.dot_general(dsb, kwin, _DN["nn"], preferred_element_type=f32) * scale
            dq_ref[pl.ds(r0, qt), :] = dq.astype(bf16)
            dkacc[pl.ds(r0, kw), :] += lax.dot_general(dsb, qv, _DN["tn"], preferred_element_type=f32) * scale
            dvacc[pl.ds(r0, kw), :] += lax.dot_general(p.astype(bf16), dov, _DN["tn"], preferred_element_type=f32)
            return carry

        lax.fori_loop(0, t // qt, tile, 0)
        dk_ref[...] = dkacc[pl.ds(pad, t), :].astype(bf16)
        dv_ref[...] = dvacc[pl.ds(pad, t), :].astype(bf16)

    col = lambda off: pl.BlockSpec((t, A_HEAD_DIM), lambda h, off=off: (0, off + h))
    tab = pl.BlockSpec((None, qt, kw), lambda h: (h, 0, 0))
    wide = jax.ShapeDtypeStruct((t, heads * A_HEAD_DIM), bf16)
    return pl.pallas_call(
        body, grid=(heads,), in_specs=[col(0), col(heads), col(2 * heads), col(0), tab],
        out_specs=[col(0), col(0), col(0), tab],
        out_shape=[wide, wide, wide, jax.ShapeDtypeStruct((heads, qt, kw), f32)],
        scratch_shapes=[pltpu.VMEM((t + pad, A_HEAD_DIM), bf16)] * 2 + [pltpu.VMEM((t + pad, A_HEAD_DIM), f32)] * 2,
        compiler_params=_params("parallel"), name="attn_bwd")(proj, proj, proj, dmix, btab)


def _shift_down(z, k):
    rows = lax.broadcasted_iota(jnp.int32, z.shape, 0)
    return jnp.where(rows >= k, pltpu.roll(z, k, 0), 0.0)


def _shift_up(z, k):
    t = z.shape[0]
    rows = lax.broadcasted_iota(jnp.int32, z.shape, 0)
    return jnp.where(rows < t - k, pltpu.roll(z, t - k, 0), 0.0)


def _conv_fwd(proj, conv_w, a_blocks, b_blocks):
    t = proj.shape[0]

    def body(b_ref, c_ref, h_ref, w_ref, o_ref):
        z = c_ref[...].astype(f32) * h_ref[...].astype(f32)
        w = w_ref[...]
        y = w[0:1, :] * _shift_down(z, 2) + w[1:2, :] * _shift_down(z, 1) + w[2:3, :] * z
        o_ref[...] = (b_ref[...].astype(f32) * y).astype(bf16)

    col = lambda off: pl.BlockSpec((t, LANE), lambda i, off=off: (0, off + i))
    return pl.pallas_call(
        body, grid=(b_blocks,),
        in_specs=[col(3 * a_blocks), col(3 * a_blocks + b_blocks), col(3 * a_blocks + 2 * b_blocks),
                  pl.BlockSpec((CONV_WIDTH, LANE), lambda i: (0, i))],
        out_specs=col(0), out_shape=jax.ShapeDtypeStruct((t, b_blocks * LANE), bf16),
        compiler_params=_params("parallel"), name="conv_fwd")(proj, proj, proj, conv_w)


def _conv_bwd(proj, dmix, conv_w, a_blocks, b_blocks):
    t = proj.shape[0]

    def body(b_ref, c_ref, h_ref, do_ref, w_ref, db_ref, dc_ref, dh_ref, dw_ref):
        bv, cv, hv = b_ref[...].astype(f32), c_ref[...].astype(f32), h_ref[...].astype(f32)
        w = w_ref[...]
        z = cv * hv
        z1, z2 = _shift_down(z, 1), _shift_down(z, 2)
        y = w[0:1, :] * z2 + w[1:2, :] * z1 + w[2:3, :] * z
        dov = do_ref[...].astype(f32)
        db_ref[...] = (dov * y).astype(bf16)
        dy = dov * bv
        dz = w[2:3, :] * dy + w[1:2, :] * _shift_up(dy, 1) + w[0:1, :] * _shift_up(dy, 2)
        dc_ref[...] = (dz * hv).astype(bf16)
        dh_ref[...] = (dz * cv).astype(bf16)
        dw_ref[0:1, :] = jnp.sum(dy * z2, axis=0, keepdims=True)
        dw_ref[1:2, :] = jnp.sum(dy * z1, axis=0, keepdims=True)
        dw_ref[2:3, :] = jnp.sum(dy * z, axis=0, keepdims=True)

    col = lambda off: pl.BlockSpec((t, LANE), lambda i, off=off: (0, off + i))
    wspec = pl.BlockSpec((CONV_WIDTH, LANE), lambda i: (0, i))
    wide = jax.ShapeDtypeStruct((t, b_blocks * LANE), bf16)
    return pl.pallas_call(
        body, grid=(b_blocks,),
        in_specs=[col(3 * a_blocks), col(3 * a_blocks + b_blocks), col(3 * a_blocks + 2 * b_blocks), col(a_blocks), wspec],
        out_specs=[col(0), col(0), col(0), wspec],
        out_shape=[wide, wide, wide, jax.ShapeDtypeStruct((CONV_WIDTH, b_blocks * LANE), f32)],
        compiler_params=_params("parallel"), name="conv_bwd")(proj, proj, proj, dmix, conv_w)


_RSQRT2 = 0.7071067811865476
_RSQRT2PI = 0.3989422804014327


def _gelu(x):
    return 0.5 * x * (1.0 + lax.erf(x * _RSQRT2))


def _gelu_grad(x):
    return 0.5 * (1.0 + lax.erf(x * _RSQRT2)) + x * jnp.exp(-0.5 * x * x) * _RSQRT2PI


def _sgu_common(a_ref, lg_ref, lb_ref, cw):
    av = a_ref[...]
    u = _gelu(av[:, :cw])
    v = _gelu(av[:, cw:])
    mu = jnp.mean(v, axis=-1, keepdims=True)
    xc = v - mu
    rstd = lax.rsqrt(jnp.mean(xc * xc, axis=-1, keepdims=True) + EPS)
    xhat = xc * rstd
    vln = xhat * lg_ref[...] + lb_ref[...]
    pos_t = lax.broadcasted_iota(jnp.int32, (C_BLOCK, C_BLOCK), 0) // CHUNK
    pos_s = lax.broadcasted_iota(jnp.int32, (C_BLOCK, C_BLOCK), 1) // CHUNK
    return av, u, xhat, rstd, vln, pos_s <= pos_t


def _sgu_fwd(a, ln_g, ln_b, w_s, bs_t):
    t, cw2 = a.shape
    cw = cw2 // 2
    groups = w_s.shape[0]
    cg = cw // groups

    def body(a_ref, lg_ref, lb_ref, ws_ref, bs_ref, m_ref):
        _, u, _, _, vln, mask = _sgu_common(a_ref, lg_ref, lb_ref, cw)
        vb = vln.astype(bf16)
        for g in range(groups):
            sl = slice(g * cg, (g + 1) * cg)
            wm = jnp.where(mask, ws_ref[g], 0.0).astype(bf16)
            s = lax.dot_general(wm, vb[:, sl], _DN["nn"], preferred_element_type=f32) + bs_ref[:, g:g + 1]
            m_ref[:, sl] = (u[:, sl] * s).astype(bf16)

    vec = pl.BlockSpec((1, cw), lambda n: (0, 0))
    return pl.pallas_call(
        body, grid=(t // C_BLOCK,),
        in_specs=[pl.BlockSpec((C_BLOCK, cw2), lambda n: (n, 0)), vec, vec,
                  pl.BlockSpec((groups, C_BLOCK, C_BLOCK), lambda n: (0, 0, 0)),
                  pl.BlockSpec((C_BLOCK, groups), lambda n: (0, 0))],
        out_specs=pl.BlockSpec((C_BLOCK, cw), lambda n: (n, 0)), out_shape=jax.ShapeDtypeStruct((t, cw), bf16),
        compiler_params=_params("parallel"), name="sgu_fwd")(a, ln_g.reshape(1, cw), ln_b.reshape(1, cw), w_s, bs_t)


def _sgu_bwd(a, dm, ln_g, ln_b, w_s, bs_t):
    t, cw2 = a.shape
    cw = cw2 // 2
    groups = w_s.shape[0]
    cg = cw // groups

    def body(a_ref, dm_ref, lg_ref, lb_ref, ws_ref, bs_ref, da_ref, dws_ref, dbs_ref, dlg_ref, dlb_ref, dvln):
        @pl.when(pl.program_id(0) == 0)
        def _():
            dws_ref[...] = jnp.zeros_like(dws_ref)
            dbs_ref[...] = jnp.zeros_like(dbs_ref)
            dlg_ref[...] = jnp.zeros_like(dlg_ref)
            dlb_ref[...] = jnp.zeros_like(dlb_ref)

        av, u, xhat, rstd, vln, mask = _sgu_common(a_ref, lg_ref, lb_ref, cw)
        vb = vln.astype(bf16)
        lane = lax.broadcasted_iota(jnp.int32, (C_BLOCK, groups), 1)
        dbs = jnp.zeros((C_BLOCK, groups), f32)
        for g in range(groups):
            sl = slice(g * cg, (g + 1) * cg)
            wm = jnp.where(mask, ws_ref[g], 0.0).astype(bf16)
            s = lax.dot_general(wm, vb[:, sl], _DN["nn"], preferred_element_type=f32) + bs_ref[:, g:g + 1]
            dmg = dm_ref[:, sl].astype(f32)
            da_ref[:, sl] = (dmg * s * _gelu_grad(av[:, sl])).astype(bf16)
            dsg = dmg * u[:, sl]
            dbs = dbs + jnp.where(lane == g, jnp.sum(dsg, axis=-1, keepdims=True), 0.0)
            dsb = dsg.astype(bf16)
            dws_ref[g] += jnp.where(mask, lax.dot_general(dsb, vb[:, sl], _DN["nt"], preferred_element_type=f32), 0.0)
            dvln[:, sl] = lax.dot_general(wm, dsb, _DN["tn"], preferred_element_type=f32)
        dbs_ref[...] += dbs
        dv = dvln[...]
        dlg_ref[...] += jnp.sum(dv * xhat, axis=0, keepdims=True)
        dlb_ref[...] += jnp.sum(dv, axis=0, keepdims=True)
        dxh = dv * lg_ref[...]
        dvv = rstd * (dxh - jnp.mean(dxh, axis=-1, keepdims=True) - xhat * jnp.mean(dxh * xhat, axis=-1, keepdims=True))
        da_ref[:, cw:] = (dvv * _gelu_grad(av[:, cw:])).astype(bf16)

    vec = pl.BlockSpec((1, cw), lambda n: (0, 0))
    wsp = pl.BlockSpec((groups, C_BLOCK, C_BLOCK), lambda n: (0, 0, 0))
    bsp = pl.BlockSpec((C_BLOCK, groups), lambda n: (0, 0))
    return pl.pallas_call(
        body, grid=(t // C_BLOCK,),
        in_specs=[pl.BlockSpec((C_BLOCK, cw2), lambda n: (n, 0)), pl.BlockSpec((C_BLOCK, cw), lambda n: (n, 0)), vec, vec, wsp, bsp],
        out_specs=[pl.BlockSpec((C_BLOCK, cw2), lambda n: (n, 0)), wsp, bsp, vec, vec],
        out_shape=[jax.ShapeDtypeStruct((t, cw2), bf16), jax.ShapeDtypeStruct(w_s.shape, f32),
                   jax.ShapeDtypeStruct(bs_t.shape, f32), jax.ShapeDtypeStruct((1, cw), f32), jax.ShapeDtypeStruct((1, cw), f32)],
        scratch_shapes=[pltpu.VMEM((C_BLOCK, cw), f32)],
        compiler_params=_params("arbitrary"), name="sgu_bwd")(a, dm, ln_g.reshape(1, cw), ln_b.reshape(1, cw), w_s, bs_t)


HBM = pl.BlockSpec(memory_space=pltpu.HBM)
SEM = pl.BlockSpec(memory_space=pltpu.SEMAPHORE)
EFFECT = pltpu.SideEffectType.DATAFLOW_SIDE_EFFECTING


def _place():
    x, y, c = lax.axis_index("x"), lax.axis_index("y"), lax.axis_index("c")
    return x, y, c, [(1 - x, y), (x, 1 - y), (1 - x, 1 - y)]


def _remote(src, dst, send_sems, recv_sems, k, to):
    return pltpu.make_async_remote_copy(src_ref=src, dst_ref=dst, send_sem=send_sems.at[k], recv_sem=recv_sems.at[k],
                                        device_id=to, device_id_type=MESH)


def _split_start(name, arrays, plan, n_copies, after):
    n = len(arrays)

    def body(*refs):
        send_sems, recv_sems, token = refs[n + 1], refs[n + 2], refs[-1]
        for cp in plan(refs[:n], send_sems, recv_sems):
            cp.start()
        token[...] = jnp.zeros_like(token)

    out = pl.pallas_call(
        body, name=name,
        out_shape=(pltpu.SemaphoreType.DMA((n_copies,)), pltpu.SemaphoreType.DMA((n_copies,)),
                   *[pltpu.HBM(a.shape, a.dtype) for a in arrays], jax.ShapeDtypeStruct((8, LANE), f32)),
        in_specs=[HBM] * n + [ANY], out_specs=(SEM, SEM, *[HBM] * n, pl.BlockSpec(memory_space=pltpu.VMEM)),
        input_output_aliases={i: 2 + i for i in range(n)},
        compiler_params=pltpu.CompilerParams(has_side_effects=EFFECT),
    )(*[pltpu.with_memory_space_constraint(a, pltpu.HBM) for a in arrays], after)
    return (out[0], out[1]), list(out[2:2 + n]), out[-1]


def _split_wait(name, arrays, sems, plan, after):
    n = len(arrays)

    def body(*refs):
        for cp in plan(refs[:n], refs[n], refs[n + 1]):
            cp.wait()

    out = pl.pallas_call(
        body, name=name, out_shape=tuple(pltpu.HBM(a.shape, a.dtype) for a in arrays),
        in_specs=[HBM] * n + [SEM, SEM, ANY], out_specs=tuple([HBM] * n), input_output_aliases={i: i for i in range(n)},
        compiler_params=pltpu.CompilerParams(has_side_effects=EFFECT),
    )(*arrays, sems[0], sems[1], after)
    return list(out)


def _row_pieces(ref_rows, split):
    rc = ref_rows // split
    return [pl.ds(s * rc, rc) for s in range(split)]


def _plan_gather_chips(bufs, send_sems, recv_sems):
    x, y, c, chips = _place()
    me = 4 * x + 2 * y + c
    cps, k = [], 0
    for b in bufs:
        for rows in _row_pieces(b.shape[1], D2D_SPLIT):
            cps.append(_remote(b.at[me, rows], b.at[me, rows], send_sems, recv_sems, k, (x, y, 1 - c)))
            k += 1
        for px, py in chips:
            cps.append(_remote(b.at[me], b.at[me], send_sems, recv_sems, k, (px, py, c)))
            k += 1
    return cps


def _plan_gather_forward(bufs, send_sems, recv_sems):
    x, y, c, chips = _place()
    cps, k = [], 0
    for b in bufs:
        for px, py in chips:
            slot = 4 * px + 2 * py + c
            for rows in _row_pieces(b.shape[1], D2D_SPLIT):
                cps.append(_remote(b.at[slot, rows], b.at[slot, rows], send_sems, recv_sems, k, (x, y, 1 - c)))
                k += 1
    return cps


def _plan_rs_sibling(arrs, send_sems, recv_sems):
    n = len(arrs) // 2
    x, y, c, _ = _place()
    cps, k = [], 0
    for g, got in zip(arrs[:n], arrs[n:]):
        for q in range(N_CHIP):
            for rows in _row_pieces(g.shape[1], RS_SPLIT):
                cps.append(_remote(g.at[N_CHIP * (1 - c) + q, rows], got.at[q, rows], send_sems, recv_sems, k, (x, y, 1 - c)))
                k += 1
    return cps


def _plan_rs_chips(arrs, send_sems, recv_sems):
    n = len(arrs) // 2
    x, y, c, chips = _place()
    q = 2 * x + y
    cps, k = [], 0
    for p, r in zip(arrs[:n], arrs[n:]):
        for px, py in chips:
            cps.append(_remote(p.at[2 * px + py], r.at[q], send_sems, recv_sems, k, (px, py, c)))
            k += 1
    return cps


class _Gather:
    def __init__(self, tag, bufs):
        self.tag, self.bufs, self.n = tag, bufs, len(bufs)

    def start_chips(self, after):
        self.sems, self.bufs, self.token = _split_start("gather_chips_start_" + self.tag, self.bufs, _plan_gather_chips,
                                                        self.n * (D2D_SPLIT + 3), after)

    def start_forward(self, after):
        self.bufs = _split_wait("gather_chips_wait_" + self.tag, self.bufs, self.sems, _plan_gather_chips, after)
        self.sems, self.bufs, _ = _split_start("gather_fwd_start_" + self.tag, self.bufs, _plan_gather_forward,
                                               self.n * 3 * D2D_SPLIT, after)

    def ready(self, after):
        return _split_wait("gather_fwd_wait_" + self.tag, self.bufs, self.sems, _plan_gather_forward, after)


class _ReduceScatter:
    def __init__(self, tag, grads, core):
        self.tag, self.n = tag, len(grads)
        lands = [lax.empty((N_CHIP,) + g.shape[1:], g.dtype) for g in grads]
        self.sems, self.arrs, self.token = _split_start("rs_sibling_start_" + tag, list(grads) + lands, _plan_rs_sibling,
                                                        self.n * N_CHIP * RS_SPLIT, core)

    def middle(self, after, core):
        arrs = _split_wait("rs_sibling_wait_" + self.tag, self.arrs, self.sems, _plan_rs_sibling, after)
        parts = [_pair_sum(g, got, core) for g, got in zip(arrs[:self.n], arrs[self.n:])]
        lands = [lax.empty(p.shape, p.dtype) for p in parts]
        self.sems, self.arrs, self.token = _split_start("rs_chips_start_" + self.tag, parts + lands, _plan_rs_chips,
                                                        self.n * 3, core)

    def finish(self, after):
        arrs = _split_wait("rs_chips_wait_" + self.tag, self.arrs, self.sems, _plan_rs_chips, after)
        return list(zip(arrs[:self.n], arrs[self.n:]))


def _cast_into_slot(name, w, layer, me):
    _, rows, cols = w.shape
    tr = 256 if rows % 256 == 0 else rows

    def body(me_ref, w_ref, o_ref):
        o_ref[...] = w_ref[...].astype(bf16)

    return pl.pallas_call(
        body,
        grid_spec=pltpu.PrefetchScalarGridSpec(
            num_scalar_prefetch=1, grid=(rows // tr,),
            in_specs=[pl.BlockSpec((None, tr, cols), lambda i, me_ref: (layer, i, 0))],
            out_specs=pl.BlockSpec((None, tr, cols), lambda i, me_ref: (me_ref[0], i, 0))),
        out_shape=jax.ShapeDtypeStruct((N_DEV, rows, cols), bf16), compiler_params=_params("parallel"), name=name)(me, w)


def _pair_sum(g, got, core):
    _, rows, cols = g.shape
    tr = 512 if rows % 512 == 0 else rows

    def body(c_ref, a_ref, b_ref, o_ref):
        o_ref[...] = (a_ref[...].astype(f32) + b_ref[...].astype(f32)).astype(bf16)

    spec = pl.BlockSpec((None, tr, cols), lambda q, i, c_ref: (q, i, 0))
    return pl.pallas_call(
        body,
        grid_spec=pltpu.PrefetchScalarGridSpec(
            num_scalar_prefetch=1, grid=(N_CHIP, rows // tr),
            in_specs=[pl.BlockSpec((None, tr, cols), lambda q, i, c_ref: (N_CHIP * c_ref[0] + q, i, 0)), spec],
            out_specs=spec),
        out_shape=jax.ShapeDtypeStruct((N_CHIP, rows, cols), bf16), compiler_params=_params("parallel", "parallel"),
        name="pair_sum")(core, g, got)


def _gather_copies(n, ins, outs, send_sems, recv_sems, local_sems):
    x, y, c, chips = _place()
    sibling = (x, y, 1 - c)

    def slot(px, py, pc):
        return 4 * px + 2 * py + pc

    def copy(i, k, block, to, src=None):
        dst = outs[i].at[slot(*block)]
        return pltpu.make_async_remote_copy(src_ref=dst if src is None else src, dst_ref=dst, send_sem=send_sems.at[i, k],
                                            recv_sem=recv_sems.at[i, k], device_id=to, device_id_type=MESH)

    started = []
    for i in range(n):
        mine = pltpu.make_async_copy(ins[i], outs[i].at[slot(x, y, c)], local_sems.at[i])
        mine.start()
        started.append(mine)
    sends = []
    for i in range(n):
        sends.append(copy(i, 0, (x, y, c), sibling, src=ins[i]))
        sends += [copy(i, 1 + j, (x, y, c), (*chip, c), src=ins[i]) for j, chip in enumerate(chips)]
    for cp in sends:
        cp.start()
    for i in range(n):
        for j, chip in enumerate(chips):
            copy(i, 1 + j, (*chip, c), (x, y, c)).wait_recv()
            fwd = copy(i, 4 + j, (*chip, c), sibling)
            fwd.start()
            sends.append(fwd)
    for i in range(n):
        copy(i, 0, sibling, (x, y, c)).wait_recv()
        for j, chip in enumerate(chips):
            copy(i, 4 + j, (*chip, 1 - c), (x, y, c)).wait_recv()
    for cp in sends:
        cp.wait_send()
    for mine in started:
        mine.wait()


def _gather_small(name, packed, reduce):
    rows = packed.shape[0]

    def body(x_ref, o_ref, buf, send_sems, recv_sems, local_sems):
        _gather_copies(1, [x_ref], [buf], send_sems, recv_sems, local_sems)
        if reduce:
            acc = buf[0]
            for j in range(1, N_DEV):
                acc = acc + buf[j]
            o_ref[...] = acc
        else:
            o_ref[...] = buf[...]

    vm = pl.BlockSpec(memory_space=pltpu.VMEM)
    return pl.pallas_call(
        body, in_specs=[vm], out_specs=vm,
        out_shape=jax.ShapeDtypeStruct((rows, LANE) if reduce else (N_DEV, rows, LANE), f32),
        scratch_shapes=[pltpu.VMEM((N_DEV, rows, LANE), f32), pltpu.SemaphoreType.DMA((1, 7)), pltpu.SemaphoreType.DMA((1, 7)),
                        pltpu.SemaphoreType.DMA((1,))],
        compiler_params=pltpu.CompilerParams(vmem_limit_bytes=VMEM_LIMIT), name=name)(packed)


def _pack(arrs):
    flat = jnp.concatenate([a.reshape(-1).astype(f32) for a in arrs])
    rows = -(-flat.shape[0] // (8 * LANE)) * 8
    return jnp.pad(flat, (0, rows * LANE - flat.shape[0])).reshape(rows, LANE)


def _unpack(buf, shapes):
    flat = buf.reshape(-1)
    out, off = [], 0
    for s in shapes:
        n = int(np.prod(s))
        out.append(flat[off:off + n].reshape(s))
        off += n
    return out


def _adam_math(w, g, m, v):
    m2 = ADAM_B1 * m + (1.0 - ADAM_B1) * g
    v2 = ADAM_B2 * v + (1.0 - ADAM_B2) * (g * g)
    m_hat = m2 / (1.0 - ADAM_B1 ** ADAM_STEP)
    v_hat = v2 / (1.0 - ADAM_B2 ** ADAM_STEP)
    delta = -ADAM_LR * (m_hat / (jnp.sqrt(v_hat) + ADAM_EPS) + ADAM_WD * w)
    return delta, m2, v2


def _adam_big(name, w, m, v, parts, chip):
    layers, rows, cols = w.shape
    tr = 256 if rows % 256 == 0 else 64 if rows % 64 == 0 else 8

    def body(chip_ref, w_ref, m_ref, v_ref, *rest):
        p_refs = rest[:N_CHIP * layers]
        g_ref, d_ref, m2_ref, v2_ref = rest[N_CHIP * layers:]
        for li in range(layers):
            @pl.when(pl.program_id(0) == li)
            def _(li=li):
                g = p_refs[N_CHIP * li][...].astype(f32)
                for q in range(1, N_CHIP):
                    g = g + p_refs[N_CHIP * li + q][...].astype(f32)
                delta, m2, v2 = _adam_math(w_ref[...], g, m_ref[...], v_ref[...])
                g_ref[...] = g
                d_ref[...] = delta
                m2_ref[...] = m2
                v2_ref[...] = v2

    spec = pl.BlockSpec((None, tr, cols), lambda l, i, c_ref: (l, i, 0))
    pspecs, operands = [], []
    for li in range(layers):
        for q in range(N_CHIP):
            pspecs.append(pl.BlockSpec((None, tr, cols),
                                       lambda l, i, c_ref, li=li, q=q: ((c_ref[0] + q) % N_CHIP, jnp.where(l == li, i, 0), 0)))
            operands.append(parts[li][0] if q == 0 else parts[li][1])
    out = jax.ShapeDtypeStruct((layers, rows, cols), f32)
    return pl.pallas_call(
        body,
        grid_spec=pltpu.PrefetchScalarGridSpec(num_scalar_prefetch=1, grid=(layers, rows // tr),
                                               in_specs=[spec, spec, spec] + pspecs, out_specs=[spec] * 4),
        out_shape=[out] * 4, compiler_params=_params("arbitrary", "arbitrary"), name=name)(chip, w, m, v, *operands)


def _adam_small(w, g, m, v):
    rows = w.shape[0]

    def body(w_ref, g_ref, m_ref, v_ref, d_ref, m2_ref, v2_ref):
        delta, m2, v2 = _adam_math(w_ref[...], g_ref[...], m_ref[...], v_ref[...])
        d_ref[...] = delta
        m2_ref[...] = m2
        v2_ref[...] = v2

    out = jax.ShapeDtypeStruct((rows, LANE), f32)
    return pl.pallas_call(body, out_shape=[out] * 3, name="adam_small")(w, g, m, v)


def kernel(x, mix_norm, ab_w_in, ab_rel_bias, ab_conv_w, ab_w_out, c_w_in, c_ln_g, c_ln_b, c_w_s, c_b_s, c_w_out, ffn_norm, ffn_w_gate, ffn_w_up, ffn_w_down, final_norm, loss_target, m_mix_norm, m_ab_w_in, m_ab_rel_bias, m_ab_conv_w, m_ab_w_out, m_c_w_in, m_c_ln_g, m_c_ln_b, m_c_w_s, m_c_b_s, m_c_w_out, m_ffn_norm, m_ffn_w_gate, m_ffn_w_up, m_ffn_w_down, m_final_norm, v_mix_norm, v_ab_w_in, v_ab_rel_bias, v_ab_conv_w, v_ab_w_out, v_c_w_in, v_c_ln_g, v_c_ln_b, v_c_w_s, v_c_b_s, v_c_w_out, v_ffn_norm, v_ffn_w_gate, v_ffn_w_up, v_ffn_w_down, v_final_norm):
    d = D_MODEL
    a_width = d // 2
    heads = a_width // A_HEAD_DIM
    a_blocks = a_width // LANE
    b_blocks = (d - a_width) // LANE
    n_even, n_odd = (DEPTH + 1) // 2, DEPTH // 2
    me_s = 4 * lax.axis_index("x") + 2 * lax.axis_index("y") + lax.axis_index("c")
    me = me_s.astype(jnp.int32).reshape(1)
    core = lax.axis_index("c").astype(jnp.int32).reshape(1)
    chip = (2 * lax.axis_index("x") + lax.axis_index("y")).astype(jnp.int32).reshape(1)

    weights = dict(mix_norm=mix_norm, ab_w_in=ab_w_in, ab_rel_bias=ab_rel_bias, ab_conv_w=ab_conv_w, ab_w_out=ab_w_out,
                   c_w_in=c_w_in, c_ln_g=c_ln_g, c_ln_b=c_ln_b, c_w_s=c_w_s, c_b_s=c_b_s, c_w_out=c_w_out,
                   ffn_norm=ffn_norm, ffn_w_gate=ffn_w_gate, ffn_w_up=ffn_w_up, ffn_w_down=ffn_w_down, final_norm=final_norm)
    mom_m = dict(mix_norm=m_mix_norm, ab_w_in=m_ab_w_in, ab_rel_bias=m_ab_rel_bias, ab_conv_w=m_ab_conv_w, ab_w_out=m_ab_w_out,
                 c_w_in=m_c_w_in, c_ln_g=m_c_ln_g, c_ln_b=m_c_ln_b, c_w_s=m_c_w_s, c_b_s=m_c_b_s, c_w_out=m_c_w_out,
                 ffn_norm=m_ffn_norm, ffn_w_gate=m_ffn_w_gate, ffn_w_up=m_ffn_w_up, ffn_w_down=m_ffn_w_down, final_norm=m_final_norm)
    mom_v = dict(mix_norm=v_mix_norm, ab_w_in=v_ab_w_in, ab_rel_bias=v_ab_rel_bias, ab_conv_w=v_ab_conv_w, ab_w_out=v_ab_w_out,
                 c_w_in=v_c_w_in, c_ln_g=v_c_ln_g, c_ln_b=v_c_ln_b, c_w_s=v_c_w_s, c_b_s=v_c_b_s, c_w_out=v_c_w_out,
                 ffn_norm=v_ffn_norm, ffn_w_gate=v_ffn_w_gate, ffn_w_up=v_ffn_w_up, ffn_w_down=v_ffn_w_down, final_norm=v_final_norm)
    order = list(weights)
    wide = ("ffn_w_gate", "ffn_w_up")
    flip = lambda a: jnp.swapaxes(a, 1, 2)
    local = {k: (flip(weights[k]), flip(mom_m[k]), flip(mom_v[k])) if k in wide else (weights[k], mom_m[k], mom_v[k]) for k in order}

    sharded_small = [ab_conv_w, c_ln_g, c_ln_b]
    gathered = _gather_small("gather_small", _pack(sharded_small), reduce=False)
    conv_parts, lng_parts, lnb_parts = [], [], []
    for j in range(N_DEV):
        cw_j, lg_j, lb_j = _unpack(gathered[j], [a.shape for a in sharded_small])
        conv_parts.append(cw_j)
        lng_parts.append(lg_j)
        lnb_parts.append(lb_j)
    conv_full = jnp.concatenate(conv_parts, axis=-1)
    lng_full = jnp.concatenate(lng_parts, axis=-1)
    lnb_full = jnp.concatenate(lnb_parts, axis=-1)

    def cast(w, li):
        return _cast_into_slot("cast_slot", w, li, me)

    units = []
    for layer in range(DEPTH):
        i = layer // 2
        if layer % 2 == 0:
            units.append(_Gather("ab%d" % i, [cast(ab_w_in, i), cast(ab_w_out, i)]))
        else:
            units.append(_Gather("c%d" % i, [cast(c_w_in, i), cast(c_w_out, i)]))
        units.append(_Gather("ffn_in%d" % layer, [cast(local["ffn_w_gate"][0], layer), cast(local["ffn_w_up"][0], layer)]))
        units.append(_Gather("ffn_out%d" % layer, [cast(ffn_w_down, layer)]))
    cursor = [0]
    tokens = []

    def start_gather(k, after):
        units[k].start_chips(after)
        tokens.append(units[k].token)

    def next_weights(after):
        k = cursor[0]
        cursor[0] = k + 1
        units[k].start_forward(after)
        if k + 2 < len(units):
            start_gather(k + 2, after)
        return units[k].ready(after)

    def started():
        deps = list(tokens)
        tokens.clear()
        return deps

    xs = x[0]
    tgt = loss_target[0]
    start_gather(0, xs)
    start_gather(1, xs)
    saved = []
    for layer in range(DEPTH):
        i = layer // 2
        w_in_g, w_out_g = next_weights(xs)
        w_out_full = w_out_g.reshape(-1, w_out_g.shape[-1])
        h = _rms_fwd(xs, mix_norm[layer])
        if layer % 2 == 0:
            proj = _mm_cols("ab_proj", h, w_in_g, bf16, started())
            btab = _bias_table(ab_rel_bias[i])
            attn = _attn_fwd(proj, btab, heads)
            conv = _conv_fwd(proj, conv_full[i], a_blocks, b_blocks)
            mixed = jnp.concatenate([attn, conv], axis=-1)
            ctx = (proj, btab)
        else:
            proj = _mm_cols("c_proj", h, w_in_g, f32, started())
            bs_t = jnp.transpose(c_b_s[i])
            mixed = _sgu_fwd(proj, lng_full[i], lnb_full[i], c_w_s[i], bs_t)
            ctx = (proj, bs_t)
        x1 = _mm_rows_res("mix_out", mixed, w_out_full, xs)
        wg_g, wu_g = next_weights(x1)
        h2 = _rms_fwd(x1, ffn_norm[layer])
        g_act, u_act, act = _ffn_in(h2, wg_g, wu_g, started())
        (wd_g,) = next_weights(g_act)
        x2 = _ffn_down(act, wd_g, x1, started())
        saved.append((xs, h, ctx, mixed, x1, h2, g_act, u_act, w_in_g, w_out_full, wg_g, wu_g, wd_g))
        xs = x2

    loss_part, dx, dxb, d_final = _loss_head(xs, final_norm, tgt)
    loss = lax.psum(loss_part[0, 0], ("x", "y", "c"))

    scatters = {}
    small = {k: [None] * weights[k].shape[0] for k in ("mix_norm", "ffn_norm", "ab_rel_bias", "ab_conv_w", "c_ln_g", "c_ln_b",
                                                       "c_w_s", "c_b_s")}
    for layer in reversed(range(DEPTH)):
        i = layer // 2
        xs, h, ctx, mixed, x1, h2, g_act, u_act, w_in_g, w_out_full, wg_g, wu_g, wd_g = saved[layer]
        dg, du, act = _ffn_bwd_act(dxb, wd_g, g_act, u_act, started())
        dwd = _ffn_dwd(act, dxb)
        dwg, dwu = _ffn_dwgu(h2, dg, du)
        rs_ffn = _ReduceScatter("ffn%d" % layer, [dwg, dwu, dwd], core)
        dh2 = _ffn_dh(dg, du, wg_g, wu_g, [rs_ffn.token])
        dx, dxb, dgn = _rms_bwd(x1, ffn_norm[layer], dh2, dx)
        small["ffn_norm"][layer] = dgn[0]
        rs_ffn.middle(dxb, core)
        for pos, k in enumerate(("ffn_w_gate", "ffn_w_up", "ffn_w_down")):
            scatters[(k, layer)] = (rs_ffn, pos)
        dmixed = _mm_nt("mix_out_bwd", dxb, w_out_full, bf16, [rs_ffn.token])
        dwout = _mm_tn_rows("mix_out_dw", mixed, dxb)
        if layer % 2 == 0:
            proj, btab = ctx
            dq, dk, dv, dtab = _attn_bwd(proj, dmixed, btab, heads)
            db, dc, dhv, dcw = _conv_bwd(proj, dmixed, conv_full[i], a_blocks, b_blocks)
            dproj = jnp.concatenate([dq, dk, dv, db, dc, dhv], axis=-1)
            small["ab_rel_bias"][i] = _bias_table_grad(dtab)
            small["ab_conv_w"][i] = dcw
            names = ("ab_w_in", "ab_w_out")
            tag = "ab"
        else:
            proj, bs_t = ctx
            dproj, dws, dbs_t, dlg, dlb = _sgu_bwd(proj, dmixed, lng_full[i], lnb_full[i], c_w_s[i], bs_t)
            small["c_w_s"][i] = dws
            small["c_b_s"][i] = jnp.transpose(dbs_t)
            small["c_ln_g"][i] = dlg[0]
            small["c_ln_b"][i] = dlb[0]
            names = ("c_w_in", "c_w_out")
            tag = "c"
        dwin = _mm_tn_cols(tag + "_proj_dw", h, dproj)
        rs_mix = _ReduceScatter("%s%d" % (tag, i), [dwin, dwout], core)
        dh = _mm_nt_cols(tag + "_proj_bwd", dproj, w_in_g, [rs_mix.token])
        dx, dxb, dgm = _rms_bwd(xs, mix_norm[layer], dh, dx)
        small["mix_norm"][layer] = dgm[0]
        rs_mix.middle(dxb, core)
        tokens.append(rs_mix.token)
        scatters[(names[0], i)] = (rs_mix, 0)
        scatters[(names[1], i)] = (rs_mix, 1)
    grad_x = dx[None]

    small_names = ["mix_norm", "ffn_norm", "ab_rel_bias", "ab_conv_w", "c_ln_g", "c_ln_b", "c_w_s", "c_b_s"]
    small_full = [jnp.stack(small[k]) for k in small_names] + [d_final[0]]
    summed = _unpack(_gather_small("reduce_small", _pack(small_full), reduce=True), [a.shape for a in small_full])
    small_grads = dict(zip(small_names + ["final_norm"], summed))
    for k in ("ab_conv_w", "c_ln_g", "c_ln_b"):
        width = weights[k].shape[-1]
        small_grads[k] = lax.dynamic_slice_in_dim(small_grads[k], me_s * width, width, axis=-1)
    small_order = [k for k in order if k in small_grads]
    shapes = [weights[k].shape for k in small_order]
    d_s, m_s, v_s = _adam_small(_pack([weights[k] for k in small_order]), _pack([small_grads[k] for k in small_order]),
                                _pack([mom_m[k] for k in small_order]), _pack([mom_v[k] for k in small_order]))
    grads, deltas, new_m, new_v = dict(small_grads), {}, {}, {}
    for k, dd, mm, vv in zip(small_order, _unpack(d_s, shapes), _unpack(m_s, shapes), _unpack(v_s, shapes)):
        deltas[k], new_m[k], new_v[k] = dd, mm, vv

    finished = {}
    last = d_s
    for k in ("c_w_in", "c_w_out", "ffn_w_gate", "ffn_w_up", "ffn_w_down", "ab_w_in", "ab_w_out"):
        parts = []
        w_k, m_k, v_k = local[k]
        for li in range(w_k.shape[0]):
            rs, pos = scatters[(k, li)]
            if id(rs) not in finished:
                finished[id(rs)] = rs.finish(last)
            parts.append(finished[id(rs)][pos])
        outs = _adam_big("adam_" + k, w_k, m_k, v_k, parts, chip)
        last = outs[1]
        grads[k], deltas[k], new_m[k], new_v[k] = [flip(o) for o in outs] if k in wide else outs

    return (loss, grad_x, *[grads[k] for k in order], *[deltas[k] for k in order], *[new_m[k] for k in order],
            *[new_v[k] for k in order])
```

```python
import numpy as np
import jax
import jax.numpy as jnp
from jax import lax
from jax.experimental import pallas as pl
from jax.experimental.pallas import tpu as pltpu

D_MODEL = 2048
SEQ = 2048
DEPTH = 4
CHUNK = 64
A_HEAD_DIM = 128
A_LEFT_CHUNKS = 8
A_MAX_REL = 256
CONV_WIDTH = 3
C_BLOCK = 128
C_GROUPS = 8
EPS = 1e-6
NEG_INF = -1e30

ADAM_LR = 0.001
ADAM_B1 = 0.9
ADAM_B2 = 0.999
ADAM_EPS = 1e-08
ADAM_WD = 0.01
ADAM_STEP = 10

N_DEV = 8
N_CHIP = 4
RS_SPLIT = 4
D2D_SPLIT = 2
GATHER_AHEAD = 3
ROWS_PER_STEP = 1024
LANE = 128
VMEM_LIMIT = 52 * 1024 * 1024

bf16 = jnp.bfloat16
f32 = jnp.float32
MESH = pl.DeviceIdType.MESH
ANY = pl.BlockSpec(memory_space=pl.ANY)


def _params(*sem):
    return pltpu.CompilerParams(dimension_semantics=sem, vmem_limit_bytes=VMEM_LIMIT)


def _perm(j):
    return (j % 2) * N_CHIP + j // 2


_DN = {"nn": (((1,), (0,)), ((), ())), "nt": (((1,), (1,)), ((), ())), "tn": (((0,), (0,)), ((), ()))}


def _matmul(name, mode, grid, operands, specs, pairs, n_acc, acc_shape, extras, extra_specs, out_shapes, out_specs,
            epilogue):
    nk = grid[2]
    n_op, n_ex, n_out = len(operands), len(extras), len(out_shapes)

    def body(*refs):
        ops = refs[:n_op]
        ex = refs[n_op:n_op + n_ex]
        outs = refs[n_op + n_ex:n_op + n_ex + n_out]
        accs = refs[n_op + n_ex + n_out:]
        k = pl.program_id(2)

        @pl.when(k == 0)
        def _():
            for acc in accs:
                acc[...] = jnp.zeros_like(acc)

        for p, (ia, ib) in enumerate(pairs):
            acc = accs[p % n_acc]
            acc[...] += lax.dot_general(ops[ia][...], ops[ib][...], _DN[mode], preferred_element_type=f32)

        @pl.when(k == nk - 1)
        def _():
            epilogue([acc[...] for acc in accs], ex, outs)

    return pl.pallas_call(
        body, grid=grid, in_specs=list(specs) + list(extra_specs), out_specs=list(out_specs),
        out_shape=list(out_shapes), scratch_shapes=[pltpu.VMEM(acc_shape, f32)] * n_acc,
        compiler_params=_params("parallel", "parallel", "arbitrary"), name=name)(*operands, *extras)


def _store(dtype):
    def ep(accs, ex, outs):
        for a, o in zip(accs, outs):
            o[...] = a.astype(dtype)
    return ep


def _mm_cols(name, h, wg, out_dtype, deps=()):
    t, kd = h.shape
    n8 = wg.shape[2]
    tm = min(t, ROWS_PER_STEP)
    return _matmul(
        name, "nn", (t // tm, N_DEV, 1), [h, wg],
        [pl.BlockSpec((tm, kd), lambda i, j, k: (i, 0)), pl.BlockSpec((None, kd, n8), lambda i, j, k: (j, 0, 0))],
        [(0, 1)], 1, (tm, n8), list(deps), [ANY] * len(deps), [jax.ShapeDtypeStruct((t, N_DEV * n8), out_dtype)],
        [pl.BlockSpec((tm, n8), lambda i, j, k: (i, j))], _store(out_dtype))[0]


def _mm_rows_res(name, a, w, res, deps=()):
    t, kd = a.shape
    n = w.shape[1]
    tm, tn = min(t, ROWS_PER_STEP), min(n, 1024)

    def ep(accs, ex, outs):
        outs[0][...] = ex[0][...] + accs[0]

    return _matmul(
        name, "nn", (t // tm, n // tn, 1), [a, w],
        [pl.BlockSpec((tm, kd), lambda i, j, k: (i, 0)), pl.BlockSpec((kd, tn), lambda i, j, k: (0, j))],
        [(0, 1)], 1, (tm, tn), [res] + list(deps), [pl.BlockSpec((tm, tn), lambda i, j, k: (i, j))] + [ANY] * len(deps),
        [jax.ShapeDtypeStruct((t, n), f32)], [pl.BlockSpec((tm, tn), lambda i, j, k: (i, j))], ep)[0]


def _mm_nt(name, a, w, out_dtype, deps=()):
    t, n = a.shape
    kd = w.shape[0]
    tm, tn = min(t, ROWS_PER_STEP), min(kd, 1024)
    return _matmul(
        name, "nt", (t // tm, kd // tn, 1), [a, w],
        [pl.BlockSpec((tm, n), lambda i, j, k: (i, 0)), pl.BlockSpec((tn, n), lambda i, j, k: (j, 0))],
        [(0, 1)], 1, (tm, tn), list(deps), [ANY] * len(deps), [jax.ShapeDtypeStruct((t, kd), out_dtype)],
        [pl.BlockSpec((tm, tn), lambda i, j, k: (i, j))], _store(out_dtype))[0]


def _mm_nt_cols(name, da, wg, deps=()):
    t = da.shape[0]
    kd, n8 = wg.shape[1], wg.shape[2]
    tm = min(t, ROWS_PER_STEP)
    return _matmul(
        name, "nt", (t // tm, 1, N_DEV), [da, wg],
        [pl.BlockSpec((tm, n8), lambda i, j, k: (i, k)), pl.BlockSpec((None, kd, n8), lambda i, j, k: (k, 0, 0))],
        [(0, 1)], 1, (tm, kd), list(deps), [ANY] * len(deps), [jax.ShapeDtypeStruct((t, kd), f32)],
        [pl.BlockSpec((tm, kd), lambda i, j, k: (i, 0))], _store(f32))[0]


def _mm_tn_cols(name, h, da):
    t, kd = h.shape
    n8 = da.shape[1] // N_DEV
    tmk, tk = min(kd, 1024), min(t, 1024)
    return _matmul(
        name, "tn", (kd // tmk, N_DEV, t // tk), [h, da],
        [pl.BlockSpec((tk, tmk), lambda i, j, k: (k, i)), pl.BlockSpec((tk, n8), lambda i, j, k: (k, j))],
        [(0, 1)], 1, (tmk, n8), [], [], [jax.ShapeDtypeStruct((N_DEV, kd, n8), bf16)],
        [pl.BlockSpec((None, tmk, n8), lambda i, j, k: (_perm(j), i, 0))], _store(bf16))[0]


def _mm_tn_rows(name, a, dx):
    t, kf = a.shape
    r8 = kf // N_DEV
    n = dx.shape[1]
    return _matmul(
        name, "tn", (N_DEV, 1, 1), [a, dx],
        [pl.BlockSpec((t, r8), lambda i, j, k: (0, i)), pl.BlockSpec((t, n), lambda i, j, k: (0, 0))],
        [(0, 1)], 1, (r8, n), [], [], [jax.ShapeDtypeStruct((N_DEV, r8, n), bf16)],
        [pl.BlockSpec((None, r8, n), lambda i, j, k: (_perm(i), 0, 0))], _store(bf16))[0]


def _ffn_in(h2, wg_t, wu_t, deps=()):
    t, kd = h2.shape
    f8 = wg_t.shape[1]
    tm = min(t, ROWS_PER_STEP)

    def ep(accs, ex, outs):
        g, u = accs
        outs[0][...] = g.astype(bf16)
        outs[1][...] = u.astype(bf16)
        outs[2][...] = (g * jax.nn.sigmoid(g) * u).astype(bf16)

    wspec = pl.BlockSpec((None, f8, kd), lambda i, j, k: (j, 0, 0))
    ospec = pl.BlockSpec((None, tm, f8), lambda i, j, k: (j, i, 0))
    return _matmul(
        "ffn_in", "nt", (t // tm, N_DEV, 1), [h2, wg_t, wu_t],
        [pl.BlockSpec((tm, kd), lambda i, j, k: (i, 0)), wspec, wspec], [(0, 1), (0, 2)], 2, (tm, f8), list(deps),
        [ANY] * len(deps), [jax.ShapeDtypeStruct((N_DEV, t, f8), bf16)] * 3, [ospec] * 3, ep)


def _ffn_down(act, wd, res, deps=()):
    _, t, f8 = act.shape
    n = wd.shape[2]
    tm = min(t, 512)

    def ep(accs, ex, outs):
        outs[0][...] = ex[0][...] + accs[0]

    return _matmul(
        "ffn_down", "nn", (t // tm, 1, N_DEV), [act, wd],
        [pl.BlockSpec((None, tm, f8), lambda i, j, k: (k, i, 0)), pl.BlockSpec((None, f8, n), lambda i, j, k: (k, 0, 0))],
        [(0, 1)], 1, (tm, n), [res] + list(deps), [pl.BlockSpec((tm, n), lambda i, j, k: (i, 0))] + [ANY] * len(deps),
        [jax.ShapeDtypeStruct((t, n), f32)], [pl.BlockSpec((tm, n), lambda i, j, k: (i, 0))], ep)[0]


def _ffn_bwd_act(dxb, wd, g, u, deps=()):
    t, n = dxb.shape
    f8 = wd.shape[1]
    tm = min(t, ROWS_PER_STEP)

    def ep(accs, ex, outs):
        dact = accs[0]
        gv = ex[0][...].astype(f32)
        uv = ex[1][...].astype(f32)
        sg = jax.nn.sigmoid(gv)
        silu = gv * sg
        outs[0][...] = (dact * uv * (sg * (1.0 + gv * (1.0 - sg)))).astype(bf16)
        outs[1][...] = (dact * silu).astype(bf16)
        outs[2][...] = (silu * uv).astype(bf16)

    bspec = pl.BlockSpec((None, tm, f8), lambda i, j, k: (j, i, 0))
    return _matmul(
        "ffn_bwd_act", "nt", (t // tm, N_DEV, 1), [dxb, wd],
        [pl.BlockSpec((tm, n), lambda i, j, k: (i, 0)), pl.BlockSpec((None, f8, n), lambda i, j, k: (j, 0, 0))],
        [(0, 1)], 1, (tm, f8), [g, u] + list(deps), [bspec, bspec] + [ANY] * len(deps),
        [jax.ShapeDtypeStruct((N_DEV, t, f8), bf16)] * 3, [bspec] * 3, ep)


def _ffn_dwd(act, dxb):
    _, t, f8 = act.shape
    n = dxb.shape[1]
    tk = min(t, 1024)
    return _matmul(
        "ffn_dwd", "tn", (N_DEV, 1, t // tk), [act, dxb],
        [pl.BlockSpec((None, tk, f8), lambda i, j, k: (i, k, 0)), pl.BlockSpec((tk, n), lambda i, j, k: (k, 0))],
        [(0, 1)], 1, (f8, n), [], [], [jax.ShapeDtypeStruct((N_DEV, f8, n), bf16)],
        [pl.BlockSpec((None, f8, n), lambda i, j, k: (_perm(i), 0, 0))], _store(bf16))[0]


def _ffn_dwgu(h2, dg, du):
    t, kd = h2.shape
    f8 = dg.shape[2]
    tk = min(t, 1024)
    aspec = pl.BlockSpec((None, tk, f8), lambda i, j, k: (i, k, 0))
    ospec = pl.BlockSpec((None, f8, kd), lambda i, j, k: (_perm(i), 0, 0))
    return _matmul(
        "ffn_dwgu", "tn", (N_DEV, 1, t // tk), [dg, du, h2],
        [aspec, aspec, pl.BlockSpec((tk, kd), lambda i, j, k: (k, 0))], [(0, 2), (1, 2)], 2, (f8, kd), [], [],
        [jax.ShapeDtypeStruct((N_DEV, f8, kd), bf16)] * 2, [ospec] * 2, _store(bf16))


def _ffn_dh(dg, du, wg_t, wu_t, deps=()):
    _, t, f8 = dg.shape
    kd = wg_t.shape[2]
    tm = min(t, ROWS_PER_STEP)
    aspec = pl.BlockSpec((None, tm, f8), lambda i, j, k: (k, i, 0))
    wspec = pl.BlockSpec((None, f8, kd), lambda i, j, k: (k, 0, 0))
    return _matmul(
        "ffn_dh", "nn", (t // tm, 1, N_DEV), [dg, du, wg_t, wu_t], [aspec, aspec, wspec, wspec], [(0, 2), (1, 3)], 1,
        (tm, kd), list(deps), [ANY] * len(deps), [jax.ShapeDtypeStruct((t, kd), f32)],
        [pl.BlockSpec((tm, kd), lambda i, j, k: (i, 0))], _store(f32))[0]


def _rms_fwd(x, g):
    t, d = x.shape
    tm = min(t, 256)

    def body(x_ref, g_ref, o_ref):
        xv = x_ref[...]
        r = lax.rsqrt(jnp.mean(xv * xv, axis=-1, keepdims=True) + EPS)
        o_ref[...] = (xv * r * g_ref[...]).astype(bf16)

    return pl.pallas_call(
        body, grid=(t // tm,), in_specs=[pl.BlockSpec((tm, d), lambda i: (i, 0)), pl.BlockSpec((1, d), lambda i: (0, 0))],
        out_specs=pl.BlockSpec((tm, d), lambda i: (i, 0)), out_shape=jax.ShapeDtypeStruct((t, d), bf16),
        compiler_params=_params("parallel"), name="rms_fwd")(x, g.reshape(1, d))


def _rms_bwd(x, g, dh, dres):
    t, d = x.shape
    tm = min(t, 256)

    def body(x_ref, g_ref, dh_ref, dres_ref, dx_ref, dxb_ref, dg_ref):
        xv = x_ref[...]
        dy = dh_ref[...].astype(f32)
        r = lax.rsqrt(jnp.mean(xv * xv, axis=-1, keepdims=True) + EPS)
        gy = dy * g_ref[...]
        dot = jnp.mean(xv * gy, axis=-1, keepdims=True)
        dx = dres_ref[...] + r * gy - xv * (r * r * r * dot)
        dx_ref[...] = dx
        dxb_ref[...] = dx.astype(bf16)

        @pl.when(pl.program_id(0) == 0)
        def _():
            dg_ref[...] = jnp.zeros_like(dg_ref)

        dg_ref[...] += jnp.sum(dy * xv * r, axis=0, keepdims=True)

    row = pl.BlockSpec((tm, d), lambda i: (i, 0))
    vec = pl.BlockSpec((1, d), lambda i: (0, 0))
    return pl.pallas_call(
        body, grid=(t // tm,), in_specs=[row, vec, row, row], out_specs=[row, row, vec],
        out_shape=[jax.ShapeDtypeStruct((t, d), f32), jax.ShapeDtypeStruct((t, d), bf16), jax.ShapeDtypeStruct((1, d), f32)],
        compiler_params=_params("arbitrary"), name="rms_bwd")(x, g.reshape(1, d), dh, dres)


def _loss_head(x, g, target):
    t, d = x.shape
    tm = min(t, 256)

    def body(x_ref, g_ref, t_ref, loss_ref, dx_ref, dxb_ref, dg_ref):
        xv = x_ref[...]
        r = lax.rsqrt(jnp.mean(xv * xv, axis=-1, keepdims=True) + EPS)
        xn = xv * r
        err = xn * g_ref[...] - t_ref[...]
        dy = err * (1.0 / d)
        gy = dy * g_ref[...]
        dot = jnp.mean(xv * gy, axis=-1, keepdims=True)
        dx = r * gy - xv * (r * r * r * dot)
        dx_ref[...] = dx
        dxb_ref[...] = dx.astype(bf16)

        @pl.when(pl.program_id(0) == 0)
        def _():
            dg_ref[...] = jnp.zeros_like(dg_ref)
            loss_ref[...] = jnp.zeros_like(loss_ref)

        dg_ref[...] += jnp.sum(dy * xn, axis=0, keepdims=True)
        loss_ref[...] += 0.5 * jnp.sum(jnp.sum(err * err, axis=-1, keepdims=True) * (1.0 / d), axis=0, keepdims=True)

    row = pl.BlockSpec((tm, d), lambda i: (i, 0))
    vec = pl.BlockSpec((1, d), lambda i: (0, 0))
    one = pl.BlockSpec((1, 1), lambda i: (0, 0))
    return pl.pallas_call(
        body, grid=(t // tm,), in_specs=[row, vec, row], out_specs=[one, row, row, vec],
        out_shape=[jax.ShapeDtypeStruct((1, 1), f32), jax.ShapeDtypeStruct((t, d), f32),
                   jax.ShapeDtypeStruct((t, d), bf16), jax.ShapeDtypeStruct((1, d), f32)],
        compiler_params=_params("arbitrary"), name="loss_head")(x, g.reshape(1, d), target)


def _attn_consts():
    qt, kw = 2 * CHUNK, (A_LEFT_CHUNKS + 2) * CHUNK
    r = np.arange(qt)[:, None]
    kc = np.arange(kw)[None, :]
    rel = np.clip(r + A_LEFT_CHUNKS * CHUNK - kc, -A_MAX_REL, A_MAX_REL) + A_MAX_REL
    dchunk = kc // CHUNK - r // CHUNK
    valid = (dchunk >= 0) & (dchunk <= A_LEFT_CHUNKS)
    m = np.arange(kw + qt)
    relidx = np.clip(A_LEFT_CHUNKS * CHUNK - (m - (qt - 1)), -A_MAX_REL, A_MAX_REL) + A_MAX_REL
    onehot = np.zeros((kw + qt, 2 * A_MAX_REL + 1), np.float32)
    onehot[m, relidx] = 1.0
    return qt, kw, rel, valid, onehot


def _bias_table(rel_bias):
    qt, kw, _, valid, onehot = _attn_consts()
    h = rel_bias.shape[0]
    w = kw + qt
    relidx = np.argmax(onehot, axis=1)
    e = jnp.roll(jnp.take(rel_bias, jnp.asarray(relidx), axis=1), -(qt - 1), axis=1)
    rows = jnp.broadcast_to(e[:, None, :], (h, qt, w)).reshape(h, qt * w)
    skew = rows[:, :qt * (w - 1)].reshape(h, qt, w - 1)[:, :, :kw]
    return jnp.where(jnp.asarray(valid)[None], skew, NEG_INF).astype(f32)


def _bias_table_grad(dtab):
    qt, kw, _, _, onehot = _attn_consts()
    h = dtab.shape[0]
    w = kw + qt
    xp = jnp.pad(dtab[:, ::-1, :], ((0, 0), (0, 0), (0, w + 1 - kw)))
    skew = xp.reshape(h, qt * (w + 1))[:, :qt * w].reshape(h, qt, w)
    de = jnp.sum(skew, axis=1)
    return jnp.dot(de, jnp.asarray(onehot), precision=lax.Precision.HIGHEST)


def _attn_scores(q_ref, kpad, btab_ref, r0, qt, kw, pad):
    qv = q_ref[pl.ds(r0, qt), :]
    kwin = kpad[pl.ds(r0, kw), :]
    s = lax.dot_general(qv, kwin, _DN["nt"], preferred_element_type=f32) * (A_HEAD_DIM ** -0.5) + btab_ref[...]
    kcol = lax.broadcasted_iota(jnp.int32, (qt, kw), 1)
    s = jnp.where(r0 + kcol >= pad, s, NEG_INF)
    p = jnp.exp(s - jnp.max(s, axis=-1, keepdims=True))
    return qv, kwin, p / jnp.sum(p, axis=-1, keepdims=True)


def _attn_fwd(proj, btab, heads):
    t = proj.shape[0]
    qt, kw = btab.shape[1], btab.shape[2]
    pad = kw - qt

    def body(q_ref, k_ref, v_ref, btab_ref, o_ref, kpad, vpad):
        zeros = jnp.zeros((pad, A_HEAD_DIM), bf16)
        kpad[pl.ds(0, pad), :] = zeros
        vpad[pl.ds(0, pad), :] = zeros
        kpad[pl.ds(pad, t), :] = k_ref[...]
        vpad[pl.ds(pad, t), :] = v_ref[...]

        def tile(i, carry):
            r0 = pl.multiple_of(i * qt, qt)
            _, _, p = _attn_scores(q_ref, kpad, btab_ref, r0, qt, kw, pad)
            o = lax.dot_general(p.astype(bf16), vpad[pl.ds(r0, kw), :], _DN["nn"], preferred_element_type=f32)
            o_ref[pl.ds(r0, qt), :] = o.astype(bf16)
            return carry

        lax.fori_loop(0, t // qt, tile, 0, unroll=2)

    col = lambda off: pl.BlockSpec((t, A_HEAD_DIM), lambda h, off=off: (0, off + h))
    return pl.pallas_call(
        body, grid=(heads,),
        in_specs=[col(0), col(heads), col(2 * heads), pl.BlockSpec((None, qt, kw), lambda h: (h, 0, 0))],
        out_specs=col(0), out_shape=jax.ShapeDtypeStruct((t, heads * A_HEAD_DIM), bf16),
        scratch_shapes=[pltpu.VMEM((t + pad, A_HEAD_DIM), bf16)] * 2,
        compiler_params=_params("parallel"), name="attn_fwd")(proj, proj, proj, btab)


def _attn_bwd(proj, dmix, btab, heads):
    t = proj.shape[0]
    qt, kw = btab.shape[1], btab.shape[2]
    pad = kw - qt
    scale = A_HEAD_DIM ** -0.5

    def body(q_ref, k_ref, v_ref, do_ref, btab_ref, dq_ref, dk_ref, dv_ref, dtab_ref, kpad, vpad, dkacc, dvacc):
        zeros = jnp.zeros((pad, A_HEAD_DIM), bf16)
        kpad[pl.ds(0, pad), :] = zeros
        vpad[pl.ds(0, pad), :] = zeros
        kpad[pl.ds(pad, t), :] = k_ref[...]
        vpad[pl.ds(pad, t), :] = v_ref[...]
        dkacc[...] = jnp.zeros_like(dkacc)
        dvacc[...] = jnp.zeros_like(dvacc)
        dtab_ref[...] = jnp.zeros_like(dtab_ref)

        def tile(i, carry):
            r0 = pl.multiple_of(i * qt, qt)
            qv, kwin, p = _attn_scores(q_ref, kpad, btab_ref, r0, qt, kw, pad)
            dov = do_ref[pl.ds(r0, qt), :]
            dp = lax.dot_general(dov, vpad[pl.ds(r0, kw), :], _DN["nt"], preferred_element_type=f32)
            ds = p * (dp - jnp.sum(p * dp, axis=-1, keepdims=True))
            dtab_ref[...] += ds
            dsb = ds.astype(bf16)
            dq = lax.dot_general(dsb, kwin, _DN["nn"], preferred_element_type=f32) * scale
            dq_ref[pl.ds(r0, qt), :] = dq.astype(bf16)
            dkacc[pl.ds(r0, kw), :] += lax.dot_general(dsb, qv, _DN["tn"], preferred_element_type=f32) * scale
            dvacc[pl.ds(r0, kw), :] += lax.dot_general(p.astype(bf16), dov, _DN["tn"], preferred_element_type=f32)
            return carry

        lax.fori_loop(0, t // qt, tile, 0, unroll=2)
        dk_ref[...] = dkacc[pl.ds(pad, t), :].astype(bf16)
        dv_ref[...] = dvacc[pl.ds(pad, t), :].astype(bf16)

    col = lambda off: pl.BlockSpec((t, A_HEAD_DIM), lambda h, off=off: (0, off + h))
    tab = pl.BlockSpec((None, qt, kw), lambda h: (h, 0, 0))
    wide = jax.ShapeDtypeStruct((t, heads * A_HEAD_DIM), bf16)
    return pl.pallas_call(
        body, grid=(heads,), in_specs=[col(0), col(heads), col(2 * heads), col(0), tab],
        out_specs=[col(0), col(0), col(0), tab],
        out_shape=[wide, wide, wide, jax.ShapeDtypeStruct((heads, qt, kw), f32)],
        scratch_shapes=[pltpu.VMEM((t + pad, A_HEAD_DIM), bf16)] * 2 + [pltpu.VMEM((t + pad, A_HEAD_DIM), f32)] * 2,
        compiler_params=_params("parallel"), name="attn_bwd")(proj, proj, proj, dmix, btab)


def _shift_down(z, k):
    rows = lax.broadcasted_iota(jnp.int32, z.shape, 0)
    return jnp.where(rows >= k, pltpu.roll(z, k, 0), 0.0)


def _shift_up(z, k):
    t = z.shape[0]
    rows = lax.broadcasted_iota(jnp.int32, z.shape, 0)
    return jnp.where(rows < t - k, pltpu.roll(z, t - k, 0), 0.0)


def _conv_fwd(proj, conv_w, a_blocks, b_blocks):
    t = proj.shape[0]

    def body(b_ref, c_ref, h_ref, w_ref, o_ref):
        z = c_ref[...].astype(f32) * h_ref[...].astype(f32)
        w = w_ref[...]
        y = w[0:1, :] * _shift_down(z, 2) + w[1:2, :] * _shift_down(z, 1) + w[2:3, :] * z
        o_ref[...] = (b_ref[...].astype(f32) * y).astype(bf16)

    col = lambda off: pl.BlockSpec((t, LANE), lambda i, off=off: (0, off + i))
    return pl.pallas_call(
        body, grid=(b_blocks,),
        in_specs=[col(3 * a_blocks), col(3 * a_blocks + b_blocks), col(3 * a_blocks + 2 * b_blocks),
                  pl.BlockSpec((CONV_WIDTH, LANE), lambda i: (0, i))],
        out_specs=col(0), out_shape=jax.ShapeDtypeStruct((t, b_blocks * LANE), bf16),
        compiler_params=_params("parallel"), name="conv_fwd")(proj, proj, proj, conv_w)


def _conv_bwd(proj, dmix, conv_w, a_blocks, b_blocks):
    t = proj.shape[0]

    def body(b_ref, c_ref, h_ref, do_ref, w_ref, db_ref, dc_ref, dh_ref, dw_ref):
        bv, cv, hv = b_ref[...].astype(f32), c_ref[...].astype(f32), h_ref[...].astype(f32)
        w = w_ref[...]
        z = cv * hv
        z1, z2 = _shift_down(z, 1), _shift_down(z, 2)
        y = w[0:1, :] * z2 + w[1:2, :] * z1 + w[2:3, :] * z
        dov = do_ref[...].astype(f32)
        db_ref[...] = (dov * y).astype(bf16)
        dy = dov * bv
        dz = w[2:3, :] * dy + w[1:2, :] * _shift_up(dy, 1) + w[0:1, :] * _shift_up(dy, 2)
        dc_ref[...] = (dz * hv).astype(bf16)
        dh_ref[...] = (dz * cv).astype(bf16)
        dw_ref[0:1, :] = jnp.sum(dy * z2, axis=0, keepdims=True)
        dw_ref[1:2, :] = jnp.sum(dy * z1, axis=0, keepdims=True)
        dw_ref[2:3, :] = jnp.sum(dy * z, axis=0, keepdims=True)

    col = lambda off: pl.BlockSpec((t, LANE), lambda i, off=off: (0, off + i))
    wspec = pl.BlockSpec((CONV_WIDTH, LANE), lambda i: (0, i))
    wide = jax.ShapeDtypeStruct((t, b_blocks * LANE), bf16)
    return pl.pallas_call(
        body, grid=(b_blocks,),
        in_specs=[col(3 * a_blocks), col(3 * a_blocks + b_blocks), col(3 * a_blocks + 2 * b_blocks), col(a_blocks), wspec],
        out_specs=[col(0), col(0), col(0), wspec],
        out_shape=[wide, wide, wide, jax.ShapeDtypeStruct((CONV_WIDTH, b_blocks * LANE), f32)],
        compiler_params=_params("parallel"), name="conv_bwd")(proj, proj, proj, dmix, conv_w)


_RSQRT2 = 0.7071067811865476
_RSQRT2PI = 0.3989422804014327


def _gelu(x):
    return 0.5 * x * (1.0 + lax.erf(x * _RSQRT2))


def _gelu_grad(x):
    return 0.5 * (1.0 + lax.erf(x * _RSQRT2)) + x * jnp.exp(-0.5 * x * x) * _RSQRT2PI


def _sgu_common(a_ref, lg_ref, lb_ref, cw):
    av = a_ref[...]
    u = _gelu(av[:, :cw])
    v = _gelu(av[:, cw:])
    mu = jnp.mean(v, axis=-1, keepdims=True)
    xc = v - mu
    rstd = lax.rsqrt(jnp.mean(xc * xc, axis=-1, keepdims=True) + EPS)
    xhat = xc * rstd
    vln = xhat * lg_ref[...] + lb_ref[...]
    pos_t = lax.broadcasted_iota(jnp.int32, (C_BLOCK, C_BLOCK), 0) // CHUNK
    pos_s = lax.broadcasted_iota(jnp.int32, (C_BLOCK, C_BLOCK), 1) // CHUNK
    return av, u, xhat, rstd, vln, pos_s <= pos_t


def _sgu_fwd(a, ln_g, ln_b, w_s, bs_t):
    t, cw2 = a.shape
    cw = cw2 // 2
    groups = w_s.shape[0]
    cg = cw // groups

    def body(a_ref, lg_ref, lb_ref, ws_ref, bs_ref, m_ref):
        _, u, _, _, vln, mask = _sgu_common(a_ref, lg_ref, lb_ref, cw)
        vb = vln.astype(bf16)
        for g in range(groups):
            sl = slice(g * cg, (g + 1) * cg)
            wm = jnp.where(mask, ws_ref[g], 0.0).astype(bf16)
            s = lax.dot_general(wm, vb[:, sl], _DN["nn"], preferred_element_type=f32) + bs_ref[:, g:g + 1]
            m_ref[:, sl] = (u[:, sl] * s).astype(bf16)

    vec = pl.BlockSpec((1, cw), lambda n: (0, 0))
    return pl.pallas_call(
        body, grid=(t // C_BLOCK,),
        in_specs=[pl.BlockSpec((C_BLOCK, cw2), lambda n: (n, 0)), vec, vec,
                  pl.BlockSpec((groups, C_BLOCK, C_BLOCK), lambda n: (0, 0, 0)),
                  pl.BlockSpec((C_BLOCK, groups), lambda n: (0, 0))],
        out_specs=pl.BlockSpec((C_BLOCK, cw), lambda n: (n, 0)), out_shape=jax.ShapeDtypeStruct((t, cw), bf16),
        compiler_params=_params("parallel"), name="sgu_fwd")(a, ln_g.reshape(1, cw), ln_b.reshape(1, cw), w_s, bs_t)


def _sgu_bwd(a, dm, ln_g, ln_b, w_s, bs_t):
    t, cw2 = a.shape
    cw = cw2 // 2
    groups = w_s.shape[0]
    cg = cw // groups

    def body(a_ref, dm_ref, lg_ref, lb_ref, ws_ref, bs_ref, da_ref, dws_ref, dbs_ref, dlg_ref, dlb_ref, dvln):
        @pl.when(pl.program_id(0) == 0)
        def _():
            dws_ref[...] = jnp.zeros_like(dws_ref)
            dbs_ref[...] = jnp.zeros_like(dbs_ref)
            dlg_ref[...] = jnp.zeros_like(dlg_ref)
            dlb_ref[...] = jnp.zeros_like(dlb_ref)

        av, u, xhat, rstd, vln, mask = _sgu_common(a_ref, lg_ref, lb_ref, cw)
        vb = vln.astype(bf16)
        lane = lax.broadcasted_iota(jnp.int32, (C_BLOCK, groups), 1)
        dbs = jnp.zeros((C_BLOCK, groups), f32)
        for g in range(groups):
            sl = slice(g * cg, (g + 1) * cg)
            wm = jnp.where(mask, ws_ref[g], 0.0).astype(bf16)
            s = lax.dot_general(wm, vb[:, sl], _DN["nn"], preferred_element_type=f32) + bs_ref[:, g:g + 1]
            dmg = dm_ref[:, sl].astype(f32)
            da_ref[:, sl] = (dmg * s * _gelu_grad(av[:, sl])).astype(bf16)
            dsg = dmg * u[:, sl]
            dbs = dbs + jnp.where(lane == g, jnp.sum(dsg, axis=-1, keepdims=True), 0.0)
            dsb = dsg.astype(bf16)
            dws_ref[g] += jnp.where(mask, lax.dot_general(dsb, vb[:, sl], _DN["nt"], preferred_element_type=f32), 0.0)
            dvln[:, sl] = lax.dot_general(wm, dsb, _DN["tn"], preferred_element_type=f32)
        dbs_ref[...] += dbs
        dv = dvln[...]
        dlg_ref[...] += jnp.sum(dv * xhat, axis=0, keepdims=True)
        dlb_ref[...] += jnp.sum(dv, axis=0, keepdims=True)
        dxh = dv * lg_ref[...]
        dvv = rstd * (dxh - jnp.mean(dxh, axis=-1, keepdims=True) - xhat * jnp.mean(dxh * xhat, axis=-1, keepdims=True))
        da_ref[:, cw:] = (dvv * _gelu_grad(av[:, cw:])).astype(bf16)

    vec = pl.BlockSpec((1, cw), lambda n: (0, 0))
    wsp = pl.BlockSpec((groups, C_BLOCK, C_BLOCK), lambda n: (0, 0, 0))
    bsp = pl.BlockSpec((C_BLOCK, groups), lambda n: (0, 0))
    return pl.pallas_call(
        body, grid=(t // C_BLOCK,),
        in_specs=[pl.BlockSpec((C_BLOCK, cw2), lambda n: (n, 0)), pl.BlockSpec((C_BLOCK, cw), lambda n: (n, 0)), vec, vec, wsp, bsp],
        out_specs=[pl.BlockSpec((C_BLOCK, cw2), lambda n: (n, 0)), wsp, bsp, vec, vec],
        out_shape=[jax.ShapeDtypeStruct((t, cw2), bf16), jax.ShapeDtypeStruct(w_s.shape, f32),
                   jax.ShapeDtypeStruct(bs_t.shape, f32), jax.ShapeDtypeStruct((1, cw), f32), jax.ShapeDtypeStruct((1, cw), f32)],
        scratch_shapes=[pltpu.VMEM((C_BLOCK, cw), f32)],
        compiler_params=_params("arbitrary"), name="sgu_bwd")(a, dm, ln_g.reshape(1, cw), ln_b.reshape(1, cw), w_s, bs_t)


HBM = pl.BlockSpec(memory_space=pltpu.HBM)
SEM = pl.BlockSpec(memory_space=pltpu.SEMAPHORE)
EFFECT = pltpu.SideEffectType.DATAFLOW_SIDE_EFFECTING


def _place():
    x, y, c = lax.axis_index("x"), lax.axis_index("y"), lax.axis_index("c")
    return x, y, c, [(1 - x, y), (x, 1 - y), (1 - x, 1 - y)]


def _remote(src, dst, send_sems, recv_sems, k, to):
    return pltpu.make_async_remote_copy(src_ref=src, dst_ref=dst, send_sem=send_sems.at[k], recv_sem=recv_sems.at[k],
                                        device_id=to, device_id_type=MESH)


def _split_start(name, arrays, plan, n_copies, after):
    n = len(arrays)

    def body(*refs):
        send_sems, recv_sems, token = refs[n + 1], refs[n + 2], refs[-1]
        for cp in plan(refs[:n], send_sems, recv_sems):
            cp.start()
        token[...] = jnp.zeros_like(token)

    out = pl.pallas_call(
        body, name=name,
        out_shape=(pltpu.SemaphoreType.DMA((n_copies,)), pltpu.SemaphoreType.DMA((n_copies,)),
                   *[pltpu.HBM(a.shape, a.dtype) for a in arrays], jax.ShapeDtypeStruct((8, LANE), f32)),
        in_specs=[HBM] * n + [ANY], out_specs=(SEM, SEM, *[HBM] * n, pl.BlockSpec(memory_space=pltpu.VMEM)),
        input_output_aliases={i: 2 + i for i in range(n)},
        compiler_params=pltpu.CompilerParams(has_side_effects=EFFECT),
    )(*[pltpu.with_memory_space_constraint(a, pltpu.HBM) for a in arrays], after)
    return (out[0], out[1]), list(out[2:2 + n]), out[-1]


def _split_wait(name, arrays, sems, plan, after):
    n = len(arrays)

    def body(*refs):
        for cp in plan(refs[:n], refs[n], refs[n + 1]):
            cp.wait()

    out = pl.pallas_call(
        body, name=name, out_shape=tuple(pltpu.HBM(a.shape, a.dtype) for a in arrays),
        in_specs=[HBM] * n + [SEM, SEM, ANY], out_specs=tuple([HBM] * n), input_output_aliases={i: i for i in range(n)},
        compiler_params=pltpu.CompilerParams(has_side_effects=EFFECT),
    )(*arrays, sems[0], sems[1], after)
    return list(out)


def _row_pieces(ref_rows, split):
    rc = ref_rows // split
    return [pl.ds(s * rc, rc) for s in range(split)]


def _plan_gather_chips(bufs, send_sems, recv_sems):
    x, y, c, chips = _place()
    me = 4 * x + 2 * y + c
    cps, k = [], 0
    for b in bufs:
        for rows in _row_pieces(b.shape[1], D2D_SPLIT):
            cps.append(_remote(b.at[me, rows], b.at[me, rows], send_sems, recv_sems, k, (x, y, 1 - c)))
            k += 1
        for px, py in chips:
            cps.append(_remote(b.at[me], b.at[me], send_sems, recv_sems, k, (px, py, c)))
            k += 1
    return cps


def _plan_gather_forward(bufs, send_sems, recv_sems):
    x, y, c, chips = _place()
    cps, k = [], 0
    for b in bufs:
        for px, py in chips:
            slot = 4 * px + 2 * py + c
            for rows in _row_pieces(b.shape[1], D2D_SPLIT):
                cps.append(_remote(b.at[slot, rows], b.at[slot, rows], send_sems, recv_sems, k, (x, y, 1 - c)))
                k += 1
    return cps


def _plan_rs_sibling(arrs, send_sems, recv_sems):
    n = len(arrs) // 2
    x, y, c, _ = _place()
    cps, k = [], 0
    for g, got in zip(arrs[:n], arrs[n:]):
        for q in range(N_CHIP):
            for rows in _row_pieces(g.shape[1], RS_SPLIT):
                cps.append(_remote(g.at[N_CHIP * (1 - c) + q, rows], got.at[q, rows], send_sems, recv_sems, k, (x, y, 1 - c)))
                k += 1
    return cps


def _plan_rs_chips(arrs, send_sems, recv_sems):
    n = len(arrs) // 2
    x, y, c, chips = _place()
    q = 2 * x + y
    cps, k = [], 0
    for p, r in zip(arrs[:n], arrs[n:]):
        for px, py in chips:
            cps.append(_remote(p.at[2 * px + py], r.at[q], send_sems, recv_sems, k, (px, py, c)))
            k += 1
    return cps


class _Gather:
    def __init__(self, tag, bufs):
        self.tag, self.bufs, self.n = tag, bufs, len(bufs)

    def start_chips(self, after):
        self.sems, self.bufs, self.token = _split_start("gather_chips_start_" + self.tag, self.bufs, _plan_gather_chips,
                                                        self.n * (D2D_SPLIT + 3), after)

    def start_forward(self, after):
        self.bufs = _split_wait("gather_chips_wait_" + self.tag, self.bufs, self.sems, _plan_gather_chips, after)
        self.sems, self.bufs, _ = _split_start("gather_fwd_start_" + self.tag, self.bufs, _plan_gather_forward,
                                               self.n * 3 * D2D_SPLIT, after)

    def ready(self, after):
        return _split_wait("gather_fwd_wait_" + self.tag, self.bufs, self.sems, _plan_gather_forward, after)


class _ReduceScatter:
    def __init__(self, tag, grads, core):
        self.tag, self.n = tag, len(grads)
        lands = [lax.empty((N_CHIP,) + g.shape[1:], g.dtype) for g in grads]
        self.sems, self.arrs, self.token = _split_start("rs_sibling_start_" + tag, list(grads) + lands, _plan_rs_sibling,
                                                        self.n * N_CHIP * RS_SPLIT, core)

    def middle(self, after, core):
        arrs = _split_wait("rs_sibling_wait_" + self.tag, self.arrs, self.sems, _plan_rs_sibling, after)
        parts = [_pair_sum(g, got, core) for g, got in zip(arrs[:self.n], arrs[self.n:])]
        lands = [lax.empty(p.shape, p.dtype) for p in parts]
        self.sems, self.arrs, self.token = _split_start("rs_chips_start_" + self.tag, parts + lands, _plan_rs_chips,
                                                        self.n * 3, core)

    def finish(self, after):
        arrs = _split_wait("rs_chips_wait_" + self.tag, self.arrs, self.sems, _plan_rs_chips, after)
        return list(zip(arrs[:self.n], arrs[self.n:]))


def _cast_into_slot(name, w, layer, me, after):
    _, rows, cols = w.shape
    tr = 256 if rows % 256 == 0 else rows

    def body(me_ref, w_ref, after_ref, o_ref):
        o_ref[...] = w_ref[...].astype(bf16)

    return pl.pallas_call(
        body,
        grid_spec=pltpu.PrefetchScalarGridSpec(
            num_scalar_prefetch=1, grid=(rows // tr,),
            in_specs=[pl.BlockSpec((None, tr, cols), lambda i, me_ref: (layer, i, 0)), ANY],
            out_specs=pl.BlockSpec((None, tr, cols), lambda i, me_ref: (me_ref[0], i, 0))),
        out_shape=jax.ShapeDtypeStruct((N_DEV, rows, cols), bf16), compiler_params=_params("parallel"), name=name)(me, w, after)


def _pair_sum(g, got, core):
    _, rows, cols = g.shape
    tr = 512 if rows % 512 == 0 else rows

    def body(c_ref, a_ref, b_ref, o_ref):
        o_ref[...] = (a_ref[...].astype(f32) + b_ref[...].astype(f32)).astype(bf16)

    spec = pl.BlockSpec((None, tr, cols), lambda q, i, c_ref: (q, i, 0))
    return pl.pallas_call(
        body,
        grid_spec=pltpu.PrefetchScalarGridSpec(
            num_scalar_prefetch=1, grid=(N_CHIP, rows // tr),
            in_specs=[pl.BlockSpec((None, tr, cols), lambda q, i, c_ref: (N_CHIP * c_ref[0] + q, i, 0)), spec],
            out_specs=spec),
        out_shape=jax.ShapeDtypeStruct((N_CHIP, rows, cols), bf16), compiler_params=_params("parallel", "parallel"),
        name="pair_sum")(core, g, got)


def _gather_copies(n, ins, outs, send_sems, recv_sems, local_sems):
    x, y, c, chips = _place()
    sibling = (x, y, 1 - c)

    def slot(px, py, pc):
        return 4 * px + 2 * py + pc

    def copy(i, k, block, to, src=None):
        dst = outs[i].at[slot(*block)]
        return pltpu.make_async_remote_copy(src_ref=dst if src is None else src, dst_ref=dst, send_sem=send_sems.at[i, k],
                                            recv_sem=recv_sems.at[i, k], device_id=to, device_id_type=MESH)

    started = []
    for i in range(n):
        mine = pltpu.make_async_copy(ins[i], outs[i].at[slot(x, y, c)], local_sems.at[i])
        mine.start()
        started.append(mine)
    sends = []
    for i in range(n):
        sends.append(copy(i, 0, (x, y, c), sibling, src=ins[i]))
        sends += [copy(i, 1 + j, (x, y, c), (*chip, c), src=ins[i]) for j, chip in enumerate(chips)]
    for cp in sends:
        cp.start()
    for i in range(n):
        for j, chip in enumerate(chips):
            copy(i, 1 + j, (*chip, c), (x, y, c)).wait_recv()
            fwd = copy(i, 4 + j, (*chip, c), sibling)
            fwd.start()
            sends.append(fwd)
    for i in range(n):
        copy(i, 0, sibling, (x, y, c)).wait_recv()
        for j, chip in enumerate(chips):
            copy(i, 4 + j, (*chip, 1 - c), (x, y, c)).wait_recv()
    for cp in sends:
        cp.wait_send()
    for mine in started:
        mine.wait()


def _gather_small(name, packed, reduce):
    rows = packed.shape[0]

    def body(x_ref, o_ref, buf, send_sems, recv_sems, local_sems):
        _gather_copies(1, [x_ref], [buf], send_sems, recv_sems, local_sems)
        if reduce:
            acc = buf[0]
            for j in range(1, N_DEV):
                acc = acc + buf[j]
            o_ref[...] = acc
        else:
            o_ref[...] = buf[...]

    vm = pl.BlockSpec(memory_space=pltpu.VMEM)
    return pl.pallas_call(
        body, in_specs=[vm], out_specs=vm,
        out_shape=jax.ShapeDtypeStruct((rows, LANE) if reduce else (N_DEV, rows, LANE), f32),
        scratch_shapes=[pltpu.VMEM((N_DEV, rows, LANE), f32), pltpu.SemaphoreType.DMA((1, 7)), pltpu.SemaphoreType.DMA((1, 7)),
                        pltpu.SemaphoreType.DMA((1,))],
        compiler_params=pltpu.CompilerParams(vmem_limit_bytes=VMEM_LIMIT), name=name)(packed)


def _pack(arrs):
    flat = jnp.concatenate([a.reshape(-1).astype(f32) for a in arrs])
    rows = -(-flat.shape[0] // (8 * LANE)) * 8
    return jnp.pad(flat, (0, rows * LANE - flat.shape[0])).reshape(rows, LANE)


def _unpack(buf, shapes):
    flat = buf.reshape(-1)
    out, off = [], 0
    for s in shapes:
        n = int(np.prod(s))
        out.append(flat[off:off + n].reshape(s))
        off += n
    return out


def _adam_math(w, g, m, v):
    m2 = ADAM_B1 * m + (1.0 - ADAM_B1) * g
    v2 = ADAM_B2 * v + (1.0 - ADAM_B2) * (g * g)
    m_hat = m2 / (1.0 - ADAM_B1 ** ADAM_STEP)
    v_hat = v2 / (1.0 - ADAM_B2 ** ADAM_STEP)
    delta = -ADAM_LR * (m_hat / (jnp.sqrt(v_hat) + ADAM_EPS) + ADAM_WD * w)
    return delta, m2, v2


def _adam_big(name, w, m, v, parts, chip):
    layers, rows, cols = w.shape
    tr = 256 if rows % 256 == 0 else 64 if rows % 64 == 0 else 8

    def body(chip_ref, w_ref, m_ref, v_ref, *rest):
        p_refs = rest[:N_CHIP * layers]
        g_ref, d_ref, m2_ref, v2_ref = rest[N_CHIP * layers:]
        for li in range(layers):
            @pl.when(pl.program_id(0) == li)
            def _(li=li):
                g = p_refs[N_CHIP * li][...].astype(f32)
                for q in range(1, N_CHIP):
                    g = g + p_refs[N_CHIP * li + q][...].astype(f32)
                delta, m2, v2 = _adam_math(w_ref[...], g, m_ref[...], v_ref[...])
                g_ref[...] = g
                d_ref[...] = delta
                m2_ref[...] = m2
                v2_ref[...] = v2

    spec = pl.BlockSpec((None, tr, cols), lambda l, i, c_ref: (l, i, 0))
    pspecs, operands = [], []
    for li in range(layers):
        for q in range(N_CHIP):
            pspecs.append(pl.BlockSpec((None, tr, cols),
                                       lambda l, i, c_ref, li=li, q=q: ((c_ref[0] + q) % N_CHIP, jnp.where(l == li, i, 0), 0)))
            operands.append(parts[li][0] if q == 0 else parts[li][1])
    out = jax.ShapeDtypeStruct((layers, rows, cols), f32)
    return pl.pallas_call(
        body,
        grid_spec=pltpu.PrefetchScalarGridSpec(num_scalar_prefetch=1, grid=(layers, rows // tr),
                                               in_specs=[spec, spec, spec] + pspecs, out_specs=[spec] * 4),
        out_shape=[out] * 4, compiler_params=_params("arbitrary", "arbitrary"), name=name)(chip, w, m, v, *operands)


def _adam_small(w, g, m, v):
    rows = w.shape[0]

    def body(w_ref, g_ref, m_ref, v_ref, d_ref, m2_ref, v2_ref):
        delta, m2, v2 = _adam_math(w_ref[...], g_ref[...], m_ref[...], v_ref[...])
        d_ref[...] = delta
        m2_ref[...] = m2
        v2_ref[...] = v2

    out = jax.ShapeDtypeStruct((rows, LANE), f32)
    return pl.pallas_call(body, out_shape=[out] * 3, name="adam_small")(w, g, m, v)


def kernel(x, mix_norm, ab_w_in, ab_rel_bias, ab_conv_w, ab_w_out, c_w_in, c_ln_g, c_ln_b, c_w_s, c_b_s, c_w_out, ffn_norm, ffn_w_gate, ffn_w_up, ffn_w_down, final_norm, loss_target, m_mix_norm, m_ab_w_in, m_ab_rel_bias, m_ab_conv_w, m_ab_w_out, m_c_w_in, m_c_ln_g, m_c_ln_b, m_c_w_s, m_c_b_s, m_c_w_out, m_ffn_norm, m_ffn_w_gate, m_ffn_w_up, m_ffn_w_down, m_final_norm, v_mix_norm, v_ab_w_in, v_ab_rel_bias, v_ab_conv_w, v_ab_w_out, v_c_w_in, v_c_ln_g, v_c_ln_b, v_c_w_s, v_c_b_s, v_c_w_out, v_ffn_norm, v_ffn_w_gate, v_ffn_w_up, v_ffn_w_down, v_final_norm):
    d = D_MODEL
    a_width = d // 2
    heads = a_width // A_HEAD_DIM
    a_blocks = a_width // LANE
    b_blocks = (d - a_width) // LANE
    n_even, n_odd = (DEPTH + 1) // 2, DEPTH // 2
    me_s = 4 * lax.axis_index("x") + 2 * lax.axis_index("y") + lax.axis_index("c")
    me = me_s.astype(jnp.int32).reshape(1)
    core = lax.axis_index("c").astype(jnp.int32).reshape(1)
    chip = (2 * lax.axis_index("x") + lax.axis_index("y")).astype(jnp.int32).reshape(1)

    weights = dict(mix_norm=mix_norm, ab_w_in=ab_w_in, ab_rel_bias=ab_rel_bias, ab_conv_w=ab_conv_w, ab_w_out=ab_w_out,
                   c_w_in=c_w_in, c_ln_g=c_ln_g, c_ln_b=c_ln_b, c_w_s=c_w_s, c_b_s=c_b_s, c_w_out=c_w_out,
                   ffn_norm=ffn_norm, ffn_w_gate=ffn_w_gate, ffn_w_up=ffn_w_up, ffn_w_down=ffn_w_down, final_norm=final_norm)
    mom_m = dict(mix_norm=m_mix_norm, ab_w_in=m_ab_w_in, ab_rel_bias=m_ab_rel_bias, ab_conv_w=m_ab_conv_w, ab_w_out=m_ab_w_out,
                 c_w_in=m_c_w_in, c_ln_g=m_c_ln_g, c_ln_b=m_c_ln_b, c_w_s=m_c_w_s, c_b_s=m_c_b_s, c_w_out=m_c_w_out,
                 ffn_norm=m_ffn_norm, ffn_w_gate=m_ffn_w_gate, ffn_w_up=m_ffn_w_up, ffn_w_down=m_ffn_w_down, final_norm=m_final_norm)
    mom_v = dict(mix_norm=v_mix_norm, ab_w_in=v_ab_w_in, ab_rel_bias=v_ab_rel_bias, ab_conv_w=v_ab_conv_w, ab_w_out=v_ab_w_out,
                 c_w_in=v_c_w_in, c_ln_g=v_c_ln_g, c_ln_b=v_c_ln_b, c_w_s=v_c_w_s, c_b_s=v_c_b_s, c_w_out=v_c_w_out,
                 ffn_norm=v_ffn_norm, ffn_w_gate=v_ffn_w_gate, ffn_w_up=v_ffn_w_up, ffn_w_down=v_ffn_w_down, final_norm=v_final_norm)
    order = list(weights)
    wide = ("ffn_w_gate", "ffn_w_up")
    flip = lambda a: jnp.swapaxes(a, 1, 2)
    local = {k: (flip(weights[k]), flip(mom_m[k]), flip(mom_v[k])) if k in wide else (weights[k], mom_m[k], mom_v[k]) for k in order}

    sharded_small = [ab_conv_w, c_ln_g, c_ln_b]
    gathered = _gather_small("gather_small", _pack(sharded_small), reduce=False)
    conv_parts, lng_parts, lnb_parts = [], [], []
    for j in range(N_DEV):
        cw_j, lg_j, lb_j = _unpack(gathered[j], [a.shape for a in sharded_small])
        conv_parts.append(cw_j)
        lng_parts.append(lg_j)
        lnb_parts.append(lb_j)
    conv_full = jnp.concatenate(conv_parts, axis=-1)
    lng_full = jnp.concatenate(lng_parts, axis=-1)
    lnb_full = jnp.concatenate(lnb_parts, axis=-1)

    gate_t, up_t = local["ffn_w_gate"][0], local["ffn_w_up"][0]
    sets = []
    for layer in range(DEPTH):
        i = layer // 2
        if layer % 2 == 0:
            sets += [("ab_in%d" % i, [(ab_w_in, i)]), ("ab_out%d" % i, [(ab_w_out, i)])]
        else:
            sets += [("c_in%d" % i, [(c_w_in, i)]), ("c_out%d" % i, [(c_w_out, i)])]
        sets += [("ffn_in%d" % layer, [(gate_t, layer), (up_t, layer)]), ("ffn_out%d" % layer, [(ffn_w_down, layer)])]
    units = [None] * len(sets)
    cursor = [0]
    tokens = []

    def start_gather(k, after):
        if k < len(sets):
            tag, members = sets[k]
            units[k] = _Gather(tag, [_cast_into_slot("cast_slot", w, li, me, after) for w, li in members])
            units[k].start_chips(after)
            tokens.append(units[k].token)

    def next_weights(after):
        k = cursor[0]
        cursor[0] = k + 1
        ready = units[k].ready(after)
        if k + 1 < len(sets):
            units[k + 1].start_forward(after)
        start_gather(k + GATHER_AHEAD, after)
        return ready

    def started():
        deps = list(tokens)
        tokens.clear()
        return deps

    xs = x[0]
    tgt = loss_target[0]
    for k in range(GATHER_AHEAD):
        start_gather(k, gathered)
    units[0].start_forward(gathered)
    saved = []
    for layer in range(DEPTH):
        i = layer // 2
        (w_in_g,) = next_weights(xs)
        h = _rms_fwd(xs, mix_norm[layer])
        if layer % 2 == 0:
            proj = _mm_cols("ab_proj", h, w_in_g, bf16, started())
            btab = _bias_table(ab_rel_bias[i])
            attn = _attn_fwd(proj, btab, heads)
            conv = _conv_fwd(proj, conv_full[i], a_blocks, b_blocks)
            mixed = jnp.concatenate([attn, conv], axis=-1)
            ctx = (proj, btab)
        else:
            proj = _mm_cols("c_proj", h, w_in_g, f32, started())
            bs_t = jnp.transpose(c_b_s[i])
            mixed = _sgu_fwd(proj, lng_full[i], lnb_full[i], c_w_s[i], bs_t)
            ctx = (proj, bs_t)
        (w_out_g,) = next_weights(mixed)
        w_out_full = w_out_g.reshape(-1, w_out_g.shape[-1])
        x1 = _mm_rows_res("mix_out", mixed, w_out_full, xs, started())
        wg_g, wu_g = next_weights(x1)
        h2 = _rms_fwd(x1, ffn_norm[layer])
        g_act, u_act, act = _ffn_in(h2, wg_g, wu_g, started())
        (wd_g,) = next_weights(g_act)
        x2 = _ffn_down(act, wd_g, x1, started())
        saved.append((xs, h, ctx, mixed, x1, h2, g_act, u_act, w_in_g, w_out_full, wg_g, wu_g, wd_g))
        xs = x2

    loss_part, dx, dxb, d_final = _loss_head(xs, final_norm, tgt)
    loss = lax.psum(loss_part[0, 0], ("x", "y", "c"))

    scatters = {}
    small = {k: [None] * weights[k].shape[0] for k in ("mix_norm", "ffn_norm", "ab_rel_bias", "ab_conv_w", "c_ln_g", "c_ln_b",
                                                       "c_w_s", "c_b_s")}
    for layer in reversed(range(DEPTH)):
        i = layer // 2
        xs, h, ctx, mixed, x1, h2, g_act, u_act, w_in_g, w_out_full, wg_g, wu_g, wd_g = saved[layer]
        dg, du, act = _ffn_bwd_act(dxb, wd_g, g_act, u_act, started())
        dwd = _ffn_dwd(act, dxb)
        dwg, dwu = _ffn_dwgu(h2, dg, du)
        rs_ffn = _ReduceScatter("ffn%d" % layer, [dwg, dwu, dwd], core)
        dh2 = _ffn_dh(dg, du, wg_g, wu_g, [rs_ffn.token])
        dx, dxb, dgn = _rms_bwd(x1, ffn_norm[layer], dh2, dx)
        small["ffn_norm"][layer] = dgn[0]
        rs_ffn.middle(dxb, core)
        for pos, k in enumerate(("ffn_w_gate", "ffn_w_up", "ffn_w_down")):
            scatters[(k, layer)] = (rs_ffn, pos)
        dmixed = _mm_nt("mix_out_bwd", dxb, w_out_full, bf16, [rs_ffn.token])
        dwout = _mm_tn_rows("mix_out_dw", mixed, dxb)
        if layer % 2 == 0:
            proj, btab = ctx
            dq, dk, dv, dtab = _attn_bwd(proj, dmixed, btab, heads)
            db, dc, dhv, dcw = _conv_bwd(proj, dmixed, conv_full[i], a_blocks, b_blocks)
            dproj = jnp.concatenate([dq, dk, dv, db, dc, dhv], axis=-1)
            small["ab_rel_bias"][i] = _bias_table_grad(dtab)
            small["ab_conv_w"][i] = dcw
            names = ("ab_w_in", "ab_w_out")
            tag = "ab"
        else:
            proj, bs_t = ctx
            dproj, dws, dbs_t, dlg, dlb = _sgu_bwd(proj, dmixed, lng_full[i], lnb_full[i], c_w_s[i], bs_t)
            small["c_w_s"][i] = dws
            small["c_b_s"][i] = jnp.transpose(dbs_t)
            small["c_ln_g"][i] = dlg[0]
            small["c_ln_b"][i] = dlb[0]
            names = ("c_w_in", "c_w_out")
            tag = "c"
        dwin = _mm_tn_cols(tag + "_proj_dw", h, dproj)
        rs_mix = _ReduceScatter("%s%d" % (tag, i), [dwin, dwout], core)
        dh = _mm_nt_cols(tag + "_proj_bwd", dproj, w_in_g, [rs_mix.token])
        dx, dxb, dgm = _rms_bwd(xs, mix_norm[layer], dh, dx)
        small["mix_norm"][layer] = dgm[0]
        rs_mix.middle(dxb, core)
        tokens.append(rs_mix.token)
        scatters[(names[0], i)] = (rs_mix, 0)
        scatters[(names[1], i)] = (rs_mix, 1)
    grad_x = dx[None]

    small_names = ["mix_norm", "ffn_norm", "ab_rel_bias", "ab_conv_w", "c_ln_g", "c_ln_b", "c_w_s", "c_b_s"]
    small_full = [jnp.stack(small[k]) for k in small_names] + [d_final[0]]
    summed = _unpack(_gather_small("reduce_small", _pack(small_full), reduce=True), [a.shape for a in small_full])
    small_grads = dict(zip(small_names + ["final_norm"], summed))
    for k in ("ab_conv_w", "c_ln_g", "c_ln_b"):
        width = weights[k].shape[-1]
        small_grads[k] = lax.dynamic_slice_in_dim(small_grads[k], me_s * width, width, axis=-1)
    small_order = [k for k in order if k in small_grads]
    shapes = [weights[k].shape for k in small_order]
    d_s, m_s, v_s = _adam_small(_pack([weights[k] for k in small_order]), _pack([small_grads[k] for k in small_order]),
                                _pack([mom_m[k] for k in small_order]), _pack([mom_v[k] for k in small_order]))
    grads, deltas, new_m, new_v = dict(small_grads), {}, {}, {}
    for k, dd, mm, vv in zip(small_order, _unpack(d_s, shapes), _unpack(m_s, shapes), _unpack(v_s, shapes)):
        deltas[k], new_m[k], new_v[k] = dd, mm, vv

    finished = {}
    last = started()[-1]
    for k in ("c_w_in", "c_w_out", "ffn_w_gate", "ffn_w_up", "ffn_w_down", "ab_w_in", "ab_w_out"):
        parts = []
        w_k, m_k, v_k = local[k]
        for li in range(w_k.shape[0]):
            rs, pos = scatters[(k, li)]
            if id(rs) not in finished:
                finished[id(rs)] = rs.finish(last)
            parts.append(finished[id(rs)][pos])
        outs = _adam_big("adam_" + k, w_k, m_k, v_k, parts, chip)
        last = outs[1]
        grads[k], deltas[k], new_m[k], new_v[k] = [flip(o) for o in outs] if k in wide else outs

    return (loss, grad_x, *[grads[k] for k in order], *[deltas[k] for k in order], *[new_m[k] for k in order],
            *[new_v[k] for k in order])
```

```python
import numpy as np
import jax
import jax.numpy as jnp
from jax import lax
from jax.experimental import pallas as pl
from jax.experimental.pallas import tpu as pltpu

D_MODEL = 2048
SEQ = 2048
DEPTH = 4
CHUNK = 64
A_HEAD_DIM = 128
A_LEFT_CHUNKS = 8
A_MAX_REL = 256
CONV_WIDTH = 3
C_BLOCK = 128
C_GROUPS = 8
EPS = 1e-6
NEG_INF = -1e30

ADAM_LR = 0.001
ADAM_B1 = 0.9
ADAM_B2 = 0.999
ADAM_EPS = 1e-08
ADAM_WD = 0.01
ADAM_STEP = 10

N_DEV = 8
N_CHIP = 4
RS_SPLIT = 4
D2D_SPLIT = 2
GATHER_AHEAD = 4
ROWS_PER_STEP = 1024
LANE = 128
VMEM_LIMIT = 52 * 1024 * 1024

bf16 = jnp.bfloat16
f32 = jnp.float32
MESH = pl.DeviceIdType.MESH
ANY = pl.BlockSpec(memory_space=pl.ANY)


def _params(*sem):
    return pltpu.CompilerParams(dimension_semantics=sem, vmem_limit_bytes=VMEM_LIMIT)


def _perm(j):
    return (j % 2) * N_CHIP + j // 2


_DN = {"nn": (((1,), (0,)), ((), ())), "nt": (((1,), (1,)), ((), ())), "tn": (((0,), (0,)), ((), ()))}


def _matmul(name, mode, grid, operands, specs, pairs, n_acc, acc_shape, extras, extra_specs, out_shapes, out_specs,
            epilogue):
    nk = grid[2]
    n_op, n_ex, n_out = len(operands), len(extras), len(out_shapes)

    def body(*refs):
        ops = refs[:n_op]
        ex = refs[n_op:n_op + n_ex]
        outs = refs[n_op + n_ex:n_op + n_ex + n_out]
        accs = refs[n_op + n_ex + n_out:]
        k = pl.program_id(2)

        @pl.when(k == 0)
        def _():
            for acc in accs:
                acc[...] = jnp.zeros_like(acc)

        for p, (ia, ib) in enumerate(pairs):
            acc = accs[p % n_acc]
            acc[...] += lax.dot_general(ops[ia][...], ops[ib][...], _DN[mode], preferred_element_type=f32)

        @pl.when(k == nk - 1)
        def _():
            epilogue([acc[...] for acc in accs], ex, outs)

    return pl.pallas_call(
        body, grid=grid, in_specs=list(specs) + list(extra_specs), out_specs=list(out_specs),
        out_shape=list(out_shapes), scratch_shapes=[pltpu.VMEM(acc_shape, f32)] * n_acc,
        compiler_params=_params("parallel", "parallel", "arbitrary"), name=name)(*operands, *extras)


def _store(dtype):
    def ep(accs, ex, outs):
        for a, o in zip(accs, outs):
            o[...] = a.astype(dtype)
    return ep


def _mm_cols(name, h, wg, out_dtype, deps=()):
    t, kd = h.shape
    n8 = wg.shape[2]
    tm = min(t, ROWS_PER_STEP)
    return _matmul(
        name, "nn", (t // tm, N_DEV, 1), [h, wg],
        [pl.BlockSpec((tm, kd), lambda i, j, k: (i, 0)), pl.BlockSpec((None, kd, n8), lambda i, j, k: (j, 0, 0))],
        [(0, 1)], 1, (tm, n8), list(deps), [ANY] * len(deps), [jax.ShapeDtypeStruct((t, N_DEV * n8), out_dtype)],
        [pl.BlockSpec((tm, n8), lambda i, j, k: (i, j))], _store(out_dtype))[0]


def _mm_rows_res(name, a, w, res, deps=()):
    t, kd = a.shape
    n = w.shape[1]
    tm, tn = min(t, ROWS_PER_STEP), min(n, 1024)

    def ep(accs, ex, outs):
        outs[0][...] = ex[0][...] + accs[0]

    return _matmul(
        name, "nn", (t // tm, n // tn, 1), [a, w],
        [pl.BlockSpec((tm, kd), lambda i, j, k: (i, 0)), pl.BlockSpec((kd, tn), lambda i, j, k: (0, j))],
        [(0, 1)], 1, (tm, tn), [res] + list(deps), [pl.BlockSpec((tm, tn), lambda i, j, k: (i, j))] + [ANY] * len(deps),
        [jax.ShapeDtypeStruct((t, n), f32)], [pl.BlockSpec((tm, tn), lambda i, j, k: (i, j))], ep)[0]


def _mm_nt(name, a, w, out_dtype, deps=()):
    t, n = a.shape
    kd = w.shape[0]
    tm, tn = min(t, ROWS_PER_STEP), min(kd, 1024)
    return _matmul(
        name, "nt", (t // tm, kd // tn, 1), [a, w],
        [pl.BlockSpec((tm, n), lambda i, j, k: (i, 0)), pl.BlockSpec((tn, n), lambda i, j, k: (j, 0))],
        [(0, 1)], 1, (tm, tn), list(deps), [ANY] * len(deps), [jax.ShapeDtypeStruct((t, kd), out_dtype)],
        [pl.BlockSpec((tm, tn), lambda i, j, k: (i, j))], _store(out_dtype))[0]


def _mm_nt_cols(name, da, wg, deps=()):
    t = da.shape[0]
    kd, n8 = wg.shape[1], wg.shape[2]
    tm = min(t, ROWS_PER_STEP)
    return _matmul(
        name, "nt", (t // tm, 1, N_DEV), [da, wg],
        [pl.BlockSpec((tm, n8), lambda i, j, k: (i, k)), pl.BlockSpec((None, kd, n8), lambda i, j, k: (k, 0, 0))],
        [(0, 1)], 1, (tm, kd), list(deps), [ANY] * len(deps), [jax.ShapeDtypeStruct((t, kd), f32)],
        [pl.BlockSpec((tm, kd), lambda i, j, k: (i, 0))], _store(f32))[0]


def _mm_tn_cols(name, h, da):
    t, kd = h.shape
    n8 = da.shape[1] // N_DEV
    tmk, tk = min(kd, 1024), min(t, 1024)
    return _matmul(
        name, "tn", (kd // tmk, N_DEV, t // tk), [h, da],
        [pl.BlockSpec((tk, tmk), lambda i, j, k: (k, i)), pl.BlockSpec((tk, n8), lambda i, j, k: (k, j))],
        [(0, 1)], 1, (tmk, n8), [], [], [jax.ShapeDtypeStruct((N_DEV, kd, n8), bf16)],
        [pl.BlockSpec((None, tmk, n8), lambda i, j, k: (_perm(j), i, 0))], _store(bf16))[0]


def _mm_tn_rows(name, a, dx):
    t, kf = a.shape
    r8 = kf // N_DEV
    n = dx.shape[1]
    return _matmul(
        name, "tn", (N_DEV, 1, 1), [a, dx],
        [pl.BlockSpec((t, r8), lambda i, j, k: (0, i)), pl.BlockSpec((t, n), lambda i, j, k: (0, 0))],
        [(0, 1)], 1, (r8, n), [], [], [jax.ShapeDtypeStruct((N_DEV, r8, n), bf16)],
        [pl.BlockSpec((None, r8, n), lambda i, j, k: (_perm(i), 0, 0))], _store(bf16))[0]


def _ffn_in(h2, wg_t, wu_t, deps=()):
    t, kd = h2.shape
    f8 = wg_t.shape[1]
    tm = min(t, ROWS_PER_STEP)

    def ep(accs, ex, outs):
        g, u = accs
        outs[0][...] = g.astype(bf16)
        outs[1][...] = u.astype(bf16)
        outs[2][...] = (g * jax.nn.sigmoid(g) * u).astype(bf16)

    wspec = pl.BlockSpec((None, f8, kd), lambda i, j, k: (j, 0, 0))
    ospec = pl.BlockSpec((None, tm, f8), lambda i, j, k: (j, i, 0))
    return _matmul(
        "ffn_in", "nt", (t // tm, N_DEV, 1), [h2, wg_t, wu_t],
        [pl.BlockSpec((tm, kd), lambda i, j, k: (i, 0)), wspec, wspec], [(0, 1), (0, 2)], 2, (tm, f8), list(deps),
        [ANY] * len(deps), [jax.ShapeDtypeStruct((N_DEV, t, f8), bf16)] * 3, [ospec] * 3, ep)


def _ffn_down(act, wd, res, deps=()):
    _, t, f8 = act.shape
    n = wd.shape[2]
    tm = min(t, 512)

    def ep(accs, ex, outs):
        outs[0][...] = ex[0][...] + accs[0]

    return _matmul(
        "ffn_down", "nn", (t // tm, 1, N_DEV), [act, wd],
        [pl.BlockSpec((None, tm, f8), lambda i, j, k: (k, i, 0)), pl.BlockSpec((None, f8, n), lambda i, j, k: (k, 0, 0))],
        [(0, 1)], 1, (tm, n), [res] + list(deps), [pl.BlockSpec((tm, n), lambda i, j, k: (i, 0))] + [ANY] * len(deps),
        [jax.ShapeDtypeStruct((t, n), f32)], [pl.BlockSpec((tm, n), lambda i, j, k: (i, 0))], ep)[0]


def _ffn_bwd_act(dxb, wd, g, u, deps=()):
    t, n = dxb.shape
    f8 = wd.shape[1]
    tm = min(t, ROWS_PER_STEP)

    def ep(accs, ex, outs):
        dact = accs[0]
        gv = ex[0][...].astype(f32)
        uv = ex[1][...].astype(f32)
        sg = jax.nn.sigmoid(gv)
        silu = gv * sg
        outs[0][...] = (dact * uv * (sg * (1.0 + gv * (1.0 - sg)))).astype(bf16)
        outs[1][...] = (dact * silu).astype(bf16)
        outs[2][...] = (silu * uv).astype(bf16)

    bspec = pl.BlockSpec((None, tm, f8), lambda i, j, k: (j, i, 0))
    return _matmul(
        "ffn_bwd_act", "nt", (t // tm, N_DEV, 1), [dxb, wd],
        [pl.BlockSpec((tm, n), lambda i, j, k: (i, 0)), pl.BlockSpec((None, f8, n), lambda i, j, k: (j, 0, 0))],
        [(0, 1)], 1, (tm, f8), [g, u] + list(deps), [bspec, bspec] + [ANY] * len(deps),
        [jax.ShapeDtypeStruct((N_DEV, t, f8), bf16)] * 3, [bspec] * 3, ep)


def _ffn_dwd(act, dxb):
    _, t, f8 = act.shape
    n = dxb.shape[1]
    tk = min(t, 1024)
    return _matmul(
        "ffn_dwd", "tn", (N_DEV, 1, t // tk), [act, dxb],
        [pl.BlockSpec((None, tk, f8), lambda i, j, k: (i, k, 0)), pl.BlockSpec((tk, n), lambda i, j, k: (k, 0))],
        [(0, 1)], 1, (f8, n), [], [], [jax.ShapeDtypeStruct((N_DEV, f8, n), bf16)],
        [pl.BlockSpec((None, f8, n), lambda i, j, k: (_perm(i), 0, 0))], _store(bf16))[0]


def _ffn_dwgu(h2, dg, du):
    t, kd = h2.shape
    f8 = dg.shape[2]
    tk = min(t, 1024)
    aspec = pl.BlockSpec((None, tk, f8), lambda i, j, k: (i, k, 0))
    ospec = pl.BlockSpec((None, f8, kd), lambda i, j, k: (_perm(i), 0, 0))
    return _matmul(
        "ffn_dwgu", "tn", (N_DEV, 1, t // tk), [dg, du, h2],
        [aspec, aspec, pl.BlockSpec((tk, kd), lambda i, j, k: (k, 0))], [(0, 2), (1, 2)], 2, (f8, kd), [], [],
        [jax.ShapeDtypeStruct((N_DEV, f8, kd), bf16)] * 2, [ospec] * 2, _store(bf16))


def _ffn_dh(dg, du, wg_t, wu_t, deps=()):
    _, t, f8 = dg.shape
    kd = wg_t.shape[2]
    tm = min(t, ROWS_PER_STEP)
    aspec = pl.BlockSpec((None, tm, f8), lambda i, j, k: (k, i, 0))
    wspec = pl.BlockSpec((None, f8, kd), lambda i, j, k: (k, 0, 0))
    return _matmul(
        "ffn_dh", "nn", (t // tm, 1, N_DEV), [dg, du, wg_t, wu_t], [aspec, aspec, wspec, wspec], [(0, 2), (1, 3)], 1,
        (tm, kd), list(deps), [ANY] * len(deps), [jax.ShapeDtypeStruct((t, kd), f32)],
        [pl.BlockSpec((tm, kd), lambda i, j, k: (i, 0))], _store(f32))[0]


def _rms_fwd(x, g):
    t, d = x.shape
    tm = min(t, 256)

    def body(x_ref, g_ref, o_ref):
        xv = x_ref[...]
        r = lax.rsqrt(jnp.mean(xv * xv, axis=-1, keepdims=True) + EPS)
        o_ref[...] = (xv * r * g_ref[...]).astype(bf16)

    return pl.pallas_call(
        body, grid=(t // tm,), in_specs=[pl.BlockSpec((tm, d), lambda i: (i, 0)), pl.BlockSpec((1, d), lambda i: (0, 0))],
        out_specs=pl.BlockSpec((tm, d), lambda i: (i, 0)), out_shape=jax.ShapeDtypeStruct((t, d), bf16),
        compiler_params=_params("parallel"), name="rms_fwd")(x, g.reshape(1, d))


def _rms_bwd(x, g, dh, dres):
    t, d = x.shape
    tm = min(t, 256)

    def body(x_ref, g_ref, dh_ref, dres_ref, dx_ref, dxb_ref, dg_ref):
        xv = x_ref[...]
        dy = dh_ref[...].astype(f32)
        r = lax.rsqrt(jnp.mean(xv * xv, axis=-1, keepdims=True) + EPS)
        gy = dy * g_ref[...]
        dot = jnp.mean(xv * gy, axis=-1, keepdims=True)
        dx = dres_ref[...] + r * gy - xv * (r * r * r * dot)
        dx_ref[...] = dx
        dxb_ref[...] = dx.astype(bf16)

        @pl.when(pl.program_id(0) == 0)
        def _():
            dg_ref[...] = jnp.zeros_like(dg_ref)

        dg_ref[...] += jnp.sum(dy * xv * r, axis=0, keepdims=True)

    row = pl.BlockSpec((tm, d), lambda i: (i, 0))
    vec = pl.BlockSpec((1, d), lambda i: (0, 0))
    return pl.pallas_call(
        body, grid=(t // tm,), in_specs=[row, vec, row, row], out_specs=[row, row, vec],
        out_shape=[jax.ShapeDtypeStruct((t, d), f32), jax.ShapeDtypeStruct((t, d), bf16), jax.ShapeDtypeStruct((1, d), f32)],
        compiler_params=_params("arbitrary"), name="rms_bwd")(x, g.reshape(1, d), dh, dres)


def _loss_head(x, g, target):
    t, d = x.shape
    tm = min(t, 256)

    def body(x_ref, g_ref, t_ref, loss_ref, dx_ref, dxb_ref, dg_ref):
        xv = x_ref[...]
        r = lax.rsqrt(jnp.mean(xv * xv, axis=-1, keepdims=True) + EPS)
        xn = xv * r
        err = xn * g_ref[...] - t_ref[...]
        dy = err * (1.0 / d)
        gy = dy * g_ref[...]
        dot = jnp.mean(xv * gy, axis=-1, keepdims=True)
        dx = r * gy - xv * (r * r * r * dot)
        dx_ref[...] = dx
        dxb_ref[...] = dx.astype(bf16)

        @pl.when(pl.program_id(0) == 0)
        def _():
            dg_ref[...] = jnp.zeros_like(dg_ref)
            loss_ref[...] = jnp.zeros_like(loss_ref)

        dg_ref[...] += jnp.sum(dy * xn, axis=0, keepdims=True)
        loss_ref[...] += 0.5 * jnp.sum(jnp.sum(err * err, axis=-1, keepdims=True) * (1.0 / d), axis=0, keepdims=True)

    row = pl.BlockSpec((tm, d), lambda i: (i, 0))
    vec = pl.BlockSpec((1, d), lambda i: (0, 0))
    one = pl.BlockSpec((1, 1), lambda i: (0, 0))
    return pl.pallas_call(
        body, grid=(t // tm,), in_specs=[row, vec, row], out_specs=[one, row, row, vec],
        out_shape=[jax.ShapeDtypeStruct((1, 1), f32), jax.ShapeDtypeStruct((t, d), f32),
                   jax.ShapeDtypeStruct((t, d), bf16), jax.ShapeDtypeStruct((1, d), f32)],
        compiler_params=_params("arbitrary"), name="loss_head")(x, g.reshape(1, d), target)


def _attn_consts():
    qt, kw = 2 * CHUNK, (A_LEFT_CHUNKS + 2) * CHUNK
    r = np.arange(qt)[:, None]
    kc = np.arange(kw)[None, :]
    rel = np.clip(r + A_LEFT_CHUNKS * CHUNK - kc, -A_MAX_REL, A_MAX_REL) + A_MAX_REL
    dchunk = kc // CHUNK - r // CHUNK
    valid = (dchunk >= 0) & (dchunk <= A_LEFT_CHUNKS)
    m = np.arange(kw + qt)
    relidx = np.clip(A_LEFT_CHUNKS * CHUNK - (m - (qt - 1)), -A_MAX_REL, A_MAX_REL) + A_MAX_REL
    onehot = np.zeros((kw + qt, 2 * A_MAX_REL + 1), np.float32)
    onehot[m, relidx] = 1.0
    return qt, kw, rel, valid, onehot


def _bias_table(rel_bias):
    qt, kw, _, valid, onehot = _attn_consts()
    h = rel_bias.shape[0]
    w = kw + qt
    relidx = np.argmax(onehot, axis=1)
    e = jnp.roll(jnp.take(rel_bias, jnp.asarray(relidx), axis=1), -(qt - 1), axis=1)
    rows = jnp.broadcast_to(e[:, None, :], (h, qt, w)).reshape(h, qt * w)
    skew = rows[:, :qt * (w - 1)].reshape(h, qt, w - 1)[:, :, :kw]
    return jnp.where(jnp.asarray(valid)[None], skew, NEG_INF).astype(f32)


def _bias_table_grad(dtab):
    qt, kw, _, _, onehot = _attn_consts()
    h = dtab.shape[0]
    w = kw + qt
    xp = jnp.pad(dtab[:, ::-1, :], ((0, 0), (0, 0), (0, w + 1 - kw)))
    skew = xp.reshape(h, qt * (w + 1))[:, :qt * w].reshape(h, qt, w)
    de = jnp.sum(skew, axis=1)
    return jnp.dot(de, jnp.asarray(onehot), precision=lax.Precision.HIGHEST)


def _attn_scores(q_ref, kpad, btab_ref, r0, qt, kw, pad):
    qv = q_ref[pl.ds(r0, qt), :]
    kwin = kpad[pl.ds(r0, kw), :]
    s = lax.dot_general(qv, kwin, _DN["nt"], preferred_element_type=f32) * (A_HEAD_DIM ** -0.5) + btab_ref[...]
    kcol = lax.broadcasted_iota(jnp.int32, (qt, kw), 1)
    s = jnp.where(r0 + kcol >= pad, s, NEG_INF)
    p = jnp.exp(s - jnp.max(s, axis=-1, keepdims=True))
    return qv, kwin, p / jnp.sum(p, axis=-1, keepdims=True)


def _attn_fwd(proj, btab, heads):
    t = proj.shape[0]
    qt, kw = btab.shape[1], btab.shape[2]
    pad = kw - qt

    def body(q_ref, k_ref, v_ref, btab_ref, o_ref, kpad, vpad):
        zeros = jnp.zeros((pad, A_HEAD_DIM), bf16)
        kpad[pl.ds(0, pad), :] = zeros
        vpad[pl.ds(0, pad), :] = zeros
        kpad[pl.ds(pad, t), :] = k_ref[...]
        vpad[pl.ds(pad, t), :] = v_ref[...]

        def tile(i, carry):
            r0 = pl.multiple_of(i * qt, qt)
            _, _, p = _attn_scores(q_ref, kpad, btab_ref, r0, qt, kw, pad)
            o = lax.dot_general(p.astype(bf16), vpad[pl.ds(r0, kw), :], _DN["nn"], preferred_element_type=f32)
            o_ref[pl.ds(r0, qt), :] = o.astype(bf16)
            return carry

        lax.fori_loop(0, t // qt, tile, 0, unroll=2)

    col = lambda off: pl.BlockSpec((t, A_HEAD_DIM), lambda h, off=off: (0, off + h))
    return pl.pallas_call(
        body, grid=(heads,),
        in_specs=[col(0), col(heads), col(2 * heads), pl.BlockSpec((None, qt, kw), lambda h: (h, 0, 0))],
        out_specs=col(0), out_shape=jax.ShapeDtypeStruct((t, heads * A_HEAD_DIM), bf16),
        scratch_shapes=[pltpu.VMEM((t + pad, A_HEAD_DIM), bf16)] * 2,
        compiler_params=_params("parallel"), name="attn_fwd")(proj, proj, proj, btab)


def _attn_bwd(proj, dmix, btab, heads):
    t = proj.shape[0]
    qt, kw = btab.shape[1], btab.shape[2]
    pad = kw - qt
    scale = A_HEAD_DIM ** -0.5

    def body(q_ref, k_ref, v_ref, do_ref, btab_ref, dq_ref, dk_ref, dv_ref, dtab_ref, kpad, vpad, dkacc, dvacc):
        zeros = jnp.zeros((pad, A_HEAD_DIM), bf16)
        kpad[pl.ds(0, pad), :] = zeros
        vpad[pl.ds(0, pad), :] = zeros
        kpad[pl.ds(pad, t), :] = k_ref[...]
        vpad[pl.ds(pad, t), :] = v_ref[...]
        dkacc[...] = jnp.zeros_like(dkacc)
        dvacc[...] = jnp.zeros_like(dvacc)
        dtab_ref[...] = jnp.zeros_like(dtab_ref)

        def tile(i, carry):
            r0 = pl.multiple_of(i * qt, qt)
            qv, kwin, p = _attn_scores(q_ref, kpad, btab_ref, r0, qt, kw, pad)
            dov = do_ref[pl.ds(r0, qt), :]
            dp = lax.dot_general(dov, vpad[pl.ds(r0, kw), :], _DN["nt"], preferred_element_type=f32)
            ds = p * (dp - jnp.sum(p * dp, axis=-1, keepdims=True))
            dtab_ref[...] += ds
            dsb = ds.astype(bf16)
            dq = lax.dot_general(dsb, kwin, _DN["nn"], preferred_element_type=f32) * scale
            dq_ref[pl.ds(r0, qt), :] = dq.astype(bf16)
            dkacc[pl.ds(r0, kw), :] += lax.dot_general(dsb, qv, _DN["tn"], preferred_element_type=f32) * scale
            dvacc[pl.ds(r0, kw), :] += lax.dot_general(p.astype(bf16), dov, _DN["tn"], preferred_element_type=f32)
            return carry

        lax.fori_loop(0, t // qt, tile, 0, unroll=2)
        dk_ref[...] = dkacc[pl.ds(pad, t), :].astype(bf16)
        dv_ref[...] = dvacc[pl.ds(pad, t), :].astype(bf16)

    col = lambda off: pl.BlockSpec((t, A_HEAD_DIM), lambda h, off=off: (0, off + h))
    tab = pl.BlockSpec((None, qt, kw), lambda h: (h, 0, 0))
    wide = jax.ShapeDtypeStruct((t, heads * A_HEAD_DIM), bf16)
    return pl.pallas_call(
        body, grid=(heads,), in_specs=[col(0), col(heads), col(2 * heads), col(0), tab],
        out_specs=[col(0), col(0), col(0), tab],
        out_shape=[wide, wide, wide, jax.ShapeDtypeStruct((heads, qt, kw), f32)],
        scratch_shapes=[pltpu.VMEM((t + pad, A_HEAD_DIM), bf16)] * 2 + [pltpu.VMEM((t + pad, A_HEAD_DIM), f32)] * 2,
        compiler_params=_params("parallel"), name="attn_bwd")(proj, proj, proj, dmix, btab)


def _shift_down(z, k):
    rows = lax.broadcasted_iota(jnp.int32, z.shape, 0)
    return jnp.where(rows >= k, pltpu.roll(z, k, 0), 0.0)


def _shift_up(z, k):
    t = z.shape[0]
    rows = lax.broadcasted_iota(jnp.int32, z.shape, 0)
    return jnp.where(rows < t - k, pltpu.roll(z, t - k, 0), 0.0)


def _conv_fwd(proj, conv_w, a_blocks, b_blocks):
    t = proj.shape[0]

    def body(b_ref, c_ref, h_ref, w_ref, o_ref):
        z = c_ref[...].astype(f32) * h_ref[...].astype(f32)
        w = w_ref[...]
        y = w[0:1, :] * _shift_down(z, 2) + w[1:2, :] * _shift_down(z, 1) + w[2:3, :] * z
        o_ref[...] = (b_ref[...].astype(f32) * y).astype(bf16)

    col = lambda off: pl.BlockSpec((t, LANE), lambda i, off=off: (0, off + i))
    return pl.pallas_call(
        body, grid=(b_blocks,),
        in_specs=[col(3 * a_blocks), col(3 * a_blocks + b_blocks), col(3 * a_blocks + 2 * b_blocks),
                  pl.BlockSpec((CONV_WIDTH, LANE), lambda i: (0, i))],
        out_specs=col(0), out_shape=jax.ShapeDtypeStruct((t, b_blocks * LANE), bf16),
        compiler_params=_params("parallel"), name="conv_fwd")(proj, proj, proj, conv_w)


def _conv_bwd(proj, dmix, conv_w, a_blocks, b_blocks):
    t = proj.shape[0]

    def body(b_ref, c_ref, h_ref, do_ref, w_ref, db_ref, dc_ref, dh_ref, dw_ref):
        bv, cv, hv = b_ref[...].astype(f32), c_ref[...].astype(f32), h_ref[...].astype(f32)
        w = w_ref[...]
        z = cv * hv
        z1, z2 = _shift_down(z, 1), _shift_down(z, 2)
        y = w[0:1, :] * z2 + w[1:2, :] * z1 + w[2:3, :] * z
        dov = do_ref[...].astype(f32)
        db_ref[...] = (dov * y).astype(bf16)
        dy = dov * bv
        dz = w[2:3, :] * dy + w[1:2, :] * _shift_up(dy, 1) + w[0:1, :] * _shift_up(dy, 2)
        dc_ref[...] = (dz * hv).astype(bf16)
        dh_ref[...] = (dz * cv).astype(bf16)
        dw_ref[0:1, :] = jnp.sum(dy * z2, axis=0, keepdims=True)
        dw_ref[1:2, :] = jnp.sum(dy * z1, axis=0, keepdims=True)
        dw_ref[2:3, :] = jnp.sum(dy * z, axis=0, keepdims=True)

    col = lambda off: pl.BlockSpec((t, LANE), lambda i, off=off: (0, off + i))
    wspec = pl.BlockSpec((CONV_WIDTH, LANE), lambda i: (0, i))
    wide = jax.ShapeDtypeStruct((t, b_blocks * LANE), bf16)
    return pl.pallas_call(
        body, grid=(b_blocks,),
        in_specs=[col(3 * a_blocks), col(3 * a_blocks + b_blocks), col(3 * a_blocks + 2 * b_blocks), col(a_blocks), wspec],
        out_specs=[col(0), col(0), col(0), wspec],
        out_shape=[wide, wide, wide, jax.ShapeDtypeStruct((CONV_WIDTH, b_blocks * LANE), f32)],
        compiler_params=_params("parallel"), name="conv_bwd")(proj, proj, proj, dmix, conv_w)


_RSQRT2 = 0.7071067811865476
_RSQRT2PI = 0.3989422804014327


def _gelu(x):
    return 0.5 * x * (1.0 + lax.erf(x * _RSQRT2))


def _gelu_grad(x):
    return 0.5 * (1.0 + lax.erf(x * _RSQRT2)) + x * jnp.exp(-0.5 * x * x) * _RSQRT2PI


def _sgu_common(a_ref, lg_ref, lb_ref, cw):
    av = a_ref[...]
    u = _gelu(av[:, :cw])
    v = _gelu(av[:, cw:])
    mu = jnp.mean(v, axis=-1, keepdims=True)
    xc = v - mu
    rstd = lax.rsqrt(jnp.mean(xc * xc, axis=-1, keepdims=True) + EPS)
    xhat = xc * rstd
    vln = xhat * lg_ref[...] + lb_ref[...]
    pos_t = lax.broadcasted_iota(jnp.int32, (C_BLOCK, C_BLOCK), 0) // CHUNK
    pos_s = lax.broadcasted_iota(jnp.int32, (C_BLOCK, C_BLOCK), 1) // CHUNK
    return av, u, xhat, rstd, vln, pos_s <= pos_t


def _sgu_fwd(a, ln_g, ln_b, w_s, bs_t):
    t, cw2 = a.shape
    cw = cw2 // 2
    groups = w_s.shape[0]
    cg = cw // groups

    def body(a_ref, lg_ref, lb_ref, ws_ref, bs_ref, m_ref):
        _, u, _, _, vln, mask = _sgu_common(a_ref, lg_ref, lb_ref, cw)
        vb = vln.astype(bf16)
        for g in range(groups):
            sl = slice(g * cg, (g + 1) * cg)
            wm = jnp.where(mask, ws_ref[g], 0.0).astype(bf16)
            s = lax.dot_general(wm, vb[:, sl], _DN["nn"], preferred_element_type=f32) + bs_ref[:, g:g + 1]
            m_ref[:, sl] = (u[:, sl] * s).astype(bf16)

    vec = pl.BlockSpec((1, cw), lambda n: (0, 0))
    return pl.pallas_call(
        body, grid=(t // C_BLOCK,),
        in_specs=[pl.BlockSpec((C_BLOCK, cw2), lambda n: (n, 0)), vec, vec,
                  pl.BlockSpec((groups, C_BLOCK, C_BLOCK), lambda n: (0, 0, 0)),
                  pl.BlockSpec((C_BLOCK, groups), lambda n: (0, 0))],
        out_specs=pl.BlockSpec((C_BLOCK, cw), lambda n: (n, 0)), out_shape=jax.ShapeDtypeStruct((t, cw), bf16),
        compiler_params=_params("parallel"), name="sgu_fwd")(a, ln_g.reshape(1, cw), ln_b.reshape(1, cw), w_s, bs_t)


def _sgu_bwd(a, dm, ln_g, ln_b, w_s, bs_t):
    t, cw2 = a.shape
    cw = cw2 // 2
    groups = w_s.shape[0]
    cg = cw // groups

    def body(a_ref, dm_ref, lg_ref, lb_ref, ws_ref, bs_ref, da_ref, dws_ref, dbs_ref, dlg_ref, dlb_ref, dvln):
        @pl.when(pl.program_id(0) == 0)
        def _():
            dws_ref[...] = jnp.zeros_like(dws_ref)
            dbs_ref[...] = jnp.zeros_like(dbs_ref)
            dlg_ref[...] = jnp.zeros_like(dlg_ref)
            dlb_ref[...] = jnp.zeros_like(dlb_ref)

        av, u, xhat, rstd, vln, mask = _sgu_common(a_ref, lg_ref, lb_ref, cw)
        vb = vln.astype(bf16)
        lane = lax.broadcasted_iota(jnp.int32, (C_BLOCK, groups), 1)
        dbs = jnp.zeros((C_BLOCK, groups), f32)
        for g in range(groups):
            sl = slice(g * cg, (g + 1) * cg)
            wm = jnp.where(mask, ws_ref[g], 0.0).astype(bf16)
            s = lax.dot_general(wm, vb[:, sl], _DN["nn"], preferred_element_type=f32) + bs_ref[:, g:g + 1]
            dmg = dm_ref[:, sl].astype(f32)
            da_ref[:, sl] = (dmg * s * _gelu_grad(av[:, sl])).astype(bf16)
            dsg = dmg * u[:, sl]
            dbs = dbs + jnp.where(lane == g, jnp.sum(dsg, axis=-1, keepdims=True), 0.0)
            dsb = dsg.astype(bf16)
            dws_ref[g] += jnp.where(mask, lax.dot_general(dsb, vb[:, sl], _DN["nt"], preferred_element_type=f32), 0.0)
            dvln[:, sl] = lax.dot_general(wm, dsb, _DN["tn"], preferred_element_type=f32)
        dbs_ref[...] += dbs
        dv = dvln[...]
        dlg_ref[...] += jnp.sum(dv * xhat, axis=0, keepdims=True)
        dlb_ref[...] += jnp.sum(dv, axis=0, keepdims=True)
        dxh = dv * lg_ref[...]
        dvv = rstd * (dxh - jnp.mean(dxh, axis=-1, keepdims=True) - xhat * jnp.mean(dxh * xhat, axis=-1, keepdims=True))
        da_ref[:, cw:] = (dvv * _gelu_grad(av[:, cw:])).astype(bf16)

    vec = pl.BlockSpec((1, cw), lambda n: (0, 0))
    wsp = pl.BlockSpec((groups, C_BLOCK, C_BLOCK), lambda n: (0, 0, 0))
    bsp = pl.BlockSpec((C_BLOCK, groups), lambda n: (0, 0))
    return pl.pallas_call(
        body, grid=(t // C_BLOCK,),
        in_specs=[pl.BlockSpec((C_BLOCK, cw2), lambda n: (n, 0)), pl.BlockSpec((C_BLOCK, cw), lambda n: (n, 0)), vec, vec, wsp, bsp],
        out_specs=[pl.BlockSpec((C_BLOCK, cw2), lambda n: (n, 0)), wsp, bsp, vec, vec],
        out_shape=[jax.ShapeDtypeStruct((t, cw2), bf16), jax.ShapeDtypeStruct(w_s.shape, f32),
                   jax.ShapeDtypeStruct(bs_t.shape, f32), jax.ShapeDtypeStruct((1, cw), f32), jax.ShapeDtypeStruct((1, cw), f32)],
        scratch_shapes=[pltpu.VMEM((C_BLOCK, cw), f32)],
        compiler_params=_params("arbitrary"), name="sgu_bwd")(a, dm, ln_g.reshape(1, cw), ln_b.reshape(1, cw), w_s, bs_t)


HBM = pl.BlockSpec(memory_space=pltpu.HBM)
SEM = pl.BlockSpec(memory_space=pltpu.SEMAPHORE)
EFFECT = pltpu.SideEffectType.DATAFLOW_SIDE_EFFECTING


def _place():
    x, y, c = lax.axis_index("x"), lax.axis_index("y"), lax.axis_index("c")
    return x, y, c, [(1 - x, y), (x, 1 - y), (1 - x, 1 - y)]


def _remote(src, dst, send_sems, recv_sems, k, to):
    return pltpu.make_async_remote_copy(src_ref=src, dst_ref=dst, send_sem=send_sems.at[k], recv_sem=recv_sems.at[k],
                                        device_id=to, device_id_type=MESH)


def _split_start(name, arrays, plan, n_copies, after):
    n = len(arrays)

    def body(*refs):
        send_sems, recv_sems, token = refs[n + 1], refs[n + 2], refs[-1]
        for cp in plan(refs[:n], send_sems, recv_sems):
            cp.start()
        token[...] = jnp.zeros_like(token)

    out = pl.pallas_call(
        body, name=name,
        out_shape=(pltpu.SemaphoreType.DMA((n_copies,)), pltpu.SemaphoreType.DMA((n_copies,)),
                   *[pltpu.HBM(a.shape, a.dtype) for a in arrays], jax.ShapeDtypeStruct((8, LANE), f32)),
        in_specs=[HBM] * n + [ANY], out_specs=(SEM, SEM, *[HBM] * n, pl.BlockSpec(memory_space=pltpu.VMEM)),
        input_output_aliases={i: 2 + i for i in range(n)},
        compiler_params=pltpu.CompilerParams(has_side_effects=EFFECT),
    )(*[pltpu.with_memory_space_constraint(a, pltpu.HBM) for a in arrays], after)
    return (out[0], out[1]), list(out[2:2 + n]), out[-1]


def _split_wait(name, arrays, sems, plan, after):
    n = len(arrays)

    def body(*refs):
        for cp in plan(refs[:n], refs[n], refs[n + 1]):
            cp.wait()

    out = pl.pallas_call(
        body, name=name, out_shape=tuple(pltpu.HBM(a.shape, a.dtype) for a in arrays),
        in_specs=[HBM] * n + [SEM, SEM, ANY], out_specs=tuple([HBM] * n), input_output_aliases={i: i for i in range(n)},
        compiler_params=pltpu.CompilerParams(has_side_effects=EFFECT),
    )(*arrays, sems[0], sems[1], after)
    return list(out)


def _row_pieces(ref_rows, split):
    rc = ref_rows // split
    return [pl.ds(s * rc, rc) for s in range(split)]


def _gather_slots():
    x, y, c, _ = _place()
    slots = (4 * x + 2 * y + c, 4 * (1 - x) + 2 * y + c, 4 * x + 2 * (1 - y) + c, 4 * (1 - x) + 2 * (1 - y) + c)
    return slots, (x, y, 1 - c), (1 - x, y, c), (x, 1 - y, c)


def _to_sibling(cps, k, b, slot, sibling, send_sems, recv_sems):
    for rows in _row_pieces(b.shape[1], D2D_SPLIT):
        cps.append(_remote(b.at[slot, rows], b.at[slot, rows], send_sems, recv_sems, k, sibling))
        k += 1
    return k


def _plan_gather_near(bufs, send_sems, recv_sems):
    (me, _, _, _), sibling, x_peer, y_peer = _gather_slots()
    cps, k = [], 0
    for b in bufs:
        k = _to_sibling(cps, k, b, me, sibling, send_sems, recv_sems)
        for peer in (x_peer, y_peer):
            cps.append(_remote(b.at[me], b.at[me], send_sems, recv_sems, k, peer))
            k += 1
    return cps


def _plan_gather_relay(bufs, send_sems, recv_sems):
    (_, x_slot, y_slot, _), sibling, x_peer, y_peer = _gather_slots()
    cps, k = [], 0
    for b in bufs:
        half = b.shape[1] // 2
        k = _to_sibling(cps, k, b, x_slot, sibling, send_sems, recv_sems)
        k = _to_sibling(cps, k, b, y_slot, sibling, send_sems, recv_sems)
        lower, upper = pl.ds(0, half), pl.ds(half, half)
        cps.append(_remote(b.at[x_slot, lower], b.at[x_slot, lower], send_sems, recv_sems, k, y_peer))
        cps.append(_remote(b.at[y_slot, upper], b.at[y_slot, upper], send_sems, recv_sems, k + 1, x_peer))
        k += 2
    return cps


def _plan_gather_far(bufs, send_sems, recv_sems):
    (_, _, _, far), sibling, _, _ = _gather_slots()
    cps, k = [], 0
    for b in bufs:
        k = _to_sibling(cps, k, b, far, sibling, send_sems, recv_sems)
    return cps


def _plan_rs_sibling(arrs, send_sems, recv_sems):
    n = len(arrs) // 2
    x, y, c, _ = _place()
    cps, k = [], 0
    for g, got in zip(arrs[:n], arrs[n:]):
        for q in range(N_CHIP):
            for rows in _row_pieces(g.shape[1], RS_SPLIT):
                cps.append(_remote(g.at[N_CHIP * (1 - c) + q, rows], got.at[q, rows], send_sems, recv_sems, k, (x, y, 1 - c)))
                k += 1
    return cps


def _plan_rs_chips(arrs, send_sems, recv_sems):
    n = len(arrs) // 2
    x, y, c, chips = _place()
    q = 2 * x + y
    cps, k = [], 0
    for p, r in zip(arrs[:n], arrs[n:]):
        for px, py in chips:
            cps.append(_remote(p.at[2 * px + py], r.at[q], send_sems, recv_sems, k, (px, py, c)))
            k += 1
    return cps


class _Gather:
    STAGES = (("near", _plan_gather_near, D2D_SPLIT + 2), ("relay", _plan_gather_relay, 2 * D2D_SPLIT + 2),
              ("far", _plan_gather_far, D2D_SPLIT))

    def __init__(self, tag, bufs):
        self.tag, self.bufs, self.stage = tag, bufs, -1

    def advance(self, after):
        if self.stage >= 0:
            name, plan, _ = self.STAGES[self.stage]
            self.bufs = _split_wait("gather_%s_wait_%s" % (name, self.tag), self.bufs, self.sems, plan, after)
        self.stage += 1
        if self.stage == len(self.STAGES):
            return self.bufs
        name, plan, per_array = self.STAGES[self.stage]
        self.sems, self.bufs, token = _split_start("gather_%s_start_%s" % (name, self.tag), self.bufs, plan,
                                                   per_array * len(self.bufs), after)
        return token

    def finish(self, after):
        out = self.advance(after)
        while not isinstance(out, list):
            out = self.advance(after)
        return out


class _ReduceScatter:
    def __init__(self, tag, grads, core):
        self.tag, self.n = tag, len(grads)
        lands = [lax.empty((N_CHIP,) + g.shape[1:], g.dtype) for g in grads]
        self.sems, self.arrs, self.token = _split_start("rs_sibling_start_" + tag, list(grads) + lands, _plan_rs_sibling,
                                                        self.n * N_CHIP * RS_SPLIT, core)

    def middle(self, after, core):
        arrs = _split_wait("rs_sibling_wait_" + self.tag, self.arrs, self.sems, _plan_rs_sibling, after)
        parts = [_pair_sum(g, got, core) for g, got in zip(arrs[:self.n], arrs[self.n:])]
        lands = [lax.empty(p.shape, p.dtype) for p in parts]
        self.sems, self.arrs, self.token = _split_start("rs_chips_start_" + self.tag, parts + lands, _plan_rs_chips,
                                                        self.n * 3, core)

    def finish(self, after):
        arrs = _split_wait("rs_chips_wait_" + self.tag, self.arrs, self.sems, _plan_rs_chips, after)
        return list(zip(arrs[:self.n], arrs[self.n:]))


def _cast_into_slot(name, w, layer, me, after):
    _, rows, cols = w.shape
    tr = 256 if rows % 256 == 0 else rows

    def body(me_ref, w_ref, after_ref, o_ref):
        o_ref[...] = w_ref[...].astype(bf16)

    return pl.pallas_call(
        body,
        grid_spec=pltpu.PrefetchScalarGridSpec(
            num_scalar_prefetch=1, grid=(rows // tr,),
            in_specs=[pl.BlockSpec((None, tr, cols), lambda i, me_ref: (layer, i, 0)), ANY],
            out_specs=pl.BlockSpec((None, tr, cols), lambda i, me_ref: (me_ref[0], i, 0))),
        out_shape=jax.ShapeDtypeStruct((N_DEV, rows, cols), bf16), compiler_params=_params("parallel"), name=name)(me, w, after)


def _pair_sum(g, got, core):
    _, rows, cols = g.shape
    tr = 512 if rows % 512 == 0 else rows

    def body(c_ref, a_ref, b_ref, o_ref):
        o_ref[...] = (a_ref[...].astype(f32) + b_ref[...].astype(f32)).astype(bf16)

    spec = pl.BlockSpec((None, tr, cols), lambda q, i, c_ref: (q, i, 0))
    return pl.pallas_call(
        body,
        grid_spec=pltpu.PrefetchScalarGridSpec(
            num_scalar_prefetch=1, grid=(N_CHIP, rows // tr),
            in_specs=[pl.BlockSpec((None, tr, cols), lambda q, i, c_ref: (N_CHIP * c_ref[0] + q, i, 0)), spec],
            out_specs=spec),
        out_shape=jax.ShapeDtypeStruct((N_CHIP, rows, cols), bf16), compiler_params=_params("parallel", "parallel"),
        name="pair_sum")(core, g, got)


def _gather_copies(n, ins, outs, send_sems, recv_sems, local_sems):
    x, y, c, chips = _place()
    sibling = (x, y, 1 - c)

    def slot(px, py, pc):
        return 4 * px + 2 * py + pc

    def copy(i, k, block, to, src=None):
        dst = outs[i].at[slot(*block)]
        return pltpu.make_async_remote_copy(src_ref=dst if src is None else src, dst_ref=dst, send_sem=send_sems.at[i, k],
                                            recv_sem=recv_sems.at[i, k], device_id=to, device_id_type=MESH)

    started = []
    for i in range(n):
        mine = pltpu.make_async_copy(ins[i], outs[i].at[slot(x, y, c)], local_sems.at[i])
        mine.start()
        started.append(mine)
    sends = []
    for i in range(n):
        sends.append(copy(i, 0, (x, y, c), sibling, src=ins[i]))
        sends += [copy(i, 1 + j, (x, y, c), (*chip, c), src=ins[i]) for j, chip in enumerate(chips)]
    for cp in sends:
        cp.start()
    for i in range(n):
        for j, chip in enumerate(chips):
            copy(i, 1 + j, (*chip, c), (x, y, c)).wait_recv()
            fwd = copy(i, 4 + j, (*chip, c), sibling)
            fwd.start()
            sends.append(fwd)
    for i in range(n):
        copy(i, 0, sibling, (x, y, c)).wait_recv()
        for j, chip in enumerate(chips):
            copy(i, 4 + j, (*chip, 1 - c), (x, y, c)).wait_recv()
    for cp in sends:
        cp.wait_send()
    for mine in started:
        mine.wait()


def _gather_small(name, packed, reduce):
    rows = packed.shape[0]

    def body(x_ref, o_ref, buf, send_sems, recv_sems, local_sems):
        _gather_copies(1, [x_ref], [buf], send_sems, recv_sems, local_sems)
        if reduce:
            acc = buf[0]
            for j in range(1, N_DEV):
                acc = acc + buf[j]
            o_ref[...] = acc
        else:
            o_ref[...] = buf[...]

    vm = pl.BlockSpec(memory_space=pltpu.VMEM)
    return pl.pallas_call(
        body, in_specs=[vm], out_specs=vm,
        out_shape=jax.ShapeDtypeStruct((rows, LANE) if reduce else (N_DEV, rows, LANE), f32),
        scratch_shapes=[pltpu.VMEM((N_DEV, rows, LANE), f32), pltpu.SemaphoreType.DMA((1, 7)), pltpu.SemaphoreType.DMA((1, 7)),
                        pltpu.SemaphoreType.DMA((1,))],
        compiler_params=pltpu.CompilerParams(vmem_limit_bytes=VMEM_LIMIT), name=name)(packed)


def _pack(arrs):
    flat = jnp.concatenate([a.reshape(-1).astype(f32) for a in arrs])
    rows = -(-flat.shape[0] // (8 * LANE)) * 8
    return jnp.pad(flat, (0, rows * LANE - flat.shape[0])).reshape(rows, LANE)


def _unpack(buf, shapes):
    flat = buf.reshape(-1)
    out, off = [], 0
    for s in shapes:
        n = int(np.prod(s))
        out.append(flat[off:off + n].reshape(s))
        off += n
    return out


def _adam_math(w, g, m, v):
    m2 = ADAM_B1 * m + (1.0 - ADAM_B1) * g
    v2 = ADAM_B2 * v + (1.0 - ADAM_B2) * (g * g)
    m_hat = m2 / (1.0 - ADAM_B1 ** ADAM_STEP)
    v_hat = v2 / (1.0 - ADAM_B2 ** ADAM_STEP)
    delta = -ADAM_LR * (m_hat / (jnp.sqrt(v_hat) + ADAM_EPS) + ADAM_WD * w)
    return delta, m2, v2


def _adam_big(name, w, m, v, parts, chip):
    layers, rows, cols = w.shape
    tr = 512 if rows % 512 == 0 else 256 if rows % 256 == 0 else rows // 4 if rows % 32 == 0 else 8

    def body(chip_ref, w_ref, m_ref, v_ref, *rest):
        p_refs = rest[:N_CHIP * layers]
        g_ref, d_ref, m2_ref, v2_ref = rest[N_CHIP * layers:]
        for li in range(layers):
            @pl.when(pl.program_id(0) == li)
            def _(li=li):
                g = p_refs[N_CHIP * li][...].astype(f32)
                for q in range(1, N_CHIP):
                    g = g + p_refs[N_CHIP * li + q][...].astype(f32)
                delta, m2, v2 = _adam_math(w_ref[...], g, m_ref[...], v_ref[...])
                g_ref[...] = g
                d_ref[...] = delta
                m2_ref[...] = m2
                v2_ref[...] = v2

    spec = pl.BlockSpec((None, tr, cols), lambda l, i, c_ref: (l, i, 0))
    pspecs, operands = [], []
    for li in range(layers):
        for q in range(N_CHIP):
            pspecs.append(pl.BlockSpec((None, tr, cols),
                                       lambda l, i, c_ref, li=li, q=q: ((c_ref[0] + q) % N_CHIP, jnp.where(l == li, i, 0), 0)))
            operands.append(parts[li][0] if q == 0 else parts[li][1])
    out = jax.ShapeDtypeStruct((layers, rows, cols), f32)
    return pl.pallas_call(
        body,
        grid_spec=pltpu.PrefetchScalarGridSpec(num_scalar_prefetch=1, grid=(layers, rows // tr),
                                               in_specs=[spec, spec, spec] + pspecs, out_specs=[spec] * 4),
        out_shape=[out] * 4, compiler_params=_params("arbitrary", "arbitrary"), name=name)(chip, w, m, v, *operands)


def _adam_small(w, g, m, v):
    rows = w.shape[0]

    def body(w_ref, g_ref, m_ref, v_ref, d_ref, m2_ref, v2_ref):
        delta, m2, v2 = _adam_math(w_ref[...], g_ref[...], m_ref[...], v_ref[...])
        d_ref[...] = delta
        m2_ref[...] = m2
        v2_ref[...] = v2

    out = jax.ShapeDtypeStruct((rows, LANE), f32)
    return pl.pallas_call(body, out_shape=[out] * 3, name="adam_small")(w, g, m, v)


def kernel(x, mix_norm, ab_w_in, ab_rel_bias, ab_conv_w, ab_w_out, c_w_in, c_ln_g, c_ln_b, c_w_s, c_b_s, c_w_out, ffn_norm, ffn_w_gate, ffn_w_up, ffn_w_down, final_norm, loss_target, m_mix_norm, m_ab_w_in, m_ab_rel_bias, m_ab_conv_w, m_ab_w_out, m_c_w_in, m_c_ln_g, m_c_ln_b, m_c_w_s, m_c_b_s, m_c_w_out, m_ffn_norm, m_ffn_w_gate, m_ffn_w_up, m_ffn_w_down, m_final_norm, v_mix_norm, v_ab_w_in, v_ab_rel_bias, v_ab_conv_w, v_ab_w_out, v_c_w_in, v_c_ln_g, v_c_ln_b, v_c_w_s, v_c_b_s, v_c_w_out, v_ffn_norm, v_ffn_w_gate, v_ffn_w_up, v_ffn_w_down, v_final_norm):
    d = D_MODEL
    a_width = d // 2
    heads = a_width // A_HEAD_DIM
    a_blocks = a_width // LANE
    b_blocks = (d - a_width) // LANE
    n_even, n_odd = (DEPTH + 1) // 2, DEPTH // 2
    me_s = 4 * lax.axis_index("x") + 2 * lax.axis_index("y") + lax.axis_index("c")
    me = me_s.astype(jnp.int32).reshape(1)
    core = lax.axis_index("c").astype(jnp.int32).reshape(1)
    chip = (2 * lax.axis_index("x") + lax.axis_index("y")).astype(jnp.int32).reshape(1)

    weights = dict(mix_norm=mix_norm, ab_w_in=ab_w_in, ab_rel_bias=ab_rel_bias, ab_conv_w=ab_conv_w, ab_w_out=ab_w_out,
                   c_w_in=c_w_in, c_ln_g=c_ln_g, c_ln_b=c_ln_b, c_w_s=c_w_s, c_b_s=c_b_s, c_w_out=c_w_out,
                   ffn_norm=ffn_norm, ffn_w_gate=ffn_w_gate, ffn_w_up=ffn_w_up, ffn_w_down=ffn_w_down, final_norm=final_norm)
    mom_m = dict(mix_norm=m_mix_norm, ab_w_in=m_ab_w_in, ab_rel_bias=m_ab_rel_bias, ab_conv_w=m_ab_conv_w, ab_w_out=m_ab_w_out,
                 c_w_in=m_c_w_in, c_ln_g=m_c_ln_g, c_ln_b=m_c_ln_b, c_w_s=m_c_w_s, c_b_s=m_c_b_s, c_w_out=m_c_w_out,
                 ffn_norm=m_ffn_norm, ffn_w_gate=m_ffn_w_gate, ffn_w_up=m_ffn_w_up, ffn_w_down=m_ffn_w_down, final_norm=m_final_norm)
    mom_v = dict(mix_norm=v_mix_norm, ab_w_in=v_ab_w_in, ab_rel_bias=v_ab_rel_bias, ab_conv_w=v_ab_conv_w, ab_w_out=v_ab_w_out,
                 c_w_in=v_c_w_in, c_ln_g=v_c_ln_g, c_ln_b=v_c_ln_b, c_w_s=v_c_w_s, c_b_s=v_c_b_s, c_w_out=v_c_w_out,
                 ffn_norm=v_ffn_norm, ffn_w_gate=v_ffn_w_gate, ffn_w_up=v_ffn_w_up, ffn_w_down=v_ffn_w_down, final_norm=v_final_norm)
    order = list(weights)
    wide = ("ffn_w_gate", "ffn_w_up")
    flip = lambda a: jnp.swapaxes(a, 1, 2)
    local = {k: (flip(weights[k]), flip(mom_m[k]), flip(mom_v[k])) if k in wide else (weights[k], mom_m[k], mom_v[k]) for k in order}

    sharded_small = [ab_conv_w, c_ln_g, c_ln_b]
    gathered = _gather_small("gather_small", _pack(sharded_small), reduce=False)
    conv_parts, lng_parts, lnb_parts = [], [], []
    for j in range(N_DEV):
        cw_j, lg_j, lb_j = _unpack(gathered[j], [a.shape for a in sharded_small])
        conv_parts.append(cw_j)
        lng_parts.append(lg_j)
        lnb_parts.append(lb_j)
    conv_full = jnp.concatenate(conv_parts, axis=-1)
    lng_full = jnp.concatenate(lng_parts, axis=-1)
    lnb_full = jnp.concatenate(lnb_parts, axis=-1)

    gate_t, up_t = local["ffn_w_gate"][0], local["ffn_w_up"][0]
    sets = []
    for layer in range(DEPTH):
        i = layer // 2
        if layer % 2 == 0:
            sets += [("ab_in%d" % i, [(ab_w_in, i)]), ("ab_out%d" % i, [(ab_w_out, i)])]
        else:
            sets += [("c_in%d" % i, [(c_w_in, i)]), ("c_out%d" % i, [(c_w_out, i)])]
        sets += [("ffn_in%d" % layer, [(gate_t, layer), (up_t, layer)]), ("ffn_out%d" % layer, [(ffn_w_down, layer)])]
    units = [None] * len(sets)
    cursor = [0]
    tokens = []

    def start_gather(k, after):
        if k < len(sets):
            tag, members = sets[k]
            units[k] = _Gather(tag, [_cast_into_slot("cast_slot", w, li, me, after) for w, li in members])
            tokens.append(units[k].advance(after))

    def next_weights(after):
        k = cursor[0]
        cursor[0] = k + 1
        ready = units[k].finish(after)
        for later in range(k + 1, min(k + len(_Gather.STAGES), len(sets)) if k else 2):
            tokens.append(units[later].advance(after))
        start_gather(k + GATHER_AHEAD, after)
        return ready

    def started():
        deps = list(tokens)
        tokens.clear()
        return deps

    xs = x[0]
    tgt = loss_target[0]
    for k in range(GATHER_AHEAD):
        start_gather(k, gathered)
    for _ in range(len(_Gather.STAGES) - 1):
        tokens.append(units[0].advance(gathered))
    saved = []
    for layer in range(DEPTH):
        i = layer // 2
        (w_in_g,) = next_weights(xs)
        h = _rms_fwd(xs, mix_norm[layer])
        if layer % 2 == 0:
            proj = _mm_cols("ab_proj", h, w_in_g, bf16, started())
            btab = _bias_table(ab_rel_bias[i])
            attn = _attn_fwd(proj, btab, heads)
            conv = _conv_fwd(proj, conv_full[i], a_blocks, b_blocks)
            mixed = jnp.concatenate([attn, conv], axis=-1)
            ctx = (proj, btab)
        else:
            proj = _mm_cols("c_proj", h, w_in_g, f32, started())
            bs_t = jnp.transpose(c_b_s[i])
            mixed = _sgu_fwd(proj, lng_full[i], lnb_full[i], c_w_s[i], bs_t)
            ctx = (proj, bs_t)
        (w_out_g,) = next_weights(mixed)
        w_out_full = w_out_g.reshape(-1, w_out_g.shape[-1])
        x1 = _mm_rows_res("mix_out", mixed, w_out_full, xs, started())
        wg_g, wu_g = next_weights(x1)
        h2 = _rms_fwd(x1, ffn_norm[layer])
        g_act, u_act, act = _ffn_in(h2, wg_g, wu_g, started())
        (wd_g,) = next_weights(g_act)
        x2 = _ffn_down(act, wd_g, x1, started())
        saved.append((xs, h, ctx, mixed, x1, h2, g_act, u_act, w_in_g, w_out_full, wg_g, wu_g, wd_g))
        xs = x2

    loss_part, dx, dxb, d_final = _loss_head(xs, final_norm, tgt)
    loss = lax.psum(loss_part[0, 0], ("x", "y", "c"))

    scatters = {}
    small = {k: [None] * weights[k].shape[0] for k in ("mix_norm", "ffn_norm", "ab_rel_bias", "ab_conv_w", "c_ln_g", "c_ln_b",
                                                       "c_w_s", "c_b_s")}
    for layer in reversed(range(DEPTH)):
        i = layer // 2
        xs, h, ctx, mixed, x1, h2, g_act, u_act, w_in_g, w_out_full, wg_g, wu_g, wd_g = saved[layer]
        dg, du, act = _ffn_bwd_act(dxb, wd_g, g_act, u_act, started())
        dwd = _ffn_dwd(act, dxb)
        dwg, dwu = _ffn_dwgu(h2, dg, du)
        rs_ffn = _ReduceScatter("ffn%d" % layer, [dwg, dwu, dwd], core)
        dh2 = _ffn_dh(dg, du, wg_g, wu_g, [rs_ffn.token])
        dx, dxb, dgn = _rms_bwd(x1, ffn_norm[layer], dh2, dx)
        small["ffn_norm"][layer] = dgn[0]
        rs_ffn.middle(dxb, core)
        for pos, k in enumerate(("ffn_w_gate", "ffn_w_up", "ffn_w_down")):
            scatters[(k, layer)] = (rs_ffn, pos)
        dmixed = _mm_nt("mix_out_bwd", dxb, w_out_full, bf16, [rs_ffn.token])
        dwout = _mm_tn_rows("mix_out_dw", mixed, dxb)
        if layer % 2 == 0:
            proj, btab = ctx
            dq, dk, dv, dtab = _attn_bwd(proj, dmixed, btab, heads)
            db, dc, dhv, dcw = _conv_bwd(proj, dmixed, conv_full[i], a_blocks, b_blocks)
            dproj = jnp.concatenate([dq, dk, dv, db, dc, dhv], axis=-1)
            small["ab_rel_bias"][i] = _bias_table_grad(dtab)
            small["ab_conv_w"][i] = dcw
            names = ("ab_w_in", "ab_w_out")
            tag = "ab"
        else:
            proj, bs_t = ctx
            dproj, dws, dbs_t, dlg, dlb = _sgu_bwd(proj, dmixed, lng_full[i], lnb_full[i], c_w_s[i], bs_t)
            small["c_w_s"][i] = dws
            small["c_b_s"][i] = jnp.transpose(dbs_t)
            small["c_ln_g"][i] = dlg[0]
            small["c_ln_b"][i] = dlb[0]
            names = ("c_w_in", "c_w_out")
            tag = "c"
        dwin = _mm_tn_cols(tag + "_proj_dw", h, dproj)
        rs_mix = _ReduceScatter("%s%d" % (tag, i), [dwin, dwout], core)
        dh = _mm_nt_cols(tag + "_proj_bwd", dproj, w_in_g, [rs_mix.token])
        dx, dxb, dgm = _rms_bwd(xs, mix_norm[layer], dh, dx)
        small["mix_norm"][layer] = dgm[0]
        rs_mix.middle(dxb, core)
        tokens.append(rs_mix.token)
        scatters[(names[0], i)] = (rs_mix, 0)
        scatters[(names[1], i)] = (rs_mix, 1)
    grad_x = dx[None]

    small_names = ["mix_norm", "ffn_norm", "ab_rel_bias", "ab_conv_w", "c_ln_g", "c_ln_b", "c_w_s", "c_b_s"]
    small_full = [jnp.stack(small[k]) for k in small_names] + [d_final[0]]
    summed = _unpack(_gather_small("reduce_small", _pack(small_full), reduce=True), [a.shape for a in small_full])
    small_grads = dict(zip(small_names + ["final_norm"], summed))
    for k in ("ab_conv_w", "c_ln_g", "c_ln_b"):
        width = weights[k].shape[-1]
        small_grads[k] = lax.dynamic_slice_in_dim(small_grads[k], me_s * width, width, axis=-1)
    small_order = [k for k in order if k in small_grads]
    shapes = [weights[k].shape for k in small_order]
    d_s, m_s, v_s = _adam_small(_pack([weights[k] for k in small_order]), _pack([small_grads[k] for k in small_order]),
                                _pack([mom_m[k] for k in small_order]), _pack([mom_v[k] for k in small_order]))
    grads, deltas, new_m, new_v = dict(small_grads), {}, {}, {}
    for k, dd, mm, vv in zip(small_order, _unpack(d_s, shapes), _unpack(m_s, shapes), _unpack(v_s, shapes)):
        deltas[k], new_m[k], new_v[k] = dd, mm, vv

    finished = {}
    last = started()[-1]
    for k in ("c_w_in", "c_w_out", "ffn_w_gate", "ffn_w_up", "ffn_w_down", "ab_w_in", "ab_w_out"):
        parts = []
        w_k, m_k, v_k = local[k]
        for li in range(w_k.shape[0]):
            rs, pos = scatters[(k, li)]
            if id(rs) not in finished:
                finished[id(rs)] = rs.finish(last)
            parts.append(finished[id(rs)][pos])
        outs = _adam_big("adam_" + k, w_k, m_k, v_k, parts, chip)
        last = outs[1]
        grads[k], deltas[k], new_m[k], new_v[k] = [flip(o) for o in outs] if k in wide else outs

    return (loss, grad_x, *[grads[k] for k in order], *[deltas[k] for k in order], *[new_m[k] for k in order],
            *[new_v[k] for k in order])
```

```python
import numpy as np
import jax
import jax.numpy as jnp
from jax import lax
from jax.experimental import pallas as pl
from jax.experimental.pallas import tpu as pltpu

D_MODEL = 2048
SEQ = 2048
DEPTH = 4
CHUNK = 64
A_HEAD_DIM = 128
A_LEFT_CHUNKS = 8
A_MAX_REL = 256
CONV_WIDTH = 3
C_BLOCK = 128
C_GROUPS = 8
EPS = 1e-6
NEG_INF = -1e30

ADAM_LR = 0.001
ADAM_B1 = 0.9
ADAM_B2 = 0.999
ADAM_EPS = 1e-08
ADAM_WD = 0.01
ADAM_STEP = 10

N_DEV = 8
N_CHIP = 4
RS_SPLIT = 4
D2D_SPLIT = 2
GATHER_AHEAD = 4
ROWS_PER_STEP = 1024
COL_CHUNK = 256
LANE = 128
VMEM_LIMIT = 52 * 1024 * 1024

bf16 = jnp.bfloat16
f32 = jnp.float32
MESH = pl.DeviceIdType.MESH
ANY = pl.BlockSpec(memory_space=pl.ANY)


def _params(*sem):
    return pltpu.CompilerParams(dimension_semantics=sem, vmem_limit_bytes=VMEM_LIMIT)


def _perm(j):
    return (j % 2) * N_CHIP + j // 2


_DN = {"nn": (((1,), (0,)), ((), ())), "nt": (((1,), (1,)), ((), ())), "tn": (((0,), (0,)), ((), ()))}


def _matmul(name, mode, grid, operands, specs, pairs, n_acc, acc_shape, extras, extra_specs, out_shapes, out_specs,
            epilogue, chunk=0):
    nk = grid[2]
    n_op, n_ex, n_out = len(operands), len(extras), len(out_shapes)

    def single(*refs):
        ops = refs[:n_op]
        ex = refs[n_op:n_op + n_ex]
        outs = refs[n_op + n_ex:]
        width = acc_shape[1]
        starts = range(0, width, chunk) if chunk else (0,)
        for c0 in starts:
            cols = slice(c0, min(c0 + chunk, width)) if chunk else slice(None)
            sums = [None] * n_acc
            for p, (ia, ib) in enumerate(pairs):
                b = ops[ib][cols, :] if mode == "nt" else ops[ib][:, cols]
                d = lax.dot_general(ops[ia][...], b, _DN[mode], preferred_element_type=f32)
                sums[p % n_acc] = d if sums[p % n_acc] is None else sums[p % n_acc] + d
            epilogue(sums, ex, outs, cols)

    def body(*refs):
        ops = refs[:n_op]
        ex = refs[n_op:n_op + n_ex]
        outs = refs[n_op + n_ex:n_op + n_ex + n_out]
        accs = refs[n_op + n_ex + n_out:]
        k = pl.program_id(2)

        @pl.when(k == 0)
        def _():
            for acc in accs:
                acc[...] = jnp.zeros_like(acc)

        for p, (ia, ib) in enumerate(pairs):
            acc = accs[p % n_acc]
            acc[...] += lax.dot_general(ops[ia][...], ops[ib][...], _DN[mode], preferred_element_type=f32)

        @pl.when(k == nk - 1)
        def _():
            epilogue([acc[...] for acc in accs], ex, outs, slice(None))

    return pl.pallas_call(
        single if nk == 1 else body, grid=grid, in_specs=list(specs) + list(extra_specs), out_specs=list(out_specs),
        out_shape=list(out_shapes), scratch_shapes=[] if nk == 1 else [pltpu.VMEM(acc_shape, f32)] * n_acc,
        compiler_params=_params("parallel", "parallel", "arbitrary"), name=name)(*operands, *extras)


def _store(dtype):
    def ep(accs, ex, outs, cols):
        for a, o in zip(accs, outs):
            o[:, cols] = a.astype(dtype)
    return ep


def _mm_cols(name, h, wg, out_dtype, deps=()):
    t, kd = h.shape
    n8 = wg.shape[2]
    tm = min(t, 2 * ROWS_PER_STEP)
    return _matmul(
        name, "nn", (t // tm, N_DEV, 1), [h, wg],
        [pl.BlockSpec((tm, kd), lambda i, j, k: (i, 0)), pl.BlockSpec((None, kd, n8), lambda i, j, k: (j, 0, 0))],
        [(0, 1)], 1, (tm, n8), list(deps), [ANY] * len(deps), [jax.ShapeDtypeStruct((t, N_DEV * n8), out_dtype)],
        [pl.BlockSpec((tm, n8), lambda i, j, k: (i, j))], _store(out_dtype), COL_CHUNK)[0]


def _norm_rows(xv, gain):
    return (xv * lax.rsqrt(jnp.mean(xv * xv, axis=-1, keepdims=True) + EPS) * gain).astype(bf16)


def _mm_rows_res(name, a, w, res, gain, deps=()):
    t, kd = a.shape
    n = w.shape[1]
    tm = min(t, 512)

    def ep(accs, ex, outs, cols):
        xv = ex[0][...] + accs[0]
        outs[0][...] = xv
        outs[1][...] = _norm_rows(xv, ex[1][...])

    row = pl.BlockSpec((tm, n), lambda i, j, k: (i, 0))
    return _matmul(
        name, "nn", (t // tm, 1, 1), [a, w],
        [pl.BlockSpec((tm, kd), lambda i, j, k: (i, 0)), pl.BlockSpec((kd, n), lambda i, j, k: (0, 0))],
        [(0, 1)], 1, (tm, n), [res, gain.reshape(1, n)] + list(deps),
        [row, pl.BlockSpec((1, n), lambda i, j, k: (0, 0))] + [ANY] * len(deps),
        [jax.ShapeDtypeStruct((t, n), f32), jax.ShapeDtypeStruct((t, n), bf16)], [row, row], ep)


def _mm_nt(name, a, w, out_dtype, deps=()):
    t, n = a.shape
    kd = w.shape[0]
    tm, tn = min(t, 2 * ROWS_PER_STEP), min(kd, 1024)
    return _matmul(
        name, "nt", (t // tm, kd // tn, 1), [a, w],
        [pl.BlockSpec((tm, n), lambda i, j, k: (i, 0)), pl.BlockSpec((tn, n), lambda i, j, k: (j, 0))],
        [(0, 1)], 1, (tm, tn), list(deps), [ANY] * len(deps), [jax.ShapeDtypeStruct((t, kd), out_dtype)],
        [pl.BlockSpec((tm, tn), lambda i, j, k: (i, j))], _store(out_dtype), COL_CHUNK)[0]


def _mm_nt_cols(name, da, wg, deps=()):
    t = da.shape[0]
    kd, n8 = wg.shape[1], wg.shape[2]
    tm = min(t, ROWS_PER_STEP)
    return _matmul(
        name, "nt", (t // tm, 1, N_DEV), [da, wg],
        [pl.BlockSpec((tm, n8), lambda i, j, k: (i, k)), pl.BlockSpec((None, kd, n8), lambda i, j, k: (k, 0, 0))],
        [(0, 1)], 1, (tm, kd), list(deps), [ANY] * len(deps), [jax.ShapeDtypeStruct((t, kd), f32)],
        [pl.BlockSpec((tm, kd), lambda i, j, k: (i, 0))], _store(f32))[0]


def _mm_tn_cols(name, h, da):
    t, kd = h.shape
    n8 = da.shape[1] // N_DEV
    tmk, tk = min(kd, 1024), min(t, 1024)
    return _matmul(
        name, "tn", (kd // tmk, N_DEV, t // tk), [h, da],
        [pl.BlockSpec((tk, tmk), lambda i, j, k: (k, i)), pl.BlockSpec((tk, n8), lambda i, j, k: (k, j))],
        [(0, 1)], 1, (tmk, n8), [], [], [jax.ShapeDtypeStruct((N_DEV, kd, n8), bf16)],
        [pl.BlockSpec((None, tmk, n8), lambda i, j, k: (_perm(j), i, 0))], _store(bf16))[0]


def _mm_tn_rows(name, a, dx):
    t, kf = a.shape
    r8 = kf // N_DEV
    n = dx.shape[1]
    return _matmul(
        name, "tn", (N_DEV, 1, 1), [a, dx],
        [pl.BlockSpec((t, r8), lambda i, j, k: (0, i)), pl.BlockSpec((t, n), lambda i, j, k: (0, 0))],
        [(0, 1)], 1, (r8, n), [], [], [jax.ShapeDtypeStruct((N_DEV, r8, n), bf16)],
        [pl.BlockSpec((None, r8, n), lambda i, j, k: (_perm(i), 0, 0))], _store(bf16))[0]


def _ffn_in(h2, wg_t, wu_t, deps=()):
    t, kd = h2.shape
    f8 = wg_t.shape[1]
    tm = min(t, ROWS_PER_STEP)

    def ep(accs, ex, outs, cols):
        g, u = accs
        outs[0][:, cols] = g.astype(bf16)
        outs[1][:, cols] = u.astype(bf16)
        outs[2][:, cols] = (g * jax.nn.sigmoid(g) * u).astype(bf16)

    wspec = pl.BlockSpec((None, f8, kd), lambda i, j, k: (j, 0, 0))
    ospec = pl.BlockSpec((None, tm, f8), lambda i, j, k: (j, i, 0))
    return _matmul(
        "ffn_in", "nt", (t // tm, N_DEV, 1), [h2, wg_t, wu_t],
        [pl.BlockSpec((tm, kd), lambda i, j, k: (i, 0)), wspec, wspec], [(0, 1), (0, 2)], 2, (tm, f8), list(deps),
        [ANY] * len(deps), [jax.ShapeDtypeStruct((N_DEV, t, f8), bf16)] * 3, [ospec] * 3, ep, COL_CHUNK)


def _ffn_down(act, wd, res, gain, deps=()):
    _, t, f8 = act.shape
    n = wd.shape[2]
    tm = min(t, 512)

    def ep(accs, ex, outs, cols):
        xv = ex[0][...] + accs[0]
        outs[0][...] = xv
        if gain is not None:
            outs[1][...] = _norm_rows(xv, ex[1][...])

    row = pl.BlockSpec((tm, n), lambda i, j, k: (i, 0))
    extras, extra_specs = [res], [row]
    shapes, specs = [jax.ShapeDtypeStruct((t, n), f32)], [row]
    if gain is not None:
        extras.append(gain.reshape(1, n))
        extra_specs.append(pl.BlockSpec((1, n), lambda i, j, k: (0, 0)))
        shapes.append(jax.ShapeDtypeStruct((t, n), bf16))
        specs.append(row)
    return _matmul(
        "ffn_down", "nn", (t // tm, 1, N_DEV), [act, wd],
        [pl.BlockSpec((None, tm, f8), lambda i, j, k: (k, i, 0)), pl.BlockSpec((None, f8, n), lambda i, j, k: (k, 0, 0))],
        [(0, 1)], 1, (tm, n), extras + list(deps), extra_specs + [ANY] * len(deps), shapes, specs, ep)


def _ffn_bwd_act(dxb, wd, g, u, deps=()):
    t, n = dxb.shape
    f8 = wd.shape[1]
    tm = min(t, ROWS_PER_STEP)

    def ep(accs, ex, outs, cols):
        dact = accs[0]
        gv = ex[0][:, cols].astype(f32)
        uv = ex[1][:, cols].astype(f32)
        sg = jax.nn.sigmoid(gv)
        silu = gv * sg
        outs[0][:, cols] = (dact * uv * (sg * (1.0 + gv * (1.0 - sg)))).astype(bf16)
        outs[1][:, cols] = (dact * silu).astype(bf16)
        outs[2][:, cols] = (silu * uv).astype(bf16)

    bspec = pl.BlockSpec((None, tm, f8), lambda i, j, k: (j, i, 0))
    return _matmul(
        "ffn_bwd_act", "nt", (t // tm, N_DEV, 1), [dxb, wd],
        [pl.BlockSpec((tm, n), lambda i, j, k: (i, 0)), pl.BlockSpec((None, f8, n), lambda i, j, k: (j, 0, 0))],
        [(0, 1)], 1, (tm, f8), [g, u] + list(deps), [bspec, bspec] + [ANY] * len(deps),
        [jax.ShapeDtypeStruct((N_DEV, t, f8), bf16)] * 3, [bspec] * 3, ep, COL_CHUNK)


def _ffn_dwd(act, dxb):
    _, t, f8 = act.shape
    n = dxb.shape[1]
    tk = min(t, 1024)
    return _matmul(
        "ffn_dwd", "tn", (N_DEV, 1, t // tk), [act, dxb],
        [pl.BlockSpec((None, tk, f8), lambda i, j, k: (i, k, 0)), pl.BlockSpec((tk, n), lambda i, j, k: (k, 0))],
        [(0, 1)], 1, (f8, n), [], [], [jax.ShapeDtypeStruct((N_DEV, f8, n), bf16)],
        [pl.BlockSpec((None, f8, n), lambda i, j, k: (_perm(i), 0, 0))], _store(bf16))[0]


def _ffn_dwgu(h2, dg, du):
    t, kd = h2.shape
    f8 = dg.shape[2]
    tk = min(t, 1024)
    aspec = pl.BlockSpec((None, tk, f8), lambda i, j, k: (i, k, 0))
    ospec = pl.BlockSpec((None, f8, kd), lambda i, j, k: (_perm(i), 0, 0))
    return _matmul(
        "ffn_dwgu", "tn", (N_DEV, 1, t // tk), [dg, du, h2],
        [aspec, aspec, pl.BlockSpec((tk, kd), lambda i, j, k: (k, 0))], [(0, 2), (1, 2)], 2, (f8, kd), [], [],
        [jax.ShapeDtypeStruct((N_DEV, f8, kd), bf16)] * 2, [ospec] * 2, _store(bf16))


def _ffn_dh(dg, du, wg_t, wu_t, deps=()):
    _, t, f8 = dg.shape
    kd = wg_t.shape[2]
    tm = min(t, ROWS_PER_STEP)
    aspec = pl.BlockSpec((None, tm, f8), lambda i, j, k: (k, i, 0))
    wspec = pl.BlockSpec((None, f8, kd), lambda i, j, k: (k, 0, 0))
    return _matmul(
        "ffn_dh", "nn", (t // tm, 1, N_DEV), [dg, du, wg_t, wu_t], [aspec, aspec, wspec, wspec], [(0, 2), (1, 3)], 1,
        (tm, kd), list(deps), [ANY] * len(deps), [jax.ShapeDtypeStruct((t, kd), f32)],
        [pl.BlockSpec((tm, kd), lambda i, j, k: (i, 0))], _store(f32))[0]


def _rms_fwd(x, g):
    t, d = x.shape
    tm = min(t, 256)

    def body(x_ref, g_ref, o_ref):
        xv = x_ref[...]
        r = lax.rsqrt(jnp.mean(xv * xv, axis=-1, keepdims=True) + EPS)
        o_ref[...] = (xv * r * g_ref[...]).astype(bf16)

    return pl.pallas_call(
        body, grid=(t // tm,), in_specs=[pl.BlockSpec((tm, d), lambda i: (i, 0)), pl.BlockSpec((1, d), lambda i: (0, 0))],
        out_specs=pl.BlockSpec((tm, d), lambda i: (i, 0)), out_shape=jax.ShapeDtypeStruct((t, d), bf16),
        compiler_params=_params("parallel"), name="rms_fwd")(x, g.reshape(1, d))


def _rms_bwd(x, g, dh, dres):
    t, d = x.shape
    tm = min(t, 256)

    def body(x_ref, g_ref, dh_ref, dres_ref, dx_ref, dxb_ref, dg_ref):
        xv = x_ref[...]
        dy = dh_ref[...].astype(f32)
        r = lax.rsqrt(jnp.mean(xv * xv, axis=-1, keepdims=True) + EPS)
        gy = dy * g_ref[...]
        dot = jnp.mean(xv * gy, axis=-1, keepdims=True)
        dx = dres_ref[...] + r * gy - xv * (r * r * r * dot)
        dx_ref[...] = dx
        dxb_ref[...] = dx.astype(bf16)

        @pl.when(pl.program_id(0) == 0)
        def _():
            dg_ref[...] = jnp.zeros_like(dg_ref)

        dg_ref[...] += jnp.sum(dy * xv * r, axis=0, keepdims=True)

    row = pl.BlockSpec((tm, d), lambda i: (i, 0))
    vec = pl.BlockSpec((1, d), lambda i: (0, 0))
    return pl.pallas_call(
        body, grid=(t // tm,), in_specs=[row, vec, row, row], out_specs=[row, row, vec],
        out_shape=[jax.ShapeDtypeStruct((t, d), f32), jax.ShapeDtypeStruct((t, d), bf16), jax.ShapeDtypeStruct((1, d), f32)],
        compiler_params=_params("arbitrary"), name="rms_bwd")(x, g.reshape(1, d), dh, dres)


def _loss_head(x, g, target):
    t, d = x.shape
    tm = min(t, 256)

    def body(x_ref, g_ref, t_ref, loss_ref, dx_ref, dxb_ref, dg_ref):
        xv = x_ref[...]
        r = lax.rsqrt(jnp.mean(xv * xv, axis=-1, keepdims=True) + EPS)
        xn = xv * r
        err = xn * g_ref[...] - t_ref[...]
        dy = err * (1.0 / d)
        gy = dy * g_ref[...]
        dot = jnp.mean(xv * gy, axis=-1, keepdims=True)
        dx = r * gy - xv * (r * r * r * dot)
        dx_ref[...] = dx
        dxb_ref[...] = dx.astype(bf16)

        @pl.when(pl.program_id(0) == 0)
        def _():
            dg_ref[...] = jnp.zeros_like(dg_ref)
            loss_ref[...] = jnp.zeros_like(loss_ref)

        dg_ref[...] += jnp.sum(dy * xn, axis=0, keepdims=True)
        loss_ref[...] += 0.5 * jnp.sum(jnp.sum(err * err, axis=-1, keepdims=True) * (1.0 / d), axis=0, keepdims=True)

    row = pl.BlockSpec((tm, d), lambda i: (i, 0))
    vec = pl.BlockSpec((1, d), lambda i: (0, 0))
    one = pl.BlockSpec((1, 1), lambda i: (0, 0))
    return pl.pallas_call(
        body, grid=(t // tm,), in_specs=[row, vec, row], out_specs=[one, row, row, vec],
        out_shape=[jax.ShapeDtypeStruct((1, 1), f32), jax.ShapeDtypeStruct((t, d), f32),
                   jax.ShapeDtypeStruct((t, d), bf16), jax.ShapeDtypeStruct((1, d), f32)],
        compiler_params=_params("arbitrary"), name="loss_head")(x, g.reshape(1, d), target)


def _attn_consts():
    qt, kw = 2 * CHUNK, (A_LEFT_CHUNKS + 2) * CHUNK
    r = np.arange(qt)[:, None]
    kc = np.arange(kw)[None, :]
    rel = np.clip(r + A_LEFT_CHUNKS * CHUNK - kc, -A_MAX_REL, A_MAX_REL) + A_MAX_REL
    dchunk = kc // CHUNK - r // CHUNK
    valid = (dchunk >= 0) & (dchunk <= A_LEFT_CHUNKS)
    m = np.arange(kw + qt)
    relidx = np.clip(A_LEFT_CHUNKS * CHUNK - (m - (qt - 1)), -A_MAX_REL, A_MAX_REL) + A_MAX_REL
    onehot = np.zeros((kw + qt, 2 * A_MAX_REL + 1), np.float32)
    onehot[m, relidx] = 1.0
    return qt, kw, rel, valid, onehot


def _bias_table(rel_bias):
    qt, kw, _, valid, onehot = _attn_consts()
    h = rel_bias.shape[0]
    w = kw + qt
    relidx = np.argmax(onehot, axis=1)
    e = jnp.roll(jnp.take(rel_bias, jnp.asarray(relidx), axis=1), -(qt - 1), axis=1)
    rows = jnp.broadcast_to(e[:, None, :], (h, qt, w)).reshape(h, qt * w)
    skew = rows[:, :qt * (w - 1)].reshape(h, qt, w - 1)[:, :, :kw]
    return jnp.where(jnp.asarray(valid)[None], skew, NEG_INF).astype(f32)


def _bias_table_grad(dtab):
    qt, kw, _, _, onehot = _attn_consts()
    h = dtab.shape[0]
    w = kw + qt
    wide = -(-(w + qt) // LANE) * LANE
    y = jnp.pad(dtab, ((0, 0), (0, 0), (qt - 1, wide - kw - (qt - 1))))
    flat = jnp.pad(y.reshape(h, qt * wide), ((0, 0), (0, qt)))
    de = jnp.sum(flat.reshape(h, qt, wide + 1), axis=1)[:, :w]
    return jnp.dot(de, jnp.asarray(onehot), precision=lax.Precision.HIGHEST)


def _attn_scores(q_ref, kpad, btab_ref, r0, qt, kw, pad):
    qv = q_ref[pl.ds(r0, qt), :]
    kwin = kpad[pl.ds(r0, kw), :]
    s = lax.dot_general(qv, kwin, _DN["nt"], preferred_element_type=f32) * (A_HEAD_DIM ** -0.5) + btab_ref[...]
    kcol = lax.broadcasted_iota(jnp.int32, (qt, kw), 1)
    s = jnp.where(r0 + kcol >= pad, s, NEG_INF)
    p = jnp.exp(s - jnp.max(s, axis=-1, keepdims=True))
    return qv, kwin, p / jnp.sum(p, axis=-1, keepdims=True)


def _attn_fwd(proj, btab, heads):
    t = proj.shape[0]
    qt, kw = btab.shape[1], btab.shape[2]
    pad = kw - qt

    def body(q_ref, k_ref, v_ref, btab_ref, o_ref, kpad, vpad):
        zeros = jnp.zeros((pad, A_HEAD_DIM), bf16)
        kpad[pl.ds(0, pad), :] = zeros
        vpad[pl.ds(0, pad), :] = zeros
        kpad[pl.ds(pad, t), :] = k_ref[...]
        vpad[pl.ds(pad, t), :] = v_ref[...]

        def tile(i, carry):
            r0 = pl.multiple_of(i * qt, qt)
            _, _, p = _attn_scores(q_ref, kpad, btab_ref, r0, qt, kw, pad)
            o = lax.dot_general(p.astype(bf16), vpad[pl.ds(r0, kw), :], _DN["nn"], preferred_element_type=f32)
            o_ref[pl.ds(r0, qt), :] = o.astype(bf16)
            return carry

        lax.fori_loop(0, t // qt, tile, 0, unroll=4)

    col = lambda off: pl.BlockSpec((t, A_HEAD_DIM), lambda h, off=off: (0, off + h))
    return pl.pallas_call(
        body, grid=(heads,),
        in_specs=[col(0), col(heads), col(2 * heads), pl.BlockSpec((None, qt, kw), lambda h: (h, 0, 0))],
        out_specs=col(0), out_shape=jax.ShapeDtypeStruct((t, heads * A_HEAD_DIM), bf16),
        scratch_shapes=[pltpu.VMEM((t + pad, A_HEAD_DIM), bf16)] * 2,
        compiler_params=_params("parallel"), name="attn_fwd")(proj, proj, proj, btab)


def _attn_bwd(proj, dmix, btab, heads):
    t = proj.shape[0]
    qt, kw = btab.shape[1], btab.shape[2]
    pad = kw - qt
    scale = A_HEAD_DIM ** -0.5

    def body(q_ref, k_ref, v_ref, do_ref, btab_ref, dq_ref, dk_ref, dv_ref, dtab_ref, kpad, vpad, dkacc, dvacc):
        zeros = jnp.zeros((pad, A_HEAD_DIM), bf16)
        kpad[pl.ds(0, pad), :] = zeros
        vpad[pl.ds(0, pad), :] = zeros
        kpad[pl.ds(pad, t), :] = k_ref[...]
        vpad[pl.ds(pad, t), :] = v_ref[...]
        dkacc[...] = jnp.zeros_like(dkacc)
        dvacc[...] = jnp.zeros_like(dvacc)
        dtab_ref[...] = jnp.zeros_like(dtab_ref)

        def tile(i, carry):
            r0 = pl.multiple_of(i * qt, qt)
            qv, kwin, p = _attn_scores(q_ref, kpad, btab_ref, r0, qt, kw, pad)
            dov = do_ref[pl.ds(r0, qt), :]
            dp = lax.dot_general(dov, vpad[pl.ds(r0, kw), :], _DN["nt"], preferred_element_type=f32)
            ds = p * (dp - jnp.sum(p * dp, axis=-1, keepdims=True))
            dtab_ref[...] += ds
            dsb = ds.astype(bf16)
            dq = lax.dot_general(dsb, kwin, _DN["nn"], preferred_element_type=f32) * scale
            dq_ref[pl.ds(r0, qt), :] = dq.astype(bf16)
            dkacc[pl.ds(r0, kw), :] += lax.dot_general(dsb, qv, _DN["tn"], preferred_element_type=f32) * scale
            dvacc[pl.ds(r0, kw), :] += lax.dot_general(p.astype(bf16), dov, _DN["tn"], preferred_element_type=f32)
            return carry

        lax.fori_loop(0, t // qt, tile, 0, unroll=4)
        dk_ref[...] = dkacc[pl.ds(pad, t), :].astype(bf16)
        dv_ref[...] = dvacc[pl.ds(pad, t), :].astype(bf16)

    col = lambda off: pl.BlockSpec((t, A_HEAD_DIM), lambda h, off=off: (0, off + h))
    tab = pl.BlockSpec((None, qt, kw), lambda h: (h, 0, 0))
    wide = jax.ShapeDtypeStruct((t, heads * A_HEAD_DIM), bf16)
    return pl.pallas_call(
        body, grid=(heads,), in_specs=[col(0), col(heads), col(2 * heads), col(0), tab],
        out_specs=[col(0), col(0), col(0), tab],
        out_shape=[wide, wide, wide, jax.ShapeDtypeStruct((heads, qt, kw), f32)],
        scratch_shapes=[pltpu.VMEM((t + pad, A_HEAD_DIM), bf16)] * 2 + [pltpu.VMEM((t + pad, A_HEAD_DIM), f32)] * 2,
        compiler_params=_params("parallel"), name="attn_bwd")(proj, proj, proj, dmix, btab)


def _shift_down(z, k):
    rows = lax.broadcasted_iota(jnp.int32, z.shape, 0)
    return jnp.where(rows >= k, pltpu.roll(z, k, 0), 0.0)


def _shift_up(z, k):
    t = z.shape[0]
    rows = lax.broadcasted_iota(jnp.int32, z.shape, 0)
    return jnp.where(rows < t - k, pltpu.roll(z, t - k, 0), 0.0)


def _conv_fwd(proj, conv_w, a_blocks, b_blocks):
    t = proj.shape[0]

    def body(b_ref, c_ref, h_ref, w_ref, o_ref):
        z = c_ref[...].astype(f32) * h_ref[...].astype(f32)
        w = w_ref[...]
        y = w[0:1, :] * _shift_down(z, 2) + w[1:2, :] * _shift_down(z, 1) + w[2:3, :] * z
        o_ref[...] = (b_ref[...].astype(f32) * y).astype(bf16)

    col = lambda off: pl.BlockSpec((t, LANE), lambda i, off=off: (0, off + i))
    return pl.pallas_call(
        body, grid=(b_blocks,),
        in_specs=[col(3 * a_blocks), col(3 * a_blocks + b_blocks), col(3 * a_blocks + 2 * b_blocks),
                  pl.BlockSpec((CONV_WIDTH, LANE), lambda i: (0, i))],
        out_specs=col(0), out_shape=jax.ShapeDtypeStruct((t, b_blocks * LANE), bf16),
        compiler_params=_params("parallel"), name="conv_fwd")(proj, proj, proj, conv_w)


def _conv_bwd(proj, dmix, conv_w, a_blocks, b_blocks):
    t = proj.shape[0]

    def body(b_ref, c_ref, h_ref, do_ref, w_ref, db_ref, dc_ref, dh_ref, dw_ref):
        bv, cv, hv = b_ref[...].astype(f32), c_ref[...].astype(f32), h_ref[...].astype(f32)
        w = w_ref[...]
        z = cv * hv
        z1, z2 = _shift_down(z, 1), _shift_down(z, 2)
        y = w[0:1, :] * z2 + w[1:2, :] * z1 + w[2:3, :] * z
        dov = do_ref[...].astype(f32)
        db_ref[...] = (dov * y).astype(bf16)
        dy = dov * bv
        dz = w[2:3, :] * dy + w[1:2, :] * _shift_up(dy, 1) + w[0:1, :] * _shift_up(dy, 2)
        dc_ref[...] = (dz * hv).astype(bf16)
        dh_ref[...] = (dz * cv).astype(bf16)
        dw_ref[0:1, :] = jnp.sum(dy * z2, axis=0, keepdims=True)
        dw_ref[1:2, :] = jnp.sum(dy * z1, axis=0, keepdims=True)
        dw_ref[2:3, :] = jnp.sum(dy * z, axis=0, keepdims=True)

    col = lambda off: pl.BlockSpec((t, LANE), lambda i, off=off: (0, off + i))
    wspec = pl.BlockSpec((CONV_WIDTH, LANE), lambda i: (0, i))
    wide = jax.ShapeDtypeStruct((t, b_blocks * LANE), bf16)
    return pl.pallas_call(
        body, grid=(b_blocks,),
        in_specs=[col(3 * a_blocks), col(3 * a_blocks + b_blocks), col(3 * a_blocks + 2 * b_blocks), col(a_blocks), wspec],
        out_specs=[col(0), col(0), col(0), wspec],
        out_shape=[wide, wide, wide, jax.ShapeDtypeStruct((CONV_WIDTH, b_blocks * LANE), f32)],
        compiler_params=_params("parallel"), name="conv_bwd")(proj, proj, proj, dmix, conv_w)


_RSQRT2 = 0.7071067811865476
_RSQRT2PI = 0.3989422804014327


def _gelu(x):
    return 0.5 * x * (1.0 + lax.erf(x * _RSQRT2))


def _gelu_grad(x):
    return 0.5 * (1.0 + lax.erf(x * _RSQRT2)) + x * jnp.exp(-0.5 * x * x) * _RSQRT2PI


def _sgu_common(a_ref, lg_ref, lb_ref, cw):
    av = a_ref[...]
    u = _gelu(av[:, :cw])
    v = _gelu(av[:, cw:])
    mu = jnp.mean(v, axis=-1, keepdims=True)
    xc = v - mu
    rstd = lax.rsqrt(jnp.mean(xc * xc, axis=-1, keepdims=True) + EPS)
    xhat = xc * rstd
    vln = xhat * lg_ref[...] + lb_ref[...]
    pos_t = lax.broadcasted_iota(jnp.int32, (C_BLOCK, C_BLOCK), 0) // CHUNK
    pos_s = lax.broadcasted_iota(jnp.int32, (C_BLOCK, C_BLOCK), 1) // CHUNK
    return av, u, xhat, rstd, vln, pos_s <= pos_t


def _sgu_fwd(a, ln_g, ln_b, w_s, bs_t):
    t, cw2 = a.shape
    cw = cw2 // 2
    groups = w_s.shape[0]
    cg = cw // groups

    def body(a_ref, lg_ref, lb_ref, ws_ref, bs_ref, m_ref):
        _, u, _, _, vln, mask = _sgu_common(a_ref, lg_ref, lb_ref, cw)
        vb = vln.astype(bf16)
        for g in range(groups):
            sl = slice(g * cg, (g + 1) * cg)
            wm = jnp.where(mask, ws_ref[g], 0.0).astype(bf16)
            s = lax.dot_general(wm, vb[:, sl], _DN["nn"], preferred_element_type=f32) + bs_ref[:, g:g + 1]
            m_ref[:, sl] = (u[:, sl] * s).astype(bf16)

    vec = pl.BlockSpec((1, cw), lambda n: (0, 0))
    return pl.pallas_call(
        body, grid=(t // C_BLOCK,),
        in_specs=[pl.BlockSpec((C_BLOCK, cw2), lambda n: (n, 0)), vec, vec,
                  pl.BlockSpec((groups, C_BLOCK, C_BLOCK), lambda n: (0, 0, 0)),
                  pl.BlockSpec((C_BLOCK, groups), lambda n: (0, 0))],
        out_specs=pl.BlockSpec((C_BLOCK, cw), lambda n: (n, 0)), out_shape=jax.ShapeDtypeStruct((t, cw), bf16),
        compiler_params=_params("parallel"), name="sgu_fwd")(a, ln_g.reshape(1, cw), ln_b.reshape(1, cw), w_s, bs_t)


def _sgu_bwd(a, dm, ln_g, ln_b, w_s, bs_t):
    t, cw2 = a.shape
    cw = cw2 // 2
    groups = w_s.shape[0]
    cg = cw // groups

    def body(a_ref, dm_ref, lg_ref, lb_ref, ws_ref, bs_ref, da_ref, dws_ref, dbs_ref, dlg_ref, dlb_ref, dvln):
        @pl.when(pl.program_id(0) == 0)
        def _():
            dws_ref[...] = jnp.zeros_like(dws_ref)
            dbs_ref[...] = jnp.zeros_like(dbs_ref)
            dlg_ref[...] = jnp.zeros_like(dlg_ref)
            dlb_ref[...] = jnp.zeros_like(dlb_ref)

        av, u, xhat, rstd, vln, mask = _sgu_common(a_ref, lg_ref, lb_ref, cw)
        vb = vln.astype(bf16)
        lane = lax.broadcasted_iota(jnp.int32, (C_BLOCK, groups), 1)
        dbs = jnp.zeros((C_BLOCK, groups), f32)
        for g in range(groups):
            sl = slice(g * cg, (g + 1) * cg)
            wm = jnp.where(mask, ws_ref[g], 0.0).astype(bf16)
            s = lax.dot_general(wm, vb[:, sl], _DN["nn"], preferred_element_type=f32) + bs_ref[:, g:g + 1]
            dmg = dm_ref[:, sl].astype(f32)
            da_ref[:, sl] = (dmg * s * _gelu_grad(av[:, sl])).astype(bf16)
            dsg = dmg * u[:, sl]
            dbs = dbs + jnp.where(lane == g, jnp.sum(dsg, axis=-1, keepdims=True), 0.0)
            dsb = dsg.astype(bf16)
            dws_ref[g] += jnp.where(mask, lax.dot_general(dsb, vb[:, sl], _DN["nt"], preferred_element_type=f32), 0.0)
            dvln[:, sl] = lax.dot_general(wm, dsb, _DN["tn"], preferred_element_type=f32)
        dbs_ref[...] += dbs
        dv = dvln[...]
        dlg_ref[...] += jnp.sum(dv * xhat, axis=0, keepdims=True)
        dlb_ref[...] += jnp.sum(dv, axis=0, keepdims=True)
        dxh = dv * lg_ref[...]
        dvv = rstd * (dxh - jnp.mean(dxh, axis=-1, keepdims=True) - xhat * jnp.mean(dxh * xhat, axis=-1, keepdims=True))
        da_ref[:, cw:] = (dvv * _gelu_grad(av[:, cw:])).astype(bf16)

    vec = pl.BlockSpec((1, cw), lambda n: (0, 0))
    wsp = pl.BlockSpec((groups, C_BLOCK, C_BLOCK), lambda n: (0, 0, 0))
    bsp = pl.BlockSpec((C_BLOCK, groups), lambda n: (0, 0))
    return pl.pallas_call(
        body, grid=(t // C_BLOCK,),
        in_specs=[pl.BlockSpec((C_BLOCK, cw2), lambda n: (n, 0)), pl.BlockSpec((C_BLOCK, cw), lambda n: (n, 0)), vec, vec, wsp, bsp],
        out_specs=[pl.BlockSpec((C_BLOCK, cw2), lambda n: (n, 0)), wsp, bsp, vec, vec],
        out_shape=[jax.ShapeDtypeStruct((t, cw2), bf16), jax.ShapeDtypeStruct(w_s.shape, f32),
                   jax.ShapeDtypeStruct(bs_t.shape, f32), jax.ShapeDtypeStruct((1, cw), f32), jax.ShapeDtypeStruct((1, cw), f32)],
        scratch_shapes=[pltpu.VMEM((C_BLOCK, cw), f32)],
        compiler_params=_params("arbitrary"), name="sgu_bwd")(a, dm, ln_g.reshape(1, cw), ln_b.reshape(1, cw), w_s, bs_t)


HBM = pl.BlockSpec(memory_space=pltpu.HBM)
SEM = pl.BlockSpec(memory_space=pltpu.SEMAPHORE)
EFFECT = pltpu.SideEffectType.DATAFLOW_SIDE_EFFECTING


def _place():
    x, y, c = lax.axis_index("x"), lax.axis_index("y"), lax.axis_index("c")
    return x, y, c, [(1 - x, y), (x, 1 - y), (1 - x, 1 - y)]


def _remote(src, dst, send_sems, recv_sems, k, to):
    return pltpu.make_async_remote_copy(src_ref=src, dst_ref=dst, send_sem=send_sems.at[k], recv_sem=recv_sems.at[k],
                                        device_id=to, device_id_type=MESH)


def _split_start(name, arrays, plan, n_copies, after):
    n = len(arrays)

    def body(*refs):
        send_sems, recv_sems, token = refs[n + 1], refs[n + 2], refs[-1]
        for cp in plan(refs[:n], send_sems, recv_sems):
            cp.start()
        token[...] = jnp.zeros_like(token)

    out = pl.pallas_call(
        body, name=name,
        out_shape=(pltpu.SemaphoreType.DMA((n_copies,)), pltpu.SemaphoreType.DMA((n_copies,)),
                   *[pltpu.HBM(a.shape, a.dtype) for a in arrays], jax.ShapeDtypeStruct((8, LANE), f32)),
        in_specs=[HBM] * n + [ANY], out_specs=(SEM, SEM, *[HBM] * n, pl.BlockSpec(memory_space=pltpu.VMEM)),
        input_output_aliases={i: 2 + i for i in range(n)},
        compiler_params=pltpu.CompilerParams(has_side_effects=EFFECT),
    )(*[pltpu.with_memory_space_constraint(a, pltpu.HBM) for a in arrays], after)
    return (out[0], out[1]), list(out[2:2 + n]), out[-1]


def _split_wait(name, arrays, sems, plan, after):
    n = len(arrays)

    def body(*refs):
        for cp in plan(refs[:n], refs[n], refs[n + 1]):
            cp.wait()

    out = pl.pallas_call(
        body, name=name, out_shape=tuple(pltpu.HBM(a.shape, a.dtype) for a in arrays),
        in_specs=[HBM] * n + [SEM, SEM, ANY], out_specs=tuple([HBM] * n), input_output_aliases={i: i for i in range(n)},
        compiler_params=pltpu.CompilerParams(has_side_effects=EFFECT),
    )(*arrays, sems[0], sems[1], after)
    return list(out)


def _row_pieces(ref_rows, split):
    rc = ref_rows // split
    return [pl.ds(s * rc, rc) for s in range(split)]


def _gather_slots():
    x, y, c, _ = _place()
    slots = (4 * x + 2 * y + c, 4 * (1 - x) + 2 * y + c, 4 * x + 2 * (1 - y) + c, 4 * (1 - x) + 2 * (1 - y) + c)
    return slots, (x, y, 1 - c), (1 - x, y, c), (x, 1 - y, c)


def _to_sibling(cps, k, b, slot, sibling, send_sems, recv_sems):
    for rows in _row_pieces(b.shape[1], D2D_SPLIT):
        cps.append(_remote(b.at[slot, rows], b.at[slot, rows], send_sems, recv_sems, k, sibling))
        k += 1
    return k


def _plan_gather_near(bufs, send_sems, recv_sems):
    (me, _, _, _), sibling, x_peer, y_peer = _gather_slots()
    cps, k = [], 0
    for b in bufs:
        k = _to_sibling(cps, k, b, me, sibling, send_sems, recv_sems)
        for peer in (x_peer, y_peer):
            cps.append(_remote(b.at[me], b.at[me], send_sems, recv_sems, k, peer))
            k += 1
    return cps


def _plan_gather_relay(bufs, send_sems, recv_sems):
    (_, x_slot, y_slot, _), sibling, x_peer, y_peer = _gather_slots()
    cps, k = [], 0
    for b in bufs:
        half = b.shape[1] // 2
        k = _to_sibling(cps, k, b, x_slot, sibling, send_sems, recv_sems)
        k = _to_sibling(cps, k, b, y_slot, sibling, send_sems, recv_sems)
        lower, upper = pl.ds(0, half), pl.ds(half, half)
        cps.append(_remote(b.at[x_slot, lower], b.at[x_slot, lower], send_sems, recv_sems, k, y_peer))
        cps.append(_remote(b.at[y_slot, upper], b.at[y_slot, upper], send_sems, recv_sems, k + 1, x_peer))
        k += 2
    return cps


def _plan_gather_far(bufs, send_sems, recv_sems):
    (_, _, _, far), sibling, _, _ = _gather_slots()
    cps, k = [], 0
    for b in bufs:
        k = _to_sibling(cps, k, b, far, sibling, send_sems, recv_sems)
    return cps


def _plan_rs_sibling(arrs, send_sems, recv_sems):
    n = len(arrs) // 2
    x, y, c, _ = _place()
    cps, k = [], 0
    for g, got in zip(arrs[:n], arrs[n:]):
        for q in range(N_CHIP):
            for rows in _row_pieces(g.shape[1], RS_SPLIT):
                cps.append(_remote(g.at[N_CHIP * (1 - c) + q, rows], got.at[q, rows], send_sems, recv_sems, k, (x, y, 1 - c)))
                k += 1
    return cps


def _plan_rs_chips(arrs, send_sems, recv_sems):
    n = len(arrs) // 2
    x, y, c, chips = _place()
    q = 2 * x + y
    cps, k = [], 0
    for p, r in zip(arrs[:n], arrs[n:]):
        for px, py in chips:
            cps.append(_remote(p.at[2 * px + py], r.at[q], send_sems, recv_sems, k, (px, py, c)))
            k += 1
    return cps


class _Gather:
    STAGES = (("near", _plan_gather_near, D2D_SPLIT + 2), ("relay", _plan_gather_relay, 2 * D2D_SPLIT + 2),
              ("far", _plan_gather_far, D2D_SPLIT))

    def __init__(self, tag, bufs):
        self.tag, self.bufs, self.stage = tag, bufs, -1

    def advance(self, after):
        if self.stage >= 0:
            name, plan, _ = self.STAGES[self.stage]
            self.bufs = _split_wait("gather_%s_wait_%s" % (name, self.tag), self.bufs, self.sems, plan, after)
        self.stage += 1
        if self.stage == len(self.STAGES):
            return self.bufs
        name, plan, per_array = self.STAGES[self.stage]
        self.sems, self.bufs, token = _split_start("gather_%s_start_%s" % (name, self.tag), self.bufs, plan,
                                                   per_array * len(self.bufs), after)
        return token

    def finish(self, after):
        out = self.advance(after)
        while not isinstance(out, list):
            out = self.advance(after)
        return out


class _ReduceScatter:
    def __init__(self, tag, grads, core):
        self.tag, self.n = tag, len(grads)
        lands = [lax.empty((N_CHIP,) + g.shape[1:], g.dtype) for g in grads]
        self.sems, self.arrs, self.token = _split_start("rs_sibling_start_" + tag, list(grads) + lands, _plan_rs_sibling,
                                                        self.n * N_CHIP * RS_SPLIT, core)

    def middle(self, after, core):
        arrs = _split_wait("rs_sibling_wait_" + self.tag, self.arrs, self.sems, _plan_rs_sibling, after)
        parts = [_pair_sum(g, got, core) for g, got in zip(arrs[:self.n], arrs[self.n:])]
        lands = [lax.empty(p.shape, p.dtype) for p in parts]
        self.sems, self.arrs, self.token = _split_start("rs_chips_start_" + self.tag, parts + lands, _plan_rs_chips,
                                                        self.n * 3, core)

    def finish(self, after):
        arrs = _split_wait("rs_chips_wait_" + self.tag, self.arrs, self.sems, _plan_rs_chips, after)
        return list(zip(arrs[:self.n], arrs[self.n:]))


def _cast_into_slot(name, w, layer, me, after):
    _, rows, cols = w.shape
    tr = 256 if rows % 256 == 0 else rows

    def body(me_ref, w_ref, after_ref, o_ref):
        o_ref[...] = w_ref[...].astype(bf16)

    return pl.pallas_call(
        body,
        grid_spec=pltpu.PrefetchScalarGridSpec(
            num_scalar_prefetch=1, grid=(rows // tr,),
            in_specs=[pl.BlockSpec((None, tr, cols), lambda i, me_ref: (layer, i, 0)), ANY],
            out_specs=pl.BlockSpec((None, tr, cols), lambda i, me_ref: (me_ref[0], i, 0))),
        out_shape=jax.ShapeDtypeStruct((N_DEV, rows, cols), bf16), compiler_params=_params("parallel"), name=name)(me, w, after)


def _pair_sum(g, got, core):
    _, rows, cols = g.shape
    tr = 512 if rows % 512 == 0 else rows

    def body(c_ref, a_ref, b_ref, o_ref):
        o_ref[...] = (a_ref[...].astype(f32) + b_ref[...].astype(f32)).astype(bf16)

    spec = pl.BlockSpec((None, tr, cols), lambda q, i, c_ref: (q, i, 0))
    return pl.pallas_call(
        body,
        grid_spec=pltpu.PrefetchScalarGridSpec(
            num_scalar_prefetch=1, grid=(N_CHIP, rows // tr),
            in_specs=[pl.BlockSpec((None, tr, cols), lambda q, i, c_ref: (N_CHIP * c_ref[0] + q, i, 0)), spec],
            out_specs=spec),
        out_shape=jax.ShapeDtypeStruct((N_CHIP, rows, cols), bf16), compiler_params=_params("parallel", "parallel"),
        name="pair_sum")(core, g, got)


def _gather_copies(n, ins, outs, send_sems, recv_sems, local_sems):
    x, y, c, chips = _place()
    sibling = (x, y, 1 - c)

    def slot(px, py, pc):
        return 4 * px + 2 * py + pc

    def copy(i, k, block, to, src=None):
        dst = outs[i].at[slot(*block)]
        return pltpu.make_async_remote_copy(src_ref=dst if src is None else src, dst_ref=dst, send_sem=send_sems.at[i, k],
                                            recv_sem=recv_sems.at[i, k], device_id=to, device_id_type=MESH)

    started = []
    for i in range(n):
        mine = pltpu.make_async_copy(ins[i], outs[i].at[slot(x, y, c)], local_sems.at[i])
        mine.start()
        started.append(mine)
    sends = []
    for i in range(n):
        sends.append(copy(i, 0, (x, y, c), sibling, src=ins[i]))
        sends += [copy(i, 1 + j, (x, y, c), (*chip, c), src=ins[i]) for j, chip in enumerate(chips)]
    for cp in sends:
        cp.start()
    for i in range(n):
        for j, chip in enumerate(chips):
            copy(i, 1 + j, (*chip, c), (x, y, c)).wait_recv()
            fwd = copy(i, 4 + j, (*chip, c), sibling)
            fwd.start()
            sends.append(fwd)
    for i in range(n):
        copy(i, 0, sibling, (x, y, c)).wait_recv()
        for j, chip in enumerate(chips):
            copy(i, 4 + j, (*chip, 1 - c), (x, y, c)).wait_recv()
    for cp in sends:
        cp.wait_send()
    for mine in started:
        mine.wait()


def _gather_small(name, packed, reduce):
    rows = packed.shape[0]

    def body(x_ref, o_ref, buf, send_sems, recv_sems, local_sems):
        _gather_copies(1, [x_ref], [buf], send_sems, recv_sems, local_sems)
        if reduce:
            acc = buf[0]
            for j in range(1, N_DEV):
                acc = acc + buf[j]
            o_ref[...] = acc
        else:
            o_ref[...] = buf[...]

    vm = pl.BlockSpec(memory_space=pltpu.VMEM)
    return pl.pallas_call(
        body, in_specs=[vm], out_specs=vm,
        out_shape=jax.ShapeDtypeStruct((rows, LANE) if reduce else (N_DEV, rows, LANE), f32),
        scratch_shapes=[pltpu.VMEM((N_DEV, rows, LANE), f32), pltpu.SemaphoreType.DMA((1, 7)), pltpu.SemaphoreType.DMA((1, 7)),
                        pltpu.SemaphoreType.DMA((1,))],
        compiler_params=pltpu.CompilerParams(vmem_limit_bytes=VMEM_LIMIT), name=name)(packed)


def _pack(arrs):
    flat = jnp.concatenate([a.reshape(-1).astype(f32) for a in arrs])
    rows = -(-flat.shape[0] // (8 * LANE)) * 8
    return jnp.pad(flat, (0, rows * LANE - flat.shape[0])).reshape(rows, LANE)


def _unpack(buf, shapes):
    flat = buf.reshape(-1)
    out, off = [], 0
    for s in shapes:
        n = int(np.prod(s))
        out.append(flat[off:off + n].reshape(s))
        off += n
    return out


def _adam_math(w, g, m, v):
    m2 = ADAM_B1 * m + (1.0 - ADAM_B1) * g
    v2 = ADAM_B2 * v + (1.0 - ADAM_B2) * (g * g)
    m_hat = m2 / (1.0 - ADAM_B1 ** ADAM_STEP)
    v_hat = v2 / (1.0 - ADAM_B2 ** ADAM_STEP)
    delta = -ADAM_LR * (m_hat / (jnp.sqrt(v_hat) + ADAM_EPS) + ADAM_WD * w)
    return delta, m2, v2


def _adam_big(name, w, m, v, parts, chip):
    layers, rows, cols = w.shape
    tr = 512 if rows % 512 == 0 else 256 if rows % 256 == 0 else rows // 4 if rows % 32 == 0 else 8

    def body(chip_ref, w_ref, m_ref, v_ref, *rest):
        p_refs = rest[:N_CHIP * layers]
        g_ref, d_ref, m2_ref, v2_ref = rest[N_CHIP * layers:]
        for li in range(layers):
            @pl.when(pl.program_id(0) == li)
            def _(li=li):
                g = p_refs[N_CHIP * li][...].astype(f32)
                for q in range(1, N_CHIP):
                    g = g + p_refs[N_CHIP * li + q][...].astype(f32)
                delta, m2, v2 = _adam_math(w_ref[...], g, m_ref[...], v_ref[...])
                g_ref[...] = g
                d_ref[...] = delta
                m2_ref[...] = m2
                v2_ref[...] = v2

    spec = pl.BlockSpec((None, tr, cols), lambda l, i, c_ref: (l, i, 0))
    pspecs, operands = [], []
    for li in range(layers):
        for q in range(N_CHIP):
            pspecs.append(pl.BlockSpec((None, tr, cols),
                                       lambda l, i, c_ref, li=li, q=q: ((c_ref[0] + q) % N_CHIP, jnp.where(l == li, i, 0), 0)))
            operands.append(parts[li][0] if q == 0 else parts[li][1])
    out = jax.ShapeDtypeStruct((layers, rows, cols), f32)
    return pl.pallas_call(
        body,
        grid_spec=pltpu.PrefetchScalarGridSpec(num_scalar_prefetch=1, grid=(layers, rows // tr),
                                               in_specs=[spec, spec, spec] + pspecs, out_specs=[spec] * 4),
        out_shape=[out] * 4, compiler_params=_params("arbitrary", "arbitrary"), name=name)(chip, w, m, v, *operands)


def _adam_small(w, g, m, v):
    rows = w.shape[0]

    def body(w_ref, g_ref, m_ref, v_ref, d_ref, m2_ref, v2_ref):
        delta, m2, v2 = _adam_math(w_ref[...], g_ref[...], m_ref[...], v_ref[...])
        d_ref[...] = delta
        m2_ref[...] = m2
        v2_ref[...] = v2

    out = jax.ShapeDtypeStruct((rows, LANE), f32)
    return pl.pallas_call(body, out_shape=[out] * 3, name="adam_small")(w, g, m, v)


def kernel(x, mix_norm, ab_w_in, ab_rel_bias, ab_conv_w, ab_w_out, c_w_in, c_ln_g, c_ln_b, c_w_s, c_b_s, c_w_out, ffn_norm, ffn_w_gate, ffn_w_up, ffn_w_down, final_norm, loss_target, m_mix_norm, m_ab_w_in, m_ab_rel_bias, m_ab_conv_w, m_ab_w_out, m_c_w_in, m_c_ln_g, m_c_ln_b, m_c_w_s, m_c_b_s, m_c_w_out, m_ffn_norm, m_ffn_w_gate, m_ffn_w_up, m_ffn_w_down, m_final_norm, v_mix_norm, v_ab_w_in, v_ab_rel_bias, v_ab_conv_w, v_ab_w_out, v_c_w_in, v_c_ln_g, v_c_ln_b, v_c_w_s, v_c_b_s, v_c_w_out, v_ffn_norm, v_ffn_w_gate, v_ffn_w_up, v_ffn_w_down, v_final_norm):
    d = D_MODEL
    a_width = d // 2
    heads = a_width // A_HEAD_DIM
    a_blocks = a_width // LANE
    b_blocks = (d - a_width) // LANE
    n_even, n_odd = (DEPTH + 1) // 2, DEPTH // 2
    me_s = 4 * lax.axis_index("x") + 2 * lax.axis_index("y") + lax.axis_index("c")
    me = me_s.astype(jnp.int32).reshape(1)
    core = lax.axis_index("c").astype(jnp.int32).reshape(1)
    chip = (2 * lax.axis_index("x") + lax.axis_index("y")).astype(jnp.int32).reshape(1)

    weights = dict(mix_norm=mix_norm, ab_w_in=ab_w_in, ab_rel_bias=ab_rel_bias, ab_conv_w=ab_conv_w, ab_w_out=ab_w_out,
                   c_w_in=c_w_in, c_ln_g=c_ln_g, c_ln_b=c_ln_b, c_w_s=c_w_s, c_b_s=c_b_s, c_w_out=c_w_out,
                   ffn_norm=ffn_norm, ffn_w_gate=ffn_w_gate, ffn_w_up=ffn_w_up, ffn_w_down=ffn_w_down, final_norm=final_norm)
    mom_m = dict(mix_norm=m_mix_norm, ab_w_in=m_ab_w_in, ab_rel_bias=m_ab_rel_bias, ab_conv_w=m_ab_conv_w, ab_w_out=m_ab_w_out,
                 c_w_in=m_c_w_in, c_ln_g=m_c_ln_g, c_ln_b=m_c_ln_b, c_w_s=m_c_w_s, c_b_s=m_c_b_s, c_w_out=m_c_w_out,
                 ffn_norm=m_ffn_norm, ffn_w_gate=m_ffn_w_gate, ffn_w_up=m_ffn_w_up, ffn_w_down=m_ffn_w_down, final_norm=m_final_norm)
    mom_v = dict(mix_norm=v_mix_norm, ab_w_in=v_ab_w_in, ab_rel_bias=v_ab_rel_bias, ab_conv_w=v_ab_conv_w, ab_w_out=v_ab_w_out,
                 c_w_in=v_c_w_in, c_ln_g=v_c_ln_g, c_ln_b=v_c_ln_b, c_w_s=v_c_w_s, c_b_s=v_c_b_s, c_w_out=v_c_w_out,
                 ffn_norm=v_ffn_norm, ffn_w_gate=v_ffn_w_gate, ffn_w_up=v_ffn_w_up, ffn_w_down=v_ffn_w_down, final_norm=v_final_norm)
    order = list(weights)
    wide = ("ffn_w_gate", "ffn_w_up")
    flip = lambda a: jnp.swapaxes(a, 1, 2)
    local = {k: (flip(weights[k]), flip(mom_m[k]), flip(mom_v[k])) if k in wide else (weights[k], mom_m[k], mom_v[k]) for k in order}

    sharded_small = [ab_conv_w, c_ln_g, c_ln_b]
    gathered = _gather_small("gather_small", _pack(sharded_small), reduce=False)
    conv_parts, lng_parts, lnb_parts = [], [], []
    for j in range(N_DEV):
        cw_j, lg_j, lb_j = _unpack(gathered[j], [a.shape for a in sharded_small])
        conv_parts.append(cw_j)
        lng_parts.append(lg_j)
        lnb_parts.append(lb_j)
    conv_full = jnp.concatenate(conv_parts, axis=-1)
    lng_full = jnp.concatenate(lng_parts, axis=-1)
    lnb_full = jnp.concatenate(lnb_parts, axis=-1)

    gate_t, up_t = local["ffn_w_gate"][0], local["ffn_w_up"][0]
    sets = []
    for layer in range(DEPTH):
        i = layer // 2
        if layer % 2 == 0:
            sets += [("ab_in%d" % i, [(ab_w_in, i)]), ("ab_out%d" % i, [(ab_w_out, i)])]
        else:
            sets += [("c_in%d" % i, [(c_w_in, i)]), ("c_out%d" % i, [(c_w_out, i)])]
        sets += [("ffn_in%d" % layer, [(gate_t, layer), (up_t, layer)]), ("ffn_out%d" % layer, [(ffn_w_down, layer)])]
    units = [None] * len(sets)
    cursor = [0]
    tokens = []

    def start_gather(k, after):
        if k < len(sets):
            tag, members = sets[k]
            units[k] = _Gather(tag, [_cast_into_slot("cast_slot", w, li, me, after) for w, li in members])
            tokens.append(units[k].advance(after))

    def next_weights(after):
        k = cursor[0]
        cursor[0] = k + 1
        ready = units[k].finish(after)
        for later in range(k + 1, min(k + len(_Gather.STAGES), len(sets)) if k else 2):
            tokens.append(units[later].advance(after))
        start_gather(k + GATHER_AHEAD, after)
        return ready

    def started():
        deps = list(tokens)
        tokens.clear()
        return deps

    xs = x[0]
    tgt = loss_target[0]
    start_gather(0, gathered)
    start_gather(1, gathered)
    tokens.append(units[0].advance(tokens[-1]))
    for k in range(2, GATHER_AHEAD):
        start_gather(k, tokens[-1])
    tokens.append(units[0].advance(tokens[-1]))
    saved = []
    h = _rms_fwd(xs, mix_norm[0])
    for layer in range(DEPTH):
        i = layer // 2
        (w_in_g,) = next_weights(xs)
        if layer % 2 == 0:
            proj = _mm_cols("ab_proj", h, w_in_g, bf16, started())
            btab = _bias_table(ab_rel_bias[i])
            attn = _attn_fwd(proj, btab, heads)
            conv = _conv_fwd(proj, conv_full[i], a_blocks, b_blocks)
            mixed = jnp.concatenate([attn, conv], axis=-1)
            ctx = (proj, btab)
        else:
            proj = _mm_cols("c_proj", h, w_in_g, f32, started())
            bs_t = jnp.transpose(c_b_s[i])
            mixed = _sgu_fwd(proj, lng_full[i], lnb_full[i], c_w_s[i], bs_t)
            ctx = (proj, bs_t)
        (w_out_g,) = next_weights(mixed)
        w_out_full = w_out_g.reshape(-1, w_out_g.shape[-1])
        x1, h2 = _mm_rows_res("mix_out", mixed, w_out_full, xs, ffn_norm[layer], started())
        wg_g, wu_g = next_weights(x1)
        g_act, u_act, act = _ffn_in(h2, wg_g, wu_g, started())
        (wd_g,) = next_weights(g_act)
        nxt = _ffn_down(act, wd_g, x1, mix_norm[layer + 1] if layer + 1 < DEPTH else None, started())
        saved.append((xs, h, ctx, mixed, x1, h2, g_act, u_act, w_in_g, w_out_full, wg_g, wu_g, wd_g))
        xs, h = nxt if len(nxt) == 2 else (nxt[0], None)

    loss_part, dx, dxb, d_final = _loss_head(xs, final_norm, tgt)
    loss = lax.psum(loss_part[0, 0], ("x", "y", "c"))

    scatters = {}
    small = {k: [None] * weights[k].shape[0] for k in ("mix_norm", "ffn_norm", "ab_rel_bias", "ab_conv_w", "c_ln_g", "c_ln_b",
                                                       "c_w_s", "c_b_s")}
    for layer in reversed(range(DEPTH)):
        i = layer // 2
        xs, h, ctx, mixed, x1, h2, g_act, u_act, w_in_g, w_out_full, wg_g, wu_g, wd_g = saved[layer]
        dg, du, act = _ffn_bwd_act(dxb, wd_g, g_act, u_act, started())
        dwd = _ffn_dwd(act, dxb)
        dwg, dwu = _ffn_dwgu(h2, dg, du)
        rs_ffn = _ReduceScatter("ffn%d" % layer, [dwg, dwu, dwd], core)
        dh2 = _ffn_dh(dg, du, wg_g, wu_g, [rs_ffn.token])
        dx, dxb, dgn = _rms_bwd(x1, ffn_norm[layer], dh2, dx)
        small["ffn_norm"][layer] = dgn[0]
        rs_ffn.middle(dxb, core)
        for pos, k in enumerate(("ffn_w_gate", "ffn_w_up", "ffn_w_down")):
            scatters[(k, layer)] = (rs_ffn, pos)
        dmixed = _mm_nt("mix_out_bwd", dxb, w_out_full, bf16, [rs_ffn.token])
        dwout = _mm_tn_rows("mix_out_dw", mixed, dxb)
        if layer % 2 == 0:
            proj, btab = ctx
            dq, dk, dv, dtab = _attn_bwd(proj, dmixed, btab, heads)
            db, dc, dhv, dcw = _conv_bwd(proj, dmixed, conv_full[i], a_blocks, b_blocks)
            dproj = jnp.concatenate([dq, dk, dv, db, dc, dhv], axis=-1)
            small["ab_rel_bias"][i] = _bias_table_grad(dtab)
            small["ab_conv_w"][i] = dcw
            names = ("ab_w_in", "ab_w_out")
            tag = "ab"
        else:
            proj, bs_t = ctx
            dproj, dws, dbs_t, dlg, dlb = _sgu_bwd(proj, dmixed, lng_full[i], lnb_full[i], c_w_s[i], bs_t)
            small["c_w_s"][i] = dws
            small["c_b_s"][i] = jnp.transpose(dbs_t)
            small["c_ln_g"][i] = dlg[0]
            small["c_ln_b"][i] = dlb[0]
            names = ("c_w_in", "c_w_out")
            tag = "c"
        dwin = _mm_tn_cols(tag + "_proj_dw", h, dproj)
        rs_mix = _ReduceScatter("%s%d" % (tag, i), [dwin, dwout], core)
        dh = _mm_nt_cols(tag + "_proj_bwd", dproj, w_in_g, [rs_mix.token])
        dx, dxb, dgm = _rms_bwd(xs, mix_norm[layer], dh, dx)
        small["mix_norm"][layer] = dgm[0]
        rs_mix.middle(dxb, core)
        tokens.append(rs_mix.token)
        scatters[(names[0], i)] = (rs_mix, 0)
        scatters[(names[1], i)] = (rs_mix, 1)
    grad_x = dx[None]

    small_names = ["mix_norm", "ffn_norm", "ab_rel_bias", "ab_conv_w", "c_ln_g", "c_ln_b", "c_w_s", "c_b_s"]
    small_full = [jnp.stack(small[k]) for k in small_names] + [d_final[0]]
    summed = _unpack(_gather_small("reduce_small", _pack(small_full), reduce=True), [a.shape for a in small_full])
    small_grads = dict(zip(small_names + ["final_norm"], summed))
    for k in ("ab_conv_w", "c_ln_g", "c_ln_b"):
        width = weights[k].shape[-1]
        small_grads[k] = lax.dynamic_slice_in_dim(small_grads[k], me_s * width, width, axis=-1)
    small_order = [k for k in order if k in small_grads]
    shapes = [weights[k].shape for k in small_order]
    d_s, m_s, v_s = _adam_small(_pack([weights[k] for k in small_order]), _pack([small_grads[k] for k in small_order]),
                                _pack([mom_m[k] for k in small_order]), _pack([mom_v[k] for k in small_order]))
    grads, deltas, new_m, new_v = dict(small_grads), {}, {}, {}
    for k, dd, mm, vv in zip(small_order, _unpack(d_s, shapes), _unpack(m_s, shapes), _unpack(v_s, shapes)):
        deltas[k], new_m[k], new_v[k] = dd, mm, vv

    finished = {}
    last = started()[-1]
    for k in ("c_w_in", "c_w_out", "ffn_w_gate", "ffn_w_up", "ffn_w_down", "ab_w_in", "ab_w_out"):
        parts = []
        w_k, m_k, v_k = local[k]
        for li in range(w_k.shape[0]):
            rs, pos = scatters[(k, li)]
            if id(rs) not in finished:
                finished[id(rs)] = rs.finish(last)
            parts.append(finished[id(rs)][pos])
        outs = _adam_big("adam_" + k, w_k, m_k, v_k, parts, chip)
        last = outs[1]
        grads[k], deltas[k], new_m[k], new_v[k] = [flip(o) for o in outs] if k in wide else outs

    return (loss, grad_x, *[grads[k] for k in order], *[deltas[k] for k in order], *[new_m[k] for k in order],
            *[new_v[k] for k in order])
```

```python
import numpy as np
import jax
import jax.numpy as jnp
from jax import lax
from jax.experimental import pallas as pl
from jax.experimental.pallas import tpu as pltpu

D_MODEL = 2048
SEQ = 2048
DEPTH = 4
CHUNK = 64
A_HEAD_DIM = 128
A_LEFT_CHUNKS = 8
A_MAX_REL = 256
CONV_WIDTH = 3
C_BLOCK = 128
C_GROUPS = 8
EPS = 1e-6
NEG_INF = -1e30

ADAM_LR = 0.001
ADAM_B1 = 0.9
ADAM_B2 = 0.999
ADAM_EPS = 1e-08
ADAM_WD = 0.01
ADAM_STEP = 10

N_DEV = 8
N_CHIP = 4
RS_SPLIT = 4
D2D_SPLIT = 2
GATHER_AHEAD = 4
ROWS_PER_STEP = 1024
COL_CHUNK = 256
LANE = 128
VMEM_LIMIT = 52 * 1024 * 1024

bf16 = jnp.bfloat16
f32 = jnp.float32
MESH = pl.DeviceIdType.MESH
ANY = pl.BlockSpec(memory_space=pl.ANY)


def _params(*sem):
    return pltpu.CompilerParams(dimension_semantics=sem, vmem_limit_bytes=VMEM_LIMIT)


def _perm(j):
    return (j % 2) * N_CHIP + j // 2


_DN = {"nn": (((1,), (0,)), ((), ())), "nt": (((1,), (1,)), ((), ())), "tn": (((0,), (0,)), ((), ()))}


def _matmul(name, mode, grid, operands, specs, pairs, n_acc, acc_shape, extras, extra_specs, out_shapes, out_specs,
            epilogue, chunk=0):
    nk = grid[2]
    n_op, n_ex, n_out = len(operands), len(extras), len(out_shapes)

    def single(*refs):
        ops = refs[:n_op]
        ex = refs[n_op:n_op + n_ex]
        outs = refs[n_op + n_ex:]
        width = acc_shape[1]
        starts = range(0, width, chunk) if chunk else (0,)
        for c0 in starts:
            cols = slice(c0, min(c0 + chunk, width)) if chunk else slice(None)
            sums = [None] * n_acc
            for p, (ia, ib) in enumerate(pairs):
                b = ops[ib][cols, :] if mode == "nt" else ops[ib][:, cols]
                d = lax.dot_general(ops[ia][...], b, _DN[mode], preferred_element_type=f32)
                sums[p % n_acc] = d if sums[p % n_acc] is None else sums[p % n_acc] + d
            epilogue(sums, ex, outs, cols)

    def body(*refs):
        ops = refs[:n_op]
        ex = refs[n_op:n_op + n_ex]
        outs = refs[n_op + n_ex:n_op + n_ex + n_out]
        accs = refs[n_op + n_ex + n_out:]
        k = pl.program_id(2)

        @pl.when(k == 0)
        def _():
            for acc in accs:
                acc[...] = jnp.zeros_like(acc)

        for p, (ia, ib) in enumerate(pairs):
            acc = accs[p % n_acc]
            acc[...] += lax.dot_general(ops[ia][...], ops[ib][...], _DN[mode], preferred_element_type=f32)

        @pl.when(k == nk - 1)
        def _():
            epilogue([acc[...] for acc in accs], ex, outs, slice(None))

    return pl.pallas_call(
        single if nk == 1 else body, grid=grid, in_specs=list(specs) + list(extra_specs), out_specs=list(out_specs),
        out_shape=list(out_shapes), scratch_shapes=[] if nk == 1 else [pltpu.VMEM(acc_shape, f32)] * n_acc,
        compiler_params=_params("parallel", "parallel", "arbitrary"), name=name)(*operands, *extras)


def _store(dtype):
    def ep(accs, ex, outs, cols):
        for a, o in zip(accs, outs):
            o[:, cols] = a.astype(dtype)
    return ep


def _mm_cols(name, h, wg, out_dtype, deps=()):
    t, kd = h.shape
    n8 = wg.shape[2]
    tm = min(t, 2 * ROWS_PER_STEP)
    return _matmul(
        name, "nn", (t // tm, N_DEV, 1), [h, wg],
        [pl.BlockSpec((tm, kd), lambda i, j, k: (i, 0)), pl.BlockSpec((None, kd, n8), lambda i, j, k: (j, 0, 0))],
        [(0, 1)], 1, (tm, n8), list(deps), [ANY] * len(deps), [jax.ShapeDtypeStruct((t, N_DEV * n8), out_dtype)],
        [pl.BlockSpec((tm, n8), lambda i, j, k: (i, j))], _store(out_dtype), COL_CHUNK)[0]


def _norm_rows(xv, gain):
    return (xv * lax.rsqrt(jnp.mean(xv * xv, axis=-1, keepdims=True) + EPS) * gain).astype(bf16)


def _mm_rows_res(name, a, w, res, gain, deps=()):
    t, kd = a.shape
    n = w.shape[1]
    tm = min(t, 512)

    def ep(accs, ex, outs, cols):
        xv = ex[0][...] + accs[0]
        outs[0][...] = xv
        outs[1][...] = _norm_rows(xv, ex[1][...])

    row = pl.BlockSpec((tm, n), lambda i, j, k: (i, 0))
    return _matmul(
        name, "nn", (t // tm, 1, 1), [a, w],
        [pl.BlockSpec((tm, kd), lambda i, j, k: (i, 0)), pl.BlockSpec((kd, n), lambda i, j, k: (0, 0))],
        [(0, 1)], 1, (tm, n), [res, gain.reshape(1, n)] + list(deps),
        [row, pl.BlockSpec((1, n), lambda i, j, k: (0, 0))] + [ANY] * len(deps),
        [jax.ShapeDtypeStruct((t, n), f32), jax.ShapeDtypeStruct((t, n), bf16)], [row, row], ep)


def _mm_nt(name, a, w, out_dtype, deps=()):
    t, n = a.shape
    kd = w.shape[0]
    tm, tn = min(t, 2 * ROWS_PER_STEP), min(kd, 1024)
    return _matmul(
        name, "nt", (t // tm, kd // tn, 1), [a, w],
        [pl.BlockSpec((tm, n), lambda i, j, k: (i, 0)), pl.BlockSpec((tn, n), lambda i, j, k: (j, 0))],
        [(0, 1)], 1, (tm, tn), list(deps), [ANY] * len(deps), [jax.ShapeDtypeStruct((t, kd), out_dtype)],
        [pl.BlockSpec((tm, tn), lambda i, j, k: (i, j))], _store(out_dtype), COL_CHUNK)[0]


def _mm_nt_cols(name, da, wg, deps=()):
    t = da.shape[0]
    kd, n8 = wg.shape[1], wg.shape[2]
    tm = min(t, ROWS_PER_STEP)
    return _matmul(
        name, "nt", (t // tm, 1, N_DEV), [da, wg],
        [pl.BlockSpec((tm, n8), lambda i, j, k: (i, k)), pl.BlockSpec((None, kd, n8), lambda i, j, k: (k, 0, 0))],
        [(0, 1)], 1, (tm, kd), list(deps), [ANY] * len(deps), [jax.ShapeDtypeStruct((t, kd), f32)],
        [pl.BlockSpec((tm, kd), lambda i, j, k: (i, 0))], _store(f32))[0]


def _mm_tn_cols(name, h, da):
    t, kd = h.shape
    n8 = da.shape[1] // N_DEV
    tmk, tk = min(kd, 1024), min(t, 2048)
    return _matmul(
        name, "tn", (kd // tmk, N_DEV, t // tk), [h, da],
        [pl.BlockSpec((tk, tmk), lambda i, j, k: (k, i)), pl.BlockSpec((tk, n8), lambda i, j, k: (k, j))],
        [(0, 1)], 1, (tmk, n8), [], [], [jax.ShapeDtypeStruct((N_DEV, kd, n8), bf16)],
        [pl.BlockSpec((None, tmk, n8), lambda i, j, k: (_perm(j), i, 0))], _store(bf16))[0]


def _mm_tn_rows(name, a, dx):
    t, kf = a.shape
    r8 = kf // N_DEV
    n = dx.shape[1]
    return _matmul(
        name, "tn", (N_DEV, 1, 1), [a, dx],
        [pl.BlockSpec((t, r8), lambda i, j, k: (0, i)), pl.BlockSpec((t, n), lambda i, j, k: (0, 0))],
        [(0, 1)], 1, (r8, n), [], [], [jax.ShapeDtypeStruct((N_DEV, r8, n), bf16)],
        [pl.BlockSpec((None, r8, n), lambda i, j, k: (_perm(i), 0, 0))], _store(bf16))[0]


def _ffn_in(h2, wg_t, wu_t, deps=()):
    t, kd = h2.shape
    f8 = wg_t.shape[1]
    tm = min(t, ROWS_PER_STEP)

    def ep(accs, ex, outs, cols):
        g, u = accs
        outs[0][:, cols] = g.astype(bf16)
        outs[1][:, cols] = u.astype(bf16)
        outs[2][:, cols] = (g * jax.nn.sigmoid(g) * u).astype(bf16)

    wspec = pl.BlockSpec((None, f8, kd), lambda i, j, k: (j, 0, 0))
    ospec = pl.BlockSpec((None, tm, f8), lambda i, j, k: (j, i, 0))
    return _matmul(
        "ffn_in", "nt", (t // tm, N_DEV, 1), [h2, wg_t, wu_t],
        [pl.BlockSpec((tm, kd), lambda i, j, k: (i, 0)), wspec, wspec], [(0, 1), (0, 2)], 2, (tm, f8), list(deps),
        [ANY] * len(deps), [jax.ShapeDtypeStruct((N_DEV, t, f8), bf16)] * 3, [ospec] * 3, ep, COL_CHUNK)


def _ffn_down(act, wd, res, gain, deps=()):
    _, t, f8 = act.shape
    n = wd.shape[2]
    tm = min(t, 512)

    def ep(accs, ex, outs, cols):
        xv = ex[0][...] + accs[0]
        outs[0][...] = xv
        if gain is not None:
            outs[1][...] = _norm_rows(xv, ex[1][...])

    row = pl.BlockSpec((tm, n), lambda i, j, k: (i, 0))
    extras, extra_specs = [res], [row]
    shapes, specs = [jax.ShapeDtypeStruct((t, n), f32)], [row]
    if gain is not None:
        extras.append(gain.reshape(1, n))
        extra_specs.append(pl.BlockSpec((1, n), lambda i, j, k: (0, 0)))
        shapes.append(jax.ShapeDtypeStruct((t, n), bf16))
        specs.append(row)
    return _matmul(
        "ffn_down", "nn", (t // tm, 1, N_DEV), [act, wd],
        [pl.BlockSpec((None, tm, f8), lambda i, j, k: (k, i, 0)), pl.BlockSpec((None, f8, n), lambda i, j, k: (k, 0, 0))],
        [(0, 1)], 1, (tm, n), extras + list(deps), extra_specs + [ANY] * len(deps), shapes, specs, ep)


def _ffn_bwd_act(dxb, wd, g, u, deps=()):
    t, n = dxb.shape
    f8 = wd.shape[1]
    tm = min(t, ROWS_PER_STEP)

    def ep(accs, ex, outs, cols):
        dact = accs[0]
        gv = ex[0][:, cols].astype(f32)
        uv = ex[1][:, cols].astype(f32)
        sg = jax.nn.sigmoid(gv)
        silu = gv * sg
        outs[0][:, cols] = (dact * uv * (sg * (1.0 + gv * (1.0 - sg)))).astype(bf16)
        outs[1][:, cols] = (dact * silu).astype(bf16)
        outs[2][:, cols] = (silu * uv).astype(bf16)

    bspec = pl.BlockSpec((None, tm, f8), lambda i, j, k: (j, i, 0))
    return _matmul(
        "ffn_bwd_act", "nt", (t // tm, N_DEV, 1), [dxb, wd],
        [pl.BlockSpec((tm, n), lambda i, j, k: (i, 0)), pl.BlockSpec((None, f8, n), lambda i, j, k: (j, 0, 0))],
        [(0, 1)], 1, (tm, f8), [g, u] + list(deps), [bspec, bspec] + [ANY] * len(deps),
        [jax.ShapeDtypeStruct((N_DEV, t, f8), bf16)] * 3, [bspec] * 3, ep, COL_CHUNK)


def _ffn_dwd(act, dxb):
    _, t, f8 = act.shape
    n = dxb.shape[1]
    tk = min(t, 2048)
    return _matmul(
        "ffn_dwd", "tn", (N_DEV, 1, t // tk), [act, dxb],
        [pl.BlockSpec((None, tk, f8), lambda i, j, k: (i, k, 0)), pl.BlockSpec((tk, n), lambda i, j, k: (k, 0))],
        [(0, 1)], 1, (f8, n), [], [], [jax.ShapeDtypeStruct((N_DEV, f8, n), bf16)],
        [pl.BlockSpec((None, f8, n), lambda i, j, k: (_perm(i), 0, 0))], _store(bf16))[0]


def _ffn_dwgu(h2, dg, du):
    t, kd = h2.shape
    f8 = dg.shape[2]
    tk, tn = min(t, 2048), min(kd, 1024)
    aspec = pl.BlockSpec((None, tk, f8), lambda i, j, k: (i, k, 0))
    ospec = pl.BlockSpec((None, f8, tn), lambda i, j, k: (_perm(i), 0, j))
    return _matmul(
        "ffn_dwgu", "tn", (N_DEV, kd // tn, t // tk), [dg, du, h2],
        [aspec, aspec, pl.BlockSpec((tk, tn), lambda i, j, k: (k, j))], [(0, 2), (1, 2)], 2, (f8, tn), [], [],
        [jax.ShapeDtypeStruct((N_DEV, f8, kd), bf16)] * 2, [ospec] * 2, _store(bf16))


def _ffn_dh(dg, du, wg_t, wu_t, deps=()):
    _, t, f8 = dg.shape
    kd = wg_t.shape[2]
    tm = min(t, ROWS_PER_STEP)
    aspec = pl.BlockSpec((None, tm, f8), lambda i, j, k: (k, i, 0))
    wspec = pl.BlockSpec((None, f8, kd), lambda i, j, k: (k, 0, 0))
    return _matmul(
        "ffn_dh", "nn", (t // tm, 1, N_DEV), [dg, du, wg_t, wu_t], [aspec, aspec, wspec, wspec], [(0, 2), (1, 3)], 1,
        (tm, kd), list(deps), [ANY] * len(deps), [jax.ShapeDtypeStruct((t, kd), f32)],
        [pl.BlockSpec((tm, kd), lambda i, j, k: (i, 0))], _store(f32))[0]


def _rms_fwd(x, g):
    t, d = x.shape
    tm = min(t, 256)

    def body(x_ref, g_ref, o_ref):
        xv = x_ref[...]
        r = lax.rsqrt(jnp.mean(xv * xv, axis=-1, keepdims=True) + EPS)
        o_ref[...] = (xv * r * g_ref[...]).astype(bf16)

    return pl.pallas_call(
        body, grid=(t // tm,), in_specs=[pl.BlockSpec((tm, d), lambda i: (i, 0)), pl.BlockSpec((1, d), lambda i: (0, 0))],
        out_specs=pl.BlockSpec((tm, d), lambda i: (i, 0)), out_shape=jax.ShapeDtypeStruct((t, d), bf16),
        compiler_params=_params("parallel"), name="rms_fwd")(x, g.reshape(1, d))


def _rms_bwd(x, g, dh, dres):
    t, d = x.shape
    tm = min(t, 512)

    def body(x_ref, g_ref, dh_ref, dres_ref, dx_ref, dxb_ref, dg_ref):
        xv = x_ref[...]
        dy = dh_ref[...].astype(f32)
        r = lax.rsqrt(jnp.mean(xv * xv, axis=-1, keepdims=True) + EPS)
        gy = dy * g_ref[...]
        dot = jnp.mean(xv * gy, axis=-1, keepdims=True)
        dx = dres_ref[...] + r * gy - xv * (r * r * r * dot)
        dx_ref[...] = dx
        dxb_ref[...] = dx.astype(bf16)

        @pl.when(pl.program_id(0) == 0)
        def _():
            dg_ref[...] = jnp.zeros_like(dg_ref)

        dg_ref[...] += jnp.sum(dy * xv * r, axis=0, keepdims=True)

    row = pl.BlockSpec((tm, d), lambda i: (i, 0))
    vec = pl.BlockSpec((1, d), lambda i: (0, 0))
    return pl.pallas_call(
        body, grid=(t // tm,), in_specs=[row, vec, row, row], out_specs=[row, row, vec],
        out_shape=[jax.ShapeDtypeStruct((t, d), f32), jax.ShapeDtypeStruct((t, d), bf16), jax.ShapeDtypeStruct((1, d), f32)],
        compiler_params=_params("arbitrary"), name="rms_bwd")(x, g.reshape(1, d), dh, dres)


def _loss_head(x, g, target):
    t, d = x.shape
    tm = min(t, 512)

    def body(x_ref, g_ref, t_ref, loss_ref, dx_ref, dxb_ref, dg_ref):
        xv = x_ref[...]
        r = lax.rsqrt(jnp.mean(xv * xv, axis=-1, keepdims=True) + EPS)
        xn = xv * r
        err = xn * g_ref[...] - t_ref[...]
        dy = err * (1.0 / d)
        gy = dy * g_ref[...]
        dot = jnp.mean(xv * gy, axis=-1, keepdims=True)
        dx = r * gy - xv * (r * r * r * dot)
        dx_ref[...] = dx
        dxb_ref[...] = dx.astype(bf16)

        @pl.when(pl.program_id(0) == 0)
        def _():
            dg_ref[...] = jnp.zeros_like(dg_ref)
            loss_ref[...] = jnp.zeros_like(loss_ref)

        dg_ref[...] += jnp.sum(dy * xn, axis=0, keepdims=True)
        loss_ref[...] += 0.5 * jnp.sum(jnp.sum(err * err, axis=-1, keepdims=True) * (1.0 / d), axis=0, keepdims=True)

    row = pl.BlockSpec((tm, d), lambda i: (i, 0))
    vec = pl.BlockSpec((1, d), lambda i: (0, 0))
    one = pl.BlockSpec((1, 1), lambda i: (0, 0))
    return pl.pallas_call(
        body, grid=(t // tm,), in_specs=[row, vec, row], out_specs=[one, row, row, vec],
        out_shape=[jax.ShapeDtypeStruct((1, 1), f32), jax.ShapeDtypeStruct((t, d), f32),
                   jax.ShapeDtypeStruct((t, d), bf16), jax.ShapeDtypeStruct((1, d), f32)],
        compiler_params=_params("arbitrary"), name="loss_head")(x, g.reshape(1, d), target)


def _attn_consts():
    qt, kw = 2 * CHUNK, (A_LEFT_CHUNKS + 2) * CHUNK
    r = np.arange(qt)[:, None]
    kc = np.arange(kw)[None, :]
    rel = np.clip(r + A_LEFT_CHUNKS * CHUNK - kc, -A_MAX_REL, A_MAX_REL) + A_MAX_REL
    dchunk = kc // CHUNK - r // CHUNK
    valid = (dchunk >= 0) & (dchunk <= A_LEFT_CHUNKS)
    m = np.arange(kw + qt)
    relidx = np.clip(A_LEFT_CHUNKS * CHUNK - (m - (qt - 1)), -A_MAX_REL, A_MAX_REL) + A_MAX_REL
    onehot = np.zeros((kw + qt, 2 * A_MAX_REL + 1), np.float32)
    onehot[m, relidx] = 1.0
    return qt, kw, rel, valid, onehot


def _bias_table(rel_bias):
    qt, kw, _, valid, onehot = _attn_consts()
    h = rel_bias.shape[0]
    w = kw + qt
    relidx = np.argmax(onehot, axis=1)
    e = jnp.roll(jnp.take(rel_bias, jnp.asarray(relidx), axis=1), -(qt - 1), axis=1)
    rows = jnp.broadcast_to(e[:, None, :], (h, qt, w)).reshape(h, qt * w)
    skew = rows[:, :qt * (w - 1)].reshape(h, qt, w - 1)[:, :, :kw]
    return jnp.where(jnp.asarray(valid)[None], skew, NEG_INF).astype(f32)


def _bias_table_grad(dtab):
    qt, kw, _, _, onehot = _attn_consts()
    h = dtab.shape[0]
    w = kw + qt
    wide = -(-(w + qt) // LANE) * LANE
    y = jnp.pad(dtab, ((0, 0), (0, 0), (qt - 1, wide - kw - (qt - 1))))
    flat = jnp.pad(y.reshape(h, qt * wide), ((0, 0), (0, qt)))
    de = jnp.sum(flat.reshape(h, qt, wide + 1), axis=1)[:, :w]
    return jnp.dot(de, jnp.asarray(onehot), precision=lax.Precision.HIGHEST)


def _attn_scores(q_ref, kpad, btab_ref, r0, qt, kw, pad):
    qv = q_ref[pl.ds(r0, qt), :]
    kwin = kpad[pl.ds(r0, kw), :]
    s = lax.dot_general(qv, kwin, _DN["nt"], preferred_element_type=f32) * (A_HEAD_DIM ** -0.5) + btab_ref[...]
    kcol = lax.broadcasted_iota(jnp.int32, (qt, kw), 1)
    s = jnp.where(r0 + kcol >= pad, s, NEG_INF)
    p = jnp.exp(s - jnp.max(s, axis=-1, keepdims=True))
    return qv, kwin, p / jnp.sum(p, axis=-1, keepdims=True)


def _attn_fwd(proj, btab, heads):
    t = proj.shape[0]
    qt, kw = btab.shape[1], btab.shape[2]
    pad = kw - qt

    def body(q_ref, k_ref, v_ref, btab_ref, o_ref, kpad, vpad):
        zeros = jnp.zeros((pad, A_HEAD_DIM), bf16)
        kpad[pl.ds(0, pad), :] = zeros
        vpad[pl.ds(0, pad), :] = zeros
        kpad[pl.ds(pad, t), :] = k_ref[...]
        vpad[pl.ds(pad, t), :] = v_ref[...]

        def tile(i, carry):
            r0 = pl.multiple_of(i * qt, qt)
            _, _, p = _attn_scores(q_ref, kpad, btab_ref, r0, qt, kw, pad)
            o = lax.dot_general(p.astype(bf16), vpad[pl.ds(r0, kw), :], _DN["nn"], preferred_element_type=f32)
            o_ref[pl.ds(r0, qt), :] = o.astype(bf16)
            return carry

        lax.fori_loop(0, t // qt, tile, 0, unroll=4)

    col = lambda off: pl.BlockSpec((t, A_HEAD_DIM), lambda h, off=off: (0, off + h))
    return pl.pallas_call(
        body, grid=(heads,),
        in_specs=[col(0), col(heads), col(2 * heads), pl.BlockSpec((None, qt, kw), lambda h: (h, 0, 0))],
        out_specs=col(0), out_shape=jax.ShapeDtypeStruct((t, heads * A_HEAD_DIM), bf16),
        scratch_shapes=[pltpu.VMEM((t + pad, A_HEAD_DIM), bf16)] * 2,
        compiler_params=_params("parallel"), name="attn_fwd")(proj, proj, proj, btab)


def _attn_bwd(proj, dmix, btab, heads):
    t = proj.shape[0]
    qt, kw = btab.shape[1], btab.shape[2]
    pad = kw - qt
    scale = A_HEAD_DIM ** -0.5

    def body(q_ref, k_ref, v_ref, do_ref, btab_ref, dq_ref, dk_ref, dv_ref, dtab_ref, kpad, vpad, dkacc, dvacc):
        zeros = jnp.zeros((pad, A_HEAD_DIM), bf16)
        kpad[pl.ds(0, pad), :] = zeros
        vpad[pl.ds(0, pad), :] = zeros
        kpad[pl.ds(pad, t), :] = k_ref[...]
        vpad[pl.ds(pad, t), :] = v_ref[...]
        dkacc[...] = jnp.zeros_like(dkacc)
        dvacc[...] = jnp.zeros_like(dvacc)
        dtab_ref[...] = jnp.zeros_like(dtab_ref)

        def tile(i, carry):
            r0 = pl.multiple_of(i * qt, qt)
            qv, kwin, p = _attn_scores(q_ref, kpad, btab_ref, r0, qt, kw, pad)
            dov = do_ref[pl.ds(r0, qt), :]
            dp = lax.dot_general(dov, vpad[pl.ds(r0, kw), :], _DN["nt"], preferred_element_type=f32)
            ds = p * (dp - jnp.sum(p * dp, axis=-1, keepdims=True))
            dtab_ref[...] += ds
            dsb = ds.astype(bf16)
            dq = lax.dot_general(dsb, kwin, _DN["nn"], preferred_element_type=f32) * scale
            dq_ref[pl.ds(r0, qt), :] = dq.astype(bf16)
            dkacc[pl.ds(r0, kw), :] += lax.dot_general(dsb, qv, _DN["tn"], preferred_element_type=f32) * scale
            dvacc[pl.ds(r0, kw), :] += lax.dot_general(p.astype(bf16), dov, _DN["tn"], preferred_element_type=f32)
            return carry

        lax.fori_loop(0, t // qt, tile, 0, unroll=4)
        dk_ref[...] = dkacc[pl.ds(pad, t), :].astype(bf16)
        dv_ref[...] = dvacc[pl.ds(pad, t), :].astype(bf16)

    col = lambda off: pl.BlockSpec((t, A_HEAD_DIM), lambda h, off=off: (0, off + h))
    tab = pl.BlockSpec((None, qt, kw), lambda h: (h, 0, 0))
    wide = jax.ShapeDtypeStruct((t, heads * A_HEAD_DIM), bf16)
    return pl.pallas_call(
        body, grid=(heads,), in_specs=[col(0), col(heads), col(2 * heads), col(0), tab],
        out_specs=[col(0), col(0), col(0), tab],
        out_shape=[wide, wide, wide, jax.ShapeDtypeStruct((heads, qt, kw), f32)],
        scratch_shapes=[pltpu.VMEM((t + pad, A_HEAD_DIM), bf16)] * 2 + [pltpu.VMEM((t + pad, A_HEAD_DIM), f32)] * 2,
        compiler_params=_params("parallel"), name="attn_bwd")(proj, proj, proj, dmix, btab)


def _shift_down(z, k):
    rows = lax.broadcasted_iota(jnp.int32, z.shape, 0)
    return jnp.where(rows >= k, pltpu.roll(z, k, 0), 0.0)


def _shift_up(z, k):
    t = z.shape[0]
    rows = lax.broadcasted_iota(jnp.int32, z.shape, 0)
    return jnp.where(rows < t - k, pltpu.roll(z, t - k, 0), 0.0)


def _conv_fwd(proj, conv_w, a_blocks, b_blocks):
    t = proj.shape[0]

    def body(b_ref, c_ref, h_ref, w_ref, o_ref):
        z = c_ref[...].astype(f32) * h_ref[...].astype(f32)
        w = w_ref[...]
        y = w[0:1, :] * _shift_down(z, 2) + w[1:2, :] * _shift_down(z, 1) + w[2:3, :] * z
        o_ref[...] = (b_ref[...].astype(f32) * y).astype(bf16)

    col = lambda off: pl.BlockSpec((t, LANE), lambda i, off=off: (0, off + i))
    return pl.pallas_call(
        body, grid=(b_blocks,),
        in_specs=[col(3 * a_blocks), col(3 * a_blocks + b_blocks), col(3 * a_blocks + 2 * b_blocks),
                  pl.BlockSpec((CONV_WIDTH, LANE), lambda i: (0, i))],
        out_specs=col(0), out_shape=jax.ShapeDtypeStruct((t, b_blocks * LANE), bf16),
        compiler_params=_params("parallel"), name="conv_fwd")(proj, proj, proj, conv_w)


def _conv_bwd(proj, dmix, conv_w, a_blocks, b_blocks):
    t = proj.shape[0]

    def body(b_ref, c_ref, h_ref, do_ref, w_ref, db_ref, dc_ref, dh_ref, dw_ref):
        bv, cv, hv = b_ref[...].astype(f32), c_ref[...].astype(f32), h_ref[...].astype(f32)
        w = w_ref[...]
        z = cv * hv
        z1, z2 = _shift_down(z, 1), _shift_down(z, 2)
        y = w[0:1, :] * z2 + w[1:2, :] * z1 + w[2:3, :] * z
        dov = do_ref[...].astype(f32)
        db_ref[...] = (dov * y).astype(bf16)
        dy = dov * bv
        dz = w[2:3, :] * dy + w[1:2, :] * _shift_up(dy, 1) + w[0:1, :] * _shift_up(dy, 2)
        dc_ref[...] = (dz * hv).astype(bf16)
        dh_ref[...] = (dz * cv).astype(bf16)
        dw_ref[0:1, :] = jnp.sum(dy * z2, axis=0, keepdims=True)
        dw_ref[1:2, :] = jnp.sum(dy * z1, axis=0, keepdims=True)
        dw_ref[2:3, :] = jnp.sum(dy * z, axis=0, keepdims=True)

    col = lambda off: pl.BlockSpec((t, LANE), lambda i, off=off: (0, off + i))
    wspec = pl.BlockSpec((CONV_WIDTH, LANE), lambda i: (0, i))
    wide = jax.ShapeDtypeStruct((t, b_blocks * LANE), bf16)
    return pl.pallas_call(
        body, grid=(b_blocks,),
        in_specs=[col(3 * a_blocks), col(3 * a_blocks + b_blocks), col(3 * a_blocks + 2 * b_blocks), col(a_blocks), wspec],
        out_specs=[col(0), col(0), col(0), wspec],
        out_shape=[wide, wide, wide, jax.ShapeDtypeStruct((CONV_WIDTH, b_blocks * LANE), f32)],
        compiler_params=_params("parallel"), name="conv_bwd")(proj, proj, proj, dmix, conv_w)


_RSQRT2 = 0.7071067811865476
_RSQRT2PI = 0.3989422804014327


def _gelu(x):
    return 0.5 * x * (1.0 + lax.erf(x * _RSQRT2))


def _gelu_grad(x):
    return 0.5 * (1.0 + lax.erf(x * _RSQRT2)) + x * jnp.exp(-0.5 * x * x) * _RSQRT2PI


def _sgu_common(a_ref, lg_ref, lb_ref, cw):
    av = a_ref[...]
    u = _gelu(av[:, :cw])
    v = _gelu(av[:, cw:])
    mu = jnp.mean(v, axis=-1, keepdims=True)
    xc = v - mu
    rstd = lax.rsqrt(jnp.mean(xc * xc, axis=-1, keepdims=True) + EPS)
    xhat = xc * rstd
    vln = xhat * lg_ref[...] + lb_ref[...]
    pos_t = lax.broadcasted_iota(jnp.int32, (C_BLOCK, C_BLOCK), 0) // CHUNK
    pos_s = lax.broadcasted_iota(jnp.int32, (C_BLOCK, C_BLOCK), 1) // CHUNK
    return av, u, xhat, rstd, vln, pos_s <= pos_t


def _sgu_fwd(a, ln_g, ln_b, w_s, bs_t):
    t, cw2 = a.shape
    cw = cw2 // 2
    groups = w_s.shape[0]
    cg = cw // groups

    def body(a_ref, lg_ref, lb_ref, ws_ref, bs_ref, m_ref):
        _, u, _, _, vln, mask = _sgu_common(a_ref, lg_ref, lb_ref, cw)
        vb = vln.astype(bf16)
        for g in range(groups):
            sl = slice(g * cg, (g + 1) * cg)
            wm = jnp.where(mask, ws_ref[g], 0.0).astype(bf16)
            s = lax.dot_general(wm, vb[:, sl], _DN["nn"], preferred_element_type=f32) + bs_ref[:, g:g + 1]
            m_ref[:, sl] = (u[:, sl] * s).astype(bf16)

    vec = pl.BlockSpec((1, cw), lambda n: (0, 0))
    return pl.pallas_call(
        body, grid=(t // C_BLOCK,),
        in_specs=[pl.BlockSpec((C_BLOCK, cw2), lambda n: (n, 0)), vec, vec,
                  pl.BlockSpec((groups, C_BLOCK, C_BLOCK), lambda n: (0, 0, 0)),
                  pl.BlockSpec((C_BLOCK, groups), lambda n: (0, 0))],
        out_specs=pl.BlockSpec((C_BLOCK, cw), lambda n: (n, 0)), out_shape=jax.ShapeDtypeStruct((t, cw), bf16),
        compiler_params=_params("parallel"), name="sgu_fwd")(a, ln_g.reshape(1, cw), ln_b.reshape(1, cw), w_s, bs_t)


def _sgu_bwd(a, dm, ln_g, ln_b, w_s, bs_t):
    t, cw2 = a.shape
    cw = cw2 // 2
    groups = w_s.shape[0]
    cg = cw // groups

    def body(a_ref, dm_ref, lg_ref, lb_ref, ws_ref, bs_ref, da_ref, dws_ref, dbs_ref, dlg_ref, dlb_ref, dvln):
        @pl.when(pl.program_id(0) == 0)
        def _():
            dws_ref[...] = jnp.zeros_like(dws_ref)
            dbs_ref[...] = jnp.zeros_like(dbs_ref)
            dlg_ref[...] = jnp.zeros_like(dlg_ref)
            dlb_ref[...] = jnp.zeros_like(dlb_ref)

        av, u, xhat, rstd, vln, mask = _sgu_common(a_ref, lg_ref, lb_ref, cw)
        vb = vln.astype(bf16)
        lane = lax.broadcasted_iota(jnp.int32, (C_BLOCK, groups), 1)
        dbs = jnp.zeros((C_BLOCK, groups), f32)
        for g in range(groups):
            sl = slice(g * cg, (g + 1) * cg)
            wm = jnp.where(mask, ws_ref[g], 0.0).astype(bf16)
            s = lax.dot_general(wm, vb[:, sl], _DN["nn"], preferred_element_type=f32) + bs_ref[:, g:g + 1]
            dmg = dm_ref[:, sl].astype(f32)
            da_ref[:, sl] = (dmg * s * _gelu_grad(av[:, sl])).astype(bf16)
            dsg = dmg * u[:, sl]
            dbs = dbs + jnp.where(lane == g, jnp.sum(dsg, axis=-1, keepdims=True), 0.0)
            dsb = dsg.astype(bf16)
            dws_ref[g] += jnp.where(mask, lax.dot_general(dsb, vb[:, sl], _DN["nt"], preferred_element_type=f32), 0.0)
            dvln[:, sl] = lax.dot_general(wm, dsb, _DN["tn"], preferred_element_type=f32)
        dbs_ref[...] += dbs
        dv = dvln[...]
        dlg_ref[...] += jnp.sum(dv * xhat, axis=0, keepdims=True)
        dlb_ref[...] += jnp.sum(dv, axis=0, keepdims=True)
        dxh = dv * lg_ref[...]
        dvv = rstd * (dxh - jnp.mean(dxh, axis=-1, keepdims=True) - xhat * jnp.mean(dxh * xhat, axis=-1, keepdims=True))
        da_ref[:, cw:] = (dvv * _gelu_grad(av[:, cw:])).astype(bf16)

    vec = pl.BlockSpec((1, cw), lambda n: (0, 0))
    wsp = pl.BlockSpec((groups, C_BLOCK, C_BLOCK), lambda n: (0, 0, 0))
    bsp = pl.BlockSpec((C_BLOCK, groups), lambda n: (0, 0))
    return pl.pallas_call(
        body, grid=(t // C_BLOCK,),
        in_specs=[pl.BlockSpec((C_BLOCK, cw2), lambda n: (n, 0)), pl.BlockSpec((C_BLOCK, cw), lambda n: (n, 0)), vec, vec, wsp, bsp],
        out_specs=[pl.BlockSpec((C_BLOCK, cw2), lambda n: (n, 0)), wsp, bsp, vec, vec],
        out_shape=[jax.ShapeDtypeStruct((t, cw2), bf16), jax.ShapeDtypeStruct(w_s.shape, f32),
                   jax.ShapeDtypeStruct(bs_t.shape, f32), jax.ShapeDtypeStruct((1, cw), f32), jax.ShapeDtypeStruct((1, cw), f32)],
        scratch_shapes=[pltpu.VMEM((C_BLOCK, cw), f32)],
        compiler_params=_params("arbitrary"), name="sgu_bwd")(a, dm, ln_g.reshape(1, cw), ln_b.reshape(1, cw), w_s, bs_t)


HBM = pl.BlockSpec(memory_space=pltpu.HBM)
SEM = pl.BlockSpec(memory_space=pltpu.SEMAPHORE)
EFFECT = pltpu.SideEffectType.DATAFLOW_SIDE_EFFECTING


def _place():
    x, y, c = lax.axis_index("x"), lax.axis_index("y"), lax.axis_index("c")
    return x, y, c, [(1 - x, y), (x, 1 - y), (1 - x, 1 - y)]


def _remote(src, dst, send_sems, recv_sems, k, to):
    return pltpu.make_async_remote_copy(src_ref=src, dst_ref=dst, send_sem=send_sems.at[k], recv_sem=recv_sems.at[k],
                                        device_id=to, device_id_type=MESH)


def _split_start(name, arrays, plan, n_copies, after):
    n = len(arrays)

    def body(*refs):
        send_sems, recv_sems, token = refs[n + 1], refs[n + 2], refs[-1]
        for cp in plan(refs[:n], send_sems, recv_sems):
            cp.start()
        token[...] = jnp.zeros_like(token)

    out = pl.pallas_call(
        body, name=name,
        out_shape=(pltpu.SemaphoreType.DMA((n_copies,)), pltpu.SemaphoreType.DMA((n_copies,)),
                   *[pltpu.HBM(a.shape, a.dtype) for a in arrays], jax.ShapeDtypeStruct((8, LANE), f32)),
        in_specs=[HBM] * n + [ANY], out_specs=(SEM, SEM, *[HBM] * n, pl.BlockSpec(memory_space=pltpu.VMEM)),
        input_output_aliases={i: 2 + i for i in range(n)},
        compiler_params=pltpu.CompilerParams(has_side_effects=EFFECT),
    )(*[pltpu.with_memory_space_constraint(a, pltpu.HBM) for a in arrays], after)
    return (out[0], out[1]), list(out[2:2 + n]), out[-1]


def _split_wait(name, arrays, sems, plan, after):
    n = len(arrays)

    def body(*refs):
        for cp in plan(refs[:n], refs[n], refs[n + 1]):
            cp.wait()

    out = pl.pallas_call(
        body, name=name, out_shape=tuple(pltpu.HBM(a.shape, a.dtype) for a in arrays),
        in_specs=[HBM] * n + [SEM, SEM, ANY], out_specs=tuple([HBM] * n), input_output_aliases={i: i for i in range(n)},
        compiler_params=pltpu.CompilerParams(has_side_effects=EFFECT),
    )(*arrays, sems[0], sems[1], after)
    return list(out)


def _row_pieces(ref_rows, split):
    rc = ref_rows // split
    return [pl.ds(s * rc, rc) for s in range(split)]


def _gather_slots():
    x, y, c, _ = _place()
    slots = (4 * x + 2 * y + c, 4 * (1 - x) + 2 * y + c, 4 * x + 2 * (1 - y) + c, 4 * (1 - x) + 2 * (1 - y) + c)
    return slots, (x, y, 1 - c), (1 - x, y, c), (x, 1 - y, c)


def _to_sibling(cps, k, b, slot, sibling, send_sems, recv_sems):
    for rows in _row_pieces(b.shape[1], D2D_SPLIT):
        cps.append(_remote(b.at[slot, rows], b.at[slot, rows], send_sems, recv_sems, k, sibling))
        k += 1
    return k


def _plan_gather_near(bufs, send_sems, recv_sems):
    (me, _, _, _), sibling, x_peer, y_peer = _gather_slots()
    cps, k = [], 0
    for b in bufs:
        k = _to_sibling(cps, k, b, me, sibling, send_sems, recv_sems)
        for peer in (x_peer, y_peer):
            cps.append(_remote(b.at[me], b.at[me], send_sems, recv_sems, k, peer))
            k += 1
    return cps


def _plan_gather_relay(bufs, send_sems, recv_sems):
    (_, x_slot, y_slot, _), sibling, x_peer, y_peer = _gather_slots()
    cps, k = [], 0
    for b in bufs:
        half = b.shape[1] // 2
        k = _to_sibling(cps, k, b, x_slot, sibling, send_sems, recv_sems)
        k = _to_sibling(cps, k, b, y_slot, sibling, send_sems, recv_sems)
        lower, upper = pl.ds(0, half), pl.ds(half, half)
        cps.append(_remote(b.at[x_slot, lower], b.at[x_slot, lower], send_sems, recv_sems, k, y_peer))
        cps.append(_remote(b.at[y_slot, upper], b.at[y_slot, upper], send_sems, recv_sems, k + 1, x_peer))
        k += 2
    return cps


def _plan_gather_far(bufs, send_sems, recv_sems):
    (_, _, _, far), sibling, _, _ = _gather_slots()
    cps, k = [], 0
    for b in bufs:
        k = _to_sibling(cps, k, b, far, sibling, send_sems, recv_sems)
    return cps


def _plan_rs_sibling(arrs, send_sems, recv_sems):
    n = len(arrs) // 2
    x, y, c, _ = _place()
    cps, k = [], 0
    for g, got in zip(arrs[:n], arrs[n:]):
        for q in range(N_CHIP):
            for rows in _row_pieces(g.shape[1], RS_SPLIT):
                cps.append(_remote(g.at[N_CHIP * (1 - c) + q, rows], got.at[q, rows], send_sems, recv_sems, k, (x, y, 1 - c)))
                k += 1
    return cps


def _plan_rs_chips(arrs, send_sems, recv_sems):
    n = len(arrs) // 2
    x, y, c, chips = _place()
    q = 2 * x + y
    cps, k = [], 0
    for p, r in zip(arrs[:n], arrs[n:]):
        for px, py in chips:
            cps.append(_remote(p.at[2 * px + py], r.at[q], send_sems, recv_sems, k, (px, py, c)))
            k += 1
    return cps


class _Gather:
    STAGES = (("near", _plan_gather_near, D2D_SPLIT + 2), ("relay", _plan_gather_relay, 2 * D2D_SPLIT + 2),
              ("far", _plan_gather_far, D2D_SPLIT))

    def __init__(self, tag, bufs):
        self.tag, self.bufs, self.stage = tag, bufs, -1

    def advance(self, after):
        if self.stage >= 0:
            name, plan, _ = self.STAGES[self.stage]
            self.bufs = _split_wait("gather_%s_wait_%s" % (name, self.tag), self.bufs, self.sems, plan, after)
        self.stage += 1
        if self.stage == len(self.STAGES):
            return self.bufs
        name, plan, per_array = self.STAGES[self.stage]
        self.sems, self.bufs, token = _split_start("gather_%s_start_%s" % (name, self.tag), self.bufs, plan,
                                                   per_array * len(self.bufs), after)
        return token

    def finish(self, after):
        out = self.advance(after)
        while not isinstance(out, list):
            out = self.advance(after)
        return out


class _ReduceScatter:
    def __init__(self, tag, grads, core):
        self.tag, self.n = tag, len(grads)
        lands = [lax.empty((N_CHIP,) + g.shape[1:], g.dtype) for g in grads]
        self.sems, self.arrs, self.token = _split_start("rs_sibling_start_" + tag, list(grads) + lands, _plan_rs_sibling,
                                                        self.n * N_CHIP * RS_SPLIT, core)

    def middle(self, after, core):
        arrs = _split_wait("rs_sibling_wait_" + self.tag, self.arrs, self.sems, _plan_rs_sibling, after)
        parts = [_pair_sum(g, got, core) for g, got in zip(arrs[:self.n], arrs[self.n:])]
        lands = [lax.empty(p.shape, p.dtype) for p in parts]
        self.sems, self.arrs, self.token = _split_start("rs_chips_start_" + self.tag, parts + lands, _plan_rs_chips,
                                                        self.n * 3, core)

    def finish(self, after):
        arrs = _split_wait("rs_chips_wait_" + self.tag, self.arrs, self.sems, _plan_rs_chips, after)
        return list(zip(arrs[:self.n], arrs[self.n:]))


def _cast_into_slot(name, w, layer, me, after):
    _, rows, cols = w.shape
    tr = 256 if rows % 256 == 0 else rows

    def body(me_ref, w_ref, after_ref, o_ref):
        o_ref[...] = w_ref[...].astype(bf16)

    return pl.pallas_call(
        body,
        grid_spec=pltpu.PrefetchScalarGridSpec(
            num_scalar_prefetch=1, grid=(rows // tr,),
            in_specs=[pl.BlockSpec((None, tr, cols), lambda i, me_ref: (layer, i, 0)), ANY],
            out_specs=pl.BlockSpec((None, tr, cols), lambda i, me_ref: (me_ref[0], i, 0))),
        out_shape=jax.ShapeDtypeStruct((N_DEV, rows, cols), bf16), compiler_params=_params("parallel"), name=name)(me, w, after)


def _pair_sum(g, got, core):
    _, rows, cols = g.shape
    tr = rows

    def body(c_ref, a_ref, b_ref, o_ref):
        o_ref[...] = (a_ref[...].astype(f32) + b_ref[...].astype(f32)).astype(bf16)

    spec = pl.BlockSpec((None, tr, cols), lambda q, i, c_ref: (q, i, 0))
    return pl.pallas_call(
        body,
        grid_spec=pltpu.PrefetchScalarGridSpec(
            num_scalar_prefetch=1, grid=(N_CHIP, rows // tr),
            in_specs=[pl.BlockSpec((None, tr, cols), lambda q, i, c_ref: (N_CHIP * c_ref[0] + q, i, 0)), spec],
            out_specs=spec),
        out_shape=jax.ShapeDtypeStruct((N_CHIP, rows, cols), bf16), compiler_params=_params("parallel", "parallel"),
        name="pair_sum")(core, g, got)


def _gather_copies(n, ins, outs, send_sems, recv_sems, local_sems):
    x, y, c, chips = _place()
    sibling = (x, y, 1 - c)

    def slot(px, py, pc):
        return 4 * px + 2 * py + pc

    def copy(i, k, block, to, src=None):
        dst = outs[i].at[slot(*block)]
        return pltpu.make_async_remote_copy(src_ref=dst if src is None else src, dst_ref=dst, send_sem=send_sems.at[i, k],
                                            recv_sem=recv_sems.at[i, k], device_id=to, device_id_type=MESH)

    started = []
    for i in range(n):
        mine = pltpu.make_async_copy(ins[i], outs[i].at[slot(x, y, c)], local_sems.at[i])
        mine.start()
        started.append(mine)
    sends = []
    for i in range(n):
        sends.append(copy(i, 0, (x, y, c), sibling, src=ins[i]))
        sends += [copy(i, 1 + j, (x, y, c), (*chip, c), src=ins[i]) for j, chip in enumerate(chips)]
    for cp in sends:
        cp.start()
    for i in range(n):
        for j, chip in enumerate(chips):
            copy(i, 1 + j, (*chip, c), (x, y, c)).wait_recv()
            fwd = copy(i, 4 + j, (*chip, c), sibling)
            fwd.start()
            sends.append(fwd)
    for i in range(n):
        copy(i, 0, sibling, (x, y, c)).wait_recv()
        for j, chip in enumerate(chips):
            copy(i, 4 + j, (*chip, 1 - c), (x, y, c)).wait_recv()
    for cp in sends:
        cp.wait_send()
    for mine in started:
        mine.wait()


def _gather_small(name, packed, reduce):
    rows = packed.shape[0]

    def body(x_ref, o_ref, buf, send_sems, recv_sems, local_sems):
        _gather_copies(1, [x_ref], [buf], send_sems, recv_sems, local_sems)
        if reduce:
            acc = buf[0]
            for j in range(1, N_DEV):
                acc = acc + buf[j]
            o_ref[...] = acc
        else:
            o_ref[...] = buf[...]

    vm = pl.BlockSpec(memory_space=pltpu.VMEM)
    return pl.pallas_call(
        body, in_specs=[vm], out_specs=vm,
        out_shape=jax.ShapeDtypeStruct((rows, LANE) if reduce else (N_DEV, rows, LANE), f32),
        scratch_shapes=[pltpu.VMEM((N_DEV, rows, LANE), f32), pltpu.SemaphoreType.DMA((1, 7)), pltpu.SemaphoreType.DMA((1, 7)),
                        pltpu.SemaphoreType.DMA((1,))],
        compiler_params=pltpu.CompilerParams(vmem_limit_bytes=VMEM_LIMIT), name=name)(packed)


def _pack(arrs):
    flat = jnp.concatenate([a.reshape(-1).astype(f32) for a in arrs])
    rows = -(-flat.shape[0] // (8 * LANE)) * 8
    return jnp.pad(flat, (0, rows * LANE - flat.shape[0])).reshape(rows, LANE)


def _unpack(buf, shapes):
    flat = buf.reshape(-1)
    out, off = [], 0
    for s in shapes:
        n = int(np.prod(s))
        out.append(flat[off:off + n].reshape(s))
        off += n
    return out


def _adam_math(w, g, m, v):
    m2 = ADAM_B1 * m + (1.0 - ADAM_B1) * g
    v2 = ADAM_B2 * v + (1.0 - ADAM_B2) * (g * g)
    m_hat = m2 / (1.0 - ADAM_B1 ** ADAM_STEP)
    v_hat = v2 / (1.0 - ADAM_B2 ** ADAM_STEP)
    delta = -ADAM_LR * (m_hat / (jnp.sqrt(v_hat) + ADAM_EPS) + ADAM_WD * w)
    return delta, m2, v2


def _adam_big(name, w, m, v, parts, chip):
    layers, rows, cols = w.shape
    tr = 512 if rows % 512 == 0 else 256 if rows % 256 == 0 else rows // 4 if rows % 32 == 0 else 8

    def body(chip_ref, w_ref, m_ref, v_ref, *rest):
        p_refs = rest[:N_CHIP * layers]
        g_ref, d_ref, m2_ref, v2_ref = rest[N_CHIP * layers:]
        for li in range(layers):
            @pl.when(pl.program_id(0) == li)
            def _(li=li):
                g = p_refs[N_CHIP * li][...].astype(f32)
                for q in range(1, N_CHIP):
                    g = g + p_refs[N_CHIP * li + q][...].astype(f32)
                delta, m2, v2 = _adam_math(w_ref[...], g, m_ref[...], v_ref[...])
                g_ref[...] = g
                d_ref[...] = delta
                m2_ref[...] = m2
                v2_ref[...] = v2

    spec = pl.BlockSpec((None, tr, cols), lambda l, i, c_ref: (l, i, 0))
    pspecs, operands = [], []
    for li in range(layers):
        for q in range(N_CHIP):
            pspecs.append(pl.BlockSpec((None, tr, cols),
                                       lambda l, i, c_ref, li=li, q=q: ((c_ref[0] + q) % N_CHIP, jnp.where(l == li, i, 0), 0)))
            operands.append(parts[li][0] if q == 0 else parts[li][1])
    out = jax.ShapeDtypeStruct((layers, rows, cols), f32)
    return pl.pallas_call(
        body,
        grid_spec=pltpu.PrefetchScalarGridSpec(num_scalar_prefetch=1, grid=(layers, rows // tr),
                                               in_specs=[spec, spec, spec] + pspecs, out_specs=[spec] * 4),
        out_shape=[out] * 4, compiler_params=_params("arbitrary", "arbitrary"), name=name)(chip, w, m, v, *operands)


def _adam_small(w, g, m, v):
    rows = w.shape[0]

    def body(w_ref, g_ref, m_ref, v_ref, d_ref, m2_ref, v2_ref):
        delta, m2, v2 = _adam_math(w_ref[...], g_ref[...], m_ref[...], v_ref[...])
        d_ref[...] = delta
        m2_ref[...] = m2
        v2_ref[...] = v2

    out = jax.ShapeDtypeStruct((rows, LANE), f32)
    return pl.pallas_call(body, out_shape=[out] * 3, name="adam_small")(w, g, m, v)


def kernel(x, mix_norm, ab_w_in, ab_rel_bias, ab_conv_w, ab_w_out, c_w_in, c_ln_g, c_ln_b, c_w_s, c_b_s, c_w_out, ffn_norm, ffn_w_gate, ffn_w_up, ffn_w_down, final_norm, loss_target, m_mix_norm, m_ab_w_in, m_ab_rel_bias, m_ab_conv_w, m_ab_w_out, m_c_w_in, m_c_ln_g, m_c_ln_b, m_c_w_s, m_c_b_s, m_c_w_out, m_ffn_norm, m_ffn_w_gate, m_ffn_w_up, m_ffn_w_down, m_final_norm, v_mix_norm, v_ab_w_in, v_ab_rel_bias, v_ab_conv_w, v_ab_w_out, v_c_w_in, v_c_ln_g, v_c_ln_b, v_c_w_s, v_c_b_s, v_c_w_out, v_ffn_norm, v_ffn_w_gate, v_ffn_w_up, v_ffn_w_down, v_final_norm):
    d = D_MODEL
    a_width = d // 2
    heads = a_width // A_HEAD_DIM
    a_blocks = a_width // LANE
    b_blocks = (d - a_width) // LANE
    n_even, n_odd = (DEPTH + 1) // 2, DEPTH // 2
    me_s = 4 * lax.axis_index("x") + 2 * lax.axis_index("y") + lax.axis_index("c")
    me = me_s.astype(jnp.int32).reshape(1)
    core = lax.axis_index("c").astype(jnp.int32).reshape(1)
    chip = (2 * lax.axis_index("x") + lax.axis_index("y")).astype(jnp.int32).reshape(1)

    weights = dict(mix_norm=mix_norm, ab_w_in=ab_w_in, ab_rel_bias=ab_rel_bias, ab_conv_w=ab_conv_w, ab_w_out=ab_w_out,
                   c_w_in=c_w_in, c_ln_g=c_ln_g, c_ln_b=c_ln_b, c_w_s=c_w_s, c_b_s=c_b_s, c_w_out=c_w_out,
                   ffn_norm=ffn_norm, ffn_w_gate=ffn_w_gate, ffn_w_up=ffn_w_up, ffn_w_down=ffn_w_down, final_norm=final_norm)
    mom_m = dict(mix_norm=m_mix_norm, ab_w_in=m_ab_w_in, ab_rel_bias=m_ab_rel_bias, ab_conv_w=m_ab_conv_w, ab_w_out=m_ab_w_out,
                 c_w_in=m_c_w_in, c_ln_g=m_c_ln_g, c_ln_b=m_c_ln_b, c_w_s=m_c_w_s, c_b_s=m_c_b_s, c_w_out=m_c_w_out,
                 ffn_norm=m_ffn_norm, ffn_w_gate=m_ffn_w_gate, ffn_w_up=m_ffn_w_up, ffn_w_down=m_ffn_w_down, final_norm=m_final_norm)
    mom_v = dict(mix_norm=v_mix_norm, ab_w_in=v_ab_w_in, ab_rel_bias=v_ab_rel_bias, ab_conv_w=v_ab_conv_w, ab_w_out=v_ab_w_out,
                 c_w_in=v_c_w_in, c_ln_g=v_c_ln_g, c_ln_b=v_c_ln_b, c_w_s=v_c_w_s, c_b_s=v_c_b_s, c_w_out=v_c_w_out,
                 ffn_norm=v_ffn_norm, ffn_w_gate=v_ffn_w_gate, ffn_w_up=v_ffn_w_up, ffn_w_down=v_ffn_w_down, final_norm=v_final_norm)
    order = list(weights)
    wide = ("ffn_w_gate", "ffn_w_up")
    flip = lambda a: jnp.swapaxes(a, 1, 2)
    local = {k: (flip(weights[k]), flip(mom_m[k]), flip(mom_v[k])) if k in wide else (weights[k], mom_m[k], mom_v[k]) for k in order}

    sharded_small = [ab_conv_w, c_ln_g, c_ln_b]
    gathered = _gather_small("gather_small", _pack(sharded_small), reduce=False)
    conv_parts, lng_parts, lnb_parts = [], [], []
    for j in range(N_DEV):
        cw_j, lg_j, lb_j = _unpack(gathered[j], [a.shape for a in sharded_small])
        conv_parts.append(cw_j)
        lng_parts.append(lg_j)
        lnb_parts.append(lb_j)
    conv_full = jnp.concatenate(conv_parts, axis=-1)
    lng_full = jnp.concatenate(lng_parts, axis=-1)
    lnb_full = jnp.concatenate(lnb_parts, axis=-1)

    gate_t, up_t = local["ffn_w_gate"][0], local["ffn_w_up"][0]
    sets = []
    for layer in range(DEPTH):
        i = layer // 2
        if layer % 2 == 0:
            sets += [("ab_in%d" % i, [(ab_w_in, i)]), ("ab_out%d" % i, [(ab_w_out, i)])]
        else:
            sets += [("c_in%d" % i, [(c_w_in, i)]), ("c_out%d" % i, [(c_w_out, i)])]
        sets += [("ffn_in%d" % layer, [(gate_t, layer), (up_t, layer)]), ("ffn_out%d" % layer, [(ffn_w_down, layer)])]
    units = [None] * len(sets)
    cursor = [0]
    tokens = []

    def start_gather(k, after):
        if k < len(sets):
            tag, members = sets[k]
            units[k] = _Gather(tag, [_cast_into_slot("cast_slot", w, li, me, after) for w, li in members])
            tokens.append(units[k].advance(after))

    def next_weights(after):
        k = cursor[0]
        cursor[0] = k + 1
        ready = units[k].finish(after)
        for later in range(k + 1, min(k + len(_Gather.STAGES), len(sets)) if k else 2):
            tokens.append(units[later].advance(after))
        start_gather(k + GATHER_AHEAD, after)
        return ready

    def started():
        deps = list(tokens)
        tokens.clear()
        return deps

    xs = x[0]
    tgt = loss_target[0]
    start_gather(0, gathered)
    start_gather(1, gathered)
    tokens.append(units[0].advance(tokens[-1]))
    for k in range(2, GATHER_AHEAD):
        start_gather(k, tokens[-1])
    tokens.append(units[0].advance(tokens[-1]))
    saved = []
    h = _rms_fwd(xs, mix_norm[0])
    for layer in range(DEPTH):
        i = layer // 2
        (w_in_g,) = next_weights(xs)
        if layer % 2 == 0:
            proj = _mm_cols("ab_proj", h, w_in_g, bf16, started())
            btab = _bias_table(ab_rel_bias[i])
            attn = _attn_fwd(proj, btab, heads)
            conv = _conv_fwd(proj, conv_full[i], a_blocks, b_blocks)
            mixed = jnp.concatenate([attn, conv], axis=-1)
            ctx = (proj, btab)
        else:
            proj = _mm_cols("c_proj", h, w_in_g, f32, started())
            bs_t = jnp.transpose(c_b_s[i])
            mixed = _sgu_fwd(proj, lng_full[i], lnb_full[i], c_w_s[i], bs_t)
            ctx = (proj, bs_t)
        (w_out_g,) = next_weights(mixed)
        w_out_full = w_out_g.reshape(-1, w_out_g.shape[-1])
        x1, h2 = _mm_rows_res("mix_out", mixed, w_out_full, xs, ffn_norm[layer], started())
        wg_g, wu_g = next_weights(x1)
        g_act, u_act, act = _ffn_in(h2, wg_g, wu_g, started())
        (wd_g,) = next_weights(g_act)
        nxt = _ffn_down(act, wd_g, x1, mix_norm[layer + 1] if layer + 1 < DEPTH else None, started())
        saved.append((xs, h, ctx, mixed, x1, h2, g_act, u_act, w_in_g, w_out_full, wg_g, wu_g, wd_g))
        xs, h = nxt if len(nxt) == 2 else (nxt[0], None)

    loss_part, dx, dxb, d_final = _loss_head(xs, final_norm, tgt)
    loss = lax.psum(loss_part[0, 0], ("x", "y", "c"))

    scatters = {}
    small = {k: [None] * weights[k].shape[0] for k in ("mix_norm", "ffn_norm", "ab_rel_bias", "ab_conv_w", "c_ln_g", "c_ln_b",
                                                       "c_w_s", "c_b_s")}
    for layer in reversed(range(DEPTH)):
        i = layer // 2
        xs, h, ctx, mixed, x1, h2, g_act, u_act, w_in_g, w_out_full, wg_g, wu_g, wd_g = saved[layer]
        dg, du, act = _ffn_bwd_act(dxb, wd_g, g_act, u_act, started())
        dwd = _ffn_dwd(act, dxb)
        dwg, dwu = _ffn_dwgu(h2, dg, du)
        rs_ffn = _ReduceScatter("ffn%d" % layer, [dwg, dwu, dwd], core)
        dh2 = _ffn_dh(dg, du, wg_g, wu_g, [rs_ffn.token])
        dx, dxb, dgn = _rms_bwd(x1, ffn_norm[layer], dh2, dx)
        small["ffn_norm"][layer] = dgn[0]
        rs_ffn.middle(dxb, core)
        for pos, k in enumerate(("ffn_w_gate", "ffn_w_up", "ffn_w_down")):
            scatters[(k, layer)] = (rs_ffn, pos)
        dmixed = _mm_nt("mix_out_bwd", dxb, w_out_full, bf16, [rs_ffn.token])
        dwout = _mm_tn_rows("mix_out_dw", mixed, dxb)
        if layer % 2 == 0:
            proj, btab = ctx
            dq, dk, dv, dtab = _attn_bwd(proj, dmixed, btab, heads)
            db, dc, dhv, dcw = _conv_bwd(proj, dmixed, conv_full[i], a_blocks, b_blocks)
            dproj = jnp.concatenate([dq, dk, dv, db, dc, dhv], axis=-1)
            small["ab_rel_bias"][i] = _bias_table_grad(dtab)
            small["ab_conv_w"][i] = dcw
            names = ("ab_w_in", "ab_w_out")
            tag = "ab"
        else:
            proj, bs_t = ctx
            dproj, dws, dbs_t, dlg, dlb = _sgu_bwd(proj, dmixed, lng_full[i], lnb_full[i], c_w_s[i], bs_t)
            small["c_w_s"][i] = dws
            small["c_b_s"][i] = jnp.transpose(dbs_t)
            small["c_ln_g"][i] = dlg[0]
            small["c_ln_b"][i] = dlb[0]
            names = ("c_w_in", "c_w_out")
            tag = "c"
        dwin = _mm_tn_cols(tag + "_proj_dw", h, dproj)
        rs_mix = _ReduceScatter("%s%d" % (tag, i), [dwin, dwout], core)
        dh = _mm_nt_cols(tag + "_proj_bwd", dproj, w_in_g, [rs_mix.token])
        dx, dxb, dgm = _rms_bwd(xs, mix_norm[layer], dh, dx)
        small["mix_norm"][layer] = dgm[0]
        rs_mix.middle(dxb, core)
        tokens.append(rs_mix.token)
        scatters[(names[0], i)] = (rs_mix, 0)
        scatters[(names[1], i)] = (rs_mix, 1)
    grad_x = dx[None]

    small_names = ["mix_norm", "ffn_norm", "ab_rel_bias", "ab_conv_w", "c_ln_g", "c_ln_b", "c_w_s", "c_b_s"]
    small_full = [jnp.stack(small[k]) for k in small_names] + [d_final[0]]
    summed = _unpack(_gather_small("reduce_small", _pack(small_full), reduce=True), [a.shape for a in small_full])
    small_grads = dict(zip(small_names + ["final_norm"], summed))
    for k in ("ab_conv_w", "c_ln_g", "c_ln_b"):
        width = weights[k].shape[-1]
        small_grads[k] = lax.dynamic_slice_in_dim(small_grads[k], me_s * width, width, axis=-1)
    small_order = [k for k in order if k in small_grads]
    shapes = [weights[k].shape for k in small_order]
    d_s, m_s, v_s = _adam_small(_pack([weights[k] for k in small_order]), _pack([small_grads[k] for k in small_order]),
                                _pack([mom_m[k] for k in small_order]), _pack([mom_v[k] for k in small_order]))
    grads, deltas, new_m, new_v = dict(small_grads), {}, {}, {}
    for k, dd, mm, vv in zip(small_order, _unpack(d_s, shapes), _unpack(m_s, shapes), _unpack(v_s, shapes)):
        deltas[k], new_m[k], new_v[k] = dd, mm, vv

    finished = {}
    last = started()[-1]
    for k in ("c_w_in", "c_w_out", "ffn_w_gate", "ffn_w_up", "ffn_w_down", "ab_w_in", "ab_w_out"):
        parts = []
        w_k, m_k, v_k = local[k]
        for li in range(w_k.shape[0]):
            rs, pos = scatters[(k, li)]
            if id(rs) not in finished:
                finished[id(rs)] = rs.finish(last)
            parts.append(finished[id(rs)][pos])
        outs = _adam_big("adam_" + k, w_k, m_k, v_k, parts, chip)
        last = outs[1]
        grads[k], deltas[k], new_m[k], new_v[k] = [flip(o) for o in outs] if k in wide else outs

    return (loss, grad_x, *[grads[k] for k in order], *[deltas[k] for k in order], *[new_m[k] for k in order],
            *[new_v[k] for k in order])
```

```python
import numpy as np
import jax
import jax.numpy as jnp
from jax import lax
from jax.experimental import pallas as pl
from jax.experimental.pallas import tpu as pltpu

D_MODEL = 2048
SEQ = 2048
DEPTH = 4
CHUNK = 64
A_HEAD_DIM = 128
A_LEFT_CHUNKS = 8
A_MAX_REL = 256
CONV_WIDTH = 3
C_BLOCK = 128
C_GROUPS = 8
EPS = 1e-6
NEG_INF = -1e30

ADAM_LR = 0.001
ADAM_B1 = 0.9
ADAM_B2 = 0.999
ADAM_EPS = 1e-08
ADAM_WD = 0.01
ADAM_STEP = 10

N_DEV = 8
N_CHIP = 4
RS_SPLIT = 4
D2D_SPLIT = 2
GATHER_AHEAD = 4
ROWS_PER_STEP = 1024
COL_CHUNK = 256
LANE = 128
VMEM_LIMIT = 52 * 1024 * 1024

bf16 = jnp.bfloat16
f32 = jnp.float32
MESH = pl.DeviceIdType.MESH
ANY = pl.BlockSpec(memory_space=pl.ANY)


def _params(*sem):
    return pltpu.CompilerParams(dimension_semantics=sem, vmem_limit_bytes=VMEM_LIMIT)


def _perm(j):
    return (j % 2) * N_CHIP + j // 2


_DN = {"nn": (((1,), (0,)), ((), ())), "nt": (((1,), (1,)), ((), ())), "tn": (((0,), (0,)), ((), ()))}


def _matmul(name, mode, grid, operands, specs, pairs, n_acc, acc_shape, extras, extra_specs, out_shapes, out_specs,
            epilogue, chunk=0):
    nk = grid[2]
    n_op, n_ex, n_out = len(operands), len(extras), len(out_shapes)

    def single(*refs):
        ops = refs[:n_op]
        ex = refs[n_op:n_op + n_ex]
        outs = refs[n_op + n_ex:]
        width = acc_shape[1]
        starts = range(0, width, chunk) if chunk else (0,)
        for c0 in starts:
            cols = slice(c0, min(c0 + chunk, width)) if chunk else slice(None)
            sums = [None] * n_acc
            for p, (ia, ib) in enumerate(pairs):
                b = ops[ib][cols, :] if mode == "nt" else ops[ib][:, cols]
                d = lax.dot_general(ops[ia][...], b, _DN[mode], preferred_element_type=f32)
                sums[p % n_acc] = d if sums[p % n_acc] is None else sums[p % n_acc] + d
            epilogue(sums, ex, outs, cols)

    def body(*refs):
        ops = refs[:n_op]
        ex = refs[n_op:n_op + n_ex]
        outs = refs[n_op + n_ex:n_op + n_ex + n_out]
        accs = refs[n_op + n_ex + n_out:]
        k = pl.program_id(2)

        @pl.when(k == 0)
        def _():
            for acc in accs:
                acc[...] = jnp.zeros_like(acc)

        for p, (ia, ib) in enumerate(pairs):
            acc = accs[p % n_acc]
            acc[...] += lax.dot_general(ops[ia][...], ops[ib][...], _DN[mode], preferred_element_type=f32)

        @pl.when(k == nk - 1)
        def _():
            epilogue([acc[...] for acc in accs], ex, outs, slice(None))

    return pl.pallas_call(
        single if nk == 1 else body, grid=grid, in_specs=list(specs) + list(extra_specs), out_specs=list(out_specs),
        out_shape=list(out_shapes), scratch_shapes=[] if nk == 1 else [pltpu.VMEM(acc_shape, f32)] * n_acc,
        compiler_params=_params("parallel", "parallel", "arbitrary"), name=name)(*operands, *extras)


def _store(dtype):
    def ep(accs, ex, outs, cols):
        for a, o in zip(accs, outs):
            o[:, cols] = a.astype(dtype)
    return ep


def _mm_cols(name, h, wg, out_dtype, deps=()):
    t, kd = h.shape
    n8 = wg.shape[2]
    tm = min(t, 2 * ROWS_PER_STEP)
    return _matmul(
        name, "nn", (t // tm, N_DEV, 1), [h, wg],
        [pl.BlockSpec((tm, kd), lambda i, j, k: (i, 0)), pl.BlockSpec((None, kd, n8), lambda i, j, k: (j, 0, 0))],
        [(0, 1)], 1, (tm, n8), list(deps), [ANY] * len(deps), [jax.ShapeDtypeStruct((t, N_DEV * n8), out_dtype)],
        [pl.BlockSpec((tm, n8), lambda i, j, k: (i, j))], _store(out_dtype), COL_CHUNK)[0]


def _norm_rows(xv, gain):
    return (xv * lax.rsqrt(jnp.mean(xv * xv, axis=-1, keepdims=True) + EPS) * gain).astype(bf16)


def _mm_rows_res(name, a, w, res, gain, deps=()):
    t, kd = a.shape
    n = w.shape[1]
    tm = min(t, 512)

    def ep(accs, ex, outs, cols):
        xv = ex[0][...] + accs[0]
        outs[0][...] = xv
        outs[1][...] = _norm_rows(xv, ex[1][...])

    row = pl.BlockSpec((tm, n), lambda i, j, k: (i, 0))
    return _matmul(
        name, "nn", (t // tm, 1, 1), [a, w],
        [pl.BlockSpec((tm, kd), lambda i, j, k: (i, 0)), pl.BlockSpec((kd, n), lambda i, j, k: (0, 0))],
        [(0, 1)], 1, (tm, n), [res, gain.reshape(1, n)] + list(deps),
        [row, pl.BlockSpec((1, n), lambda i, j, k: (0, 0))] + [ANY] * len(deps),
        [jax.ShapeDtypeStruct((t, n), f32), jax.ShapeDtypeStruct((t, n), bf16)], [row, row], ep)


def _mm_nt(name, a, w, out_dtype, deps=()):
    t, n = a.shape
    kd = w.shape[0]
    tm, tn = min(t, 2 * ROWS_PER_STEP), min(kd, 1024)
    return _matmul(
        name, "nt", (t // tm, kd // tn, 1), [a, w],
        [pl.BlockSpec((tm, n), lambda i, j, k: (i, 0)), pl.BlockSpec((tn, n), lambda i, j, k: (j, 0))],
        [(0, 1)], 1, (tm, tn), list(deps), [ANY] * len(deps), [jax.ShapeDtypeStruct((t, kd), out_dtype)],
        [pl.BlockSpec((tm, tn), lambda i, j, k: (i, j))], _store(out_dtype), COL_CHUNK)[0]


def _mm_nt_cols(name, da, wg, deps=()):
    t = da.shape[0]
    kd, n8 = wg.shape[1], wg.shape[2]
    tm = min(t, ROWS_PER_STEP)
    return _matmul(
        name, "nt", (t // tm, 1, N_DEV), [da, wg],
        [pl.BlockSpec((tm, n8), lambda i, j, k: (i, k)), pl.BlockSpec((None, kd, n8), lambda i, j, k: (k, 0, 0))],
        [(0, 1)], 1, (tm, kd), list(deps), [ANY] * len(deps), [jax.ShapeDtypeStruct((t, kd), f32)],
        [pl.BlockSpec((tm, kd), lambda i, j, k: (i, 0))], _store(f32))[0]


def _mm_tn_cols(name, h, da):
    t, kd = h.shape
    n8 = da.shape[1] // N_DEV
    tmk, tk = min(kd, 1024), min(t, 2048)
    return _matmul(
        name, "tn", (kd // tmk, N_DEV, t // tk), [h, da],
        [pl.BlockSpec((tk, tmk), lambda i, j, k: (k, i)), pl.BlockSpec((tk, n8), lambda i, j, k: (k, j))],
        [(0, 1)], 1, (tmk, n8), [], [], [jax.ShapeDtypeStruct((N_DEV, kd, n8), bf16)],
        [pl.BlockSpec((None, tmk, n8), lambda i, j, k: (_perm(j), i, 0))], _store(bf16))[0]


def _mm_tn_rows(name, a, dx):
    t, kf = a.shape
    r8 = kf // N_DEV
    n = dx.shape[1]
    return _matmul(
        name, "tn", (N_DEV, 1, 1), [a, dx],
        [pl.BlockSpec((t, r8), lambda i, j, k: (0, i)), pl.BlockSpec((t, n), lambda i, j, k: (0, 0))],
        [(0, 1)], 1, (r8, n), [], [], [jax.ShapeDtypeStruct((N_DEV, r8, n), bf16)],
        [pl.BlockSpec((None, r8, n), lambda i, j, k: (_perm(i), 0, 0))], _store(bf16))[0]


def _ffn_in(h2, wg_t, wu_t, deps=()):
    t, kd = h2.shape
    f8 = wg_t.shape[1]
    tm = min(t, ROWS_PER_STEP)

    def ep(accs, ex, outs, cols):
        g, u = accs
        outs[0][:, cols] = g.astype(bf16)
        outs[1][:, cols] = u.astype(bf16)
        outs[2][:, cols] = (g * jax.nn.sigmoid(g) * u).astype(bf16)

    wspec = pl.BlockSpec((None, f8, kd), lambda i, j, k: (j, 0, 0))
    ospec = pl.BlockSpec((None, tm, f8), lambda i, j, k: (j, i, 0))
    return _matmul(
        "ffn_in", "nt", (t // tm, N_DEV, 1), [h2, wg_t, wu_t],
        [pl.BlockSpec((tm, kd), lambda i, j, k: (i, 0)), wspec, wspec], [(0, 1), (0, 2)], 2, (tm, f8), list(deps),
        [ANY] * len(deps), [jax.ShapeDtypeStruct((N_DEV, t, f8), bf16)] * 3, [ospec] * 3, ep, COL_CHUNK)


def _ffn_down(act, wd, res, gain, deps=()):
    _, t, f8 = act.shape
    n = wd.shape[2]
    tm = min(t, 512)

    def ep(accs, ex, outs, cols):
        xv = ex[0][...] + accs[0]
        outs[0][...] = xv
        if gain is not None:
            outs[1][...] = _norm_rows(xv, ex[1][...])

    row = pl.BlockSpec((tm, n), lambda i, j, k: (i, 0))
    extras, extra_specs = [res], [row]
    shapes, specs = [jax.ShapeDtypeStruct((t, n), f32)], [row]
    if gain is not None:
        extras.append(gain.reshape(1, n))
        extra_specs.append(pl.BlockSpec((1, n), lambda i, j, k: (0, 0)))
        shapes.append(jax.ShapeDtypeStruct((t, n), bf16))
        specs.append(row)
    return _matmul(
        "ffn_down", "nn", (t // tm, 1, N_DEV), [act, wd],
        [pl.BlockSpec((None, tm, f8), lambda i, j, k: (k, i, 0)), pl.BlockSpec((None, f8, n), lambda i, j, k: (k, 0, 0))],
        [(0, 1)], 1, (tm, n), extras + list(deps), extra_specs + [ANY] * len(deps), shapes, specs, ep)


def _ffn_bwd_act(dxb, wd, g, u, deps=()):
    t, n = dxb.shape
    f8 = wd.shape[1]
    tm = min(t, ROWS_PER_STEP)

    def ep(accs, ex, outs, cols):
        dact = accs[0]
        gv = ex[0][:, cols].astype(f32)
        uv = ex[1][:, cols].astype(f32)
        sg = jax.nn.sigmoid(gv)
        silu = gv * sg
        outs[0][:, cols] = (dact * uv * (sg * (1.0 + gv * (1.0 - sg)))).astype(bf16)
        outs[1][:, cols] = (dact * silu).astype(bf16)

    bspec = pl.BlockSpec((None, tm, f8), lambda i, j, k: (j, i, 0))
    return _matmul(
        "ffn_bwd_act", "nt", (t // tm, N_DEV, 1), [dxb, wd],
        [pl.BlockSpec((tm, n), lambda i, j, k: (i, 0)), pl.BlockSpec((None, f8, n), lambda i, j, k: (j, 0, 0))],
        [(0, 1)], 1, (tm, f8), [g, u] + list(deps), [bspec, bspec] + [ANY] * len(deps),
        [jax.ShapeDtypeStruct((N_DEV, t, f8), bf16)] * 2, [bspec] * 2, ep, COL_CHUNK)


def _ffn_dwd(act, dxb):
    _, t, f8 = act.shape
    n = dxb.shape[1]
    tk = min(t, 2048)
    return _matmul(
        "ffn_dwd", "tn", (N_DEV, 1, t // tk), [act, dxb],
        [pl.BlockSpec((None, tk, f8), lambda i, j, k: (i, k, 0)), pl.BlockSpec((tk, n), lambda i, j, k: (k, 0))],
        [(0, 1)], 1, (f8, n), [], [], [jax.ShapeDtypeStruct((N_DEV, f8, n), bf16)],
        [pl.BlockSpec((None, f8, n), lambda i, j, k: (_perm(i), 0, 0))], _store(bf16))[0]


def _ffn_dwgu(h2, dg, du):
    t, kd = h2.shape
    f8 = dg.shape[2]
    tk, tn = min(t, 2048), min(kd, 1024)
    aspec = pl.BlockSpec((None, tk, f8), lambda i, j, k: (i, k, 0))
    ospec = pl.BlockSpec((None, f8, tn), lambda i, j, k: (_perm(i), 0, j))
    return _matmul(
        "ffn_dwgu", "tn", (N_DEV, kd // tn, t // tk), [dg, du, h2],
        [aspec, aspec, pl.BlockSpec((tk, tn), lambda i, j, k: (k, j))], [(0, 2), (1, 2)], 2, (f8, tn), [], [],
        [jax.ShapeDtypeStruct((N_DEV, f8, kd), bf16)] * 2, [ospec] * 2, _store(bf16))


def _ffn_dh(dg, du, wg_t, wu_t, deps=()):
    _, t, f8 = dg.shape
    kd = wg_t.shape[2]
    tm = min(t, ROWS_PER_STEP)
    aspec = pl.BlockSpec((None, tm, f8), lambda i, j, k: (k, i, 0))
    wspec = pl.BlockSpec((None, f8, kd), lambda i, j, k: (k, 0, 0))
    return _matmul(
        "ffn_dh", "nn", (t // tm, 1, N_DEV), [dg, du, wg_t, wu_t], [aspec, aspec, wspec, wspec], [(0, 2), (1, 3)], 1,
        (tm, kd), list(deps), [ANY] * len(deps), [jax.ShapeDtypeStruct((t, kd), f32)],
        [pl.BlockSpec((tm, kd), lambda i, j, k: (i, 0))], _store(f32))[0]


def _rms_fwd(x, g):
    t, d = x.shape
    tm = min(t, 256)

    def body(x_ref, g_ref, o_ref):
        xv = x_ref[...]
        r = lax.rsqrt(jnp.mean(xv * xv, axis=-1, keepdims=True) + EPS)
        o_ref[...] = (xv * r * g_ref[...]).astype(bf16)

    return pl.pallas_call(
        body, grid=(t // tm,), in_specs=[pl.BlockSpec((tm, d), lambda i: (i, 0)), pl.BlockSpec((1, d), lambda i: (0, 0))],
        out_specs=pl.BlockSpec((tm, d), lambda i: (i, 0)), out_shape=jax.ShapeDtypeStruct((t, d), bf16),
        compiler_params=_params("parallel"), name="rms_fwd")(x, g.reshape(1, d))


def _rms_bwd(x, g, dh, dres):
    t, d = x.shape
    tm = min(t, 512)

    def body(x_ref, g_ref, dh_ref, dres_ref, dx_ref, dxb_ref, dg_ref):
        xv = x_ref[...]
        dy = dh_ref[...].astype(f32)
        r = lax.rsqrt(jnp.mean(xv * xv, axis=-1, keepdims=True) + EPS)
        gy = dy * g_ref[...]
        dot = jnp.mean(xv * gy, axis=-1, keepdims=True)
        dx = dres_ref[...] + r * gy - xv * (r * r * r * dot)
        dx_ref[...] = dx
        dxb_ref[...] = dx.astype(bf16)

        @pl.when(pl.program_id(0) == 0)
        def _():
            dg_ref[...] = jnp.zeros_like(dg_ref)

        dg_ref[...] += jnp.sum(dy * xv * r, axis=0, keepdims=True)

    row = pl.BlockSpec((tm, d), lambda i: (i, 0))
    vec = pl.BlockSpec((1, d), lambda i: (0, 0))
    return pl.pallas_call(
        body, grid=(t // tm,), in_specs=[row, vec, row, row], out_specs=[row, row, vec],
        out_shape=[jax.ShapeDtypeStruct((t, d), f32), jax.ShapeDtypeStruct((t, d), bf16), jax.ShapeDtypeStruct((1, d), f32)],
        compiler_params=_params("arbitrary"), name="rms_bwd")(x, g.reshape(1, d), dh, dres)


def _loss_head(x, g, target):
    t, d = x.shape
    tm = min(t, 512)

    def body(x_ref, g_ref, t_ref, loss_ref, dx_ref, dxb_ref, dg_ref):
        xv = x_ref[...]
        r = lax.rsqrt(jnp.mean(xv * xv, axis=-1, keepdims=True) + EPS)
        xn = xv * r
        err = xn * g_ref[...] - t_ref[...]
        dy = err * (1.0 / d)
        gy = dy * g_ref[...]
        dot = jnp.mean(xv * gy, axis=-1, keepdims=True)
        dx = r * gy - xv * (r * r * r * dot)
        dx_ref[...] = dx
        dxb_ref[...] = dx.astype(bf16)

        @pl.when(pl.program_id(0) == 0)
        def _():
            dg_ref[...] = jnp.zeros_like(dg_ref)
            loss_ref[...] = jnp.zeros_like(loss_ref)

        dg_ref[...] += jnp.sum(dy * xn, axis=0, keepdims=True)
        loss_ref[...] += 0.5 * jnp.sum(jnp.sum(err * err, axis=-1, keepdims=True) * (1.0 / d), axis=0, keepdims=True)

    row = pl.BlockSpec((tm, d), lambda i: (i, 0))
    vec = pl.BlockSpec((1, d), lambda i: (0, 0))
    one = pl.BlockSpec((1, 1), lambda i: (0, 0))
    return pl.pallas_call(
        body, grid=(t // tm,), in_specs=[row, vec, row], out_specs=[one, row, row, vec],
        out_shape=[jax.ShapeDtypeStruct((1, 1), f32), jax.ShapeDtypeStruct((t, d), f32),
                   jax.ShapeDtypeStruct((t, d), bf16), jax.ShapeDtypeStruct((1, d), f32)],
        compiler_params=_params("arbitrary"), name="loss_head")(x, g.reshape(1, d), target)


def _attn_consts():
    qt, kw = 2 * CHUNK, (A_LEFT_CHUNKS + 2) * CHUNK
    r = np.arange(qt)[:, None]
    kc = np.arange(kw)[None, :]
    rel = np.clip(r + A_LEFT_CHUNKS * CHUNK - kc, -A_MAX_REL, A_MAX_REL) + A_MAX_REL
    dchunk = kc // CHUNK - r // CHUNK
    valid = (dchunk >= 0) & (dchunk <= A_LEFT_CHUNKS)
    m = np.arange(kw + qt)
    relidx = np.clip(A_LEFT_CHUNKS * CHUNK - (m - (qt - 1)), -A_MAX_REL, A_MAX_REL) + A_MAX_REL
    onehot = np.zeros((kw + qt, 2 * A_MAX_REL + 1), np.float32)
    onehot[m, relidx] = 1.0
    return qt, kw, rel, valid, onehot


def _bias_table(rel_bias):
    qt, kw, _, valid, onehot = _attn_consts()
    h = rel_bias.shape[0]
    w = kw + qt
    relidx = np.argmax(onehot, axis=1)
    e = jnp.roll(jnp.take(rel_bias, jnp.asarray(relidx), axis=1), -(qt - 1), axis=1)
    rows = jnp.broadcast_to(e[:, None, :], (h, qt, w)).reshape(h, qt * w)
    skew = rows[:, :qt * (w - 1)].reshape(h, qt, w - 1)[:, :, :kw]
    return jnp.where(jnp.asarray(valid)[None], skew, NEG_INF).astype(f32)


def _bias_table_grad(dtab):
    qt, kw, _, _, onehot = _attn_consts()
    h = dtab.shape[0]
    w = kw + qt
    wide = -(-(w + qt) // LANE) * LANE
    y = jnp.pad(dtab, ((0, 0), (0, 0), (qt - 1, wide - kw - (qt - 1))))
    flat = jnp.pad(y.reshape(h, qt * wide), ((0, 0), (0, qt)))
    de = jnp.sum(flat.reshape(h, qt, wide + 1), axis=1)[:, :w]
    return jnp.dot(de, jnp.asarray(onehot), precision=lax.Precision.HIGHEST)


def _attn_scores(q_ref, kpad, btab_ref, r0, qt, kw, pad):
    qv = q_ref[pl.ds(r0, qt), :]
    kwin = kpad[pl.ds(r0, kw), :]
    s = lax.dot_general(qv, kwin, _DN["nt"], preferred_element_type=f32) * (A_HEAD_DIM ** -0.5) + btab_ref[...]
    kcol = lax.broadcasted_iota(jnp.int32, (qt, kw), 1)
    s = jnp.where(r0 + kcol >= pad, s, NEG_INF)
    p = jnp.exp(s - jnp.max(s, axis=-1, keepdims=True))
    return qv, kwin, p / jnp.sum(p, axis=-1, keepdims=True)


def _attn_fwd(proj, btab, heads):
    t = proj.shape[0]
    qt, kw = btab.shape[1], btab.shape[2]
    pad = kw - qt

    def body(q_ref, k_ref, v_ref, btab_ref, o_ref, kpad, vpad):
        zeros = jnp.zeros((pad, A_HEAD_DIM), bf16)
        kpad[pl.ds(0, pad), :] = zeros
        vpad[pl.ds(0, pad), :] = zeros
        kpad[pl.ds(pad, t), :] = k_ref[...]
        vpad[pl.ds(pad, t), :] = v_ref[...]

        def tile(i, carry):
            r0 = pl.multiple_of(i * qt, qt)
            _, _, p = _attn_scores(q_ref, kpad, btab_ref, r0, qt, kw, pad)
            o = lax.dot_general(p.astype(bf16), vpad[pl.ds(r0, kw), :], _DN["nn"], preferred_element_type=f32)
            o_ref[pl.ds(r0, qt), :] = o.astype(bf16)
            return carry

        lax.fori_loop(0, t // qt, tile, 0, unroll=4)

    col = lambda off: pl.BlockSpec((t, A_HEAD_DIM), lambda h, off=off: (0, off + h))
    return pl.pallas_call(
        body, grid=(heads,),
        in_specs=[col(0), col(heads), col(2 * heads), pl.BlockSpec((None, qt, kw), lambda h: (h, 0, 0))],
        out_specs=col(0), out_shape=jax.ShapeDtypeStruct((t, heads * A_HEAD_DIM), bf16),
        scratch_shapes=[pltpu.VMEM((t + pad, A_HEAD_DIM), bf16)] * 2,
        compiler_params=_params("parallel"), name="attn_fwd")(proj, proj, proj, btab)


def _attn_bwd(proj, dmix, btab, heads):
    t = proj.shape[0]
    qt, kw = btab.shape[1], btab.shape[2]
    pad = kw - qt
    scale = A_HEAD_DIM ** -0.5

    def body(q_ref, k_ref, v_ref, do_ref, btab_ref, dq_ref, dk_ref, dv_ref, dtab_ref, kpad, vpad, dkacc, dvacc):
        zeros = jnp.zeros((pad, A_HEAD_DIM), bf16)
        kpad[pl.ds(0, pad), :] = zeros
        vpad[pl.ds(0, pad), :] = zeros
        kpad[pl.ds(pad, t), :] = k_ref[...]
        vpad[pl.ds(pad, t), :] = v_ref[...]
        dkacc[...] = jnp.zeros_like(dkacc)
        dvacc[...] = jnp.zeros_like(dvacc)
        dtab_ref[...] = jnp.zeros_like(dtab_ref)

        def tile(i, carry):
            r0 = pl.multiple_of(i * qt, qt)
            qv, kwin, p = _attn_scores(q_ref, kpad, btab_ref, r0, qt, kw, pad)
            dov = do_ref[pl.ds(r0, qt), :]
            dp = lax.dot_general(dov, vpad[pl.ds(r0, kw), :], _DN["nt"], preferred_element_type=f32)
            ds = p * (dp - jnp.sum(p * dp, axis=-1, keepdims=True))
            dtab_ref[...] += ds
            dsb = ds.astype(bf16)
            dq = lax.dot_general(dsb, kwin, _DN["nn"], preferred_element_type=f32) * scale
            dq_ref[pl.ds(r0, qt), :] = dq.astype(bf16)
            dkacc[pl.ds(r0, kw), :] += lax.dot_general(dsb, qv, _DN["tn"], preferred_element_type=f32) * scale
            dvacc[pl.ds(r0, kw), :] += lax.dot_general(p.astype(bf16), dov, _DN["tn"], preferred_element_type=f32)
            return carry

        lax.fori_loop(0, t // qt, tile, 0, unroll=4)
        dk_ref[...] = dkacc[pl.ds(pad, t), :].astype(bf16)
        dv_ref[...] = dvacc[pl.ds(pad, t), :].astype(bf16)

    col = lambda off: pl.BlockSpec((t, A_HEAD_DIM), lambda h, off=off: (0, off + h))
    tab = pl.BlockSpec((None, qt, kw), lambda h: (h, 0, 0))
    wide = jax.ShapeDtypeStruct((t, heads * A_HEAD_DIM), bf16)
    return pl.pallas_call(
        body, grid=(heads,), in_specs=[col(0), col(heads), col(2 * heads), col(0), tab],
        out_specs=[col(0), col(0), col(0), tab],
        out_shape=[wide, wide, wide, jax.ShapeDtypeStruct((heads, qt, kw), f32)],
        scratch_shapes=[pltpu.VMEM((t + pad, A_HEAD_DIM), bf16)] * 2 + [pltpu.VMEM((t + pad, A_HEAD_DIM), f32)] * 2,
        compiler_params=_params("parallel"), name="attn_bwd")(proj, proj, proj, dmix, btab)


def _shift_down(z, k):
    rows = lax.broadcasted_iota(jnp.int32, z.shape, 0)
    return jnp.where(rows >= k, pltpu.roll(z, k, 0), 0.0)


def _shift_up(z, k):
    t = z.shape[0]
    rows = lax.broadcasted_iota(jnp.int32, z.shape, 0)
    return jnp.where(rows < t - k, pltpu.roll(z, t - k, 0), 0.0)


def _conv_fwd(proj, conv_w, a_blocks, b_blocks):
    t = proj.shape[0]

    def body(b_ref, c_ref, h_ref, w_ref, o_ref):
        z = c_ref[...].astype(f32) * h_ref[...].astype(f32)
        w = w_ref[...]
        y = w[0:1, :] * _shift_down(z, 2) + w[1:2, :] * _shift_down(z, 1) + w[2:3, :] * z
        o_ref[...] = (b_ref[...].astype(f32) * y).astype(bf16)

    col = lambda off: pl.BlockSpec((t, LANE), lambda i, off=off: (0, off + i))
    return pl.pallas_call(
        body, grid=(b_blocks,),
        in_specs=[col(3 * a_blocks), col(3 * a_blocks + b_blocks), col(3 * a_blocks + 2 * b_blocks),
                  pl.BlockSpec((CONV_WIDTH, LANE), lambda i: (0, i))],
        out_specs=col(0), out_shape=jax.ShapeDtypeStruct((t, b_blocks * LANE), bf16),
        compiler_params=_params("parallel"), name="conv_fwd")(proj, proj, proj, conv_w)


def _conv_bwd(proj, dmix, conv_w, a_blocks, b_blocks):
    t = proj.shape[0]

    def body(b_ref, c_ref, h_ref, do_ref, w_ref, db_ref, dc_ref, dh_ref, dw_ref):
        bv, cv, hv = b_ref[...].astype(f32), c_ref[...].astype(f32), h_ref[...].astype(f32)
        w = w_ref[...]
        z = cv * hv
        z1, z2 = _shift_down(z, 1), _shift_down(z, 2)
        y = w[0:1, :] * z2 + w[1:2, :] * z1 + w[2:3, :] * z
        dov = do_ref[...].astype(f32)
        db_ref[...] = (dov * y).astype(bf16)
        dy = dov * bv
        dz = w[2:3, :] * dy + w[1:2, :] * _shift_up(dy, 1) + w[0:1, :] * _shift_up(dy, 2)
        dc_ref[...] = (dz * hv).astype(bf16)
        dh_ref[...] = (dz * cv).astype(bf16)
        dw_ref[0:1, :] = jnp.sum(dy * z2, axis=0, keepdims=True)
        dw_ref[1:2, :] = jnp.sum(dy * z1, axis=0, keepdims=True)
        dw_ref[2:3, :] = jnp.sum(dy * z, axis=0, keepdims=True)

    col = lambda off: pl.BlockSpec((t, LANE), lambda i, off=off: (0, off + i))
    wspec = pl.BlockSpec((CONV_WIDTH, LANE), lambda i: (0, i))
    wide = jax.ShapeDtypeStruct((t, b_blocks * LANE), bf16)
    return pl.pallas_call(
        body, grid=(b_blocks,),
        in_specs=[col(3 * a_blocks), col(3 * a_blocks + b_blocks), col(3 * a_blocks + 2 * b_blocks), col(a_blocks), wspec],
        out_specs=[col(0), col(0), col(0), wspec],
        out_shape=[wide, wide, wide, jax.ShapeDtypeStruct((CONV_WIDTH, b_blocks * LANE), f32)],
        compiler_params=_params("parallel"), name="conv_bwd")(proj, proj, proj, dmix, conv_w)


_RSQRT2 = 0.7071067811865476
_RSQRT2PI = 0.3989422804014327


def _gelu(x):
    return 0.5 * x * (1.0 + lax.erf(x * _RSQRT2))


def _gelu_grad(x):
    return 0.5 * (1.0 + lax.erf(x * _RSQRT2)) + x * jnp.exp(-0.5 * x * x) * _RSQRT2PI


def _sgu_common(a_ref, lg_ref, lb_ref, cw):
    av = a_ref[...]
    u = _gelu(av[:, :cw])
    v = _gelu(av[:, cw:])
    mu = jnp.mean(v, axis=-1, keepdims=True)
    xc = v - mu
    rstd = lax.rsqrt(jnp.mean(xc * xc, axis=-1, keepdims=True) + EPS)
    xhat = xc * rstd
    vln = xhat * lg_ref[...] + lb_ref[...]
    pos_t = lax.broadcasted_iota(jnp.int32, (C_BLOCK, C_BLOCK), 0) // CHUNK
    pos_s = lax.broadcasted_iota(jnp.int32, (C_BLOCK, C_BLOCK), 1) // CHUNK
    return av, u, xhat, rstd, vln, pos_s <= pos_t


def _sgu_fwd(a, ln_g, ln_b, w_s, bs_t):
    t, cw2 = a.shape
    cw = cw2 // 2
    groups = w_s.shape[0]
    cg = cw // groups

    def body(a_ref, lg_ref, lb_ref, ws_ref, bs_ref, m_ref):
        _, u, _, _, vln, mask = _sgu_common(a_ref, lg_ref, lb_ref, cw)
        vb = vln.astype(bf16)
        for g in range(groups):
            sl = slice(g * cg, (g + 1) * cg)
            wm = jnp.where(mask, ws_ref[g], 0.0).astype(bf16)
            s = lax.dot_general(wm, vb[:, sl], _DN["nn"], preferred_element_type=f32) + bs_ref[:, g:g + 1]
            m_ref[:, sl] = (u[:, sl] * s).astype(bf16)

    vec = pl.BlockSpec((1, cw), lambda n: (0, 0))
    return pl.pallas_call(
        body, grid=(t // C_BLOCK,),
        in_specs=[pl.BlockSpec((C_BLOCK, cw2), lambda n: (n, 0)), vec, vec,
                  pl.BlockSpec((groups, C_BLOCK, C_BLOCK), lambda n: (0, 0, 0)),
                  pl.BlockSpec((C_BLOCK, groups), lambda n: (0, 0))],
        out_specs=pl.BlockSpec((C_BLOCK, cw), lambda n: (n, 0)), out_shape=jax.ShapeDtypeStruct((t, cw), bf16),
        compiler_params=_params("parallel"), name="sgu_fwd")(a, ln_g.reshape(1, cw), ln_b.reshape(1, cw), w_s, bs_t)


def _sgu_bwd(a, dm, ln_g, ln_b, w_s, bs_t):
    t, cw2 = a.shape
    cw = cw2 // 2
    groups = w_s.shape[0]
    cg = cw // groups

    def body(a_ref, dm_ref, lg_ref, lb_ref, ws_ref, bs_ref, da_ref, dws_ref, dbs_ref, dlg_ref, dlb_ref, dvln):
        @pl.when(pl.program_id(0) == 0)
        def _():
            dws_ref[...] = jnp.zeros_like(dws_ref)
            dbs_ref[...] = jnp.zeros_like(dbs_ref)
            dlg_ref[...] = jnp.zeros_like(dlg_ref)
            dlb_ref[...] = jnp.zeros_like(dlb_ref)

        av, u, xhat, rstd, vln, mask = _sgu_common(a_ref, lg_ref, lb_ref, cw)
        vb = vln.astype(bf16)
        lane = lax.broadcasted_iota(jnp.int32, (C_BLOCK, groups), 1)
        dbs = jnp.zeros((C_BLOCK, groups), f32)
        for g in range(groups):
            sl = slice(g * cg, (g + 1) * cg)
            wm = jnp.where(mask, ws_ref[g], 0.0).astype(bf16)
            s = lax.dot_general(wm, vb[:, sl], _DN["nn"], preferred_element_type=f32) + bs_ref[:, g:g + 1]
            dmg = dm_ref[:, sl].astype(f32)
            da_ref[:, sl] = (dmg * s * _gelu_grad(av[:, sl])).astype(bf16)
            dsg = dmg * u[:, sl]
            dbs = dbs + jnp.where(lane == g, jnp.sum(dsg, axis=-1, keepdims=True), 0.0)
            dsb = dsg.astype(bf16)
            dws_ref[g] += jnp.where(mask, lax.dot_general(dsb, vb[:, sl], _DN["nt"], preferred_element_type=f32), 0.0)
            dvln[:, sl] = lax.dot_general(wm, dsb, _DN["tn"], preferred_element_type=f32)
        dbs_ref[...] += dbs
        dv = dvln[...]
        dlg_ref[...] += jnp.sum(dv * xhat, axis=0, keepdims=True)
        dlb_ref[...] += jnp.sum(dv, axis=0, keepdims=True)
        dxh = dv * lg_ref[...]
        dvv = rstd * (dxh - jnp.mean(dxh, axis=-1, keepdims=True) - xhat * jnp.mean(dxh * xhat, axis=-1, keepdims=True))
        da_ref[:, cw:] = (dvv * _gelu_grad(av[:, cw:])).astype(bf16)

    vec = pl.BlockSpec((1, cw), lambda n: (0, 0))
    wsp = pl.BlockSpec((groups, C_BLOCK, C_BLOCK), lambda n: (0, 0, 0))
    bsp = pl.BlockSpec((C_BLOCK, groups), lambda n: (0, 0))
    return pl.pallas_call(
        body, grid=(t // C_BLOCK,),
        in_specs=[pl.BlockSpec((C_BLOCK, cw2), lambda n: (n, 0)), pl.BlockSpec((C_BLOCK, cw), lambda n: (n, 0)), vec, vec, wsp, bsp],
        out_specs=[pl.BlockSpec((C_BLOCK, cw2), lambda n: (n, 0)), wsp, bsp, vec, vec],
        out_shape=[jax.ShapeDtypeStruct((t, cw2), bf16), jax.ShapeDtypeStruct(w_s.shape, f32),
                   jax.ShapeDtypeStruct(bs_t.shape, f32), jax.ShapeDtypeStruct((1, cw), f32), jax.ShapeDtypeStruct((1, cw), f32)],
        scratch_shapes=[pltpu.VMEM((C_BLOCK, cw), f32)],
        compiler_params=_params("arbitrary"), name="sgu_bwd")(a, dm, ln_g.reshape(1, cw), ln_b.reshape(1, cw), w_s, bs_t)


HBM = pl.BlockSpec(memory_space=pltpu.HBM)
SEM = pl.BlockSpec(memory_space=pltpu.SEMAPHORE)
EFFECT = pltpu.SideEffectType.DATAFLOW_SIDE_EFFECTING


def _place():
    x, y, c = lax.axis_index("x"), lax.axis_index("y"), lax.axis_index("c")
    return x, y, c, [(1 - x, y), (x, 1 - y), (1 - x, 1 - y)]


def _remote(src, dst, send_sems, recv_sems, k, to):
    return pltpu.make_async_remote_copy(src_ref=src, dst_ref=dst, send_sem=send_sems.at[k], recv_sem=recv_sems.at[k],
                                        device_id=to, device_id_type=MESH)


def _split_start(name, arrays, plan, n_copies, after):
    n = len(arrays)

    def body(*refs):
        send_sems, recv_sems, token = refs[n + 1], refs[n + 2], refs[-1]
        for cp in plan(refs[:n], send_sems, recv_sems):
            cp.start()
        token[...] = jnp.zeros_like(token)

    out = pl.pallas_call(
        body, name=name,
        out_shape=(pltpu.SemaphoreType.DMA((n_copies,)), pltpu.SemaphoreType.DMA((n_copies,)),
                   *[pltpu.HBM(a.shape, a.dtype) for a in arrays], jax.ShapeDtypeStruct((8, LANE), f32)),
        in_specs=[HBM] * n + [ANY], out_specs=(SEM, SEM, *[HBM] * n, pl.BlockSpec(memory_space=pltpu.VMEM)),
        input_output_aliases={i: 2 + i for i in range(n)},
        compiler_params=pltpu.CompilerParams(has_side_effects=EFFECT),
    )(*[pltpu.with_memory_space_constraint(a, pltpu.HBM) for a in arrays], after)
    return (out[0], out[1]), list(out[2:2 + n]), out[-1]


def _split_wait(name, arrays, sems, plan, after):
    n = len(arrays)

    def body(*refs):
        for cp in plan(refs[:n], refs[n], refs[n + 1]):
            cp.wait()

    out = pl.pallas_call(
        body, name=name, out_shape=tuple(pltpu.HBM(a.shape, a.dtype) for a in arrays),
        in_specs=[HBM] * n + [SEM, SEM, ANY], out_specs=tuple([HBM] * n), input_output_aliases={i: i for i in range(n)},
        compiler_params=pltpu.CompilerParams(has_side_effects=EFFECT),
    )(*arrays, sems[0], sems[1], after)
    return list(out)


def _row_pieces(ref_rows, split):
    rc = ref_rows // split
    return [pl.ds(s * rc, rc) for s in range(split)]


def _gather_slots():
    x, y, c, _ = _place()
    slots = (4 * x + 2 * y + c, 4 * (1 - x) + 2 * y + c, 4 * x + 2 * (1 - y) + c, 4 * (1 - x) + 2 * (1 - y) + c)
    return slots, (x, y, 1 - c), (1 - x, y, c), (x, 1 - y, c)


def _to_sibling(cps, k, b, slot, sibling, send_sems, recv_sems):
    for rows in _row_pieces(b.shape[1], D2D_SPLIT):
        cps.append(_remote(b.at[slot, rows], b.at[slot, rows], send_sems, recv_sems, k, sibling))
        k += 1
    return k


def _plan_gather_near(bufs, send_sems, recv_sems):
    (me, _, _, _), sibling, x_peer, y_peer = _gather_slots()
    cps, k = [], 0
    for b in bufs:
        k = _to_sibling(cps, k, b, me, sibling, send_sems, recv_sems)
        for peer in (x_peer, y_peer):
            cps.append(_remote(b.at[me], b.at[me], send_sems, recv_sems, k, peer))
            k += 1
    return cps


def _plan_gather_relay(bufs, send_sems, recv_sems):
    (_, x_slot, y_slot, _), sibling, x_peer, y_peer = _gather_slots()
    cps, k = [], 0
    for b in bufs:
        half = b.shape[1] // 2
        k = _to_sibling(cps, k, b, x_slot, sibling, send_sems, recv_sems)
        k = _to_sibling(cps, k, b, y_slot, sibling, send_sems, recv_sems)
        lower, upper = pl.ds(0, half), pl.ds(half, half)
        cps.append(_remote(b.at[x_slot, lower], b.at[x_slot, lower], send_sems, recv_sems, k, y_peer))
        cps.append(_remote(b.at[y_slot, upper], b.at[y_slot, upper], send_sems, recv_sems, k + 1, x_peer))
        k += 2
    return cps


def _plan_gather_far(bufs, send_sems, recv_sems):
    (_, _, _, far), sibling, _, _ = _gather_slots()
    cps, k = [], 0
    for b in bufs:
        k = _to_sibling(cps, k, b, far, sibling, send_sems, recv_sems)
    return cps


def _plan_rs_sibling(arrs, send_sems, recv_sems):
    n = len(arrs) // 2
    x, y, c, _ = _place()
    cps, k = [], 0
    for g, got in zip(arrs[:n], arrs[n:]):
        for q in range(N_CHIP):
            for rows in _row_pieces(g.shape[1], RS_SPLIT):
                cps.append(_remote(g.at[N_CHIP * (1 - c) + q, rows], got.at[q, rows], send_sems, recv_sems, k, (x, y, 1 - c)))
                k += 1
    return cps


def _plan_rs_chips(arrs, send_sems, recv_sems):
    n = len(arrs) // 2
    x, y, c, chips = _place()
    q = 2 * x + y
    cps, k = [], 0
    for p, r in zip(arrs[:n], arrs[n:]):
        for px, py in chips:
            cps.append(_remote(p.at[2 * px + py], r.at[q], send_sems, recv_sems, k, (px, py, c)))
            k += 1
    return cps


class _Gather:
    STAGES = (("near", _plan_gather_near, D2D_SPLIT + 2), ("relay", _plan_gather_relay, 2 * D2D_SPLIT + 2),
              ("far", _plan_gather_far, D2D_SPLIT))

    def __init__(self, tag, bufs):
        self.tag, self.bufs, self.stage = tag, bufs, -1

    def advance(self, after):
        if self.stage >= 0:
            name, plan, _ = self.STAGES[self.stage]
            self.bufs = _split_wait("gather_%s_wait_%s" % (name, self.tag), self.bufs, self.sems, plan, after)
        self.stage += 1
        if self.stage == len(self.STAGES):
            return self.bufs
        name, plan, per_array = self.STAGES[self.stage]
        self.sems, self.bufs, token = _split_start("gather_%s_start_%s" % (name, self.tag), self.bufs, plan,
                                                   per_array * len(self.bufs), after)
        return token

    def finish(self, after):
        out = self.advance(after)
        while not isinstance(out, list):
            out = self.advance(after)
        return out


class _ReduceScatter:
    def __init__(self, tag, grads, core):
        self.tag, self.n = tag, len(grads)
        lands = [lax.empty((N_CHIP,) + g.shape[1:], g.dtype) for g in grads]
        self.sems, self.arrs, self.token = _split_start("rs_sibling_start_" + tag, list(grads) + lands, _plan_rs_sibling,
                                                        self.n * N_CHIP * RS_SPLIT, core)

    def middle(self, after, core):
        arrs = _split_wait("rs_sibling_wait_" + self.tag, self.arrs, self.sems, _plan_rs_sibling, after)
        parts = [_pair_sum(g, got, core) for g, got in zip(arrs[:self.n], arrs[self.n:])]
        lands = [lax.empty(p.shape, p.dtype) for p in parts]
        self.sems, self.arrs, self.token = _split_start("rs_chips_start_" + self.tag, parts + lands, _plan_rs_chips,
                                                        self.n * 3, core)

    def finish(self, after):
        arrs = _split_wait("rs_chips_wait_" + self.tag, self.arrs, self.sems, _plan_rs_chips, after)
        return list(zip(arrs[:self.n], arrs[self.n:]))


def _cast_into_slot(name, w, layer, me, after, dtype=bf16):
    _, rows, cols = w.shape
    tr = 256 if rows % 256 == 0 else rows

    def body(me_ref, w_ref, after_ref, o_ref):
        o_ref[...] = w_ref[...].astype(dtype)

    return pl.pallas_call(
        body,
        grid_spec=pltpu.PrefetchScalarGridSpec(
            num_scalar_prefetch=1, grid=(rows // tr,),
            in_specs=[pl.BlockSpec((None, tr, cols), lambda i, me_ref: (layer, i, 0)), ANY],
            out_specs=pl.BlockSpec((None, tr, cols), lambda i, me_ref: (me_ref[0], i, 0))),
        out_shape=jax.ShapeDtypeStruct((N_DEV, rows, cols), dtype), compiler_params=_params("parallel"), name=name)(me, w, after)


def _pair_sum(g, got, core):
    _, rows, cols = g.shape
    tr = rows

    def body(c_ref, a_ref, b_ref, o_ref):
        o_ref[...] = (a_ref[...].astype(f32) + b_ref[...].astype(f32)).astype(bf16)

    spec = pl.BlockSpec((None, tr, cols), lambda q, i, c_ref: (q, i, 0))
    return pl.pallas_call(
        body,
        grid_spec=pltpu.PrefetchScalarGridSpec(
            num_scalar_prefetch=1, grid=(N_CHIP, rows // tr),
            in_specs=[pl.BlockSpec((None, tr, cols), lambda q, i, c_ref: (N_CHIP * c_ref[0] + q, i, 0)), spec],
            out_specs=spec),
        out_shape=jax.ShapeDtypeStruct((N_CHIP, rows, cols), bf16), compiler_params=_params("parallel", "parallel"),
        name="pair_sum")(core, g, got)


def _gather_copies(n, ins, outs, send_sems, recv_sems, local_sems):
    x, y, c, chips = _place()
    sibling = (x, y, 1 - c)

    def slot(px, py, pc):
        return 4 * px + 2 * py + pc

    def copy(i, k, block, to, src=None):
        dst = outs[i].at[slot(*block)]
        return pltpu.make_async_remote_copy(src_ref=dst if src is None else src, dst_ref=dst, send_sem=send_sems.at[i, k],
                                            recv_sem=recv_sems.at[i, k], device_id=to, device_id_type=MESH)

    started = []
    for i in range(n):
        mine = pltpu.make_async_copy(ins[i], outs[i].at[slot(x, y, c)], local_sems.at[i])
        mine.start()
        started.append(mine)
    sends = []
    for i in range(n):
        sends.append(copy(i, 0, (x, y, c), sibling, src=ins[i]))
        sends += [copy(i, 1 + j, (x, y, c), (*chip, c), src=ins[i]) for j, chip in enumerate(chips)]
    for cp in sends:
        cp.start()
    for i in range(n):
        for j, chip in enumerate(chips):
            copy(i, 1 + j, (*chip, c), (x, y, c)).wait_recv()
            fwd = copy(i, 4 + j, (*chip, c), sibling)
            fwd.start()
            sends.append(fwd)
    for i in range(n):
        copy(i, 0, sibling, (x, y, c)).wait_recv()
        for j, chip in enumerate(chips):
            copy(i, 4 + j, (*chip, 1 - c), (x, y, c)).wait_recv()
    for cp in sends:
        cp.wait_send()
    for mine in started:
        mine.wait()


def _gather_small(name, packed):
    rows = packed.shape[0]

    def body(x_ref, o_ref, buf, send_sems, recv_sems, local_sems):
        _gather_copies(1, [x_ref], [buf], send_sems, recv_sems, local_sems)
        o_ref[...] = buf[...]

    vm = pl.BlockSpec(memory_space=pltpu.VMEM)
    return pl.pallas_call(
        body, in_specs=[vm], out_specs=vm, out_shape=jax.ShapeDtypeStruct((N_DEV, rows, LANE), f32),
        scratch_shapes=[pltpu.VMEM((N_DEV, rows, LANE), f32), pltpu.SemaphoreType.DMA((1, 7)), pltpu.SemaphoreType.DMA((1, 7)),
                        pltpu.SemaphoreType.DMA((1,))],
        compiler_params=pltpu.CompilerParams(vmem_limit_bytes=VMEM_LIMIT), name=name)(packed)


def _plan_all_peers(bufs, send_sems, recv_sems):
    x, y, c, _ = _place()
    me = 4 * x + 2 * y + c
    peers = [(px, py, pc) for px in (x, 1 - x) for py in (y, 1 - y) for pc in (c, 1 - c)][1:]
    cps, k = [], 0
    for b in bufs:
        for peer in peers:
            cps.append(_remote(b.at[me], b.at[me], send_sems, recv_sems, k, peer))
            k += 1
    return cps


def _sum_slots(buf):
    n, rows, cols = buf.shape

    def body(b_ref, o_ref):
        acc = b_ref[0]
        for j in range(1, n):
            acc = acc + b_ref[j]
        o_ref[...] = acc

    return pl.pallas_call(body, out_shape=jax.ShapeDtypeStruct((rows, cols), buf.dtype),
                          compiler_params=pltpu.CompilerParams(vmem_limit_bytes=VMEM_LIMIT), name="sum_slots")(buf)


def _pack(arrs):
    flat = jnp.concatenate([a.reshape(-1).astype(f32) for a in arrs])
    rows = -(-flat.shape[0] // (8 * LANE)) * 8
    return jnp.pad(flat, (0, rows * LANE - flat.shape[0])).reshape(rows, LANE)


def _unpack(buf, shapes):
    flat = buf.reshape(-1)
    out, off = [], 0
    for s in shapes:
        n = int(np.prod(s))
        out.append(flat[off:off + n].reshape(s))
        off += n
    return out


def _adam_math(w, g, m, v):
    m2 = ADAM_B1 * m + (1.0 - ADAM_B1) * g
    v2 = ADAM_B2 * v + (1.0 - ADAM_B2) * (g * g)
    m_hat = m2 / (1.0 - ADAM_B1 ** ADAM_STEP)
    v_hat = v2 / (1.0 - ADAM_B2 ** ADAM_STEP)
    delta = -ADAM_LR * (m_hat / (jnp.sqrt(v_hat) + ADAM_EPS) + ADAM_WD * w)
    return delta, m2, v2


def _adam_big(name, w, m, v, parts, chip, after):
    layers, rows, cols = w.shape
    tr = 512 if rows % 512 == 0 else 256 if rows % 256 == 0 else rows // 4 if rows % 32 == 0 else 8

    def body(chip_ref, w_ref, m_ref, v_ref, *rest):
        p_refs = rest[:N_CHIP * layers]
        g_ref, d_ref, m2_ref, v2_ref = rest[N_CHIP * layers + 1:]
        for li in range(layers):
            @pl.when(pl.program_id(0) == li)
            def _(li=li):
                g = p_refs[N_CHIP * li][...].astype(f32)
                for q in range(1, N_CHIP):
                    g = g + p_refs[N_CHIP * li + q][...].astype(f32)
                delta, m2, v2 = _adam_math(w_ref[...], g, m_ref[...], v_ref[...])
                g_ref[...] = g
                d_ref[...] = delta
                m2_ref[...] = m2
                v2_ref[...] = v2

    spec = pl.BlockSpec((None, tr, cols), lambda l, i, c_ref: (l, i, 0))
    pspecs, operands = [], []
    for li in range(layers):
        for q in range(N_CHIP):
            pspecs.append(pl.BlockSpec((None, tr, cols),
                                       lambda l, i, c_ref, li=li, q=q: ((c_ref[0] + q) % N_CHIP, jnp.where(l == li, i, 0), 0)))
            operands.append(parts[li][0] if q == 0 else parts[li][1])
    out = jax.ShapeDtypeStruct((layers, rows, cols), f32)
    return pl.pallas_call(
        body,
        grid_spec=pltpu.PrefetchScalarGridSpec(num_scalar_prefetch=1, grid=(layers, rows // tr),
                                               in_specs=[spec, spec, spec] + pspecs + [ANY], out_specs=[spec] * 4),
        out_shape=[out] * 4, compiler_params=_params("arbitrary", "arbitrary"), name=name)(chip, w, m, v, *operands, after)


def _adam_small(w, g, m, v):
    rows = w.shape[0]

    def body(w_ref, g_ref, m_ref, v_ref, d_ref, m2_ref, v2_ref):
        delta, m2, v2 = _adam_math(w_ref[...], g_ref[...], m_ref[...], v_ref[...])
        d_ref[...] = delta
        m2_ref[...] = m2
        v2_ref[...] = v2

    out = jax.ShapeDtypeStruct((rows, LANE), f32)
    return pl.pallas_call(body, out_shape=[out] * 3, name="adam_small")(w, g, m, v)


def kernel(x, mix_norm, ab_w_in, ab_rel_bias, ab_conv_w, ab_w_out, c_w_in, c_ln_g, c_ln_b, c_w_s, c_b_s, c_w_out, ffn_norm, ffn_w_gate, ffn_w_up, ffn_w_down, final_norm, loss_target, m_mix_norm, m_ab_w_in, m_ab_rel_bias, m_ab_conv_w, m_ab_w_out, m_c_w_in, m_c_ln_g, m_c_ln_b, m_c_w_s, m_c_b_s, m_c_w_out, m_ffn_norm, m_ffn_w_gate, m_ffn_w_up, m_ffn_w_down, m_final_norm, v_mix_norm, v_ab_w_in, v_ab_rel_bias, v_ab_conv_w, v_ab_w_out, v_c_w_in, v_c_ln_g, v_c_ln_b, v_c_w_s, v_c_b_s, v_c_w_out, v_ffn_norm, v_ffn_w_gate, v_ffn_w_up, v_ffn_w_down, v_final_norm):
    d = D_MODEL
    a_width = d // 2
    heads = a_width // A_HEAD_DIM
    a_blocks = a_width // LANE
    b_blocks = (d - a_width) // LANE
    n_even, n_odd = (DEPTH + 1) // 2, DEPTH // 2
    me_s = 4 * lax.axis_index("x") + 2 * lax.axis_index("y") + lax.axis_index("c")
    me = me_s.astype(jnp.int32).reshape(1)
    core = lax.axis_index("c").astype(jnp.int32).reshape(1)
    chip = (2 * lax.axis_index("x") + lax.axis_index("y")).astype(jnp.int32).reshape(1)

    weights = dict(mix_norm=mix_norm, ab_w_in=ab_w_in, ab_rel_bias=ab_rel_bias, ab_conv_w=ab_conv_w, ab_w_out=ab_w_out,
                   c_w_in=c_w_in, c_ln_g=c_ln_g, c_ln_b=c_ln_b, c_w_s=c_w_s, c_b_s=c_b_s, c_w_out=c_w_out,
                   ffn_norm=ffn_norm, ffn_w_gate=ffn_w_gate, ffn_w_up=ffn_w_up, ffn_w_down=ffn_w_down, final_norm=final_norm)
    mom_m = dict(mix_norm=m_mix_norm, ab_w_in=m_ab_w_in, ab_rel_bias=m_ab_rel_bias, ab_conv_w=m_ab_conv_w, ab_w_out=m_ab_w_out,
                 c_w_in=m_c_w_in, c_ln_g=m_c_ln_g, c_ln_b=m_c_ln_b, c_w_s=m_c_w_s, c_b_s=m_c_b_s, c_w_out=m_c_w_out,
                 ffn_norm=m_ffn_norm, ffn_w_gate=m_ffn_w_gate, ffn_w_up=m_ffn_w_up, ffn_w_down=m_ffn_w_down, final_norm=m_final_norm)
    mom_v = dict(mix_norm=v_mix_norm, ab_w_in=v_ab_w_in, ab_rel_bias=v_ab_rel_bias, ab_conv_w=v_ab_conv_w, ab_w_out=v_ab_w_out,
                 c_w_in=v_c_w_in, c_ln_g=v_c_ln_g, c_ln_b=v_c_ln_b, c_w_s=v_c_w_s, c_b_s=v_c_b_s, c_w_out=v_c_w_out,
                 ffn_norm=v_ffn_norm, ffn_w_gate=v_ffn_w_gate, ffn_w_up=v_ffn_w_up, ffn_w_down=v_ffn_w_down, final_norm=v_final_norm)
    order = list(weights)
    wide = ("ffn_w_gate", "ffn_w_up")
    flip = lambda a: jnp.swapaxes(a, 1, 2)
    local = {k: (flip(weights[k]), flip(mom_m[k]), flip(mom_v[k])) if k in wide else (weights[k], mom_m[k], mom_v[k]) for k in order}

    sharded_small = [ab_conv_w, c_ln_g, c_ln_b]
    gathered = _gather_small("gather_small", _pack(sharded_small))
    conv_parts, lng_parts, lnb_parts = [], [], []
    for j in range(N_DEV):
        cw_j, lg_j, lb_j = _unpack(gathered[j], [a.shape for a in sharded_small])
        conv_parts.append(cw_j)
        lng_parts.append(lg_j)
        lnb_parts.append(lb_j)
    conv_full = jnp.concatenate(conv_parts, axis=-1)
    lng_full = jnp.concatenate(lng_parts, axis=-1)
    lnb_full = jnp.concatenate(lnb_parts, axis=-1)

    gate_t, up_t = local["ffn_w_gate"][0], local["ffn_w_up"][0]
    sets = []
    for layer in range(DEPTH):
        i = layer // 2
        if layer % 2 == 0:
            sets += [("ab_in%d" % i, [(ab_w_in, i)]), ("ab_out%d" % i, [(ab_w_out, i)])]
        else:
            sets += [("c_in%d" % i, [(c_w_in, i)]), ("c_out%d" % i, [(c_w_out, i)])]
        sets += [("ffn_in%d" % layer, [(gate_t, layer), (up_t, layer)]), ("ffn_out%d" % layer, [(ffn_w_down, layer)])]
    units = [None] * len(sets)
    cursor = [0]
    tokens = []

    def start_gather(k, after):
        if k < len(sets):
            tag, members = sets[k]
            units[k] = _Gather(tag, [_cast_into_slot("cast_slot", w, li, me, after) for w, li in members])
            tokens.append(units[k].advance(after))

    def next_weights(after):
        k = cursor[0]
        cursor[0] = k + 1
        ready = units[k].finish(after)
        for later in range(k + 1, min(k + len(_Gather.STAGES), len(sets)) if k else 2):
            tokens.append(units[later].advance(after))
        start_gather(k + GATHER_AHEAD, after)
        return ready

    def started():
        deps = list(tokens)
        tokens.clear()
        return deps

    xs = x[0]
    tgt = loss_target[0]
    start_gather(0, gathered)
    start_gather(1, gathered)
    tokens.append(units[0].advance(tokens[-1]))
    for k in range(2, GATHER_AHEAD):
        start_gather(k, tokens[-1])
    tokens.append(units[0].advance(tokens[-1]))
    saved = []
    h = _rms_fwd(xs, mix_norm[0])
    for layer in range(DEPTH):
        i = layer // 2
        (w_in_g,) = next_weights(xs)
        if layer % 2 == 0:
            proj = _mm_cols("ab_proj", h, w_in_g, bf16, started())
            btab = _bias_table(ab_rel_bias[i])
            attn = _attn_fwd(proj, btab, heads)
            conv = _conv_fwd(proj, conv_full[i], a_blocks, b_blocks)
            mixed = jnp.concatenate([attn, conv], axis=-1)
            ctx = (proj, btab)
        else:
            proj = _mm_cols("c_proj", h, w_in_g, f32, started())
            bs_t = jnp.transpose(c_b_s[i])
            mixed = _sgu_fwd(proj, lng_full[i], lnb_full[i], c_w_s[i], bs_t)
            ctx = (proj, bs_t)
        (w_out_g,) = next_weights(mixed)
        w_out_full = w_out_g.reshape(-1, w_out_g.shape[-1])
        x1, h2 = _mm_rows_res("mix_out", mixed, w_out_full, xs, ffn_norm[layer], started())
        wg_g, wu_g = next_weights(x1)
        g_act, u_act, act = _ffn_in(h2, wg_g, wu_g, started())
        (wd_g,) = next_weights(g_act)
        nxt = _ffn_down(act, wd_g, x1, mix_norm[layer + 1] if layer + 1 < DEPTH else None, started())
        saved.append((xs, h, ctx, mixed, x1, h2, g_act, u_act, act, w_in_g, w_out_full, wg_g, wu_g, wd_g))
        xs, h = nxt if len(nxt) == 2 else (nxt[0], None)

    loss_part, dx, dxb, d_final = _loss_head(xs, final_norm, tgt)
    loss = lax.psum(loss_part[0, 0], ("x", "y", "c"))

    scatters = {}
    small = {k: [None] * weights[k].shape[0] for k in ("mix_norm", "ffn_norm", "ab_rel_bias", "ab_conv_w", "c_ln_g", "c_ln_b",
                                                       "c_w_s", "c_b_s")}
    for layer in reversed(range(DEPTH)):
        i = layer // 2
        xs, h, ctx, mixed, x1, h2, g_act, u_act, act, w_in_g, w_out_full, wg_g, wu_g, wd_g = saved[layer]
        dg, du = _ffn_bwd_act(dxb, wd_g, g_act, u_act, started())
        dwd = _ffn_dwd(act, dxb)
        dwg, dwu = _ffn_dwgu(h2, dg, du)
        rs_ffn = _ReduceScatter("ffn%d" % layer, [dwg, dwu, dwd], core)
        dh2 = _ffn_dh(dg, du, wg_g, wu_g, [rs_ffn.token])
        dx, dxb, dgn = _rms_bwd(x1, ffn_norm[layer], dh2, dx)
        small["ffn_norm"][layer] = dgn[0]
        rs_ffn.middle(dxb, core)
        for pos, k in enumerate(("ffn_w_gate", "ffn_w_up", "ffn_w_down")):
            scatters[(k, layer)] = (rs_ffn, pos)
        dmixed = _mm_nt("mix_out_bwd", dxb, w_out_full, bf16, [rs_ffn.token])
        dwout = _mm_tn_rows("mix_out_dw", mixed, dxb)
        if layer % 2 == 0:
            proj, btab = ctx
            dq, dk, dv, dtab = _attn_bwd(proj, dmixed, btab, heads)
            db, dc, dhv, dcw = _conv_bwd(proj, dmixed, conv_full[i], a_blocks, b_blocks)
            dproj = jnp.concatenate([dq, dk, dv, db, dc, dhv], axis=-1)
            small["ab_rel_bias"][i] = _bias_table_grad(dtab)
            small["ab_conv_w"][i] = dcw
            names = ("ab_w_in", "ab_w_out")
            tag = "ab"
        else:
            proj, bs_t = ctx
            dproj, dws, dbs_t, dlg, dlb = _sgu_bwd(proj, dmixed, lng_full[i], lnb_full[i], c_w_s[i], bs_t)
            small["c_w_s"][i] = dws
            small["c_b_s"][i] = jnp.transpose(dbs_t)
            small["c_ln_g"][i] = dlg[0]
            small["c_ln_b"][i] = dlb[0]
            names = ("c_w_in", "c_w_out")
            tag = "c"
        dwin = _mm_tn_cols(tag + "_proj_dw", h, dproj)
        rs_mix = _ReduceScatter("%s%d" % (tag, i), [dwin, dwout], core)
        dh = _mm_nt_cols(tag + "_proj_bwd", dproj, w_in_g, [rs_mix.token])
        dx, dxb, dgm = _rms_bwd(xs, mix_norm[layer], dh, dx)
        small["mix_norm"][layer] = dgm[0]
        rs_mix.middle(dxb, core)
        tokens.append(rs_mix.token)
        scatters[(names[0], i)] = (rs_mix, 0)
        scatters[(names[1], i)] = (rs_mix, 1)
    grad_x = dx[None]

    small_names = ["mix_norm", "ffn_norm", "ab_rel_bias", "ab_conv_w", "c_ln_g", "c_ln_b", "c_w_s", "c_b_s"]
    small_full = [jnp.stack(small[k]) for k in small_names] + [d_final[0]]
    small_slot = _cast_into_slot("small_slot", _pack(small_full)[None], 0, me, started()[-1], f32)
    small_sems, (small_buf,), last = _split_start("small_sum_start", [small_slot], _plan_all_peers, N_DEV - 1, core)

    grads, deltas, new_m, new_v = {}, {}, {}, {}
    finished = {}
    for k in ("c_w_in", "c_w_out", "ffn_w_gate", "ffn_w_up", "ffn_w_down", "ab_w_in", "ab_w_out"):
        parts = []
        w_k, m_k, v_k = local[k]
        for li in range(w_k.shape[0]):
            rs, pos = scatters[(k, li)]
            if id(rs) not in finished:
                finished[id(rs)] = rs.finish(last)
            parts.append(finished[id(rs)][pos])
        outs = _adam_big("adam_" + k, w_k, m_k, v_k, parts, chip, last)
        last = outs[1]
        grads[k], deltas[k], new_m[k], new_v[k] = [flip(o) for o in outs] if k in wide else outs

    (small_buf,) = _split_wait("small_sum_wait", [small_buf], small_sems, _plan_all_peers, last)
    summed = _unpack(_sum_slots(small_buf), [a.shape for a in small_full])
    small_grads = dict(zip(small_names + ["final_norm"], summed))
    for k in ("ab_conv_w", "c_ln_g", "c_ln_b"):
        width = weights[k].shape[-1]
        small_grads[k] = lax.dynamic_slice_in_dim(small_grads[k], me_s * width, width, axis=-1)
    small_order = [k for k in order if k in small_grads]
    shapes = [weights[k].shape for k in small_order]
    d_s, m_s, v_s = _adam_small(_pack([weights[k] for k in small_order]), _pack([small_grads[k] for k in small_order]),
                                _pack([mom_m[k] for k in small_order]), _pack([mom_v[k] for k in small_order]))
    grads.update(small_grads)
    for k, dd, mm, vv in zip(small_order, _unpack(d_s, shapes), _unpack(m_s, shapes), _unpack(v_s, shapes)):
        deltas[k], new_m[k], new_v[k] = dd, mm, vv

    return (loss, grad_x, *[grads[k] for k in order], *[deltas[k] for k in order], *[new_m[k] for k in order],
            *[new_v[k] for k in order])
```

```python
import numpy as np
import jax
import jax.numpy as jnp
from jax import lax
from jax.experimental import pallas as pl
from jax.experimental.pallas import tpu as pltpu

D_MODEL = 2048
SEQ = 2048
DEPTH = 4
CHUNK = 64
A_HEAD_DIM = 128
A_LEFT_CHUNKS = 8
A_MAX_REL = 256
CONV_WIDTH = 3
C_BLOCK = 128
C_GROUPS = 8
EPS = 1e-6
NEG_INF = -1e30

ADAM_LR = 0.001
ADAM_B1 = 0.9
ADAM_B2 = 0.999
ADAM_EPS = 1e-08
ADAM_WD = 0.01
ADAM_STEP = 10

N_DEV = 8
N_CHIP = 4
RS_SPLIT = 4
D2D_SPLIT = 2
GATHER_AHEAD = 4
ROWS_PER_STEP = 1024
COL_CHUNK = 256
LANE = 128
VMEM_LIMIT = 52 * 1024 * 1024

bf16 = jnp.bfloat16
f32 = jnp.float32
MESH = pl.DeviceIdType.MESH
ANY = pl.BlockSpec(memory_space=pl.ANY)


def _params(*sem):
    return pltpu.CompilerParams(dimension_semantics=sem, vmem_limit_bytes=VMEM_LIMIT)


def _perm(j):
    return (j % 2) * N_CHIP + j // 2


_DN = {"nn": (((1,), (0,)), ((), ())), "nt": (((1,), (1,)), ((), ())), "tn": (((0,), (0,)), ((), ()))}


def _matmul(name, mode, grid, operands, specs, pairs, n_acc, acc_shape, extras, extra_specs, out_shapes, out_specs,
            epilogue, chunk=0):
    nk = grid[2]
    n_op, n_ex, n_out = len(operands), len(extras), len(out_shapes)

    def single(*refs):
        ops = refs[:n_op]
        ex = refs[n_op:n_op + n_ex]
        outs = refs[n_op + n_ex:]
        width = acc_shape[1]
        starts = range(0, width, chunk) if chunk else (0,)
        for c0 in starts:
            cols = slice(c0, min(c0 + chunk, width)) if chunk else slice(None)
            sums = [None] * n_acc
            for p, (ia, ib) in enumerate(pairs):
                b = ops[ib][cols, :] if mode == "nt" else ops[ib][:, cols]
                d = lax.dot_general(ops[ia][...], b, _DN[mode], preferred_element_type=f32)
                sums[p % n_acc] = d if sums[p % n_acc] is None else sums[p % n_acc] + d
            epilogue(sums, ex, outs, cols)

    def body(*refs):
        ops = refs[:n_op]
        ex = refs[n_op:n_op + n_ex]
        outs = refs[n_op + n_ex:n_op + n_ex + n_out]
        accs = refs[n_op + n_ex + n_out:]
        k = pl.program_id(2)

        @pl.when(k == 0)
        def _():
            for acc in accs:
                acc[...] = jnp.zeros_like(acc)

        for p, (ia, ib) in enumerate(pairs):
            acc = accs[p % n_acc]
            acc[...] += lax.dot_general(ops[ia][...], ops[ib][...], _DN[mode], preferred_element_type=f32)

        @pl.when(k == nk - 1)
        def _():
            epilogue([acc[...] for acc in accs], ex, outs, slice(None))

    return pl.pallas_call(
        single if nk == 1 else body, grid=grid, in_specs=list(specs) + list(extra_specs), out_specs=list(out_specs),
        out_shape=list(out_shapes), scratch_shapes=[] if nk == 1 else [pltpu.VMEM(acc_shape, f32)] * n_acc,
        compiler_params=_params("parallel", "parallel", "arbitrary"), name=name)(*operands, *extras)


def _store(dtype):
    def ep(accs, ex, outs, cols):
        for a, o in zip(accs, outs):
            o[:, cols] = a.astype(dtype)
    return ep


def _mm_cols(name, h, wg, out_dtype, deps=()):
    t, kd = h.shape
    n8 = wg.shape[2]
    tm = min(t, 2 * ROWS_PER_STEP)
    return _matmul(
        name, "nn", (t // tm, N_DEV, 1), [h, wg],
        [pl.BlockSpec((tm, kd), lambda i, j, k: (i, 0)), pl.BlockSpec((None, kd, n8), lambda i, j, k: (j, 0, 0))],
        [(0, 1)], 1, (tm, n8), list(deps), [ANY] * len(deps), [jax.ShapeDtypeStruct((t, N_DEV * n8), out_dtype)],
        [pl.BlockSpec((tm, n8), lambda i, j, k: (i, j))], _store(out_dtype), COL_CHUNK)[0]


def _norm_rows(xv, gain):
    return (xv * lax.rsqrt(jnp.mean(xv * xv, axis=-1, keepdims=True) + EPS) * gain).astype(bf16)


def _mm_rows_res(name, a, w, res, gain, deps=()):
    t, kd = a.shape
    n = w.shape[1]
    tm = min(t, 512)

    def ep(accs, ex, outs, cols):
        xv = ex[0][...] + accs[0]
        outs[0][...] = xv
        outs[1][...] = _norm_rows(xv, ex[1][...])

    row = pl.BlockSpec((tm, n), lambda i, j, k: (i, 0))
    return _matmul(
        name, "nn", (t // tm, 1, 1), [a, w],
        [pl.BlockSpec((tm, kd), lambda i, j, k: (i, 0)), pl.BlockSpec((kd, n), lambda i, j, k: (0, 0))],
        [(0, 1)], 1, (tm, n), [res, gain.reshape(1, n)] + list(deps),
        [row, pl.BlockSpec((1, n), lambda i, j, k: (0, 0))] + [ANY] * len(deps),
        [jax.ShapeDtypeStruct((t, n), f32), jax.ShapeDtypeStruct((t, n), bf16)], [row, row], ep)


def _mm_nt(name, a, w, out_dtype, deps=()):
    t, n = a.shape
    kd = w.shape[0]
    tm, tn = min(t, 2 * ROWS_PER_STEP), min(kd, 1024)
    return _matmul(
        name, "nt", (t // tm, kd // tn, 1), [a, w],
        [pl.BlockSpec((tm, n), lambda i, j, k: (i, 0)), pl.BlockSpec((tn, n), lambda i, j, k: (j, 0))],
        [(0, 1)], 1, (tm, tn), list(deps), [ANY] * len(deps), [jax.ShapeDtypeStruct((t, kd), out_dtype)],
        [pl.BlockSpec((tm, tn), lambda i, j, k: (i, j))], _store(out_dtype), COL_CHUNK)[0]


def _mm_nt_cols(name, da, wg, deps=()):
    t = da.shape[0]
    kd, n8 = wg.shape[1], wg.shape[2]
    tm = min(t, ROWS_PER_STEP)
    return _matmul(
        name, "nt", (t // tm, 1, N_DEV), [da, wg],
        [pl.BlockSpec((tm, n8), lambda i, j, k: (i, k)), pl.BlockSpec((None, kd, n8), lambda i, j, k: (k, 0, 0))],
        [(0, 1)], 1, (tm, kd), list(deps), [ANY] * len(deps), [jax.ShapeDtypeStruct((t, kd), bf16)],
        [pl.BlockSpec((tm, kd), lambda i, j, k: (i, 0))], _store(bf16))[0]


def _mm_tn_cols(name, h, da):
    t, kd = h.shape
    n8 = da.shape[1] // N_DEV
    tmk, tk = min(kd, 1024), min(t, 2048)
    return _matmul(
        name, "tn", (kd // tmk, N_DEV, t // tk), [h, da],
        [pl.BlockSpec((tk, tmk), lambda i, j, k: (k, i)), pl.BlockSpec((tk, n8), lambda i, j, k: (k, j))],
        [(0, 1)], 1, (tmk, n8), [], [], [jax.ShapeDtypeStruct((N_DEV, kd, n8), bf16)],
        [pl.BlockSpec((None, tmk, n8), lambda i, j, k: (_perm(j), i, 0))], _store(bf16))[0]


def _mm_tn_rows(name, a, dx):
    t, kf = a.shape
    r8 = kf // N_DEV
    n = dx.shape[1]
    return _matmul(
        name, "tn", (N_DEV, 1, 1), [a, dx],
        [pl.BlockSpec((t, r8), lambda i, j, k: (0, i)), pl.BlockSpec((t, n), lambda i, j, k: (0, 0))],
        [(0, 1)], 1, (r8, n), [], [], [jax.ShapeDtypeStruct((N_DEV, r8, n), bf16)],
        [pl.BlockSpec((None, r8, n), lambda i, j, k: (_perm(i), 0, 0))], _store(bf16))[0]


def _ffn_in(h2, wg_t, wu_t, deps=()):
    t, kd = h2.shape
    f8 = wg_t.shape[1]
    tm = min(t, ROWS_PER_STEP)

    def ep(accs, ex, outs, cols):
        g, u = accs
        outs[0][:, cols] = g.astype(bf16)
        outs[1][:, cols] = u.astype(bf16)
        outs[2][:, cols] = (g * jax.nn.sigmoid(g) * u).astype(bf16)

    wspec = pl.BlockSpec((None, f8, kd), lambda i, j, k: (j, 0, 0))
    ospec = pl.BlockSpec((None, tm, f8), lambda i, j, k: (j, i, 0))
    return _matmul(
        "ffn_in", "nt", (t // tm, N_DEV, 1), [h2, wg_t, wu_t],
        [pl.BlockSpec((tm, kd), lambda i, j, k: (i, 0)), wspec, wspec], [(0, 1), (0, 2)], 2, (tm, f8), list(deps),
        [ANY] * len(deps), [jax.ShapeDtypeStruct((N_DEV, t, f8), bf16)] * 3, [ospec] * 3, ep, COL_CHUNK)


def _ffn_down(act, wd, res, gain, deps=()):
    _, t, f8 = act.shape
    n = wd.shape[2]
    tm = min(t, 512)

    def ep(accs, ex, outs, cols):
        xv = ex[0][...] + accs[0]
        outs[0][...] = xv
        if gain is not None:
            outs[1][...] = _norm_rows(xv, ex[1][...])

    row = pl.BlockSpec((tm, n), lambda i, j, k: (i, 0))
    extras, extra_specs = [res], [row]
    shapes, specs = [jax.ShapeDtypeStruct((t, n), f32)], [row]
    if gain is not None:
        extras.append(gain.reshape(1, n))
        extra_specs.append(pl.BlockSpec((1, n), lambda i, j, k: (0, 0)))
        shapes.append(jax.ShapeDtypeStruct((t, n), bf16))
        specs.append(row)
    return _matmul(
        "ffn_down", "nn", (t // tm, 1, N_DEV), [act, wd],
        [pl.BlockSpec((None, tm, f8), lambda i, j, k: (k, i, 0)), pl.BlockSpec((None, f8, n), lambda i, j, k: (k, 0, 0))],
        [(0, 1)], 1, (tm, n), extras + list(deps), extra_specs + [ANY] * len(deps), shapes, specs, ep)


def _ffn_bwd_act(dxb, wd, g, u, deps=()):
    t, n = dxb.shape
    f8 = wd.shape[1]
    tm = min(t, ROWS_PER_STEP)

    def ep(accs, ex, outs, cols):
        dact = accs[0]
        gv = ex[0][:, cols].astype(f32)
        uv = ex[1][:, cols].astype(f32)
        sg = jax.nn.sigmoid(gv)
        silu = gv * sg
        outs[0][:, cols] = (dact * uv * (sg * (1.0 + gv * (1.0 - sg)))).astype(bf16)
        outs[1][:, cols] = (dact * silu).astype(bf16)

    bspec = pl.BlockSpec((None, tm, f8), lambda i, j, k: (j, i, 0))
    return _matmul(
        "ffn_bwd_act", "nt", (t // tm, N_DEV, 1), [dxb, wd],
        [pl.BlockSpec((tm, n), lambda i, j, k: (i, 0)), pl.BlockSpec((None, f8, n), lambda i, j, k: (j, 0, 0))],
        [(0, 1)], 1, (tm, f8), [g, u] + list(deps), [bspec, bspec] + [ANY] * len(deps),
        [jax.ShapeDtypeStruct((N_DEV, t, f8), bf16)] * 2, [bspec] * 2, ep, COL_CHUNK)


def _ffn_dwd(act, dxb):
    _, t, f8 = act.shape
    n = dxb.shape[1]
    tk = min(t, 2048)
    return _matmul(
        "ffn_dwd", "tn", (N_DEV, 1, t // tk), [act, dxb],
        [pl.BlockSpec((None, tk, f8), lambda i, j, k: (i, k, 0)), pl.BlockSpec((tk, n), lambda i, j, k: (k, 0))],
        [(0, 1)], 1, (f8, n), [], [], [jax.ShapeDtypeStruct((N_DEV, f8, n), bf16)],
        [pl.BlockSpec((None, f8, n), lambda i, j, k: (_perm(i), 0, 0))], _store(bf16))[0]


def _ffn_dwgu(h2, dg, du):
    t, kd = h2.shape
    f8 = dg.shape[2]
    tk, tn = min(t, 2048), min(kd, 1024)
    aspec = pl.BlockSpec((None, tk, f8), lambda i, j, k: (i, k, 0))
    ospec = pl.BlockSpec((None, f8, tn), lambda i, j, k: (_perm(i), 0, j))
    return _matmul(
        "ffn_dwgu", "tn", (N_DEV, kd // tn, t // tk), [dg, du, h2],
        [aspec, aspec, pl.BlockSpec((tk, tn), lambda i, j, k: (k, j))], [(0, 2), (1, 2)], 2, (f8, tn), [], [],
        [jax.ShapeDtypeStruct((N_DEV, f8, kd), bf16)] * 2, [ospec] * 2, _store(bf16))


def _ffn_dh(dg, du, wg_t, wu_t, deps=()):
    _, t, f8 = dg.shape
    kd = wg_t.shape[2]
    tm = min(t, ROWS_PER_STEP)
    aspec = pl.BlockSpec((None, tm, f8), lambda i, j, k: (k, i, 0))
    wspec = pl.BlockSpec((None, f8, kd), lambda i, j, k: (k, 0, 0))
    return _matmul(
        "ffn_dh", "nn", (t // tm, 1, N_DEV), [dg, du, wg_t, wu_t], [aspec, aspec, wspec, wspec], [(0, 2), (1, 3)], 1,
        (tm, kd), list(deps), [ANY] * len(deps), [jax.ShapeDtypeStruct((t, kd), bf16)],
        [pl.BlockSpec((tm, kd), lambda i, j, k: (i, 0))], _store(bf16))[0]


def _rms_fwd(x, g):
    t, d = x.shape
    tm = min(t, 256)

    def body(x_ref, g_ref, o_ref):
        xv = x_ref[...]
        r = lax.rsqrt(jnp.mean(xv * xv, axis=-1, keepdims=True) + EPS)
        o_ref[...] = (xv * r * g_ref[...]).astype(bf16)

    return pl.pallas_call(
        body, grid=(t // tm,), in_specs=[pl.BlockSpec((tm, d), lambda i: (i, 0)), pl.BlockSpec((1, d), lambda i: (0, 0))],
        out_specs=pl.BlockSpec((tm, d), lambda i: (i, 0)), out_shape=jax.ShapeDtypeStruct((t, d), bf16),
        compiler_params=_params("parallel"), name="rms_fwd")(x, g.reshape(1, d))


def _rms_bwd(x, g, dh, dres):
    t, d = x.shape
    tm = min(t, 512)

    def body(x_ref, g_ref, dh_ref, dres_ref, dx_ref, dxb_ref, dg_ref):
        xv = x_ref[...]
        dy = dh_ref[...].astype(f32)
        r = lax.rsqrt(jnp.mean(xv * xv, axis=-1, keepdims=True) + EPS)
        gy = dy * g_ref[...]
        dot = jnp.mean(xv * gy, axis=-1, keepdims=True)
        dx = dres_ref[...] + r * gy - xv * (r * r * r * dot)
        dx_ref[...] = dx
        dxb_ref[...] = dx.astype(bf16)

        @pl.when(pl.program_id(0) == 0)
        def _():
            dg_ref[...] = jnp.zeros_like(dg_ref)

        dg_ref[...] += jnp.sum(dy * xv * r, axis=0, keepdims=True)

    row = pl.BlockSpec((tm, d), lambda i: (i, 0))
    vec = pl.BlockSpec((1, d), lambda i: (0, 0))
    return pl.pallas_call(
        body, grid=(t // tm,), in_specs=[row, vec, row, row], out_specs=[row, row, vec],
        out_shape=[jax.ShapeDtypeStruct((t, d), f32), jax.ShapeDtypeStruct((t, d), bf16), jax.ShapeDtypeStruct((1, d), f32)],
        compiler_params=_params("arbitrary"), name="rms_bwd")(x, g.reshape(1, d), dh, dres)


def _loss_head(x, g, target):
    t, d = x.shape
    tm = min(t, 512)

    def body(x_ref, g_ref, t_ref, loss_ref, dx_ref, dxb_ref, dg_ref):
        xv = x_ref[...]
        r = lax.rsqrt(jnp.mean(xv * xv, axis=-1, keepdims=True) + EPS)
        xn = xv * r
        err = xn * g_ref[...] - t_ref[...]
        dy = err * (1.0 / d)
        gy = dy * g_ref[...]
        dot = jnp.mean(xv * gy, axis=-1, keepdims=True)
        dx = r * gy - xv * (r * r * r * dot)
        dx_ref[...] = dx
        dxb_ref[...] = dx.astype(bf16)

        @pl.when(pl.program_id(0) == 0)
        def _():
            dg_ref[...] = jnp.zeros_like(dg_ref)
            loss_ref[...] = jnp.zeros_like(loss_ref)

        dg_ref[...] += jnp.sum(dy * xn, axis=0, keepdims=True)
        loss_ref[...] += 0.5 * jnp.sum(jnp.sum(err * err, axis=-1, keepdims=True) * (1.0 / d), axis=0, keepdims=True)

    row = pl.BlockSpec((tm, d), lambda i: (i, 0))
    vec = pl.BlockSpec((1, d), lambda i: (0, 0))
    one = pl.BlockSpec((1, 1), lambda i: (0, 0))
    return pl.pallas_call(
        body, grid=(t // tm,), in_specs=[row, vec, row], out_specs=[one, row, row, vec],
        out_shape=[jax.ShapeDtypeStruct((1, 1), f32), jax.ShapeDtypeStruct((t, d), f32),
                   jax.ShapeDtypeStruct((t, d), bf16), jax.ShapeDtypeStruct((1, d), f32)],
        compiler_params=_params("arbitrary"), name="loss_head")(x, g.reshape(1, d), target)


def _attn_consts():
    qt, kw = 2 * CHUNK, (A_LEFT_CHUNKS + 2) * CHUNK
    r = np.arange(qt)[:, None]
    kc = np.arange(kw)[None, :]
    rel = np.clip(r + A_LEFT_CHUNKS * CHUNK - kc, -A_MAX_REL, A_MAX_REL) + A_MAX_REL
    dchunk = kc // CHUNK - r // CHUNK
    valid = (dchunk >= 0) & (dchunk <= A_LEFT_CHUNKS)
    m = np.arange(kw + qt)
    relidx = np.clip(A_LEFT_CHUNKS * CHUNK - (m - (qt - 1)), -A_MAX_REL, A_MAX_REL) + A_MAX_REL
    onehot = np.zeros((kw + qt, 2 * A_MAX_REL + 1), np.float32)
    onehot[m, relidx] = 1.0
    return qt, kw, rel, valid, onehot


def _bias_table(rel_bias):
    qt, kw, _, valid, onehot = _attn_consts()
    h = rel_bias.shape[0]
    w = kw + qt
    relidx = np.argmax(onehot, axis=1)
    e = jnp.roll(jnp.take(rel_bias, jnp.asarray(relidx), axis=1), -(qt - 1), axis=1)
    rows = jnp.broadcast_to(e[:, None, :], (h, qt, w)).reshape(h, qt * w)
    skew = rows[:, :qt * (w - 1)].reshape(h, qt, w - 1)[:, :, :kw]
    return jnp.where(jnp.asarray(valid)[None], skew, NEG_INF).astype(f32)


def _bias_table_grad(dtab):
    qt, kw, _, _, onehot = _attn_consts()
    h = dtab.shape[0]
    w = kw + qt
    wide = -(-(w + qt) // LANE) * LANE
    y = jnp.pad(dtab, ((0, 0), (0, 0), (qt - 1, wide - kw - (qt - 1))))
    flat = jnp.pad(y.reshape(h, qt * wide), ((0, 0), (0, qt)))
    de = jnp.sum(flat.reshape(h, qt, wide + 1), axis=1)[:, :w]
    return jnp.dot(de, jnp.asarray(onehot), precision=lax.Precision.HIGHEST)


def _attn_scores(q_ref, kpad, btab_ref, r0, qt, kw, pad):
    qv = q_ref[pl.ds(r0, qt), :]
    kwin = kpad[pl.ds(r0, kw), :]
    s = lax.dot_general(qv, kwin, _DN["nt"], preferred_element_type=f32) * (A_HEAD_DIM ** -0.5) + btab_ref[...]
    kcol = lax.broadcasted_iota(jnp.int32, (qt, kw), 1)
    s = jnp.where(r0 + kcol >= pad, s, NEG_INF)
    p = jnp.exp(s - jnp.max(s, axis=-1, keepdims=True))
    return qv, kwin, p / jnp.sum(p, axis=-1, keepdims=True)


def _attn_fwd(proj, btab, heads, width):
    t = proj.shape[0]
    qt, kw = btab.shape[1], btab.shape[2]
    pad = kw - qt

    def body(q_ref, k_ref, v_ref, btab_ref, o_ref, kpad, vpad):
        zeros = jnp.zeros((pad, A_HEAD_DIM), bf16)
        kpad[pl.ds(0, pad), :] = zeros
        vpad[pl.ds(0, pad), :] = zeros
        kpad[pl.ds(pad, t), :] = k_ref[...]
        vpad[pl.ds(pad, t), :] = v_ref[...]

        def tile(i, carry):
            r0 = pl.multiple_of(i * qt, qt)
            _, _, p = _attn_scores(q_ref, kpad, btab_ref, r0, qt, kw, pad)
            o = lax.dot_general(p.astype(bf16), vpad[pl.ds(r0, kw), :], _DN["nn"], preferred_element_type=f32)
            o_ref[pl.ds(r0, qt), :] = o.astype(bf16)
            return carry

        lax.fori_loop(0, t // qt, tile, 0, unroll=4)

    col = lambda off: pl.BlockSpec((t, A_HEAD_DIM), lambda h, off=off: (0, off + h))
    return pl.pallas_call(
        body, grid=(heads,),
        in_specs=[col(0), col(heads), col(2 * heads), pl.BlockSpec((None, qt, kw), lambda h: (h, 0, 0))],
        out_specs=col(0), out_shape=jax.ShapeDtypeStruct((t, width), bf16),
        scratch_shapes=[pltpu.VMEM((t + pad, A_HEAD_DIM), bf16)] * 2,
        compiler_params=_params("parallel"), name="attn_fwd")(proj, proj, proj, btab)


def _attn_bwd(proj, dmix, btab, heads):
    t = proj.shape[0]
    qt, kw = btab.shape[1], btab.shape[2]
    pad = kw - qt
    scale = A_HEAD_DIM ** -0.5

    def body(q_ref, k_ref, v_ref, do_ref, btab_ref, dq_ref, dk_ref, dv_ref, dtab_ref, kpad, vpad, dkacc, dvacc):
        zeros = jnp.zeros((pad, A_HEAD_DIM), bf16)
        kpad[pl.ds(0, pad), :] = zeros
        vpad[pl.ds(0, pad), :] = zeros
        kpad[pl.ds(pad, t), :] = k_ref[...]
        vpad[pl.ds(pad, t), :] = v_ref[...]
        dkacc[...] = jnp.zeros_like(dkacc)
        dvacc[...] = jnp.zeros_like(dvacc)
        dtab_ref[...] = jnp.zeros_like(dtab_ref)

        def tile(i, carry):
            r0 = pl.multiple_of(i * qt, qt)
            qv, kwin, p = _attn_scores(q_ref, kpad, btab_ref, r0, qt, kw, pad)
            dov = do_ref[pl.ds(r0, qt), :]
            dp = lax.dot_general(dov, vpad[pl.ds(r0, kw), :], _DN["nt"], preferred_element_type=f32)
            ds = p * (dp - jnp.sum(p * dp, axis=-1, keepdims=True))
            dtab_ref[...] += ds
            dsb = ds.astype(bf16)
            dq = lax.dot_general(dsb, kwin, _DN["nn"], preferred_element_type=f32) * scale
            dq_ref[pl.ds(r0, qt), :] = dq.astype(bf16)
            dkacc[pl.ds(r0, kw), :] += lax.dot_general(dsb, qv, _DN["tn"], preferred_element_type=f32) * scale
            dvacc[pl.ds(r0, kw), :] += lax.dot_general(p.astype(bf16), dov, _DN["tn"], preferred_element_type=f32)
            return carry

        lax.fori_loop(0, t // qt, tile, 0, unroll=4)
        dk_ref[...] = dkacc[pl.ds(pad, t), :].astype(bf16)
        dv_ref[...] = dvacc[pl.ds(pad, t), :].astype(bf16)

    col = lambda off: pl.BlockSpec((t, A_HEAD_DIM), lambda h, off=off: (0, off + h))
    tab = pl.BlockSpec((None, qt, kw), lambda h: (h, 0, 0))
    wide = jax.ShapeDtypeStruct((t, heads * A_HEAD_DIM), bf16)
    return pl.pallas_call(
        body, grid=(heads,), in_specs=[col(0), col(heads), col(2 * heads), col(0), tab],
        out_specs=[col(0), col(0), col(0), tab],
        out_shape=[wide, wide, wide, jax.ShapeDtypeStruct((heads, qt, kw), f32)],
        scratch_shapes=[pltpu.VMEM((t + pad, A_HEAD_DIM), bf16)] * 2 + [pltpu.VMEM((t + pad, A_HEAD_DIM), f32)] * 2,
        compiler_params=_params("parallel"), name="attn_bwd")(proj, proj, proj, dmix, btab)


def _shift_down(z, k):
    rows = lax.broadcasted_iota(jnp.int32, z.shape, 0)
    return jnp.where(rows >= k, pltpu.roll(z, k, 0), 0.0)


def _shift_up(z, k):
    t = z.shape[0]
    rows = lax.broadcasted_iota(jnp.int32, z.shape, 0)
    return jnp.where(rows < t - k, pltpu.roll(z, t - k, 0), 0.0)


def _conv_fwd(proj, conv_w, attn_wide, a_blocks, b_blocks):
    t = proj.shape[0]

    def body(b_ref, c_ref, h_ref, w_ref, wide_ref, o_ref):
        z = c_ref[...].astype(f32) * h_ref[...].astype(f32)
        w = w_ref[...]
        y = w[0:1, :] * _shift_down(z, 2) + w[1:2, :] * _shift_down(z, 1) + w[2:3, :] * z
        o_ref[...] = (b_ref[...].astype(f32) * y).astype(bf16)

    col = lambda off: pl.BlockSpec((t, LANE), lambda i, off=off: (0, off + i))
    return pl.pallas_call(
        body, grid=(b_blocks,),
        in_specs=[col(3 * a_blocks), col(3 * a_blocks + b_blocks), col(3 * a_blocks + 2 * b_blocks),
                  pl.BlockSpec((CONV_WIDTH, LANE), lambda i: (0, i)), ANY],
        out_specs=col(a_blocks), out_shape=jax.ShapeDtypeStruct((t, (a_blocks + b_blocks) * LANE), bf16),
        input_output_aliases={4: 0},
        compiler_params=_params("parallel"), name="conv_fwd")(proj, proj, proj, conv_w, attn_wide)


def _conv_bwd(proj, dmix, conv_w, a_blocks, b_blocks):
    t = proj.shape[0]

    def body(b_ref, c_ref, h_ref, do_ref, w_ref, db_ref, dc_ref, dh_ref, dw_ref):
        bv, cv, hv = b_ref[...].astype(f32), c_ref[...].astype(f32), h_ref[...].astype(f32)
        w = w_ref[...]
        z = cv * hv
        z1, z2 = _shift_down(z, 1), _shift_down(z, 2)
        y = w[0:1, :] * z2 + w[1:2, :] * z1 + w[2:3, :] * z
        dov = do_ref[...].astype(f32)
        db_ref[...] = (dov * y).astype(bf16)
        dy = dov * bv
        dz = w[2:3, :] * dy + w[1:2, :] * _shift_up(dy, 1) + w[0:1, :] * _shift_up(dy, 2)
        dc_ref[...] = (dz * hv).astype(bf16)
        dh_ref[...] = (dz * cv).astype(bf16)
        dw_ref[0:1, :] = jnp.sum(dy * z2, axis=0, keepdims=True)
        dw_ref[1:2, :] = jnp.sum(dy * z1, axis=0, keepdims=True)
        dw_ref[2:3, :] = jnp.sum(dy * z, axis=0, keepdims=True)

    col = lambda off: pl.BlockSpec((t, LANE), lambda i, off=off: (0, off + i))
    wspec = pl.BlockSpec((CONV_WIDTH, LANE), lambda i: (0, i))
    wide = jax.ShapeDtypeStruct((t, b_blocks * LANE), bf16)
    return pl.pallas_call(
        body, grid=(b_blocks,),
        in_specs=[col(3 * a_blocks), col(3 * a_blocks + b_blocks), col(3 * a_blocks + 2 * b_blocks), col(a_blocks), wspec],
        out_specs=[col(0), col(0), col(0), wspec],
        out_shape=[wide, wide, wide, jax.ShapeDtypeStruct((CONV_WIDTH, b_blocks * LANE), f32)],
        compiler_params=_params("parallel"), name="conv_bwd")(proj, proj, proj, dmix, conv_w)


_RSQRT2 = 0.7071067811865476
_RSQRT2PI = 0.3989422804014327


def _gelu(x):
    return 0.5 * x * (1.0 + lax.erf(x * _RSQRT2))


def _gelu_grad(x):
    return 0.5 * (1.0 + lax.erf(x * _RSQRT2)) + x * jnp.exp(-0.5 * x * x) * _RSQRT2PI


def _sgu_common(a_ref, lg_ref, lb_ref, cw):
    av = a_ref[...]
    u = _gelu(av[:, :cw])
    v = _gelu(av[:, cw:])
    mu = jnp.mean(v, axis=-1, keepdims=True)
    xc = v - mu
    rstd = lax.rsqrt(jnp.mean(xc * xc, axis=-1, keepdims=True) + EPS)
    xhat = xc * rstd
    vln = xhat * lg_ref[...] + lb_ref[...]
    pos_t = lax.broadcasted_iota(jnp.int32, (C_BLOCK, C_BLOCK), 0) // CHUNK
    pos_s = lax.broadcasted_iota(jnp.int32, (C_BLOCK, C_BLOCK), 1) // CHUNK
    return av, u, xhat, rstd, vln, pos_s <= pos_t


def _sgu_fwd(a, ln_g, ln_b, w_s, bs_t):
    t, cw2 = a.shape
    cw = cw2 // 2
    groups = w_s.shape[0]
    cg = cw // groups

    def body(a_ref, lg_ref, lb_ref, ws_ref, bs_ref, m_ref):
        _, u, _, _, vln, mask = _sgu_common(a_ref, lg_ref, lb_ref, cw)
        vb = vln.astype(bf16)
        for g in range(groups):
            sl = slice(g * cg, (g + 1) * cg)
            wm = jnp.where(mask, ws_ref[g], 0.0).astype(bf16)
            s = lax.dot_general(wm, vb[:, sl], _DN["nn"], preferred_element_type=f32) + bs_ref[:, g:g + 1]
            m_ref[:, sl] = (u[:, sl] * s).astype(bf16)

    vec = pl.BlockSpec((1, cw), lambda n: (0, 0))
    return pl.pallas_call(
        body, grid=(t // C_BLOCK,),
        in_specs=[pl.BlockSpec((C_BLOCK, cw2), lambda n: (n, 0)), vec, vec,
                  pl.BlockSpec((groups, C_BLOCK, C_BLOCK), lambda n: (0, 0, 0)),
                  pl.BlockSpec((C_BLOCK, groups), lambda n: (0, 0))],
        out_specs=pl.BlockSpec((C_BLOCK, cw), lambda n: (n, 0)), out_shape=jax.ShapeDtypeStruct((t, cw), bf16),
        compiler_params=_params("parallel"), name="sgu_fwd")(a, ln_g.reshape(1, cw), ln_b.reshape(1, cw), w_s, bs_t)


def _sgu_bwd(a, dm, ln_g, ln_b, w_s, bs_t):
    t, cw2 = a.shape
    cw = cw2 // 2
    groups = w_s.shape[0]
    cg = cw // groups

    def body(a_ref, dm_ref, lg_ref, lb_ref, ws_ref, bs_ref, da_ref, dws_ref, dbs_ref, dlg_ref, dlb_ref, dvln):
        @pl.when(pl.program_id(0) == 0)
        def _():
            dws_ref[...] = jnp.zeros_like(dws_ref)
            dbs_ref[...] = jnp.zeros_like(dbs_ref)
            dlg_ref[...] = jnp.zeros_like(dlg_ref)
            dlb_ref[...] = jnp.zeros_like(dlb_ref)

        av, u, xhat, rstd, vln, mask = _sgu_common(a_ref, lg_ref, lb_ref, cw)
        vb = vln.astype(bf16)
        lane = lax.broadcasted_iota(jnp.int32, (C_BLOCK, groups), 1)
        dbs = jnp.zeros((C_BLOCK, groups), f32)
        for g in range(groups):
            sl = slice(g * cg, (g + 1) * cg)
            wm = jnp.where(mask, ws_ref[g], 0.0).astype(bf16)
            s = lax.dot_general(wm, vb[:, sl], _DN["nn"], preferred_element_type=f32) + bs_ref[:, g:g + 1]
            dmg = dm_ref[:, sl].astype(f32)
            da_ref[:, sl] = (dmg * s * _gelu_grad(av[:, sl])).astype(bf16)
            dsg = dmg * u[:, sl]
            dbs = dbs + jnp.where(lane == g, jnp.sum(dsg, axis=-1, keepdims=True), 0.0)
            dsb = dsg.astype(bf16)
            dws_ref[g] += jnp.where(mask, lax.dot_general(dsb, vb[:, sl], _DN["nt"], preferred_element_type=f32), 0.0)
            dvln[:, sl] = lax.dot_general(wm, dsb, _DN["tn"], preferred_element_type=f32)
        dbs_ref[...] += dbs
        dv = dvln[...]
        dlg_ref[...] += jnp.sum(dv * xhat, axis=0, keepdims=True)
        dlb_ref[...] += jnp.sum(dv, axis=0, keepdims=True)
        dxh = dv * lg_ref[...]
        dvv = rstd * (dxh - jnp.mean(dxh, axis=-1, keepdims=True) - xhat * jnp.mean(dxh * xhat, axis=-1, keepdims=True))
        da_ref[:, cw:] = (dvv * _gelu_grad(av[:, cw:])).astype(bf16)

    vec = pl.BlockSpec((1, cw), lambda n: (0, 0))
    wsp = pl.BlockSpec((groups, C_BLOCK, C_BLOCK), lambda n: (0, 0, 0))
    bsp = pl.BlockSpec((C_BLOCK, groups), lambda n: (0, 0))
    return pl.pallas_call(
        body, grid=(t // C_BLOCK,),
        in_specs=[pl.BlockSpec((C_BLOCK, cw2), lambda n: (n, 0)), pl.BlockSpec((C_BLOCK, cw), lambda n: (n, 0)), vec, vec, wsp, bsp],
        out_specs=[pl.BlockSpec((C_BLOCK, cw2), lambda n: (n, 0)), wsp, bsp, vec, vec],
        out_shape=[jax.ShapeDtypeStruct((t, cw2), bf16), jax.ShapeDtypeStruct(w_s.shape, f32),
                   jax.ShapeDtypeStruct(bs_t.shape, f32), jax.ShapeDtypeStruct((1, cw), f32), jax.ShapeDtypeStruct((1, cw), f32)],
        scratch_shapes=[pltpu.VMEM((C_BLOCK, cw), f32)],
        compiler_params=_params("arbitrary"), name="sgu_bwd")(a, dm, ln_g.reshape(1, cw), ln_b.reshape(1, cw), w_s, bs_t)


HBM = pl.BlockSpec(memory_space=pltpu.HBM)
SEM = pl.BlockSpec(memory_space=pltpu.SEMAPHORE)
EFFECT = pltpu.SideEffectType.DATAFLOW_SIDE_EFFECTING


def _place():
    x, y, c = lax.axis_index("x"), lax.axis_index("y"), lax.axis_index("c")
    return x, y, c, [(1 - x, y), (x, 1 - y), (1 - x, 1 - y)]


def _remote(src, dst, send_sems, recv_sems, k, to):
    return pltpu.make_async_remote_copy(src_ref=src, dst_ref=dst, send_sem=send_sems.at[k], recv_sem=recv_sems.at[k],
                                        device_id=to, device_id_type=MESH)


def _split_start(name, arrays, plan, n_copies, after):
    n = len(arrays)

    def body(*refs):
        send_sems, recv_sems, token = refs[n + 1], refs[n + 2], refs[-1]
        for cp in plan(refs[:n], send_sems, recv_sems):
            cp.start()
        token[...] = jnp.zeros_like(token)

    out = pl.pallas_call(
        body, name=name,
        out_shape=(pltpu.SemaphoreType.DMA((n_copies,)), pltpu.SemaphoreType.DMA((n_copies,)),
                   *[pltpu.HBM(a.shape, a.dtype) for a in arrays], jax.ShapeDtypeStruct((8, LANE), f32)),
        in_specs=[HBM] * n + [ANY], out_specs=(SEM, SEM, *[HBM] * n, pl.BlockSpec(memory_space=pltpu.VMEM)),
        input_output_aliases={i: 2 + i for i in range(n)},
        compiler_params=pltpu.CompilerParams(has_side_effects=EFFECT),
    )(*[pltpu.with_memory_space_constraint(a, pltpu.HBM) for a in arrays], after)
    return (out[0], out[1]), list(out[2:2 + n]), out[-1]


def _split_wait(name, arrays, sems, plan, after):
    n = len(arrays)

    def body(*refs):
        for cp in plan(refs[:n], refs[n], refs[n + 1]):
            cp.wait()

    out = pl.pallas_call(
        body, name=name, out_shape=tuple(pltpu.HBM(a.shape, a.dtype) for a in arrays),
        in_specs=[HBM] * n + [SEM, SEM, ANY], out_specs=tuple([HBM] * n), input_output_aliases={i: i for i in range(n)},
        compiler_params=pltpu.CompilerParams(has_side_effects=EFFECT),
    )(*arrays, sems[0], sems[1], after)
    return list(out)


def _row_pieces(ref_rows, split):
    rc = ref_rows // split
    return [pl.ds(s * rc, rc) for s in range(split)]


def _gather_slots():
    x, y, c, _ = _place()
    slots = (4 * x + 2 * y + c, 4 * (1 - x) + 2 * y + c, 4 * x + 2 * (1 - y) + c, 4 * (1 - x) + 2 * (1 - y) + c)
    return slots, (x, y, 1 - c), (1 - x, y, c), (x, 1 - y, c)


def _to_sibling(cps, k, b, slot, sibling, send_sems, recv_sems):
    for rows in _row_pieces(b.shape[1], D2D_SPLIT):
        cps.append(_remote(b.at[slot, rows], b.at[slot, rows], send_sems, recv_sems, k, sibling))
        k += 1
    return k


def _plan_gather_near(bufs, send_sems, recv_sems):
    (me, _, _, _), sibling, x_peer, y_peer = _gather_slots()
    cps, k = [], 0
    for b in bufs:
        k = _to_sibling(cps, k, b, me, sibling, send_sems, recv_sems)
        for peer in (x_peer, y_peer):
            cps.append(_remote(b.at[me], b.at[me], send_sems, recv_sems, k, peer))
            k += 1
    return cps


def _plan_gather_relay(bufs, send_sems, recv_sems):
    (_, x_slot, y_slot, _), sibling, x_peer, y_peer = _gather_slots()
    cps, k = [], 0
    for b in bufs:
        half = b.shape[1] // 2
        k = _to_sibling(cps, k, b, x_slot, sibling, send_sems, recv_sems)
        k = _to_sibling(cps, k, b, y_slot, sibling, send_sems, recv_sems)
        lower, upper = pl.ds(0, half), pl.ds(half, half)
        cps.append(_remote(b.at[x_slot, lower], b.at[x_slot, lower], send_sems, recv_sems, k, y_peer))
        cps.append(_remote(b.at[y_slot, upper], b.at[y_slot, upper], send_sems, recv_sems, k + 1, x_peer))
        k += 2
    return cps


def _plan_gather_far(bufs, send_sems, recv_sems):
    (_, _, _, far), sibling, _, _ = _gather_slots()
    cps, k = [], 0
    for b in bufs:
        k = _to_sibling(cps, k, b, far, sibling, send_sems, recv_sems)
    return cps


def _plan_rs_sibling(arrs, send_sems, recv_sems):
    n = len(arrs) // 2
    x, y, c, _ = _place()
    cps, k = [], 0
    for g, got in zip(arrs[:n], arrs[n:]):
        for q in range(N_CHIP):
            for rows in _row_pieces(g.shape[1], RS_SPLIT):
                cps.append(_remote(g.at[N_CHIP * (1 - c) + q, rows], got.at[q, rows], send_sems, recv_sems, k, (x, y, 1 - c)))
                k += 1
    return cps


def _plan_rs_chips(arrs, send_sems, recv_sems):
    n = len(arrs) // 2
    x, y, c, chips = _place()
    q = 2 * x + y
    cps, k = [], 0
    for p, r in zip(arrs[:n], arrs[n:]):
        for px, py in chips:
            cps.append(_remote(p.at[2 * px + py], r.at[q], send_sems, recv_sems, k, (px, py, c)))
            k += 1
    return cps


class _Gather:
    STAGES = (("near", _plan_gather_near, D2D_SPLIT + 2), ("relay", _plan_gather_relay, 2 * D2D_SPLIT + 2),
              ("far", _plan_gather_far, D2D_SPLIT))

    def __init__(self, tag, bufs):
        self.tag, self.bufs, self.stage = tag, bufs, -1

    def advance(self, after):
        if self.stage >= 0:
            name, plan, _ = self.STAGES[self.stage]
            self.bufs = _split_wait("gather_%s_wait_%s" % (name, self.tag), self.bufs, self.sems, plan, after)
        self.stage += 1
        if self.stage == len(self.STAGES):
            return self.bufs
        name, plan, per_array = self.STAGES[self.stage]
        self.sems, self.bufs, token = _split_start("gather_%s_start_%s" % (name, self.tag), self.bufs, plan,
                                                   per_array * len(self.bufs), after)
        return token

    def finish(self, after):
        out = self.advance(after)
        while not isinstance(out, list):
            out = self.advance(after)
        return out


class _ReduceScatter:
    def __init__(self, tag, grads, core):
        self.tag, self.n = tag, len(grads)
        lands = [lax.empty((N_CHIP,) + g.shape[1:], g.dtype) for g in grads]
        self.sems, self.arrs, self.token = _split_start("rs_sibling_start_" + tag, list(grads) + lands, _plan_rs_sibling,
                                                        self.n * N_CHIP * RS_SPLIT, core)

    def middle(self, after, core):
        arrs = _split_wait("rs_sibling_wait_" + self.tag, self.arrs, self.sems, _plan_rs_sibling, after)
        parts = [_pair_sum(g, got, core) for g, got in zip(arrs[:self.n], arrs[self.n:])]
        lands = [lax.empty(p.shape, p.dtype) for p in parts]
        self.sems, self.arrs, self.token = _split_start("rs_chips_start_" + self.tag, parts + lands, _plan_rs_chips,
                                                        self.n * 3, core)

    def finish(self, after):
        arrs = _split_wait("rs_chips_wait_" + self.tag, self.arrs, self.sems, _plan_rs_chips, after)
        return list(zip(arrs[:self.n], arrs[self.n:]))


def _cast_into_slot(name, w, layer, me, after, dtype=bf16):
    _, rows, cols = w.shape
    tr = 256 if rows % 256 == 0 else rows

    def body(me_ref, w_ref, after_ref, o_ref):
        o_ref[...] = w_ref[...].astype(dtype)

    return pl.pallas_call(
        body,
        grid_spec=pltpu.PrefetchScalarGridSpec(
            num_scalar_prefetch=1, grid=(rows // tr,),
            in_specs=[pl.BlockSpec((None, tr, cols), lambda i, me_ref: (layer, i, 0)), ANY],
            out_specs=pl.BlockSpec((None, tr, cols), lambda i, me_ref: (me_ref[0], i, 0))),
        out_shape=jax.ShapeDtypeStruct((N_DEV, rows, cols), dtype), compiler_params=_params("parallel"), name=name)(me, w, after)


def _pair_sum(g, got, core):
    _, rows, cols = g.shape
    tr = rows

    def body(c_ref, a_ref, b_ref, o_ref):
        o_ref[...] = (a_ref[...].astype(f32) + b_ref[...].astype(f32)).astype(bf16)

    spec = pl.BlockSpec((None, tr, cols), lambda q, i, c_ref: (q, i, 0))
    return pl.pallas_call(
        body,
        grid_spec=pltpu.PrefetchScalarGridSpec(
            num_scalar_prefetch=1, grid=(N_CHIP, rows // tr),
            in_specs=[pl.BlockSpec((None, tr, cols), lambda q, i, c_ref: (N_CHIP * c_ref[0] + q, i, 0)), spec],
            out_specs=spec),
        out_shape=jax.ShapeDtypeStruct((N_CHIP, rows, cols), bf16), compiler_params=_params("parallel", "parallel"),
        name="pair_sum")(core, g, got)


def _gather_copies(n, ins, outs, send_sems, recv_sems, local_sems):
    x, y, c, chips = _place()
    sibling = (x, y, 1 - c)

    def slot(px, py, pc):
        return 4 * px + 2 * py + pc

    def copy(i, k, block, to, src=None):
        dst = outs[i].at[slot(*block)]
        return pltpu.make_async_remote_copy(src_ref=dst if src is None else src, dst_ref=dst, send_sem=send_sems.at[i, k],
                                            recv_sem=recv_sems.at[i, k], device_id=to, device_id_type=MESH)

    started = []
    for i in range(n):
        mine = pltpu.make_async_copy(ins[i], outs[i].at[slot(x, y, c)], local_sems.at[i])
        mine.start()
        started.append(mine)
    sends = []
    for i in range(n):
        sends.append(copy(i, 0, (x, y, c), sibling, src=ins[i]))
        sends += [copy(i, 1 + j, (x, y, c), (*chip, c), src=ins[i]) for j, chip in enumerate(chips)]
    for cp in sends:
        cp.start()
    for i in range(n):
        for j, chip in enumerate(chips):
            copy(i, 1 + j, (*chip, c), (x, y, c)).wait_recv()
            fwd = copy(i, 4 + j, (*chip, c), sibling)
            fwd.start()
            sends.append(fwd)
    for i in range(n):
        copy(i, 0, sibling, (x, y, c)).wait_recv()
        for j, chip in enumerate(chips):
            copy(i, 4 + j, (*chip, 1 - c), (x, y, c)).wait_recv()
    for cp in sends:
        cp.wait_send()
    for mine in started:
        mine.wait()


def _gather_small(name, packed):
    rows = packed.shape[0]

    def body(x_ref, o_ref, buf, send_sems, recv_sems, local_sems):
        _gather_copies(1, [x_ref], [buf], send_sems, recv_sems, local_sems)
        o_ref[...] = buf[...]

    vm = pl.BlockSpec(memory_space=pltpu.VMEM)
    return pl.pallas_call(
        body, in_specs=[vm], out_specs=vm, out_shape=jax.ShapeDtypeStruct((N_DEV, rows, LANE), f32),
        scratch_shapes=[pltpu.VMEM((N_DEV, rows, LANE), f32), pltpu.SemaphoreType.DMA((1, 7)), pltpu.SemaphoreType.DMA((1, 7)),
                        pltpu.SemaphoreType.DMA((1,))],
        compiler_params=pltpu.CompilerParams(vmem_limit_bytes=VMEM_LIMIT), name=name)(packed)


def _plan_all_peers(bufs, send_sems, recv_sems):
    x, y, c, _ = _place()
    me = 4 * x + 2 * y + c
    peers = [(px, py, pc) for px in (x, 1 - x) for py in (y, 1 - y) for pc in (c, 1 - c)][1:]
    cps, k = [], 0
    for b in bufs:
        for peer in peers:
            cps.append(_remote(b.at[me], b.at[me], send_sems, recv_sems, k, peer))
            k += 1
    return cps


def _sum_slots(buf):
    n, rows, cols = buf.shape

    def body(b_ref, o_ref):
        acc = b_ref[0]
        for j in range(1, n):
            acc = acc + b_ref[j]
        o_ref[...] = acc

    return pl.pallas_call(body, out_shape=jax.ShapeDtypeStruct((rows, cols), buf.dtype),
                          compiler_params=pltpu.CompilerParams(vmem_limit_bytes=VMEM_LIMIT), name="sum_slots")(buf)


def _pack(arrs):
    flat = jnp.concatenate([a.reshape(-1).astype(f32) for a in arrs])
    rows = -(-flat.shape[0] // (8 * LANE)) * 8
    return jnp.pad(flat, (0, rows * LANE - flat.shape[0])).reshape(rows, LANE)


def _unpack(buf, shapes):
    flat = buf.reshape(-1)
    out, off = [], 0
    for s in shapes:
        n = int(np.prod(s))
        out.append(flat[off:off + n].reshape(s))
        off += n
    return out


def _adam_math(w, g, m, v):
    m2 = ADAM_B1 * m + (1.0 - ADAM_B1) * g
    v2 = ADAM_B2 * v + (1.0 - ADAM_B2) * (g * g)
    m_hat = m2 / (1.0 - ADAM_B1 ** ADAM_STEP)
    v_hat = v2 / (1.0 - ADAM_B2 ** ADAM_STEP)
    delta = -ADAM_LR * (m_hat / (jnp.sqrt(v_hat) + ADAM_EPS) + ADAM_WD * w)
    return delta, m2, v2


def _adam_big(name, w, m, v, parts, chip, after):
    layers, rows, cols = w.shape
    tr = 512 if rows % 512 == 0 else 256 if rows % 256 == 0 else rows // 4 if rows % 32 == 0 else 8

    def body(chip_ref, w_ref, m_ref, v_ref, *rest):
        p_refs = rest[:N_CHIP * layers]
        g_ref, d_ref, m2_ref, v2_ref = rest[N_CHIP * layers + 1:]
        for li in range(layers):
            @pl.when(pl.program_id(0) == li)
            def _(li=li):
                g = p_refs[N_CHIP * li][...].astype(f32)
                for q in range(1, N_CHIP):
                    g = g + p_refs[N_CHIP * li + q][...].astype(f32)
                delta, m2, v2 = _adam_math(w_ref[...], g, m_ref[...], v_ref[...])
                g_ref[...] = g
                d_ref[...] = delta
                m2_ref[...] = m2
                v2_ref[...] = v2

    spec = pl.BlockSpec((None, tr, cols), lambda l, i, c_ref: (l, i, 0))
    pspecs, operands = [], []
    for li in range(layers):
        for q in range(N_CHIP):
            pspecs.append(pl.BlockSpec((None, tr, cols),
                                       lambda l, i, c_ref, li=li, q=q: ((c_ref[0] + q) % N_CHIP, jnp.where(l == li, i, 0), 0)))
            operands.append(parts[li][0] if q == 0 else parts[li][1])
    out = jax.ShapeDtypeStruct((layers, rows, cols), f32)
    return pl.pallas_call(
        body,
        grid_spec=pltpu.PrefetchScalarGridSpec(num_scalar_prefetch=1, grid=(layers, rows // tr),
                                               in_specs=[spec, spec, spec] + pspecs + [ANY], out_specs=[spec] * 4),
        out_shape=[out] * 4, compiler_params=_params("arbitrary", "arbitrary"), name=name)(chip, w, m, v, *operands, after)


def _adam_small(w, g, m, v):
    rows = w.shape[0]

    def body(w_ref, g_ref, m_ref, v_ref, d_ref, m2_ref, v2_ref):
        delta, m2, v2 = _adam_math(w_ref[...], g_ref[...], m_ref[...], v_ref[...])
        d_ref[...] = delta
        m2_ref[...] = m2
        v2_ref[...] = v2

    out = jax.ShapeDtypeStruct((rows, LANE), f32)
    return pl.pallas_call(body, out_shape=[out] * 3, name="adam_small")(w, g, m, v)


def kernel(x, mix_norm, ab_w_in, ab_rel_bias, ab_conv_w, ab_w_out, c_w_in, c_ln_g, c_ln_b, c_w_s, c_b_s, c_w_out, ffn_norm, ffn_w_gate, ffn_w_up, ffn_w_down, final_norm, loss_target, m_mix_norm, m_ab_w_in, m_ab_rel_bias, m_ab_conv_w, m_ab_w_out, m_c_w_in, m_c_ln_g, m_c_ln_b, m_c_w_s, m_c_b_s, m_c_w_out, m_ffn_norm, m_ffn_w_gate, m_ffn_w_up, m_ffn_w_down, m_final_norm, v_mix_norm, v_ab_w_in, v_ab_rel_bias, v_ab_conv_w, v_ab_w_out, v_c_w_in, v_c_ln_g, v_c_ln_b, v_c_w_s, v_c_b_s, v_c_w_out, v_ffn_norm, v_ffn_w_gate, v_ffn_w_up, v_ffn_w_down, v_final_norm):
    d = D_MODEL
    a_width = d // 2
    heads = a_width // A_HEAD_DIM
    a_blocks = a_width // LANE
    b_blocks = (d - a_width) // LANE
    n_even, n_odd = (DEPTH + 1) // 2, DEPTH // 2
    me_s = 4 * lax.axis_index("x") + 2 * lax.axis_index("y") + lax.axis_index("c")
    me = me_s.astype(jnp.int32).reshape(1)
    core = lax.axis_index("c").astype(jnp.int32).reshape(1)
    chip = (2 * lax.axis_index("x") + lax.axis_index("y")).astype(jnp.int32).reshape(1)

    weights = dict(mix_norm=mix_norm, ab_w_in=ab_w_in, ab_rel_bias=ab_rel_bias, ab_conv_w=ab_conv_w, ab_w_out=ab_w_out,
                   c_w_in=c_w_in, c_ln_g=c_ln_g, c_ln_b=c_ln_b, c_w_s=c_w_s, c_b_s=c_b_s, c_w_out=c_w_out,
                   ffn_norm=ffn_norm, ffn_w_gate=ffn_w_gate, ffn_w_up=ffn_w_up, ffn_w_down=ffn_w_down, final_norm=final_norm)
    mom_m = dict(mix_norm=m_mix_norm, ab_w_in=m_ab_w_in, ab_rel_bias=m_ab_rel_bias, ab_conv_w=m_ab_conv_w, ab_w_out=m_ab_w_out,
                 c_w_in=m_c_w_in, c_ln_g=m_c_ln_g, c_ln_b=m_c_ln_b, c_w_s=m_c_w_s, c_b_s=m_c_b_s, c_w_out=m_c_w_out,
                 ffn_norm=m_ffn_norm, ffn_w_gate=m_ffn_w_gate, ffn_w_up=m_ffn_w_up, ffn_w_down=m_ffn_w_down, final_norm=m_final_norm)
    mom_v = dict(mix_norm=v_mix_norm, ab_w_in=v_ab_w_in, ab_rel_bias=v_ab_rel_bias, ab_conv_w=v_ab_conv_w, ab_w_out=v_ab_w_out,
                 c_w_in=v_c_w_in, c_ln_g=v_c_ln_g, c_ln_b=v_c_ln_b, c_w_s=v_c_w_s, c_b_s=v_c_b_s, c_w_out=v_c_w_out,
                 ffn_norm=v_ffn_norm, ffn_w_gate=v_ffn_w_gate, ffn_w_up=v_ffn_w_up, ffn_w_down=v_ffn_w_down, final_norm=v_final_norm)
    order = list(weights)
    wide = ("ffn_w_gate", "ffn_w_up")
    flip = lambda a: jnp.swapaxes(a, 1, 2)
    local = {k: (flip(weights[k]), flip(mom_m[k]), flip(mom_v[k])) if k in wide else (weights[k], mom_m[k], mom_v[k]) for k in order}

    sharded_small = [ab_conv_w, c_ln_g, c_ln_b]
    gathered = _gather_small("gather_small", _pack(sharded_small))
    conv_parts, lng_parts, lnb_parts = [], [], []
    for j in range(N_DEV):
        cw_j, lg_j, lb_j = _unpack(gathered[j], [a.shape for a in sharded_small])
        conv_parts.append(cw_j)
        lng_parts.append(lg_j)
        lnb_parts.append(lb_j)
    conv_full = jnp.concatenate(conv_parts, axis=-1)
    lng_full = jnp.concatenate(lng_parts, axis=-1)
    lnb_full = jnp.concatenate(lnb_parts, axis=-1)

    gate_t, up_t = local["ffn_w_gate"][0], local["ffn_w_up"][0]
    sets = []
    for layer in range(DEPTH):
        i = layer // 2
        if layer % 2 == 0:
            sets += [("ab_in%d" % i, [(ab_w_in, i)]), ("ab_out%d" % i, [(ab_w_out, i)])]
        else:
            sets += [("c_in%d" % i, [(c_w_in, i)]), ("c_out%d" % i, [(c_w_out, i)])]
        sets += [("ffn_in%d" % layer, [(gate_t, layer), (up_t, layer)]), ("ffn_out%d" % layer, [(ffn_w_down, layer)])]
    units = [None] * len(sets)
    cursor = [0]
    tokens = []

    def start_gather(k, after):
        if k < len(sets):
            tag, members = sets[k]
            units[k] = _Gather(tag, [_cast_into_slot("cast_slot", w, li, me, after) for w, li in members])
            tokens.append(units[k].advance(after))

    def next_weights(after):
        k = cursor[0]
        cursor[0] = k + 1
        ready = units[k].finish(after)
        for later in range(k + 1, min(k + len(_Gather.STAGES), len(sets)) if k else 2):
            tokens.append(units[later].advance(after))
        start_gather(k + GATHER_AHEAD, after)
        return ready

    def started():
        deps = list(tokens)
        tokens.clear()
        return deps

    xs = x[0]
    tgt = loss_target[0]
    start_gather(0, gathered)
    start_gather(1, gathered)
    tokens.append(units[0].advance(tokens[-1]))
    for k in range(2, GATHER_AHEAD):
        start_gather(k, tokens[-1])
    tokens.append(units[0].advance(tokens[-1]))
    saved = []
    h = _rms_fwd(xs, mix_norm[0])
    for layer in range(DEPTH):
        i = layer // 2
        (w_in_g,) = next_weights(xs)
        if layer % 2 == 0:
            proj = _mm_cols("ab_proj", h, w_in_g, bf16, started())
            btab = _bias_table(ab_rel_bias[i])
            attn = _attn_fwd(proj, btab, heads, d)
            mixed = _conv_fwd(proj, conv_full[i], attn, a_blocks, b_blocks)
            ctx = (proj, btab)
        else:
            proj = _mm_cols("c_proj", h, w_in_g, f32, started())
            bs_t = jnp.transpose(c_b_s[i])
            mixed = _sgu_fwd(proj, lng_full[i], lnb_full[i], c_w_s[i], bs_t)
            ctx = (proj, bs_t)
        (w_out_g,) = next_weights(mixed)
        w_out_full = w_out_g.reshape(-1, w_out_g.shape[-1])
        x1, h2 = _mm_rows_res("mix_out", mixed, w_out_full, xs, ffn_norm[layer], started())
        wg_g, wu_g = next_weights(x1)
        g_act, u_act, act = _ffn_in(h2, wg_g, wu_g, started())
        (wd_g,) = next_weights(g_act)
        nxt = _ffn_down(act, wd_g, x1, mix_norm[layer + 1] if layer + 1 < DEPTH else None, started())
        saved.append((xs, h, ctx, mixed, x1, h2, g_act, u_act, act, w_in_g, w_out_full, wg_g, wu_g, wd_g))
        xs, h = nxt if len(nxt) == 2 else (nxt[0], None)

    loss_part, dx, dxb, d_final = _loss_head(xs, final_norm, tgt)
    loss = lax.psum(loss_part[0, 0], ("x", "y", "c"))

    scatters = {}
    small = {k: [None] * weights[k].shape[0] for k in ("mix_norm", "ffn_norm", "ab_rel_bias", "ab_conv_w", "c_ln_g", "c_ln_b",
                                                       "c_w_s", "c_b_s")}
    for layer in reversed(range(DEPTH)):
        i = layer // 2
        xs, h, ctx, mixed, x1, h2, g_act, u_act, act, w_in_g, w_out_full, wg_g, wu_g, wd_g = saved[layer]
        dg, du = _ffn_bwd_act(dxb, wd_g, g_act, u_act, started())
        dwd = _ffn_dwd(act, dxb)
        dwg, dwu = _ffn_dwgu(h2, dg, du)
        rs_ffn = _ReduceScatter("ffn%d" % layer, [dwg, dwu, dwd], core)
        dh2 = _ffn_dh(dg, du, wg_g, wu_g, [rs_ffn.token])
        dx, dxb, dgn = _rms_bwd(x1, ffn_norm[layer], dh2, dx)
        small["ffn_norm"][layer] = dgn[0]
        rs_ffn.middle(dxb, core)
        for pos, k in enumerate(("ffn_w_gate", "ffn_w_up", "ffn_w_down")):
            scatters[(k, layer)] = (rs_ffn, pos)
        dmixed = _mm_nt("mix_out_bwd", dxb, w_out_full, bf16, [rs_ffn.token])
        dwout = _mm_tn_rows("mix_out_dw", mixed, dxb)
        if layer % 2 == 0:
            proj, btab = ctx
            dq, dk, dv, dtab = _attn_bwd(proj, dmixed, btab, heads)
            db, dc, dhv, dcw = _conv_bwd(proj, dmixed, conv_full[i], a_blocks, b_blocks)
            dproj = jnp.concatenate([dq, dk, dv, db, dc, dhv], axis=-1)
            small["ab_rel_bias"][i] = _bias_table_grad(dtab)
            small["ab_conv_w"][i] = dcw
            names = ("ab_w_in", "ab_w_out")
            tag = "ab"
        else:
            proj, bs_t = ctx
            dproj, dws, dbs_t, dlg, dlb = _sgu_bwd(proj, dmixed, lng_full[i], lnb_full[i], c_w_s[i], bs_t)
            small["c_w_s"][i] = dws
            small["c_b_s"][i] = jnp.transpose(dbs_t)
            small["c_ln_g"][i] = dlg[0]
            small["c_ln_b"][i] = dlb[0]
            names = ("c_w_in", "c_w_out")
            tag = "c"
        dwin = _mm_tn_cols(tag + "_proj_dw", h, dproj)
        rs_mix = _ReduceScatter("%s%d" % (tag, i), [dwin, dwout], core)
        dh = _mm_nt_cols(tag + "_proj_bwd", dproj, w_in_g, [rs_mix.token])
        dx, dxb, dgm = _rms_bwd(xs, mix_norm[layer], dh, dx)
        small["mix_norm"][layer] = dgm[0]
        rs_mix.middle(dxb, core)
        tokens.append(rs_mix.token)
        scatters[(names[0], i)] = (rs_mix, 0)
        scatters[(names[1], i)] = (rs_mix, 1)
    grad_x = dx[None]

    small_names = ["mix_norm", "ffn_norm", "ab_rel_bias", "ab_conv_w", "c_ln_g", "c_ln_b", "c_w_s", "c_b_s"]
    small_full = [jnp.stack(small[k]) for k in small_names] + [d_final[0]]
    small_slot = _cast_into_slot("small_slot", _pack(small_full)[None], 0, me, started()[-1], f32)
    small_sems, (small_buf,), last = _split_start("small_sum_start", [small_slot], _plan_all_peers, N_DEV - 1, core)

    grads, deltas, new_m, new_v = {}, {}, {}, {}
    finished = {}
    for k in ("c_w_in", "c_w_out", "ffn_w_gate", "ffn_w_up", "ffn_w_down", "ab_w_in", "ab_w_out"):
        parts = []
        w_k, m_k, v_k = local[k]
        for li in range(w_k.shape[0]):
            rs, pos = scatters[(k, li)]
            if id(rs) not in finished:
                finished[id(rs)] = rs.finish(last)
            parts.append(finished[id(rs)][pos])
        outs = _adam_big("adam_" + k, w_k, m_k, v_k, parts, chip, last)
        last = outs[1]
        grads[k], deltas[k], new_m[k], new_v[k] = [flip(o) for o in outs] if k in wide else outs

    (small_buf,) = _split_wait("small_sum_wait", [small_buf], small_sems, _plan_all_peers, last)
    summed = _unpack(_sum_slots(small_buf), [a.shape for a in small_full])
    small_grads = dict(zip(small_names + ["final_norm"], summed))
    for k in ("ab_conv_w", "c_ln_g", "c_ln_b"):
        width = weights[k].shape[-1]
        small_grads[k] = lax.dynamic_slice_in_dim(small_grads[k], me_s * width, width, axis=-1)
    small_order = [k for k in order if k in small_grads]
    shapes = [weights[k].shape for k in small_order]
    d_s, m_s, v_s = _adam_small(_pack([weights[k] for k in small_order]), _pack([small_grads[k] for k in small_order]),
                                _pack([mom_m[k] for k in small_order]), _pack([mom_v[k] for k in small_order]))
    grads.update(small_grads)
    for k, dd, mm, vv in zip(small_order, _unpack(d_s, shapes), _unpack(m_s, shapes), _unpack(v_s, shapes)):
        deltas[k], new_m[k], new_v[k] = dd, mm, vv

    return (loss, grad_x, *[grads[k] for k in order], *[deltas[k] for k in order], *[new_m[k] for k in order],
            *[new_v[k] for k in order])
```

```python
import numpy as np
import jax
import jax.numpy as jnp
from jax import lax
from jax.experimental import pallas as pl
from jax.experimental.pallas import tpu as pltpu

D_MODEL = 2048
SEQ = 2048
DEPTH = 4
CHUNK = 64
A_HEAD_DIM = 128
A_LEFT_CHUNKS = 8
A_MAX_REL = 256
CONV_WIDTH = 3
C_BLOCK = 128
C_GROUPS = 8
EPS = 1e-6
NEG_INF = -1e30

ADAM_LR = 0.001
ADAM_B1 = 0.9
ADAM_B2 = 0.999
ADAM_EPS = 1e-08
ADAM_WD = 0.01
ADAM_STEP = 10

N_DEV = 8
N_CHIP = 4
RS_SPLIT = 4
D2D_SPLIT = 2
GATHER_AHEAD = 4
ROWS_PER_STEP = 1024
COL_CHUNK = 256
LANE = 128
VMEM_LIMIT = 52 * 1024 * 1024

bf16 = jnp.bfloat16
f32 = jnp.float32
MESH = pl.DeviceIdType.MESH
ANY = pl.BlockSpec(memory_space=pl.ANY)


def _params(*sem):
    return pltpu.CompilerParams(dimension_semantics=sem, vmem_limit_bytes=VMEM_LIMIT)


def _perm(j):
    return (j % 2) * N_CHIP + j // 2


_DN = {"nn": (((1,), (0,)), ((), ())), "nt": (((1,), (1,)), ((), ())), "tn": (((0,), (0,)), ((), ()))}


def _matmul(name, mode, grid, operands, specs, pairs, n_acc, acc_shape, extras, extra_specs, out_shapes, out_specs,
            epilogue, chunk=0):
    nk = grid[2]
    n_op, n_ex, n_out = len(operands), len(extras), len(out_shapes)

    def single(*refs):
        ops = refs[:n_op]
        ex = refs[n_op:n_op + n_ex]
        outs = refs[n_op + n_ex:]
        width = acc_shape[1]
        starts = range(0, width, chunk) if chunk else (0,)
        for c0 in starts:
            cols = slice(c0, min(c0 + chunk, width)) if chunk else slice(None)
            sums = [None] * n_acc
            for p, (ia, ib) in enumerate(pairs):
                b = ops[ib][cols, :] if mode == "nt" else ops[ib][:, cols]
                d = lax.dot_general(ops[ia][...], b, _DN[mode], preferred_element_type=f32)
                sums[p % n_acc] = d if sums[p % n_acc] is None else sums[p % n_acc] + d
            epilogue(sums, ex, outs, cols)

    def body(*refs):
        ops = refs[:n_op]
        ex = refs[n_op:n_op + n_ex]
        outs = refs[n_op + n_ex:n_op + n_ex + n_out]
        accs = refs[n_op + n_ex + n_out:]
        k = pl.program_id(2)

        @pl.when(k == 0)
        def _():
            for acc in accs:
                acc[...] = jnp.zeros_like(acc)

        for p, (ia, ib) in enumerate(pairs):
            acc = accs[p % n_acc]
            acc[...] += lax.dot_general(ops[ia][...], ops[ib][...], _DN[mode], preferred_element_type=f32)

        @pl.when(k == nk - 1)
        def _():
            epilogue([acc[...] for acc in accs], ex, outs, slice(None))

    return pl.pallas_call(
        single if nk == 1 else body, grid=grid, in_specs=list(specs) + list(extra_specs), out_specs=list(out_specs),
        out_shape=list(out_shapes), scratch_shapes=[] if nk == 1 else [pltpu.VMEM(acc_shape, f32)] * n_acc,
        compiler_params=_params("parallel", "parallel", "arbitrary"), name=name)(*operands, *extras)


def _store(dtype):
    def ep(accs, ex, outs, cols):
        for a, o in zip(accs, outs):
            o[:, cols] = a.astype(dtype)
    return ep


def _mm_cols(name, h, wg, out_dtype, deps=()):
    t, kd = h.shape
    n8 = wg.shape[2]
    tm = min(t, 2 * ROWS_PER_STEP)
    return _matmul(
        name, "nn", (t // tm, N_DEV, 1), [h, wg],
        [pl.BlockSpec((tm, kd), lambda i, j, k: (i, 0)), pl.BlockSpec((None, kd, n8), lambda i, j, k: (j, 0, 0))],
        [(0, 1)], 1, (tm, n8), list(deps), [ANY] * len(deps), [jax.ShapeDtypeStruct((t, N_DEV * n8), out_dtype)],
        [pl.BlockSpec((tm, n8), lambda i, j, k: (i, j))], _store(out_dtype), COL_CHUNK)[0]


def _norm_rows(xv, gain):
    return (xv * lax.rsqrt(jnp.mean(xv * xv, axis=-1, keepdims=True) + EPS) * gain).astype(bf16)


def _mm_rows_res(name, a, w, res, gain, deps=()):
    t, kd = a.shape
    n = w.shape[1]
    tm = min(t, 512)

    def ep(accs, ex, outs, cols):
        xv = ex[0][...] + accs[0]
        outs[0][...] = xv
        outs[1][...] = _norm_rows(xv, ex[1][...])

    row = pl.BlockSpec((tm, n), lambda i, j, k: (i, 0))
    return _matmul(
        name, "nn", (t // tm, 1, 1), [a, w],
        [pl.BlockSpec((tm, kd), lambda i, j, k: (i, 0)), pl.BlockSpec((kd, n), lambda i, j, k: (0, 0))],
        [(0, 1)], 1, (tm, n), [res, gain.reshape(1, n)] + list(deps),
        [row, pl.BlockSpec((1, n), lambda i, j, k: (0, 0))] + [ANY] * len(deps),
        [jax.ShapeDtypeStruct((t, n), f32), jax.ShapeDtypeStruct((t, n), bf16)], [row, row], ep)


def _mm_nt(name, a, w, out_dtype, deps=()):
    t, n = a.shape
    kd = w.shape[0]
    tm, tn = min(t, 2 * ROWS_PER_STEP), min(kd, 1024)
    return _matmul(
        name, "nt", (t // tm, kd // tn, 1), [a, w],
        [pl.BlockSpec((tm, n), lambda i, j, k: (i, 0)), pl.BlockSpec((tn, n), lambda i, j, k: (j, 0))],
        [(0, 1)], 1, (tm, tn), list(deps), [ANY] * len(deps), [jax.ShapeDtypeStruct((t, kd), out_dtype)],
        [pl.BlockSpec((tm, tn), lambda i, j, k: (i, j))], _store(out_dtype), COL_CHUNK)[0]


def _mm_nt_cols(name, da, wg, deps=()):
    t = da.shape[0]
    kd, n8 = wg.shape[1], wg.shape[2]
    tm = min(t, ROWS_PER_STEP)
    return _matmul(
        name, "nt", (t // tm, 1, N_DEV), [da, wg],
        [pl.BlockSpec((tm, n8), lambda i, j, k: (i, k)), pl.BlockSpec((None, kd, n8), lambda i, j, k: (k, 0, 0))],
        [(0, 1)], 1, (tm, kd), list(deps), [ANY] * len(deps), [jax.ShapeDtypeStruct((t, kd), bf16)],
        [pl.BlockSpec((tm, kd), lambda i, j, k: (i, 0))], _store(bf16))[0]


def _mm_nt_cols_norm(name, da, wg, x, g, dres, deps=()):
    t = da.shape[0]
    kd, n8 = wg.shape[1], wg.shape[2]
    tm = min(t, 512)

    def ep(accs, ex, outs, cols):
        xv = ex[0][...]
        dy = accs[0]
        r = lax.rsqrt(jnp.mean(xv * xv, axis=-1, keepdims=True) + EPS)
        gy = dy * ex[1][...]
        dot = jnp.mean(xv * gy, axis=-1, keepdims=True)
        dx = ex[2][...] + r * gy - xv * (r * r * r * dot)
        outs[0][...] = dx
        outs[1][...] = dx.astype(bf16)
        outs[2][...] = jnp.sum(dy * xv * r, axis=0, keepdims=True)

    row = pl.BlockSpec((tm, kd), lambda i, j, k: (i, 0))
    return _matmul(
        name, "nt", (t // tm, 1, N_DEV), [da, wg],
        [pl.BlockSpec((tm, n8), lambda i, j, k: (i, k)), pl.BlockSpec((None, kd, n8), lambda i, j, k: (k, 0, 0))],
        [(0, 1)], 1, (tm, kd), [x, g.reshape(1, kd), dres] + list(deps),
        [row, pl.BlockSpec((1, kd), lambda i, j, k: (0, 0)), row] + [ANY] * len(deps),
        [jax.ShapeDtypeStruct((t, kd), f32), jax.ShapeDtypeStruct((t, kd), bf16), jax.ShapeDtypeStruct((t // tm, 1, kd), f32)],
        [row, row, pl.BlockSpec((None, 1, kd), lambda i, j, k: (i, 0, 0))], ep)


def _mm_tn_cols(name, h, da):
    t, kd = h.shape
    n8 = da.shape[1] // N_DEV
    tmk, tk = min(kd, 1024), min(t, 2048)
    return _matmul(
        name, "tn", (kd // tmk, N_DEV, t // tk), [h, da],
        [pl.BlockSpec((tk, tmk), lambda i, j, k: (k, i)), pl.BlockSpec((tk, n8), lambda i, j, k: (k, j))],
        [(0, 1)], 1, (tmk, n8), [], [], [jax.ShapeDtypeStruct((N_DEV, kd, n8), bf16)],
        [pl.BlockSpec((None, tmk, n8), lambda i, j, k: (_perm(j), i, 0))], _store(bf16))[0]


def _mm_tn_rows(name, a, dx):
    t, kf = a.shape
    r8 = kf // N_DEV
    n = dx.shape[1]
    return _matmul(
        name, "tn", (N_DEV, 1, 1), [a, dx],
        [pl.BlockSpec((t, r8), lambda i, j, k: (0, i)), pl.BlockSpec((t, n), lambda i, j, k: (0, 0))],
        [(0, 1)], 1, (r8, n), [], [], [jax.ShapeDtypeStruct((N_DEV, r8, n), bf16)],
        [pl.BlockSpec((None, r8, n), lambda i, j, k: (_perm(i), 0, 0))], _store(bf16))[0]


def _ffn_in(h2, wg_t, wu_t, deps=()):
    t, kd = h2.shape
    f8 = wg_t.shape[1]
    tm = min(t, ROWS_PER_STEP)

    def ep(accs, ex, outs, cols):
        g, u = accs
        outs[0][:, cols] = g.astype(bf16)
        outs[1][:, cols] = u.astype(bf16)
        outs[2][:, cols] = (g * jax.nn.sigmoid(g) * u).astype(bf16)

    wspec = pl.BlockSpec((None, f8, kd), lambda i, j, k: (j, 0, 0))
    ospec = pl.BlockSpec((None, tm, f8), lambda i, j, k: (j, i, 0))
    return _matmul(
        "ffn_in", "nt", (t // tm, N_DEV, 1), [h2, wg_t, wu_t],
        [pl.BlockSpec((tm, kd), lambda i, j, k: (i, 0)), wspec, wspec], [(0, 1), (0, 2)], 2, (tm, f8), list(deps),
        [ANY] * len(deps), [jax.ShapeDtypeStruct((N_DEV, t, f8), bf16)] * 3, [ospec] * 3, ep, COL_CHUNK)


def _ffn_down(act, wd, res, gain, deps=()):
    _, t, f8 = act.shape
    n = wd.shape[2]
    tm = min(t, 512)

    def ep(accs, ex, outs, cols):
        xv = ex[0][...] + accs[0]
        outs[0][...] = xv
        if gain is not None:
            outs[1][...] = _norm_rows(xv, ex[1][...])

    row = pl.BlockSpec((tm, n), lambda i, j, k: (i, 0))
    extras, extra_specs = [res], [row]
    shapes, specs = [jax.ShapeDtypeStruct((t, n), f32)], [row]
    if gain is not None:
        extras.append(gain.reshape(1, n))
        extra_specs.append(pl.BlockSpec((1, n), lambda i, j, k: (0, 0)))
        shapes.append(jax.ShapeDtypeStruct((t, n), bf16))
        specs.append(row)
    return _matmul(
        "ffn_down", "nn", (t // tm, 1, N_DEV), [act, wd],
        [pl.BlockSpec((None, tm, f8), lambda i, j, k: (k, i, 0)), pl.BlockSpec((None, f8, n), lambda i, j, k: (k, 0, 0))],
        [(0, 1)], 1, (tm, n), extras + list(deps), extra_specs + [ANY] * len(deps), shapes, specs, ep)


def _ffn_bwd_act(dxb, wd, g, u, deps=()):
    t, n = dxb.shape
    f8 = wd.shape[1]
    tm = min(t, ROWS_PER_STEP)

    def ep(accs, ex, outs, cols):
        dact = accs[0]
        gv = ex[0][:, cols].astype(f32)
        uv = ex[1][:, cols].astype(f32)
        sg = jax.nn.sigmoid(gv)
        silu = gv * sg
        outs[0][:, cols] = (dact * uv * (sg * (1.0 + gv * (1.0 - sg)))).astype(bf16)
        outs[1][:, cols] = (dact * silu).astype(bf16)

    bspec = pl.BlockSpec((None, tm, f8), lambda i, j, k: (j, i, 0))
    return _matmul(
        "ffn_bwd_act", "nt", (t // tm, N_DEV, 1), [dxb, wd],
        [pl.BlockSpec((tm, n), lambda i, j, k: (i, 0)), pl.BlockSpec((None, f8, n), lambda i, j, k: (j, 0, 0))],
        [(0, 1)], 1, (tm, f8), [g, u] + list(deps), [bspec, bspec] + [ANY] * len(deps),
        [jax.ShapeDtypeStruct((N_DEV, t, f8), bf16)] * 2, [bspec] * 2, ep, COL_CHUNK)


def _ffn_dwd(act, dxb):
    _, t, f8 = act.shape
    n = dxb.shape[1]
    tk = min(t, 2048)
    return _matmul(
        "ffn_dwd", "tn", (N_DEV, 1, t // tk), [act, dxb],
        [pl.BlockSpec((None, tk, f8), lambda i, j, k: (i, k, 0)), pl.BlockSpec((tk, n), lambda i, j, k: (k, 0))],
        [(0, 1)], 1, (f8, n), [], [], [jax.ShapeDtypeStruct((N_DEV, f8, n), bf16)],
        [pl.BlockSpec((None, f8, n), lambda i, j, k: (_perm(i), 0, 0))], _store(bf16))[0]


def _ffn_dwgu(h2, dg, du):
    t, kd = h2.shape
    f8 = dg.shape[2]
    tk, tn = min(t, 2048), min(kd, 1024)
    aspec = pl.BlockSpec((None, tk, f8), lambda i, j, k: (i, k, 0))
    ospec = pl.BlockSpec((None, f8, tn), lambda i, j, k: (_perm(i), 0, j))
    return _matmul(
        "ffn_dwgu", "tn", (N_DEV, kd // tn, t // tk), [dg, du, h2],
        [aspec, aspec, pl.BlockSpec((tk, tn), lambda i, j, k: (k, j))], [(0, 2), (1, 2)], 2, (f8, tn), [], [],
        [jax.ShapeDtypeStruct((N_DEV, f8, kd), bf16)] * 2, [ospec] * 2, _store(bf16))


def _ffn_dh(dg, du, wg_t, wu_t, deps=()):
    _, t, f8 = dg.shape
    kd = wg_t.shape[2]
    tm = min(t, ROWS_PER_STEP)
    aspec = pl.BlockSpec((None, tm, f8), lambda i, j, k: (k, i, 0))
    wspec = pl.BlockSpec((None, f8, kd), lambda i, j, k: (k, 0, 0))
    return _matmul(
        "ffn_dh", "nn", (t // tm, 1, N_DEV), [dg, du, wg_t, wu_t], [aspec, aspec, wspec, wspec], [(0, 2), (1, 3)], 1,
        (tm, kd), list(deps), [ANY] * len(deps), [jax.ShapeDtypeStruct((t, kd), bf16)],
        [pl.BlockSpec((tm, kd), lambda i, j, k: (i, 0))], _store(bf16))[0]


def _rms_fwd(x, g):
    t, d = x.shape
    tm = min(t, 256)

    def body(x_ref, g_ref, o_ref):
        xv = x_ref[...]
        r = lax.rsqrt(jnp.mean(xv * xv, axis=-1, keepdims=True) + EPS)
        o_ref[...] = (xv * r * g_ref[...]).astype(bf16)

    return pl.pallas_call(
        body, grid=(t // tm,), in_specs=[pl.BlockSpec((tm, d), lambda i: (i, 0)), pl.BlockSpec((1, d), lambda i: (0, 0))],
        out_specs=pl.BlockSpec((tm, d), lambda i: (i, 0)), out_shape=jax.ShapeDtypeStruct((t, d), bf16),
        compiler_params=_params("parallel"), name="rms_fwd")(x, g.reshape(1, d))


def _rms_bwd(x, g, dh, dres):
    t, d = x.shape
    tm = min(t, 512)

    def body(x_ref, g_ref, dh_ref, dres_ref, dx_ref, dxb_ref, dg_ref):
        xv = x_ref[...]
        dy = dh_ref[...].astype(f32)
        r = lax.rsqrt(jnp.mean(xv * xv, axis=-1, keepdims=True) + EPS)
        gy = dy * g_ref[...]
        dot = jnp.mean(xv * gy, axis=-1, keepdims=True)
        dx = dres_ref[...] + r * gy - xv * (r * r * r * dot)
        dx_ref[...] = dx
        dxb_ref[...] = dx.astype(bf16)

        @pl.when(pl.program_id(0) == 0)
        def _():
            dg_ref[...] = jnp.zeros_like(dg_ref)

        dg_ref[...] += jnp.sum(dy * xv * r, axis=0, keepdims=True)

    row = pl.BlockSpec((tm, d), lambda i: (i, 0))
    vec = pl.BlockSpec((1, d), lambda i: (0, 0))
    return pl.pallas_call(
        body, grid=(t // tm,), in_specs=[row, vec, row, row], out_specs=[row, row, vec],
        out_shape=[jax.ShapeDtypeStruct((t, d), f32), jax.ShapeDtypeStruct((t, d), bf16), jax.ShapeDtypeStruct((1, d), f32)],
        compiler_params=_params("arbitrary"), name="rms_bwd")(x, g.reshape(1, d), dh, dres)


def _loss_head(x, g, target):
    t, d = x.shape
    tm = min(t, 512)

    def body(x_ref, g_ref, t_ref, loss_ref, dx_ref, dxb_ref, dg_ref):
        xv = x_ref[...]
        r = lax.rsqrt(jnp.mean(xv * xv, axis=-1, keepdims=True) + EPS)
        xn = xv * r
        err = xn * g_ref[...] - t_ref[...]
        dy = err * (1.0 / d)
        gy = dy * g_ref[...]
        dot = jnp.mean(xv * gy, axis=-1, keepdims=True)
        dx = r * gy - xv * (r * r * r * dot)
        dx_ref[...] = dx
        dxb_ref[...] = dx.astype(bf16)

        @pl.when(pl.program_id(0) == 0)
        def _():
            dg_ref[...] = jnp.zeros_like(dg_ref)
            loss_ref[...] = jnp.zeros_like(loss_ref)

        dg_ref[...] += jnp.sum(dy * xn, axis=0, keepdims=True)
        loss_ref[...] += 0.5 * jnp.sum(jnp.sum(err * err, axis=-1, keepdims=True) * (1.0 / d), axis=0, keepdims=True)

    row = pl.BlockSpec((tm, d), lambda i: (i, 0))
    vec = pl.BlockSpec((1, d), lambda i: (0, 0))
    one = pl.BlockSpec((1, 1), lambda i: (0, 0))
    return pl.pallas_call(
        body, grid=(t // tm,), in_specs=[row, vec, row], out_specs=[one, row, row, vec],
        out_shape=[jax.ShapeDtypeStruct((1, 1), f32), jax.ShapeDtypeStruct((t, d), f32),
                   jax.ShapeDtypeStruct((t, d), bf16), jax.ShapeDtypeStruct((1, d), f32)],
        compiler_params=_params("arbitrary"), name="loss_head")(x, g.reshape(1, d), target)


def _attn_consts():
    qt, kw = 2 * CHUNK, (A_LEFT_CHUNKS + 2) * CHUNK
    r = np.arange(qt)[:, None]
    kc = np.arange(kw)[None, :]
    rel = np.clip(r + A_LEFT_CHUNKS * CHUNK - kc, -A_MAX_REL, A_MAX_REL) + A_MAX_REL
    dchunk = kc // CHUNK - r // CHUNK
    valid = (dchunk >= 0) & (dchunk <= A_LEFT_CHUNKS)
    m = np.arange(kw + qt)
    relidx = np.clip(A_LEFT_CHUNKS * CHUNK - (m - (qt - 1)), -A_MAX_REL, A_MAX_REL) + A_MAX_REL
    onehot = np.zeros((kw + qt, 2 * A_MAX_REL + 1), np.float32)
    onehot[m, relidx] = 1.0
    return qt, kw, rel, valid, onehot


def _bias_table(rel_bias):
    qt, kw, _, valid, onehot = _attn_consts()
    h = rel_bias.shape[0]
    w = kw + qt
    relidx = np.argmax(onehot, axis=1)
    e = jnp.roll(jnp.take(rel_bias, jnp.asarray(relidx), axis=1), -(qt - 1), axis=1)
    rows = jnp.broadcast_to(e[:, None, :], (h, qt, w)).reshape(h, qt * w)
    skew = rows[:, :qt * (w - 1)].reshape(h, qt, w - 1)[:, :, :kw]
    return jnp.where(jnp.asarray(valid)[None], skew, NEG_INF).astype(f32)


def _bias_table_grad(dtab):
    qt, kw, _, _, onehot = _attn_consts()
    h = dtab.shape[0]
    w = kw + qt
    wide = -(-(w + qt) // LANE) * LANE
    y = jnp.pad(dtab, ((0, 0), (0, 0), (qt - 1, wide - kw - (qt - 1))))
    flat = jnp.pad(y.reshape(h, qt * wide), ((0, 0), (0, qt)))
    de = jnp.sum(flat.reshape(h, qt, wide + 1), axis=1)[:, :w]
    return jnp.dot(de, jnp.asarray(onehot), precision=lax.Precision.HIGHEST)


def _attn_scores(q_ref, kpad, btab_ref, r0, qt, kw, pad):
    qv = q_ref[pl.ds(r0, qt), :]
    kwin = kpad[pl.ds(r0, kw), :]
    s = lax.dot_general(qv, kwin, _DN["nt"], preferred_element_type=f32) * (A_HEAD_DIM ** -0.5) + btab_ref[...]
    kcol = lax.broadcasted_iota(jnp.int32, (qt, kw), 1)
    s = jnp.where(r0 + kcol >= pad, s, NEG_INF)
    p = jnp.exp(s - jnp.max(s, axis=-1, keepdims=True))
    return qv, kwin, p / jnp.sum(p, axis=-1, keepdims=True)


def _attn_fwd(proj, btab, heads, width):
    t = proj.shape[0]
    qt, kw = btab.shape[1], btab.shape[2]
    pad = kw - qt

    def body(q_ref, k_ref, v_ref, btab_ref, o_ref, kpad, vpad):
        zeros = jnp.zeros((pad, A_HEAD_DIM), bf16)
        kpad[pl.ds(0, pad), :] = zeros
        vpad[pl.ds(0, pad), :] = zeros
        kpad[pl.ds(pad, t), :] = k_ref[...]
        vpad[pl.ds(pad, t), :] = v_ref[...]

        def tile(i, carry):
            r0 = pl.multiple_of(i * qt, qt)
            _, _, p = _attn_scores(q_ref, kpad, btab_ref, r0, qt, kw, pad)
            o = lax.dot_general(p.astype(bf16), vpad[pl.ds(r0, kw), :], _DN["nn"], preferred_element_type=f32)
            o_ref[pl.ds(r0, qt), :] = o.astype(bf16)
            return carry

        lax.fori_loop(0, t // qt, tile, 0, unroll=4)

    col = lambda off: pl.BlockSpec((t, A_HEAD_DIM), lambda h, off=off: (0, off + h))
    return pl.pallas_call(
        body, grid=(heads,),
        in_specs=[col(0), col(heads), col(2 * heads), pl.BlockSpec((None, qt, kw), lambda h: (h, 0, 0))],
        out_specs=col(0), out_shape=jax.ShapeDtypeStruct((t, width), bf16),
        scratch_shapes=[pltpu.VMEM((t + pad, A_HEAD_DIM), bf16)] * 2,
        compiler_params=_params("parallel"), name="attn_fwd")(proj, proj, proj, btab)


def _attn_bwd(proj, dmix, btab, heads):
    t = proj.shape[0]
    qt, kw = btab.shape[1], btab.shape[2]
    pad = kw - qt
    scale = A_HEAD_DIM ** -0.5

    def body(q_ref, k_ref, v_ref, do_ref, btab_ref, dq_ref, dk_ref, dv_ref, dtab_ref, kpad, vpad, dkacc, dvacc):
        zeros = jnp.zeros((pad, A_HEAD_DIM), bf16)
        kpad[pl.ds(0, pad), :] = zeros
        vpad[pl.ds(0, pad), :] = zeros
        kpad[pl.ds(pad, t), :] = k_ref[...]
        vpad[pl.ds(pad, t), :] = v_ref[...]
        dkacc[...] = jnp.zeros_like(dkacc)
        dvacc[...] = jnp.zeros_like(dvacc)
        dtab_ref[...] = jnp.zeros_like(dtab_ref)

        def tile(i, carry):
            r0 = pl.multiple_of(i * qt, qt)
            qv, kwin, p = _attn_scores(q_ref, kpad, btab_ref, r0, qt, kw, pad)
            dov = do_ref[pl.ds(r0, qt), :]
            dp = lax.dot_general(dov, vpad[pl.ds(r0, kw), :], _DN["nt"], preferred_element_type=f32)
            ds = p * (dp - jnp.sum(p * dp, axis=-1, keepdims=True))
            dtab_ref[...] += ds
            dsb = ds.astype(bf16)
            dq = lax.dot_general(dsb, kwin, _DN["nn"], preferred_element_type=f32) * scale
            dq_ref[pl.ds(r0, qt), :] = dq.astype(bf16)
            dkacc[pl.ds(r0, kw), :] += lax.dot_general(dsb, qv, _DN["tn"], preferred_element_type=f32) * scale
            dvacc[pl.ds(r0, kw), :] += lax.dot_general(p.astype(bf16), dov, _DN["tn"], preferred_element_type=f32)
            return carry

        lax.fori_loop(0, t // qt, tile, 0, unroll=4)
        dk_ref[...] = dkacc[pl.ds(pad, t), :].astype(bf16)
        dv_ref[...] = dvacc[pl.ds(pad, t), :].astype(bf16)

    col = lambda off: pl.BlockSpec((t, A_HEAD_DIM), lambda h, off=off: (0, off + h))
    tab = pl.BlockSpec((None, qt, kw), lambda h: (h, 0, 0))
    wide = jax.ShapeDtypeStruct((t, heads * A_HEAD_DIM), bf16)
    return pl.pallas_call(
        body, grid=(heads,), in_specs=[col(0), col(heads), col(2 * heads), col(0), tab],
        out_specs=[col(0), col(0), col(0), tab],
        out_shape=[wide, wide, wide, jax.ShapeDtypeStruct((heads, qt, kw), f32)],
        scratch_shapes=[pltpu.VMEM((t + pad, A_HEAD_DIM), bf16)] * 2 + [pltpu.VMEM((t + pad, A_HEAD_DIM), f32)] * 2,
        compiler_params=_params("parallel"), name="attn_bwd")(proj, proj, proj, dmix, btab)


def _shift_down(z, k):
    rows = lax.broadcasted_iota(jnp.int32, z.shape, 0)
    return jnp.where(rows >= k, pltpu.roll(z, k, 0), 0.0)


def _shift_up(z, k):
    t = z.shape[0]
    rows = lax.broadcasted_iota(jnp.int32, z.shape, 0)
    return jnp.where(rows < t - k, pltpu.roll(z, t - k, 0), 0.0)


def _conv_fwd(proj, conv_w, attn_wide, a_blocks, b_blocks):
    t = proj.shape[0]

    def body(b_ref, c_ref, h_ref, w_ref, wide_ref, o_ref):
        z = c_ref[...].astype(f32) * h_ref[...].astype(f32)
        w = w_ref[...]
        y = w[0:1, :] * _shift_down(z, 2) + w[1:2, :] * _shift_down(z, 1) + w[2:3, :] * z
        o_ref[...] = (b_ref[...].astype(f32) * y).astype(bf16)

    col = lambda off: pl.BlockSpec((t, LANE), lambda i, off=off: (0, off + i))
    return pl.pallas_call(
        body, grid=(b_blocks,),
        in_specs=[col(3 * a_blocks), col(3 * a_blocks + b_blocks), col(3 * a_blocks + 2 * b_blocks),
                  pl.BlockSpec((CONV_WIDTH, LANE), lambda i: (0, i)), ANY],
        out_specs=col(a_blocks), out_shape=jax.ShapeDtypeStruct((t, (a_blocks + b_blocks) * LANE), bf16),
        input_output_aliases={4: 0},
        compiler_params=_params("parallel"), name="conv_fwd")(proj, proj, proj, conv_w, attn_wide)


def _conv_bwd(proj, dmix, conv_w, a_blocks, b_blocks):
    t = proj.shape[0]

    def body(b_ref, c_ref, h_ref, do_ref, w_ref, db_ref, dc_ref, dh_ref, dw_ref):
        bv, cv, hv = b_ref[...].astype(f32), c_ref[...].astype(f32), h_ref[...].astype(f32)
        w = w_ref[...]
        z = cv * hv
        z1, z2 = _shift_down(z, 1), _shift_down(z, 2)
        y = w[0:1, :] * z2 + w[1:2, :] * z1 + w[2:3, :] * z
        dov = do_ref[...].astype(f32)
        db_ref[...] = (dov * y).astype(bf16)
        dy = dov * bv
        dz = w[2:3, :] * dy + w[1:2, :] * _shift_up(dy, 1) + w[0:1, :] * _shift_up(dy, 2)
        dc_ref[...] = (dz * hv).astype(bf16)
        dh_ref[...] = (dz * cv).astype(bf16)
        dw_ref[0:1, :] = jnp.sum(dy * z2, axis=0, keepdims=True)
        dw_ref[1:2, :] = jnp.sum(dy * z1, axis=0, keepdims=True)
        dw_ref[2:3, :] = jnp.sum(dy * z, axis=0, keepdims=True)

    col = lambda off: pl.BlockSpec((t, LANE), lambda i, off=off: (0, off + i))
    wspec = pl.BlockSpec((CONV_WIDTH, LANE), lambda i: (0, i))
    wide = jax.ShapeDtypeStruct((t, b_blocks * LANE), bf16)
    return pl.pallas_call(
        body, grid=(b_blocks,),
        in_specs=[col(3 * a_blocks), col(3 * a_blocks + b_blocks), col(3 * a_blocks + 2 * b_blocks), col(a_blocks), wspec],
        out_specs=[col(0), col(0), col(0), wspec],
        out_shape=[wide, wide, wide, jax.ShapeDtypeStruct((CONV_WIDTH, b_blocks * LANE), f32)],
        compiler_params=_params("parallel"), name="conv_bwd")(proj, proj, proj, dmix, conv_w)


_RSQRT2 = 0.7071067811865476
_RSQRT2PI = 0.3989422804014327


def _gelu(x):
    return 0.5 * x * (1.0 + lax.erf(x * _RSQRT2))


def _gelu_grad(x):
    return 0.5 * (1.0 + lax.erf(x * _RSQRT2)) + x * jnp.exp(-0.5 * x * x) * _RSQRT2PI


def _sgu_common(a_ref, lg_ref, lb_ref, cw):
    av = a_ref[...]
    u = _gelu(av[:, :cw])
    v = _gelu(av[:, cw:])
    mu = jnp.mean(v, axis=-1, keepdims=True)
    xc = v - mu
    rstd = lax.rsqrt(jnp.mean(xc * xc, axis=-1, keepdims=True) + EPS)
    xhat = xc * rstd
    vln = xhat * lg_ref[...] + lb_ref[...]
    pos_t = lax.broadcasted_iota(jnp.int32, (C_BLOCK, C_BLOCK), 0) // CHUNK
    pos_s = lax.broadcasted_iota(jnp.int32, (C_BLOCK, C_BLOCK), 1) // CHUNK
    return av, u, xhat, rstd, vln, pos_s <= pos_t


def _sgu_fwd(a, ln_g, ln_b, w_s, bs_t):
    t, cw2 = a.shape
    cw = cw2 // 2
    groups = w_s.shape[0]
    cg = cw // groups

    def body(a_ref, lg_ref, lb_ref, ws_ref, bs_ref, m_ref):
        _, u, _, _, vln, mask = _sgu_common(a_ref, lg_ref, lb_ref, cw)
        vb = vln.astype(bf16)
        for g in range(groups):
            sl = slice(g * cg, (g + 1) * cg)
            wm = jnp.where(mask, ws_ref[g], 0.0).astype(bf16)
            s = lax.dot_general(wm, vb[:, sl], _DN["nn"], preferred_element_type=f32) + bs_ref[:, g:g + 1]
            m_ref[:, sl] = (u[:, sl] * s).astype(bf16)

    vec = pl.BlockSpec((1, cw), lambda n: (0, 0))
    return pl.pallas_call(
        body, grid=(t // C_BLOCK,),
        in_specs=[pl.BlockSpec((C_BLOCK, cw2), lambda n: (n, 0)), vec, vec,
                  pl.BlockSpec((groups, C_BLOCK, C_BLOCK), lambda n: (0, 0, 0)),
                  pl.BlockSpec((C_BLOCK, groups), lambda n: (0, 0))],
        out_specs=pl.BlockSpec((C_BLOCK, cw), lambda n: (n, 0)), out_shape=jax.ShapeDtypeStruct((t, cw), bf16),
        compiler_params=_params("parallel"), name="sgu_fwd")(a, ln_g.reshape(1, cw), ln_b.reshape(1, cw), w_s, bs_t)


def _sgu_bwd(a, dm, ln_g, ln_b, w_s, bs_t):
    t, cw2 = a.shape
    cw = cw2 // 2
    groups = w_s.shape[0]
    cg = cw // groups

    def body(a_ref, dm_ref, lg_ref, lb_ref, ws_ref, bs_ref, da_ref, dws_ref, dbs_ref, dlg_ref, dlb_ref, dvln):
        @pl.when(pl.program_id(0) == 0)
        def _():
            dws_ref[...] = jnp.zeros_like(dws_ref)
            dbs_ref[...] = jnp.zeros_like(dbs_ref)
            dlg_ref[...] = jnp.zeros_like(dlg_ref)
            dlb_ref[...] = jnp.zeros_like(dlb_ref)

        av, u, xhat, rstd, vln, mask = _sgu_common(a_ref, lg_ref, lb_ref, cw)
        vb = vln.astype(bf16)
        lane = lax.broadcasted_iota(jnp.int32, (C_BLOCK, groups), 1)
        dbs = jnp.zeros((C_BLOCK, groups), f32)
        for g in range(groups):
            sl = slice(g * cg, (g + 1) * cg)
            wm = jnp.where(mask, ws_ref[g], 0.0).astype(bf16)
            s = lax.dot_general(wm, vb[:, sl], _DN["nn"], preferred_element_type=f32) + bs_ref[:, g:g + 1]
            dmg = dm_ref[:, sl].astype(f32)
            da_ref[:, sl] = (dmg * s * _gelu_grad(av[:, sl])).astype(bf16)
            dsg = dmg * u[:, sl]
            dbs = dbs + jnp.where(lane == g, jnp.sum(dsg, axis=-1, keepdims=True), 0.0)
            dsb = dsg.astype(bf16)
            dws_ref[g] += jnp.where(mask, lax.dot_general(dsb, vb[:, sl], _DN["nt"], preferred_element_type=f32), 0.0)
            dvln[:, sl] = lax.dot_general(wm, dsb, _DN["tn"], preferred_element_type=f32)
        dbs_ref[...] += dbs
        dv = dvln[...]
        dlg_ref[...] += jnp.sum(dv * xhat, axis=0, keepdims=True)
        dlb_ref[...] += jnp.sum(dv, axis=0, keepdims=True)
        dxh = dv * lg_ref[...]
        dvv = rstd * (dxh - jnp.mean(dxh, axis=-1, keepdims=True) - xhat * jnp.mean(dxh * xhat, axis=-1, keepdims=True))
        da_ref[:, cw:] = (dvv * _gelu_grad(av[:, cw:])).astype(bf16)

    vec = pl.BlockSpec((1, cw), lambda n: (0, 0))
    wsp = pl.BlockSpec((groups, C_BLOCK, C_BLOCK), lambda n: (0, 0, 0))
    bsp = pl.BlockSpec((C_BLOCK, groups), lambda n: (0, 0))
    return pl.pallas_call(
        body, grid=(t // C_BLOCK,),
        in_specs=[pl.BlockSpec((C_BLOCK, cw2), lambda n: (n, 0)), pl.BlockSpec((C_BLOCK, cw), lambda n: (n, 0)), vec, vec, wsp, bsp],
        out_specs=[pl.BlockSpec((C_BLOCK, cw2), lambda n: (n, 0)), wsp, bsp, vec, vec],
        out_shape=[jax.ShapeDtypeStruct((t, cw2), bf16), jax.ShapeDtypeStruct(w_s.shape, f32),
                   jax.ShapeDtypeStruct(bs_t.shape, f32), jax.ShapeDtypeStruct((1, cw), f32), jax.ShapeDtypeStruct((1, cw), f32)],
        scratch_shapes=[pltpu.VMEM((C_BLOCK, cw), f32)],
        compiler_params=_params("arbitrary"), name="sgu_bwd")(a, dm, ln_g.reshape(1, cw), ln_b.reshape(1, cw), w_s, bs_t)


HBM = pl.BlockSpec(memory_space=pltpu.HBM)
SEM = pl.BlockSpec(memory_space=pltpu.SEMAPHORE)
EFFECT = pltpu.SideEffectType.DATAFLOW_SIDE_EFFECTING


def _place():
    x, y, c = lax.axis_index("x"), lax.axis_index("y"), lax.axis_index("c")
    return x, y, c, [(1 - x, y), (x, 1 - y), (1 - x, 1 - y)]


def _remote(src, dst, send_sems, recv_sems, k, to):
    return pltpu.make_async_remote_copy(src_ref=src, dst_ref=dst, send_sem=send_sems.at[k], recv_sem=recv_sems.at[k],
                                        device_id=to, device_id_type=MESH)


def _split_start(name, arrays, plan, n_copies, after):
    n = len(arrays)

    def body(*refs):
        send_sems, recv_sems, token = refs[n + 1], refs[n + 2], refs[-1]
        for cp in plan(refs[:n], send_sems, recv_sems):
            cp.start()
        token[...] = jnp.zeros_like(token)

    out = pl.pallas_call(
        body, name=name,
        out_shape=(pltpu.SemaphoreType.DMA((n_copies,)), pltpu.SemaphoreType.DMA((n_copies,)),
                   *[pltpu.HBM(a.shape, a.dtype) for a in arrays], jax.ShapeDtypeStruct((8, LANE), f32)),
        in_specs=[HBM] * n + [ANY], out_specs=(SEM, SEM, *[HBM] * n, pl.BlockSpec(memory_space=pltpu.VMEM)),
        input_output_aliases={i: 2 + i for i in range(n)},
        compiler_params=pltpu.CompilerParams(has_side_effects=EFFECT),
    )(*[pltpu.with_memory_space_constraint(a, pltpu.HBM) for a in arrays], after)
    return (out[0], out[1]), list(out[2:2 + n]), out[-1]


def _split_wait(name, arrays, sems, plan, after):
    n = len(arrays)

    def body(*refs):
        for cp in plan(refs[:n], refs[n], refs[n + 1]):
            cp.wait()

    out = pl.pallas_call(
        body, name=name, out_shape=tuple(pltpu.HBM(a.shape, a.dtype) for a in arrays),
        in_specs=[HBM] * n + [SEM, SEM, ANY], out_specs=tuple([HBM] * n), input_output_aliases={i: i for i in range(n)},
        compiler_params=pltpu.CompilerParams(has_side_effects=EFFECT),
    )(*arrays, sems[0], sems[1], after)
    return list(out)


def _row_pieces(ref_rows, split):
    rc = ref_rows // split
    return [pl.ds(s * rc, rc) for s in range(split)]


def _gather_slots():
    x, y, c, _ = _place()
    slots = (4 * x + 2 * y + c, 4 * (1 - x) + 2 * y + c, 4 * x + 2 * (1 - y) + c, 4 * (1 - x) + 2 * (1 - y) + c)
    return slots, (x, y, 1 - c), (1 - x, y, c), (x, 1 - y, c)


def _to_sibling(cps, k, b, slot, sibling, send_sems, recv_sems):
    for rows in _row_pieces(b.shape[1], D2D_SPLIT):
        cps.append(_remote(b.at[slot, rows], b.at[slot, rows], send_sems, recv_sems, k, sibling))
        k += 1
    return k


def _plan_gather_near(bufs, send_sems, recv_sems):
    (me, _, _, _), sibling, x_peer, y_peer = _gather_slots()
    cps, k = [], 0
    for b in bufs:
        k = _to_sibling(cps, k, b, me, sibling, send_sems, recv_sems)
        for peer in (x_peer, y_peer):
            cps.append(_remote(b.at[me], b.at[me], send_sems, recv_sems, k, peer))
            k += 1
    return cps


def _plan_gather_relay(bufs, send_sems, recv_sems):
    (_, x_slot, y_slot, _), sibling, x_peer, y_peer = _gather_slots()
    cps, k = [], 0
    for b in bufs:
        half = b.shape[1] // 2
        k = _to_sibling(cps, k, b, x_slot, sibling, send_sems, recv_sems)
        k = _to_sibling(cps, k, b, y_slot, sibling, send_sems, recv_sems)
        lower, upper = pl.ds(0, half), pl.ds(half, half)
        cps.append(_remote(b.at[x_slot, lower], b.at[x_slot, lower], send_sems, recv_sems, k, y_peer))
        cps.append(_remote(b.at[y_slot, upper], b.at[y_slot, upper], send_sems, recv_sems, k + 1, x_peer))
        k += 2
    return cps


def _plan_gather_far(bufs, send_sems, recv_sems):
    (_, _, _, far), sibling, _, _ = _gather_slots()
    cps, k = [], 0
    for b in bufs:
        k = _to_sibling(cps, k, b, far, sibling, send_sems, recv_sems)
    return cps


def _plan_rs_sibling(arrs, send_sems, recv_sems):
    n = len(arrs) // 2
    x, y, c, _ = _place()
    cps, k = [], 0
    for g, got in zip(arrs[:n], arrs[n:]):
        for q in range(N_CHIP):
            for rows in _row_pieces(g.shape[1], RS_SPLIT):
                cps.append(_remote(g.at[N_CHIP * (1 - c) + q, rows], got.at[q, rows], send_sems, recv_sems, k, (x, y, 1 - c)))
                k += 1
    return cps


def _plan_rs_chips(arrs, send_sems, recv_sems):
    n = len(arrs) // 2
    x, y, c, chips = _place()
    q = 2 * x + y
    cps, k = [], 0
    for p, r in zip(arrs[:n], arrs[n:]):
        for px, py in chips:
            cps.append(_remote(p.at[2 * px + py], r.at[q], send_sems, recv_sems, k, (px, py, c)))
            k += 1
    return cps


class _Gather:
    STAGES = (("near", _plan_gather_near, D2D_SPLIT + 2), ("relay", _plan_gather_relay, 2 * D2D_SPLIT + 2),
              ("far", _plan_gather_far, D2D_SPLIT))

    def __init__(self, tag, bufs):
        self.tag, self.bufs, self.stage = tag, bufs, -1

    def advance(self, after):
        if self.stage >= 0:
            name, plan, _ = self.STAGES[self.stage]
            self.bufs = _split_wait("gather_%s_wait_%s" % (name, self.tag), self.bufs, self.sems, plan, after)
        self.stage += 1
        if self.stage == len(self.STAGES):
            return self.bufs
        name, plan, per_array = self.STAGES[self.stage]
        self.sems, self.bufs, token = _split_start("gather_%s_start_%s" % (name, self.tag), self.bufs, plan,
                                                   per_array * len(self.bufs), after)
        return token

    def finish(self, after):
        out = self.advance(after)
        while not isinstance(out, list):
            out = self.advance(after)
        return out


class _ReduceScatter:
    def __init__(self, tag, grads, core):
        self.tag, self.n = tag, len(grads)
        lands = [lax.empty((N_CHIP,) + g.shape[1:], g.dtype) for g in grads]
        self.sems, self.arrs, self.token = _split_start("rs_sibling_start_" + tag, list(grads) + lands, _plan_rs_sibling,
                                                        self.n * N_CHIP * RS_SPLIT, core)

    def middle(self, after, core):
        arrs = _split_wait("rs_sibling_wait_" + self.tag, self.arrs, self.sems, _plan_rs_sibling, after)
        parts = [_pair_sum(g, got, core) for g, got in zip(arrs[:self.n], arrs[self.n:])]
        lands = [lax.empty(p.shape, p.dtype) for p in parts]
        self.sems, self.arrs, self.token = _split_start("rs_chips_start_" + self.tag, parts + lands, _plan_rs_chips,
                                                        self.n * 3, core)

    def finish(self, after):
        arrs = _split_wait("rs_chips_wait_" + self.tag, self.arrs, self.sems, _plan_rs_chips, after)
        return list(zip(arrs[:self.n], arrs[self.n:]))


def _cast_into_slot(name, w, layer, me, after, dtype=bf16):
    _, rows, cols = w.shape
    tr = 256 if rows % 256 == 0 else rows

    def body(me_ref, w_ref, after_ref, o_ref):
        o_ref[...] = w_ref[...].astype(dtype)

    return pl.pallas_call(
        body,
        grid_spec=pltpu.PrefetchScalarGridSpec(
            num_scalar_prefetch=1, grid=(rows // tr,),
            in_specs=[pl.BlockSpec((None, tr, cols), lambda i, me_ref: (layer, i, 0)), ANY],
            out_specs=pl.BlockSpec((None, tr, cols), lambda i, me_ref: (me_ref[0], i, 0))),
        out_shape=jax.ShapeDtypeStruct((N_DEV, rows, cols), dtype), compiler_params=_params("parallel"), name=name)(me, w, after)


def _pair_sum(g, got, core):
    _, rows, cols = g.shape
    tr = rows

    def body(c_ref, a_ref, b_ref, o_ref):
        o_ref[...] = (a_ref[...].astype(f32) + b_ref[...].astype(f32)).astype(bf16)

    spec = pl.BlockSpec((None, tr, cols), lambda q, i, c_ref: (q, i, 0))
    return pl.pallas_call(
        body,
        grid_spec=pltpu.PrefetchScalarGridSpec(
            num_scalar_prefetch=1, grid=(N_CHIP, rows // tr),
            in_specs=[pl.BlockSpec((None, tr, cols), lambda q, i, c_ref: (N_CHIP * c_ref[0] + q, i, 0)), spec],
            out_specs=spec),
        out_shape=jax.ShapeDtypeStruct((N_CHIP, rows, cols), bf16), compiler_params=_params("parallel", "parallel"),
        name="pair_sum")(core, g, got)


def _gather_copies(n, ins, outs, send_sems, recv_sems, local_sems):
    x, y, c, chips = _place()
    sibling = (x, y, 1 - c)

    def slot(px, py, pc):
        return 4 * px + 2 * py + pc

    def copy(i, k, block, to, src=None):
        dst = outs[i].at[slot(*block)]
        return pltpu.make_async_remote_copy(src_ref=dst if src is None else src, dst_ref=dst, send_sem=send_sems.at[i, k],
                                            recv_sem=recv_sems.at[i, k], device_id=to, device_id_type=MESH)

    started = []
    for i in range(n):
        mine = pltpu.make_async_copy(ins[i], outs[i].at[slot(x, y, c)], local_sems.at[i])
        mine.start()
        started.append(mine)
    sends = []
    for i in range(n):
        sends.append(copy(i, 0, (x, y, c), sibling, src=ins[i]))
        sends += [copy(i, 1 + j, (x, y, c), (*chip, c), src=ins[i]) for j, chip in enumerate(chips)]
    for cp in sends:
        cp.start()
    for i in range(n):
        for j, chip in enumerate(chips):
            copy(i, 1 + j, (*chip, c), (x, y, c)).wait_recv()
            fwd = copy(i, 4 + j, (*chip, c), sibling)
            fwd.start()
            sends.append(fwd)
    for i in range(n):
        copy(i, 0, sibling, (x, y, c)).wait_recv()
        for j, chip in enumerate(chips):
            copy(i, 4 + j, (*chip, 1 - c), (x, y, c)).wait_recv()
    for cp in sends:
        cp.wait_send()
    for mine in started:
        mine.wait()


def _gather_small(name, packed):
    rows = packed.shape[0]

    def body(x_ref, o_ref, buf, send_sems, recv_sems, local_sems):
        _gather_copies(1, [x_ref], [buf], send_sems, recv_sems, local_sems)
        o_ref[...] = buf[...]

    vm = pl.BlockSpec(memory_space=pltpu.VMEM)
    return pl.pallas_call(
        body, in_specs=[vm], out_specs=vm, out_shape=jax.ShapeDtypeStruct((N_DEV, rows, LANE), f32),
        scratch_shapes=[pltpu.VMEM((N_DEV, rows, LANE), f32), pltpu.SemaphoreType.DMA((1, 7)), pltpu.SemaphoreType.DMA((1, 7)),
                        pltpu.SemaphoreType.DMA((1,))],
        compiler_params=pltpu.CompilerParams(vmem_limit_bytes=VMEM_LIMIT), name=name)(packed)


def _plan_all_peers(bufs, send_sems, recv_sems):
    x, y, c, _ = _place()
    me = 4 * x + 2 * y + c
    peers = [(px, py, pc) for px in (x, 1 - x) for py in (y, 1 - y) for pc in (c, 1 - c)][1:]
    cps, k = [], 0
    for b in bufs:
        for peer in peers:
            cps.append(_remote(b.at[me], b.at[me], send_sems, recv_sems, k, peer))
            k += 1
    return cps


def _sum_slots(buf):
    n, rows, cols = buf.shape

    def body(b_ref, o_ref):
        acc = b_ref[0]
        for j in range(1, n):
            acc = acc + b_ref[j]
        o_ref[...] = acc

    return pl.pallas_call(body, out_shape=jax.ShapeDtypeStruct((rows, cols), buf.dtype),
                          compiler_params=pltpu.CompilerParams(vmem_limit_bytes=VMEM_LIMIT), name="sum_slots")(buf)


def _pack(arrs):
    flat = jnp.concatenate([a.reshape(-1).astype(f32) for a in arrs])
    rows = -(-flat.shape[0] // (8 * LANE)) * 8
    return jnp.pad(flat, (0, rows * LANE - flat.shape[0])).reshape(rows, LANE)


def _unpack(buf, shapes):
    flat = buf.reshape(-1)
    out, off = [], 0
    for s in shapes:
        n = int(np.prod(s))
        out.append(flat[off:off + n].reshape(s))
        off += n
    return out


def _adam_math(w, g, m, v):
    m2 = ADAM_B1 * m + (1.0 - ADAM_B1) * g
    v2 = ADAM_B2 * v + (1.0 - ADAM_B2) * (g * g)
    m_hat = m2 / (1.0 - ADAM_B1 ** ADAM_STEP)
    v_hat = v2 / (1.0 - ADAM_B2 ** ADAM_STEP)
    delta = -ADAM_LR * (m_hat / (jnp.sqrt(v_hat) + ADAM_EPS) + ADAM_WD * w)
    return delta, m2, v2


def _adam_big(name, w, m, v, parts, chip, after):
    layers, rows, cols = w.shape
    tr = 512 if rows % 512 == 0 else 256 if rows % 256 == 0 else rows // 4 if rows % 32 == 0 else 8

    def body(chip_ref, w_ref, m_ref, v_ref, *rest):
        p_refs = rest[:N_CHIP * layers]
        g_ref, d_ref, m2_ref, v2_ref = rest[N_CHIP * layers + 1:]
        for li in range(layers):
            @pl.when(pl.program_id(0) == li)
            def _(li=li):
                g = p_refs[N_CHIP * li][...].astype(f32)
                for q in range(1, N_CHIP):
                    g = g + p_refs[N_CHIP * li + q][...].astype(f32)
                delta, m2, v2 = _adam_math(w_ref[...], g, m_ref[...], v_ref[...])
                g_ref[...] = g
                d_ref[...] = delta
                m2_ref[...] = m2
                v2_ref[...] = v2

    spec = pl.BlockSpec((None, tr, cols), lambda l, i, c_ref: (l, i, 0))
    pspecs, operands = [], []
    for li in range(layers):
        for q in range(N_CHIP):
            pspecs.append(pl.BlockSpec((None, tr, cols),
                                       lambda l, i, c_ref, li=li, q=q: ((c_ref[0] + q) % N_CHIP, jnp.where(l == li, i, 0), 0)))
            operands.append(parts[li][0] if q == 0 else parts[li][1])
    out = jax.ShapeDtypeStruct((layers, rows, cols), f32)
    return pl.pallas_call(
        body,
        grid_spec=pltpu.PrefetchScalarGridSpec(num_scalar_prefetch=1, grid=(layers, rows // tr),
                                               in_specs=[spec, spec, spec] + pspecs + [ANY], out_specs=[spec] * 4),
        out_shape=[out] * 4, compiler_params=_params("arbitrary", "arbitrary"), name=name)(chip, w, m, v, *operands, after)


def _adam_small(w, g, m, v):
    rows = w.shape[0]

    def body(w_ref, g_ref, m_ref, v_ref, d_ref, m2_ref, v2_ref):
        delta, m2, v2 = _adam_math(w_ref[...], g_ref[...], m_ref[...], v_ref[...])
        d_ref[...] = delta
        m2_ref[...] = m2
        v2_ref[...] = v2

    out = jax.ShapeDtypeStruct((rows, LANE), f32)
    return pl.pallas_call(body, out_shape=[out] * 3, name="adam_small")(w, g, m, v)


def kernel(x, mix_norm, ab_w_in, ab_rel_bias, ab_conv_w, ab_w_out, c_w_in, c_ln_g, c_ln_b, c_w_s, c_b_s, c_w_out, ffn_norm, ffn_w_gate, ffn_w_up, ffn_w_down, final_norm, loss_target, m_mix_norm, m_ab_w_in, m_ab_rel_bias, m_ab_conv_w, m_ab_w_out, m_c_w_in, m_c_ln_g, m_c_ln_b, m_c_w_s, m_c_b_s, m_c_w_out, m_ffn_norm, m_ffn_w_gate, m_ffn_w_up, m_ffn_w_down, m_final_norm, v_mix_norm, v_ab_w_in, v_ab_rel_bias, v_ab_conv_w, v_ab_w_out, v_c_w_in, v_c_ln_g, v_c_ln_b, v_c_w_s, v_c_b_s, v_c_w_out, v_ffn_norm, v_ffn_w_gate, v_ffn_w_up, v_ffn_w_down, v_final_norm):
    d = D_MODEL
    a_width = d // 2
    heads = a_width // A_HEAD_DIM
    a_blocks = a_width // LANE
    b_blocks = (d - a_width) // LANE
    n_even, n_odd = (DEPTH + 1) // 2, DEPTH // 2
    me_s = 4 * lax.axis_index("x") + 2 * lax.axis_index("y") + lax.axis_index("c")
    me = me_s.astype(jnp.int32).reshape(1)
    core = lax.axis_index("c").astype(jnp.int32).reshape(1)
    chip = (2 * lax.axis_index("x") + lax.axis_index("y")).astype(jnp.int32).reshape(1)

    weights = dict(mix_norm=mix_norm, ab_w_in=ab_w_in, ab_rel_bias=ab_rel_bias, ab_conv_w=ab_conv_w, ab_w_out=ab_w_out,
                   c_w_in=c_w_in, c_ln_g=c_ln_g, c_ln_b=c_ln_b, c_w_s=c_w_s, c_b_s=c_b_s, c_w_out=c_w_out,
                   ffn_norm=ffn_norm, ffn_w_gate=ffn_w_gate, ffn_w_up=ffn_w_up, ffn_w_down=ffn_w_down, final_norm=final_norm)
    mom_m = dict(mix_norm=m_mix_norm, ab_w_in=m_ab_w_in, ab_rel_bias=m_ab_rel_bias, ab_conv_w=m_ab_conv_w, ab_w_out=m_ab_w_out,
                 c_w_in=m_c_w_in, c_ln_g=m_c_ln_g, c_ln_b=m_c_ln_b, c_w_s=m_c_w_s, c_b_s=m_c_b_s, c_w_out=m_c_w_out,
                 ffn_norm=m_ffn_norm, ffn_w_gate=m_ffn_w_gate, ffn_w_up=m_ffn_w_up, ffn_w_down=m_ffn_w_down, final_norm=m_final_norm)
    mom_v = dict(mix_norm=v_mix_norm, ab_w_in=v_ab_w_in, ab_rel_bias=v_ab_rel_bias, ab_conv_w=v_ab_conv_w, ab_w_out=v_ab_w_out,
                 c_w_in=v_c_w_in, c_ln_g=v_c_ln_g, c_ln_b=v_c_ln_b, c_w_s=v_c_w_s, c_b_s=v_c_b_s, c_w_out=v_c_w_out,
                 ffn_norm=v_ffn_norm, ffn_w_gate=v_ffn_w_gate, ffn_w_up=v_ffn_w_up, ffn_w_down=v_ffn_w_down, final_norm=v_final_norm)
    order = list(weights)
    wide = ("ffn_w_gate", "ffn_w_up")
    flip = lambda a: jnp.swapaxes(a, 1, 2)
    local = {k: (flip(weights[k]), flip(mom_m[k]), flip(mom_v[k])) if k in wide else (weights[k], mom_m[k], mom_v[k]) for k in order}

    sharded_small = [ab_conv_w, c_ln_g, c_ln_b]
    gathered = _gather_small("gather_small", _pack(sharded_small))
    conv_parts, lng_parts, lnb_parts = [], [], []
    for j in range(N_DEV):
        cw_j, lg_j, lb_j = _unpack(gathered[j], [a.shape for a in sharded_small])
        conv_parts.append(cw_j)
        lng_parts.append(lg_j)
        lnb_parts.append(lb_j)
    conv_full = jnp.concatenate(conv_parts, axis=-1)
    lng_full = jnp.concatenate(lng_parts, axis=-1)
    lnb_full = jnp.concatenate(lnb_parts, axis=-1)

    gate_t, up_t = local["ffn_w_gate"][0], local["ffn_w_up"][0]
    sets = []
    for layer in range(DEPTH):
        i = layer // 2
        if layer % 2 == 0:
            sets += [("ab_in%d" % i, [(ab_w_in, i)]), ("ab_out%d" % i, [(ab_w_out, i)])]
        else:
            sets += [("c_in%d" % i, [(c_w_in, i)]), ("c_out%d" % i, [(c_w_out, i)])]
        sets += [("ffn_in%d" % layer, [(gate_t, layer), (up_t, layer)]), ("ffn_out%d" % layer, [(ffn_w_down, layer)])]
    units = [None] * len(sets)
    cursor = [0]
    tokens = []

    def start_gather(k, after):
        if k < len(sets):
            tag, members = sets[k]
            units[k] = _Gather(tag, [_cast_into_slot("cast_slot", w, li, me, after) for w, li in members])
            tokens.append(units[k].advance(after))

    def next_weights(after):
        k = cursor[0]
        cursor[0] = k + 1
        ready = units[k].finish(after)
        for later in range(k + 1, min(k + len(_Gather.STAGES), len(sets)) if k else 2):
            tokens.append(units[later].advance(after))
        start_gather(k + GATHER_AHEAD, after)
        return ready

    def started():
        deps = list(tokens)
        tokens.clear()
        return deps

    xs = x[0]
    tgt = loss_target[0]
    start_gather(0, gathered)
    start_gather(1, gathered)
    tokens.append(units[0].advance(tokens[-1]))
    for k in range(2, GATHER_AHEAD):
        start_gather(k, tokens[-1])
    tokens.append(units[0].advance(tokens[-1]))
    saved = []
    h = _rms_fwd(xs, mix_norm[0])
    for layer in range(DEPTH):
        i = layer // 2
        (w_in_g,) = next_weights(xs)
        if layer % 2 == 0:
            proj = _mm_cols("ab_proj", h, w_in_g, bf16, started())
            btab = _bias_table(ab_rel_bias[i])
            attn = _attn_fwd(proj, btab, heads, d)
            mixed = _conv_fwd(proj, conv_full[i], attn, a_blocks, b_blocks)
            ctx = (proj, btab)
        else:
            proj = _mm_cols("c_proj", h, w_in_g, f32, started())
            bs_t = jnp.transpose(c_b_s[i])
            mixed = _sgu_fwd(proj, lng_full[i], lnb_full[i], c_w_s[i], bs_t)
            ctx = (proj, bs_t)
        (w_out_g,) = next_weights(mixed)
        w_out_full = w_out_g.reshape(-1, w_out_g.shape[-1])
        x1, h2 = _mm_rows_res("mix_out", mixed, w_out_full, xs, ffn_norm[layer], started())
        wg_g, wu_g = next_weights(x1)
        g_act, u_act, act = _ffn_in(h2, wg_g, wu_g, started())
        (wd_g,) = next_weights(g_act)
        nxt = _ffn_down(act, wd_g, x1, mix_norm[layer + 1] if layer + 1 < DEPTH else None, started())
        saved.append((xs, h, ctx, mixed, x1, h2, g_act, u_act, act, w_in_g, w_out_full, wg_g, wu_g, wd_g))
        xs, h = nxt if len(nxt) == 2 else (nxt[0], None)

    loss_part, dx, dxb, d_final = _loss_head(xs, final_norm, tgt)
    loss = lax.psum(loss_part[0, 0], ("x", "y", "c"))

    scatters = {}
    small = {k: [None] * weights[k].shape[0] for k in ("mix_norm", "ffn_norm", "ab_rel_bias", "ab_conv_w", "c_ln_g", "c_ln_b",
                                                       "c_w_s", "c_b_s")}
    for layer in reversed(range(DEPTH)):
        i = layer // 2
        xs, h, ctx, mixed, x1, h2, g_act, u_act, act, w_in_g, w_out_full, wg_g, wu_g, wd_g = saved[layer]
        dg, du = _ffn_bwd_act(dxb, wd_g, g_act, u_act, started())
        dwd = _ffn_dwd(act, dxb)
        dwg, dwu = _ffn_dwgu(h2, dg, du)
        rs_ffn = _ReduceScatter("ffn%d" % layer, [dwg, dwu, dwd], core)
        dh2 = _ffn_dh(dg, du, wg_g, wu_g, [rs_ffn.token])
        dx, dxb, dgn = _rms_bwd(x1, ffn_norm[layer], dh2, dx)
        small["ffn_norm"][layer] = dgn[0]
        rs_ffn.middle(dxb, core)
        for pos, k in enumerate(("ffn_w_gate", "ffn_w_up", "ffn_w_down")):
            scatters[(k, layer)] = (rs_ffn, pos)
        dmixed = _mm_nt("mix_out_bwd", dxb, w_out_full, bf16, [rs_ffn.token])
        dwout = _mm_tn_rows("mix_out_dw", mixed, dxb)
        if layer % 2 == 0:
            proj, btab = ctx
            dq, dk, dv, dtab = _attn_bwd(proj, dmixed, btab, heads)
            db, dc, dhv, dcw = _conv_bwd(proj, dmixed, conv_full[i], a_blocks, b_blocks)
            dproj = jnp.concatenate([dq, dk, dv, db, dc, dhv], axis=-1)
            small["ab_rel_bias"][i] = _bias_table_grad(dtab)
            small["ab_conv_w"][i] = dcw
            names = ("ab_w_in", "ab_w_out")
            tag = "ab"
        else:
            proj, bs_t = ctx
            dproj, dws, dbs_t, dlg, dlb = _sgu_bwd(proj, dmixed, lng_full[i], lnb_full[i], c_w_s[i], bs_t)
            small["c_w_s"][i] = dws
            small["c_b_s"][i] = jnp.transpose(dbs_t)
            small["c_ln_g"][i] = dlg[0]
            small["c_ln_b"][i] = dlb[0]
            names = ("c_w_in", "c_w_out")
            tag = "c"
        dwin = _mm_tn_cols(tag + "_proj_dw", h, dproj)
        rs_mix = _ReduceScatter("%s%d" % (tag, i), [dwin, dwout], core)
        dx, dxb, dgm = _mm_nt_cols_norm(tag + "_proj_bwd", dproj, w_in_g, xs, mix_norm[layer], dx, [rs_mix.token])
        small["mix_norm"][layer] = jnp.sum(dgm, axis=0)[0]
        rs_mix.middle(dxb, core)
        tokens.append(rs_mix.token)
        scatters[(names[0], i)] = (rs_mix, 0)
        scatters[(names[1], i)] = (rs_mix, 1)
    grad_x = dx[None]

    small_names = ["mix_norm", "ffn_norm", "ab_rel_bias", "ab_conv_w", "c_ln_g", "c_ln_b", "c_w_s", "c_b_s"]
    small_full = [jnp.stack(small[k]) for k in small_names] + [d_final[0]]
    small_slot = _cast_into_slot("small_slot", _pack(small_full)[None], 0, me, started()[-1], f32)
    small_sems, (small_buf,), last = _split_start("small_sum_start", [small_slot], _plan_all_peers, N_DEV - 1, core)

    grads, deltas, new_m, new_v = {}, {}, {}, {}
    finished = {}
    for k in ("c_w_in", "c_w_out", "ffn_w_gate", "ffn_w_up", "ffn_w_down", "ab_w_in", "ab_w_out"):
        parts = []
        w_k, m_k, v_k = local[k]
        for li in range(w_k.shape[0]):
            rs, pos = scatters[(k, li)]
            if id(rs) not in finished:
                finished[id(rs)] = rs.finish(last)
            parts.append(finished[id(rs)][pos])
        outs = _adam_big("adam_" + k, w_k, m_k, v_k, parts, chip, last)
        last = outs[1]
        grads[k], deltas[k], new_m[k], new_v[k] = [flip(o) for o in outs] if k in wide else outs

    (small_buf,) = _split_wait("small_sum_wait", [small_buf], small_sems, _plan_all_peers, last)
    summed = _unpack(_sum_slots(small_buf), [a.shape for a in small_full])
    small_grads = dict(zip(small_names + ["final_norm"], summed))
    for k in ("ab_conv_w", "c_ln_g", "c_ln_b"):
        width = weights[k].shape[-1]
        small_grads[k] = lax.dynamic_slice_in_dim(small_grads[k], me_s * width, width, axis=-1)
    small_order = [k for k in order if k in small_grads]
    shapes = [weights[k].shape for k in small_order]
    d_s, m_s, v_s = _adam_small(_pack([weights[k] for k in small_order]), _pack([small_grads[k] for k in small_order]),
                                _pack([mom_m[k] for k in small_order]), _pack([mom_v[k] for k in small_order]))
    grads.update(small_grads)
    for k, dd, mm, vv in zip(small_order, _unpack(d_s, shapes), _unpack(m_s, shapes), _unpack(v_s, shapes)):
        deltas[k], new_m[k], new_v[k] = dd, mm, vv

    return (loss, grad_x, *[grads[k] for k in order], *[deltas[k] for k in order], *[new_m[k] for k in order],
            *[new_v[k] for k in order])
```
